```python
import math
import jax, jax.numpy as jnp
from jax import lax
import numpy as np

D_MODEL = 1024
BATCH = 2
SEQ = 8192
DEPTH = 1

MIX_WIDTH = D_MODEL
RG_WIDTH = MIX_WIDTH // 2
RG_BLOCKS = 8
RG_BLOCK = RG_WIDTH // RG_BLOCKS
CONV_WIDTH = 4
C_RG = 8.0
GLA_HEADS = 4
GLA_VAL = MIX_WIDTH - RG_WIDTH
GLA_KEY = GLA_VAL // 2
GLA_DK = GLA_KEY // GLA_HEADS
GLA_DV = GLA_VAL // GLA_HEADS
GATE_RANK = 16
GATE_NORM = 16.0
CHUNK = 64
IN_WIDTH = 2 * RG_WIDTH + 2 * GLA_KEY + 2 * GLA_VAL + GATE_RANK
N_GROUPS = 4
EXPERTS_PER_GROUP = 8
N_EXPERTS = N_GROUPS * EXPERTS_PER_GROUP
TOP_K = 2
D_EXPERT = D_MODEL
EXPERT_BLOCK = 128
EPS = 1e-6

kernel_name = "hymba_rglru_gla_hmoe_layer"


def rmsnorm(x, g):
    xf = x.astype(jnp.float32)
    y = xf * lax.rsqrt(jnp.mean(xf * xf, axis=-1, keepdims=True) + EPS)
    return (y * g.astype(jnp.float32)).astype(x.dtype)


def causal_depthwise_conv(x, w, b):
    c = x.shape[-1]
    y = lax.conv_general_dilated(
        x, w[:, None, :].astype(x.dtype), window_strides=(1,),
        padding=[(CONV_WIDTH - 1, 0)], dimension_numbers=("NWC", "WIO", "NWC"),
        feature_group_count=c)
    return y + b.astype(x.dtype)


def rg_lru(xc, w_a, b_a, w_x, b_x, lam):
    bsz, s, _ = xc.shape
    xf = xc.astype(jnp.float32)
    xh = xf.reshape(bsz, s, RG_BLOCKS, RG_BLOCK)
    r = jax.nn.sigmoid(jnp.einsum('bshi,hij->bshj', xh, w_a.astype(jnp.float32)) + b_a.astype(jnp.float32))
    i = jax.nn.sigmoid(jnp.einsum('bshi,hij->bshj', xh, w_x.astype(jnp.float32)) + b_x.astype(jnp.float32))
    r = r.reshape(bsz, s, RG_WIDTH)
    i = i.reshape(bsz, s, RG_WIDTH)
    log_a = -C_RG * r * jax.nn.softplus(-lam.astype(jnp.float32))
    a = jnp.exp(log_a)
    u = jnp.sqrt(-jnp.expm1(2.0 * log_a)) * (i * xf)

    def combine(e1, e2):
        a1, b1 = e1
        a2, b2 = e2
        return a1 * a2, a2 * b1 + b2

    _, h = lax.associative_scan(combine, (a, u), axis=1)
    return h


def gla_chunked(q, k, v, log_alpha):
    bsz, s, h, dk = q.shape
    dv = v.shape[-1]
    n = s // CHUNK

    def to_chunks(t):
        return t.reshape(bsz, n, CHUNK, h, t.shape[-1]).transpose(1, 0, 3, 2, 4)

    q, k, v, la = to_chunks(q * (dk ** -0.5)), to_chunks(k), to_chunks(v), to_chunks(log_alpha)
    b = jnp.cumsum(la, axis=3)
    b_last = b[:, :, :, -1:, :]
    q_s = q * jnp.exp(b)
    k_s = k * jnp.exp(-b)
    k_end = k * jnp.exp(b_last - b)
    decay = jnp.exp(b_last[:, :, :, 0, :])

    mask = jnp.tril(jnp.ones((CHUNK, CHUNK), dtype=bool))
    att = jnp.einsum('nbhid,nbhjd->nbhij', q_s, k_s)
    att = jnp.where(mask, att, 0.0)
    o_intra = jnp.einsum('nbhij,nbhje->nbhie', att, v)

    def step(state, inp):
        qc, kc, vc, dc = inp
        o = jnp.einsum('bhcd,bhde->bhce', qc, state)
        state = dc[..., None] * state + jnp.einsum('bhcd,bhce->bhde', kc, vc)
        return state, o

    s0 = jnp.zeros((bsz, h, dk, dv), jnp.float32)
    _, o_inter = lax.scan(step, s0, (q_s, k_end, v, decay))
    o = o_intra + o_inter
    return o.transpose(1, 0, 3, 2, 4).reshape(bsz, s, h, dv)


def mixer(hn, w_in, conv_w, conv_b, w_rg_a, b_rg_a, w_rg_x, b_rg_x, rg_lambda, rg_norm,
          w_alpha_up, b_alpha, gla_norm, w_out):
    bsz, s, _ = hn.shape
    proj = hn @ w_in.astype(hn.dtype)
    cuts = np.cumsum([RG_WIDTH, RG_WIDTH, GLA_KEY, GLA_KEY, GLA_VAL, GLA_VAL]).tolist()
    xr, yr, q, k, v, g, a_low = jnp.split(proj, cuts, axis=-1)

    xc = causal_depthwise_conv(xr, conv_w, conv_b)
    h_rg = rg_lru(xc, w_rg_a, b_rg_a, w_rg_x, b_rg_x, rg_lambda)
    y_rg = h_rg * jax.nn.gelu(yr.astype(jnp.float32))
    y_rg = rmsnorm(y_rg, rg_norm)

    gate_logits = a_low.astype(jnp.float32) @ w_alpha_up.astype(jnp.float32) + b_alpha.astype(jnp.float32)
    log_alpha = jax.nn.log_sigmoid(gate_logits) / GATE_NORM
    qh = q.astype(jnp.float32).reshape(bsz, s, GLA_HEADS, GLA_DK)
    kh = k.astype(jnp.float32).reshape(bsz, s, GLA_HEADS, GLA_DK)
    vh = v.astype(jnp.float32).reshape(bsz, s, GLA_HEADS, GLA_DV)
    lah = log_alpha.reshape(bsz, s, GLA_HEADS, GLA_DK)
    o = gla_chunked(qh, kh, vh, lah)
    o = rmsnorm(o, gla_norm) * jax.nn.silu(g.astype(jnp.float32).reshape(bsz, s, GLA_HEADS, GLA_DV))
    y_gla = o.reshape(bsz, s, GLA_VAL)

    y = jnp.concatenate([y_rg, y_gla], axis=-1).astype(hn.dtype)
    return y @ w_out.astype(hn.dtype)


def hierarchical_moe(hn, w_router_group, b_router_group, w_router_expert, b_router_expert,
                     w_exp_gate, w_exp_up, w_exp_down):
    bsz, s, d = hn.shape
    t = bsz * s
    xt = hn.reshape(t, d)
    pg = jax.nn.softmax((xt @ w_router_group.astype(xt.dtype)).astype(jnp.float32)
                        + b_router_group.astype(jnp.float32), axis=-1)
    g_idx = jnp.argmax(pg, axis=-1)
    g_w = jnp.max(pg, axis=-1)
    le = ((xt @ w_router_expert.astype(xt.dtype)).astype(jnp.float32)
          + b_router_expert.astype(jnp.float32)).reshape(t, N_GROUPS, EXPERTS_PER_GROUP)
    le = jnp.take_along_axis(le, g_idx[:, None, None], axis=1)[:, 0]
    pe = jax.nn.softmax(le, axis=-1)
    top_p, top_i = lax.top_k(pe, TOP_K)
    top_p = top_p / jnp.sum(top_p, axis=-1, keepdims=True)
    weights = g_w[:, None] * top_p
    experts = g_idx[:, None] * EXPERTS_PER_GROUP + top_i

    n_assign = t * TOP_K
    e_flat = experts.reshape(n_assign).astype(jnp.int32)
    w_flat = weights.reshape(n_assign)
    tok_flat = jnp.repeat(jnp.arange(t, dtype=jnp.int32), TOP_K)
    order = jnp.argsort(e_flat, stable=True)
    sorted_e = e_flat[order]
    counts = jnp.bincount(e_flat, length=N_EXPERTS).astype(jnp.int32)
    starts = jnp.cumsum(counts) - counts
    padded = ((counts + EXPERT_BLOCK - 1) // EXPERT_BLOCK) * EXPERT_BLOCK
    pends = jnp.cumsum(padded)
    pstarts = pends - padded
    rank = jnp.arange(n_assign, dtype=jnp.int32) - starts[sorted_e]
    dest = pstarts[sorted_e] + rank
    n_blocks = (n_assign + N_EXPERTS * (EXPERT_BLOCK - 1) + EXPERT_BLOCK - 1) // EXPERT_BLOCK
    cap = n_blocks * EXPERT_BLOCK
    slot_tok = jnp.zeros((cap,), jnp.int32).at[dest].set(tok_flat[order])
    slot_w = jnp.zeros((cap,), jnp.float32).at[dest].set(w_flat[order])
    block_start = jnp.arange(n_blocks, dtype=jnp.int32) * EXPERT_BLOCK
    block_e = jnp.clip(jnp.searchsorted(pends, block_start, side='right'), 0, N_EXPERTS - 1)

    xb = xt[slot_tok].reshape(n_blocks, EXPERT_BLOCK, d)

    def expert_block(args):
        xblk, e = args
        wg = w_exp_gate[e].astype(xblk.dtype)
        wu = w_exp_up[e].astype(xblk.dtype)
        wd = w_exp_down[e].astype(xblk.dtype)
        return (jax.nn.gelu(xblk @ wg) * (xblk @ wu)) @ wd

    yb = lax.map(expert_block, (xb, block_e)).reshape(cap, d)
    out = jnp.zeros((t, d), jnp.float32).at[slot_tok].add(yb.astype(jnp.float32) * slot_w[:, None])
    return out.astype(hn.dtype).reshape(bsz, s, d)


def setup_inputs(seed: int = 0) -> dict:
    key = jax.random.key(seed)
    ks = jax.random.split(key, 24)
    f32 = jnp.float32
    nrm = lambda k, shape, scale: jax.random.normal(k, shape, f32) * scale
    u = jax.random.uniform(ks[9], (DEPTH, RG_WIDTH), f32, minval=0.9, maxval=0.999)
    a0 = u ** (1.0 / C_RG)
    rg_lambda = jnp.log(a0) - jnp.log1p(-a0)
    return {
        "x": jax.random.normal(ks[0], (BATCH, SEQ, D_MODEL), f32),
        "norm_mix": 1.0 + nrm(ks[1], (DEPTH, D_MODEL), 0.02),
        "w_in": nrm(ks[2], (DEPTH, D_MODEL, IN_WIDTH), D_MODEL ** -0.5),
        "conv_w": nrm(ks[3], (DEPTH, CONV_WIDTH, RG_WIDTH), CONV_WIDTH ** -0.5),
        "conv_b": nrm(ks[4], (DEPTH, RG_WIDTH), 0.01),
        "w_rg_a": nrm(ks[5], (DEPTH, RG_BLOCKS, RG_BLOCK, RG_BLOCK), RG_BLOCK ** -0.5),
        "b_rg_a": nrm(ks[6], (DEPTH, RG_BLOCKS, RG_BLOCK), 0.01),
        "w_rg_x": nrm(ks[7], (DEPTH, RG_BLOCKS, RG_BLOCK, RG_BLOCK), RG_BLOCK ** -0.5),
        "b_rg_x": nrm(ks[8], (DEPTH, RG_BLOCKS, RG_BLOCK), 0.01),
        "rg_lambda": rg_lambda,
        "rg_norm": 1.0 + nrm(ks[10], (DEPTH, RG_WIDTH), 0.02),
        "w_alpha_up": nrm(ks[11], (DEPTH, GATE_RANK, GLA_KEY), GATE_RANK ** -0.5),
        "b_alpha": nrm(ks[12], (DEPTH, GLA_KEY), 0.01),
        "gla_norm": 1.0 + nrm(ks[13], (DEPTH, GLA_DV), 0.02),
        "w_out": nrm(ks[14], (DEPTH, MIX_WIDTH, D_MODEL), MIX_WIDTH ** -0.5),
        "norm_ffn": 1.0 + nrm(ks[15], (DEPTH, D_MODEL), 0.02),
        "w_router_group": nrm(ks[16], (DEPTH, D_MODEL, N_GROUPS), D_MODEL ** -0.5),
        "b_router_group": nrm(ks[17], (DEPTH, N_GROUPS), 0.01),
        "w_router_expert": nrm(ks[18], (DEPTH, D_MODEL, N_EXPERTS), D_MODEL ** -0.5),
        "b_router_expert": nrm(ks[19], (DEPTH, N_EXPERTS), 0.01),
        "w_exp_gate": nrm(ks[20], (DEPTH, N_EXPERTS, D_MODEL, D_EXPERT), D_MODEL ** -0.5),
        "w_exp_up": nrm(ks[21], (DEPTH, N_EXPERTS, D_MODEL, D_EXPERT), D_MODEL ** -0.5),
        "w_exp_down": nrm(ks[22], (DEPTH, N_EXPERTS, D_EXPERT, D_MODEL), D_EXPERT ** -0.5),
        "norm_final": 1.0 + nrm(ks[23], (D_MODEL,), 0.02),
    }


def reference(x, norm_mix, w_in, conv_w, conv_b, w_rg_a, b_rg_a, w_rg_x, b_rg_x, rg_lambda,
              rg_norm, w_alpha_up, b_alpha, gla_norm, w_out, norm_ffn, w_router_group,
              b_router_group, w_router_expert, b_router_expert, w_exp_gate, w_exp_up,
              w_exp_down, norm_final):
    h = x
    for l in range(DEPTH):
        hn = rmsnorm(h, norm_mix[l])
        h = h + mixer(hn, w_in[l], conv_w[l], conv_b[l], w_rg_a[l], b_rg_a[l], w_rg_x[l],
                      b_rg_x[l], rg_lambda[l], rg_norm[l], w_alpha_up[l], b_alpha[l],
                      gla_norm[l], w_out[l])
        hn = rmsnorm(h, norm_ffn[l])
        h = h + hierarchical_moe(hn, w_router_group[l], b_router_group[l], w_router_expert[l],
                                 b_router_expert[l], w_exp_gate[l], w_exp_up[l], w_exp_down[l])
    return rmsnorm(h, norm_final)
```

```python
import functools

import jax
import jax.numpy as jnp
from jax import lax
from jax.experimental import pallas as pl
from jax.experimental.pallas import tpu as pltpu

F32 = jnp.float32
BF16 = jnp.bfloat16
HIGHEST = lax.Precision.HIGHEST

D_MODEL = 1024
RG_WIDTH = 512
RG_BLOCKS = 8
RG_BLOCK = 64
CONV_WIDTH = 4
C_RG = 8.0
GLA_HEADS = 4
GLA_VAL = 512
GLA_KEY = 256
GLA_DK = 64
GLA_DV = 128
GATE_RANK = 16
GATE_NORM = 16.0
CHUNK = 64
N_GROUPS = 4
EXPERTS_PER_GROUP = 8
N_EXPERTS = 32
EPS = 1e-6

LANES = 128
SUBLANES = 8
VMEM_LIMIT = 56 * 1024 * 1024

TM_PROJ = 512
TS_RG = 256
TS_GLA = 256
TM_FINAL = 512
BM = 256
TT_PLAN = 512


def _dot(a, b):
    return jnp.dot(a, b, preferred_element_type=F32)


def _softplus(z):
    return jnp.maximum(z, 0.0) + jnp.log1p(jnp.exp(-jnp.abs(z)))


def _rms(x, g):
    return x * lax.rsqrt(jnp.mean(x * x, axis=-1, keepdims=True) + EPS) * g


def _inproj_kernel(x_ref, g_ref, wxr, wyr, wq, wk, wv, wg, wal, wup, bal,
                   xr_o, yr_o, q_o, k_o, v_o, g_o, la_o):
    hn = _rms(x_ref[...], g_ref[...]).astype(BF16)
    xr_o[...] = _dot(hn, wxr[...])
    yr_o[...] = _dot(hn, wyr[...])
    q_o[...] = _dot(hn, wq[...])
    k_o[...] = _dot(hn, wk[...])
    v_o[...] = _dot(hn, wv[...])
    g_o[...] = _dot(hn, wg[...])
    a_low = _dot(hn, wal[...])
    z = jnp.dot(a_low, wup[...], precision=HIGHEST, preferred_element_type=F32) + bal[...]
    log_sig = jnp.minimum(z, 0.0) - jnp.log1p(jnp.exp(-jnp.abs(z)))
    la_o[...] = log_sig * (1.0 / GATE_NORM)


def _inproj(x2, norm_mix, w_in, w_alpha_up, b_alpha):
    t = x2.shape[0]
    c = [0, 512, 1024, 1280, 1536, 2048, 2560, 2576]
    wb = w_in.astype(BF16)
    segs = [wb[:, c[i]:c[i + 1]] for i in range(6)]
    wal = jnp.pad(wb[:, c[6]:c[7]], ((0, 0), (0, LANES - GATE_RANK)))
    wup = jnp.pad(w_alpha_up.astype(F32), ((0, LANES - GATE_RANK), (0, 0)))
    widths = [512, 512, 256, 256, 512, 512, 256]
    row = lambda i: (i, 0)
    fixed = lambda i: (0, 0)
    in_specs = [pl.BlockSpec((TM_PROJ, D_MODEL), row), pl.BlockSpec((1, D_MODEL), fixed)]
    in_specs += [pl.BlockSpec((D_MODEL, w), fixed) for w in widths[:6]]
    in_specs += [pl.BlockSpec((D_MODEL, LANES), fixed), pl.BlockSpec((LANES, GLA_KEY), fixed),
                 pl.BlockSpec((1, GLA_KEY), fixed)]
    return pl.pallas_call(
        _inproj_kernel,
        grid=(t // TM_PROJ,),
        in_specs=in_specs,
        out_specs=[pl.BlockSpec((TM_PROJ, w), row) for w in widths],
        out_shape=[jax.ShapeDtypeStruct((t, w), F32) for w in widths],
        compiler_params=pltpu.CompilerParams(dimension_semantics=("arbitrary",),
                                             vmem_limit_bytes=VMEM_LIMIT),
        name="inproj",
    )(x2, norm_mix.reshape(1, D_MODEL), *segs, wal, wup, b_alpha.reshape(1, GLA_KEY))


def _rglru_kernel(xr_ref, yr_ref, cw_ref, cb_ref, wa_ref, ba_ref, wx_ref, bx_ref, lam_ref, gn_ref,
                  o_ref, ext_ref, hc_ref):
    ts = xr_ref.shape[0]

    @pl.when(pl.program_id(1) == 0)
    def _():
        ext_ref[0:SUBLANES, :] = jnp.zeros((SUBLANES, RG_WIDTH), F32)
        hc_ref[...] = jnp.zeros_like(hc_ref)

    x = xr_ref[...]
    ext_ref[SUBLANES:SUBLANES + ts, :] = x
    cw = cw_ref[...]
    xc = cb_ref[...] + cw[CONV_WIDTH - 1:CONV_WIDTH, :] * x
    for s in range(1, CONV_WIDTH):
        xc = xc + cw[CONV_WIDTH - 1 - s:CONV_WIDTH - s, :] * ext_ref[SUBLANES - s:SUBLANES - s + ts, :]
    ext_ref[0:SUBLANES, :] = x[ts - SUBLANES:ts, :]

    xb = xc.astype(BF16)
    r = jax.nn.sigmoid(_dot(xb, wa_ref[...]) + ba_ref[...])
    gate_i = jax.nn.sigmoid(_dot(xb, wx_ref[...]) + bx_ref[...])
    log_a = (-C_RG) * r * _softplus(-lam_ref[...])
    a = jnp.exp(log_a)
    u = jnp.sqrt(jnp.tanh(-log_a) * (1.0 + a * a)) * (gate_i * xc)

    row = lax.broadcasted_iota(jnp.int32, (ts, RG_WIDTH), 0)
    d = 1
    while d < ts:
        keep = row >= d
        a_sh = jnp.where(keep, pltpu.roll(a, d, axis=0), 1.0)
        u_sh = jnp.where(keep, pltpu.roll(u, d, axis=0), 0.0)
        u = a * u_sh + u
        a = a * a_sh
        d *= 2
    h = u + a * hc_ref[0:1, :]
    hc_ref[...] = jnp.broadcast_to(h[ts - 1:ts, :], hc_ref.shape)

    y = h * jax.nn.gelu(yr_ref[...])
    o_ref[...] = _rms(y, gn_ref[...])


def _block_diag(w):
    eye = jnp.eye(RG_BLOCKS, dtype=w.dtype)
    return jnp.einsum('hij,hg->higj', w, eye).reshape(RG_WIDTH, RG_WIDTH)


def _rglru(xr, yr, conv_w, conv_b, w_a, b_a, w_x, b_x, lam, gn, bsz, seq):
    nt = seq // TS_RG
    row = lambda b, i: (b * nt + i, 0)
    fixed = lambda b, i: (0, 0)
    vec = lambda v: v.reshape(1, RG_WIDTH).astype(F32)
    return pl.pallas_call(
        _rglru_kernel,
        grid=(bsz, nt),
        in_specs=[pl.BlockSpec((TS_RG, RG_WIDTH), row), pl.BlockSpec((TS_RG, RG_WIDTH), row),
                  pl.BlockSpec((CONV_WIDTH, RG_WIDTH), fixed), pl.BlockSpec((1, RG_WIDTH), fixed),
                  pl.BlockSpec((RG_WIDTH, RG_WIDTH), fixed), pl.BlockSpec((1, RG_WIDTH), fixed),
                  pl.BlockSpec((RG_WIDTH, RG_WIDTH), fixed), pl.BlockSpec((1, RG_WIDTH), fixed),
                  pl.BlockSpec((1, RG_WIDTH), fixed), pl.BlockSpec((1, RG_WIDTH), fixed)],
        out_specs=pl.BlockSpec((TS_RG, RG_WIDTH), row),
        out_shape=jax.ShapeDtypeStruct((bsz * seq, RG_WIDTH), F32),
        scratch_shapes=[pltpu.VMEM((TS_RG + SUBLANES, RG_WIDTH), F32),
                        pltpu.VMEM((SUBLANES, RG_WIDTH), F32)],
        compiler_params=pltpu.CompilerParams(dimension_semantics=("arbitrary", "arbitrary"),
                                             vmem_limit_bytes=VMEM_LIMIT),
        name="rglru",
    )(xr, yr, conv_w.astype(F32), vec(conv_b), _block_diag(w_a).astype(BF16), vec(b_a),
      _block_diag(w_x).astype(BF16), vec(b_x), vec(lam), vec(gn))


def _gla_kernel(q_ref, k_ref, v_ref, g_ref, la_ref, gn_ref, o_ref, st_ref):
    ts = q_ref.shape[0]

    @pl.when(pl.program_id(1) == 0)
    def _():
        st_ref[...] = jnp.zeros_like(st_ref)

    ri = lax.broadcasted_iota(jnp.int32, (CHUNK, CHUNK), 0)
    ci = lax.broadcasted_iota(jnp.int32, (CHUNK, CHUNK), 1)
    causal = ri >= ci
    tril = causal.astype(F32)
    scale = GLA_DK ** -0.5
    gn = gn_ref[...]

    for c in range(ts // CHUNK):
        rows = slice(c * CHUNK, (c + 1) * CHUNK)
        la = la_ref[rows, :]
        b = jnp.dot(tril, la, precision=HIGHEST, preferred_element_type=F32)
        b_last = b[CHUNK - 1:CHUNK, :]
        q_s = (q_ref[rows, :] * scale) * jnp.exp(b)
        kk = k_ref[rows, :]
        k_s = kk * jnp.exp(-b)
        k_end = kk * jnp.exp(b_last - b)
        decay = jnp.exp(b_last)
        st = st_ref[...]
        new_parts = []
        for h in range(GLA_HEADS):
            ks = slice(h * GLA_DK, (h + 1) * GLA_DK)
            vs = slice(h * GLA_DV, (h + 1) * GLA_DV)
            qh = q_s[:, ks].astype(BF16)
            v_h = v_ref[rows, vs]
            vb = v_h.astype(BF16)
            att = lax.dot_general(qh, k_s[:, ks].astype(BF16), (((1,), (1,)), ((), ())),
                                  preferred_element_type=F32)
            att = jnp.where(causal, att, 0.0)
            o = _dot(att.astype(BF16), vb)
            o = o + lax.dot_general(qh, st[:, ks].astype(BF16), (((1,), (1,)), ((), ())),
                                    preferred_element_type=F32)
            new_parts.append(lax.dot_general(vb, k_end[:, ks].astype(BF16), (((0,), (0,)), ((), ())),
                                             preferred_element_type=F32))
            o = _rms(o, gn) * jax.nn.silu(g_ref[rows, vs])
            o_ref[rows, vs] = o
        st_ref[...] = decay * st + jnp.concatenate(new_parts, axis=1)


def _gla(q, k, v, g, la, gn, bsz, seq):
    nt = seq // TS_GLA
    row = lambda b, i: (b * nt + i, 0)
    fixed = lambda b, i: (0, 0)
    return pl.pallas_call(
        _gla_kernel,
        grid=(bsz, nt),
        in_specs=[pl.BlockSpec((TS_GLA, GLA_KEY), row), pl.BlockSpec((TS_GLA, GLA_KEY), row),
                  pl.BlockSpec((TS_GLA, GLA_VAL), row), pl.BlockSpec((TS_GLA, GLA_VAL), row),
                  pl.BlockSpec((TS_GLA, GLA_KEY), row), pl.BlockSpec((1, GLA_DV), fixed)],
        out_specs=pl.BlockSpec((TS_GLA, GLA_VAL), row),
        out_shape=jax.ShapeDtypeStruct((bsz * seq, GLA_VAL), F32),
        scratch_shapes=[pltpu.VMEM((GLA_DV, GLA_KEY), F32)],
        compiler_params=pltpu.CompilerParams(dimension_semantics=("arbitrary", "arbitrary"),
                                             vmem_limit_bytes=VMEM_LIMIT),
        name="gla",
    )(q, k, v, g, la, gn.reshape(1, GLA_DV).astype(F32))


def _outproj_kernel(yrg_ref, ygla_ref, x_ref, wo1_ref, wo2_ref, gn_ref, wr_ref, br_ref,
                    h_o, hn_o, rt_o, info_o):
    tm = x_ref.shape[0]
    h = x_ref[...] + _dot(yrg_ref[...].astype(BF16), wo1_ref[...]) \
        + _dot(ygla_ref[...].astype(BF16), wo2_ref[...])
    h_o[...] = h
    hn = _rms(h, gn_ref[...])
    hn_o[...] = hn
    logits = jnp.dot(hn, wr_ref[...], precision=HIGHEST, preferred_element_type=F32) + br_ref[...]

    lane = lax.broadcasted_iota(jnp.int32, (tm, LANES), 1)
    neg = -jnp.inf
    glog = jnp.where(lane < N_GROUPS, logits, neg)
    gmax = jnp.max(glog, axis=-1, keepdims=True)
    gidx = jnp.min(jnp.where(glog == gmax, lane, LANES), axis=-1, keepdims=True)
    g_w = 1.0 / jnp.sum(jnp.exp(glog - gmax), axis=-1, keepdims=True)
    lo = N_GROUPS + gidx * EXPERTS_PER_GROUP
    in_group = jnp.logical_and(lane >= lo, lane < lo + EXPERTS_PER_GROUP)
    le = jnp.where(in_group, logits, neg)
    m1 = jnp.max(le, axis=-1, keepdims=True)
    i1 = jnp.min(jnp.where(le == m1, lane, LANES), axis=-1, keepdims=True)
    le2 = jnp.where(lane == i1, neg, le)
    m2 = jnp.max(le2, axis=-1, keepdims=True)
    i2 = jnp.min(jnp.where(le2 == m2, lane, LANES), axis=-1, keepdims=True)
    t2 = jnp.exp(m2 - m1)
    w1 = g_w / (1.0 + t2)
    w2 = g_w * t2 / (1.0 + t2)
    info = jnp.where(lane == 0, (i1 - N_GROUPS).astype(F32),
                     jnp.where(lane == 1, (i2 - N_GROUPS).astype(F32),
                               jnp.where(lane == 2, w1, jnp.where(lane == 3, w2, 0.0))))
    info_o[...] = info
    rt_o[...] = info.T[0:SUBLANES, :]


def _outproj(y_rg, y_gla, x2, w_out, norm_ffn, w_rg, b_rg, w_re, b_re):
    t = x2.shape[0]
    wo = w_out.astype(BF16)
    wr = jnp.pad(jnp.concatenate([w_rg, w_re], axis=1).astype(F32),
                 ((0, 0), (0, LANES - N_GROUPS - N_EXPERTS)))
    br = jnp.pad(jnp.concatenate([b_rg, b_re]).astype(F32), (0, LANES - N_GROUPS - N_EXPERTS))
    row = lambda i: (i, 0)
    fixed = lambda i: (0, 0)
    return pl.pallas_call(
        _outproj_kernel,
        grid=(t // TM_PROJ,),
        in_specs=[pl.BlockSpec((TM_PROJ, RG_WIDTH), row), pl.BlockSpec((TM_PROJ, GLA_VAL), row),
                  pl.BlockSpec((TM_PROJ, D_MODEL), row),
                  pl.BlockSpec((RG_WIDTH, D_MODEL), fixed), pl.BlockSpec((GLA_VAL, D_MODEL), fixed),
                  pl.BlockSpec((1, D_MODEL), fixed), pl.BlockSpec((D_MODEL, LANES), fixed),
                  pl.BlockSpec((1, LANES), fixed)],
        out_specs=[pl.BlockSpec((TM_PROJ, D_MODEL), row), pl.BlockSpec((TM_PROJ, D_MODEL), row),
                   pl.BlockSpec((SUBLANES, TM_PROJ), lambda i: (0, i)),
                   pl.BlockSpec((TM_PROJ, LANES), row)],
        out_shape=[jax.ShapeDtypeStruct((t, D_MODEL), F32), jax.ShapeDtypeStruct((t, D_MODEL), F32),
                   jax.ShapeDtypeStruct((SUBLANES, t), F32), jax.ShapeDtypeStruct((t, LANES), F32)],
        compiler_params=pltpu.CompilerParams(dimension_semantics=("arbitrary",),
                                             vmem_limit_bytes=VMEM_LIMIT),
        name="outproj",
    )(y_rg, y_gla, x2, wo[:RG_WIDTH], wo[RG_WIDTH:], norm_ffn.reshape(1, D_MODEL).astype(F32),
      wr, br.reshape(1, LANES))


def _plan_kernel(rt_ref, dest_o, meta_o, tri_ref, *, n_tok, nb_max):
    n_tiles = n_tok // TT_PLAN
    esub = lax.broadcasted_iota(jnp.int32, (N_EXPERTS, TT_PLAN), 0).astype(F32)

    def onehots(i):
        off = pl.multiple_of(i * TT_PLAN, TT_PLAN)
        e1 = rt_ref[0:1, pl.ds(off, TT_PLAN)]
        e2 = rt_ref[1:2, pl.ds(off, TT_PLAN)]
        m1 = jnp.where(esub == e1, 1.0, 0.0)
        m2 = jnp.where(esub == e2, 1.0, 0.0)
        return off, m1, m2

    def count_body(i, cnt):
        _, m1, m2 = onehots(i)
        return cnt + jnp.sum(m1 + m2, axis=1, keepdims=True)

    counts = lax.fori_loop(0, n_tiles, count_body, jnp.zeros((N_EXPERTS, 1), F32))
    nblk = jnp.floor((counts + (BM - 1)) * (1.0 / BM))
    ei = lax.broadcasted_iota(jnp.int32, (N_EXPERTS, N_EXPERTS), 0)
    ej = lax.broadcasted_iota(jnp.int32, (N_EXPERTS, N_EXPERTS), 1)
    nblk_row = jnp.sum(jnp.where(ei == ej, nblk, 0.0), axis=0, keepdims=True)
    bstart = jnp.sum(jnp.where(ej < ei, nblk_row, 0.0), axis=1, keepdims=True)
    bend = bstart + nblk
    n_used = jnp.sum(nblk, axis=0, keepdims=True)

    meta_w = meta_o.shape[1]
    blane = lax.broadcasted_iota(jnp.int32, (N_EXPERTS, meta_w), 1).astype(F32)
    owner = jnp.sum(jnp.where(bend <= blane, 1.0, 0.0), axis=0, keepdims=True)
    owner = jnp.minimum(owner, N_EXPERTS - 1.0)
    lane1 = lax.broadcasted_iota(jnp.int32, (1, meta_w), 1)
    meta_o[0:1, :] = jnp.where(lane1 == nb_max, n_used, owner).astype(jnp.int32)
    owned = jnp.logical_and(bend > blane, bstart <= blane)
    valid_end = jnp.sum(jnp.where(owned, bstart * float(BM) + counts, 0.0), axis=0, keepdims=True)
    n_valid = jnp.clip(valid_end - lane1.astype(F32) * float(BM), 0.0, float(BM))
    meta_o[1:2, :] = n_valid.astype(jnp.int32)

    ti = lax.broadcasted_iota(jnp.int32, (TT_PLAN, TT_PLAN), 0)
    tj = lax.broadcasted_iota(jnp.int32, (TT_PLAN, TT_PLAN), 1)
    tri_ref[...] = jnp.where(ti <= tj, 1.0, 0.0).astype(BF16)

    def dest_body(i, carry):
        off, m1, m2 = onehots(i)
        m = m1 + m2
        incl = _dot(m.astype(BF16), tri_ref[...])
        pos = carry + incl - m
        dest_o[0:1, pl.ds(off, TT_PLAN)] = jnp.sum(m1 * pos, axis=0, keepdims=True).astype(jnp.int32)
        dest_o[1:2, pl.ds(off, TT_PLAN)] = jnp.sum(m2 * pos, axis=0, keepdims=True).astype(jnp.int32)
        return carry + jnp.sum(m, axis=1, keepdims=True)

    lax.fori_loop(0, n_tiles, dest_body, bstart * float(BM))


def _plan(rt, n_tok, nb_max):
    meta_w = ((nb_max + 1 + LANES - 1) // LANES) * LANES
    return pl.pallas_call(
        functools.partial(_plan_kernel, n_tok=n_tok, nb_max=nb_max),
        out_shape=[jax.ShapeDtypeStruct((2, n_tok), jnp.int32),
                   jax.ShapeDtypeStruct((2, meta_w), jnp.int32)],
        scratch_shapes=[pltpu.VMEM((TT_PLAN, TT_PLAN), BF16)],
        compiler_params=pltpu.CompilerParams(vmem_limit_bytes=VMEM_LIMIT),
        name="plan",
    )(rt)


def _invert_kernel(dest_ref, init_hbm, slot_ref, sem, *, n_tok):
    fill = pltpu.make_async_copy(init_hbm, slot_ref, sem)
    fill.start()
    fill.wait()

    def body(t, _):
        slot_ref[dest_ref[t]] = t
        slot_ref[dest_ref[n_tok + t]] = n_tok + t
        return 0

    lax.fori_loop(0, n_tok, body, 0, unroll=8)


def _invert(dest_flat, n_tok, cap):
    init = jnp.zeros((cap,), jnp.int32)
    smem = pl.BlockSpec(memory_space=pltpu.SMEM)
    return pl.pallas_call(
        functools.partial(_invert_kernel, n_tok=n_tok),
        in_specs=[smem, pl.BlockSpec(memory_space=pl.ANY)],
        out_specs=smem,
        out_shape=jax.ShapeDtypeStruct((cap,), jnp.int32),
        scratch_shapes=[pltpu.SemaphoreType.DMA],
        name="invert",
    )(dest_flat, init)


def _experts_kernel(meta_ref, slot_ref, hn_hbm, wg_ref, wu_ref, wd_ref, y_hbm,
                    xbuf, ybuf, wbf, gsem, ssem, *, n_tok, nb_max, meta_w):
    b = pl.program_id(0)
    n_used = meta_ref[nb_max]
    slot = lax.rem(b, 2)

    def gather_copy(blk, buf, j):
        dst = slot_ref[blk * BM + j]
        tok = jnp.bitwise_and(dst, n_tok - 1)
        return pltpu.make_async_copy(hn_hbm.at[pl.ds(tok, 1), :], xbuf.at[buf, pl.ds(j, 1), :],
                                     gsem.at[buf])

    def scatter_copy(blk, buf, j):
        dst = slot_ref[blk * BM + j]
        return pltpu.make_async_copy(ybuf.at[buf, pl.ds(j, 1), :], y_hbm.at[pl.ds(dst, 1), :],
                                     ssem.at[buf])

    def for_valid_rows(blk, fn):
        n_valid = meta_ref[meta_w + blk]
        n_groups = lax.shift_right_logical(n_valid, 3)

        def group(gi, c):
            for r in range(SUBLANES):
                fn(gi * SUBLANES + r)
            return c

        lax.fori_loop(0, n_groups, group, 0)
        lax.fori_loop(n_groups * SUBLANES, n_valid, lambda j, c: (fn(j), c)[1], 0)

    def start_gather(blk, buf):
        for_valid_rows(blk, lambda j: gather_copy(blk, buf, j).start())

    def wait_gather(blk, buf):
        for_valid_rows(blk, lambda j: gather_copy(blk, buf, j).wait())

    def start_scatter(blk, buf):
        for_valid_rows(blk, lambda j: scatter_copy(blk, buf, j).start())

    def wait_scatter(blk, buf):
        for_valid_rows(blk, lambda j: scatter_copy(blk, buf, j).wait())

    @pl.when(b == 0)
    def _():
        xbuf[...] = jnp.zeros_like(xbuf)

    @pl.when(jnp.logical_and(b == 0, n_used > 0))
    def _():
        start_gather(0, 0)

    @pl.when(b + 1 < n_used)
    def _():
        start_gather(b + 1, 1 - slot)

    @pl.when(b < n_used)
    def _():
        e_now = meta_ref[b]
        e_prev = meta_ref[jnp.maximum(b - 1, 0)]

        @pl.when(jnp.logical_or(b == 0, e_now != e_prev))
        def _():
            wbf[0] = wg_ref[...].astype(BF16)
            wbf[1] = wu_ref[...].astype(BF16)
            wbf[2] = wd_ref[...].astype(BF16)

        wait_gather(b, slot)
        x = xbuf[slot].astype(BF16)
        gate = _dot(x, wbf[0])
        up = _dot(x, wbf[1])
        mid = (jax.nn.gelu(gate) * up).astype(BF16)

        @pl.when(b >= 2)
        def _():
            wait_scatter(b - 2, slot)

        ybuf[slot] = _dot(mid, wbf[2])
        start_scatter(b, slot)

        @pl.when(b == n_used - 1)
        def _():
            @pl.when(b >= 1)
            def _():
                wait_scatter(b - 1, 1 - slot)
            wait_scatter(b, slot)


def _experts(meta, slot_dst, hn, w_gate, w_up, w_down, n_tok, nb_max):
    meta_w = meta.shape[1]
    wmap = lambda b, meta_ref, slot_ref: (meta_ref[b], 0, 0)
    wspec = pl.BlockSpec((None, D_MODEL, D_MODEL), wmap)
    grid_spec = pltpu.PrefetchScalarGridSpec(
        num_scalar_prefetch=2,
        grid=(nb_max,),
        in_specs=[pl.BlockSpec(memory_space=pl.ANY), wspec, wspec, wspec],
        out_specs=pl.BlockSpec(memory_space=pl.ANY),
        scratch_shapes=[pltpu.VMEM((2, BM, D_MODEL), F32), pltpu.VMEM((2, BM, D_MODEL), F32),
                        pltpu.VMEM((3, D_MODEL, D_MODEL), BF16),
                        pltpu.SemaphoreType.DMA((2,)), pltpu.SemaphoreType.DMA((2,))],
    )
    return pl.pallas_call(
        functools.partial(_experts_kernel, n_tok=n_tok, nb_max=nb_max, meta_w=meta_w),
        grid_spec=grid_spec,
        out_shape=jax.ShapeDtypeStruct((2 * n_tok, D_MODEL), F32),
        compiler_params=pltpu.CompilerParams(dimension_semantics=("arbitrary",),
                                             vmem_limit_bytes=VMEM_LIMIT),
        name="experts",
    )(meta.reshape(-1), slot_dst, hn, w_gate, w_up, w_down)


def _final_kernel(h_ref, ya_ref, yb_ref, info_ref, gn_ref, o_ref):
    wts = info_ref[...]
    h = h_ref[...] + wts[:, 2:3] * ya_ref[...] + wts[:, 3:4] * yb_ref[...]
    o_ref[...] = _rms(h, gn_ref[...])


def _final(h, y2, info, norm_final, n_tok):
    nt = n_tok // TM_FINAL
    row = lambda i: (i, 0)
    return pl.pallas_call(
        _final_kernel,
        grid=(nt,),
        in_specs=[pl.BlockSpec((TM_FINAL, D_MODEL), row),
                  pl.BlockSpec((TM_FINAL, D_MODEL), row),
                  pl.BlockSpec((TM_FINAL, D_MODEL), lambda i: (i + nt, 0)),
                  pl.BlockSpec((TM_FINAL, LANES), row),
                  pl.BlockSpec((1, D_MODEL), lambda i: (0, 0))],
        out_specs=pl.BlockSpec((TM_FINAL, D_MODEL), row),
        out_shape=jax.ShapeDtypeStruct((n_tok, D_MODEL), F32),
        compiler_params=pltpu.CompilerParams(dimension_semantics=("arbitrary",),
                                             vmem_limit_bytes=VMEM_LIMIT),
        name="final",
    )(h, y2, y2, info, norm_final.reshape(1, D_MODEL).astype(F32))


def kernel(x, norm_mix, w_in, conv_w, conv_b, w_rg_a, b_rg_a, w_rg_x, b_rg_x, rg_lambda, rg_norm, w_alpha_up, b_alpha, gla_norm, w_out, norm_ffn, w_router_group, b_router_group, w_router_expert, b_router_expert, w_exp_gate, w_exp_up, w_exp_down, norm_final):
    bsz, seq, d = x.shape
    assert d == D_MODEL and norm_mix.shape[0] == 1, "single-layer model of width D_MODEL expected"
    n_tok = bsz * seq
    assert n_tok & (n_tok - 1) == 0, "token count must be a power of two (row id masking)"
    n_assign = 2 * n_tok
    nb_max = (n_assign + N_EXPERTS * (BM - 1)) // BM
    cap = nb_max * BM

    x2 = x.reshape(n_tok, d)
    xr, yr, q, k, v, g, la = _inproj(x2, norm_mix[0], w_in[0], w_alpha_up[0], b_alpha[0])
    y_rg = _rglru(xr, yr, conv_w[0], conv_b[0], w_rg_a[0], b_rg_a[0], w_rg_x[0], b_rg_x[0],
                  rg_lambda[0], rg_norm[0], bsz, seq)
    y_gla = _gla(q, k, v, g, la, gla_norm[0], bsz, seq)
    h, hn, rt, info = _outproj(y_rg, y_gla, x2, w_out[0], norm_ffn[0], w_router_group[0],
                         b_router_group[0], w_router_expert[0], b_router_expert[0])
    dest, meta = _plan(rt, n_tok, nb_max)
    slot_dst = _invert(dest.reshape(n_assign), n_tok, cap)
    y2 = _experts(meta, slot_dst, hn, w_exp_gate[0], w_exp_up[0], w_exp_down[0], n_tok, nb_max)
    out = _final(h, y2, info, norm_final, n_tok)
    return out.reshape(bsz, seq, d)
```

```python
import functools

import jax
import jax.numpy as jnp
from jax import lax
from jax.experimental import pallas as pl
from jax.experimental.pallas import tpu as pltpu

F32 = jnp.float32
BF16 = jnp.bfloat16
HIGHEST = lax.Precision.HIGHEST

D_MODEL = 1024
RG_WIDTH = 512
RG_BLOCKS = 8
RG_BLOCK = 64
CONV_WIDTH = 4
C_RG = 8.0
GLA_HEADS = 4
GLA_VAL = 512
GLA_KEY = 256
GLA_DK = 64
GLA_DV = 128
GATE_RANK = 16
GATE_NORM = 16.0
CHUNK = 64
N_GROUPS = 4
EXPERTS_PER_GROUP = 8
N_EXPERTS = 32
EPS = 1e-6

LANES = 128
SUBLANES = 8
VMEM_LIMIT = 56 * 1024 * 1024

TM_PROJ = 512
TS_RG = 256
TS_GLA = 256
TM_FINAL = 512
BM = 256
COL_CHUNK = 256
N_COL_CHUNKS = D_MODEL // COL_CHUNK
ROWS_PER_PHASE = BM // (2 * N_COL_CHUNKS)
TT_PLAN = 512


def _dot(a, b):
    return jnp.dot(a, b, preferred_element_type=F32)


def _softplus(z):
    return jnp.maximum(z, 0.0) + jnp.log1p(jnp.exp(-jnp.abs(z)))


def _rms(x, g):
    return x * lax.rsqrt(jnp.mean(x * x, axis=-1, keepdims=True) + EPS) * g


TILE_ROWS = D_MODEL // LANES


def _store_token_tiles(ref, val):
    n = val.shape[0]
    for c in range(TILE_ROWS):
        ref[pl.ds(c, n, stride=TILE_ROWS), :] = val[:, c * LANES:(c + 1) * LANES]


def _load_token_tiles(ref, n):
    return jnp.concatenate([ref[pl.ds(c, n, stride=TILE_ROWS), :] for c in range(TILE_ROWS)], axis=1)


def _inproj_kernel(x_ref, g_ref, wxr, wyr, wq, wk, wv, wg, wal, wup, bal,
                   xr_o, yr_o, q_o, k_o, v_o, g_o, la_o):
    hn = _rms(x_ref[...], g_ref[...]).astype(BF16)
    xr_o[...] = _dot(hn, wxr[...])
    yr_o[...] = _dot(hn, wyr[...])
    q_o[...] = _dot(hn, wq[...])
    k_o[...] = _dot(hn, wk[...])
    v_o[...] = _dot(hn, wv[...])
    g_o[...] = _dot(hn, wg[...])
    a_low = _dot(hn, wal[...])
    z = jnp.dot(a_low, wup[...], precision=HIGHEST, preferred_element_type=F32) + bal[...]
    log_sig = jnp.minimum(z, 0.0) - jnp.log1p(jnp.exp(-jnp.abs(z)))
    la_o[...] = log_sig * (1.0 / GATE_NORM)


def _inproj(x2, norm_mix, w_in, w_alpha_up, b_alpha):
    t = x2.shape[0]
    c = [0, 512, 1024, 1280, 1536, 2048, 2560, 2576]
    wb = w_in.astype(BF16)
    segs = [wb[:, c[i]:c[i + 1]] for i in range(6)]
    wal = jnp.pad(wb[:, c[6]:c[7]], ((0, 0), (0, LANES - GATE_RANK)))
    wup = jnp.pad(w_alpha_up.astype(F32), ((0, LANES - GATE_RANK), (0, 0)))
    widths = [512, 512, 256, 256, 512, 512, 256]
    row = lambda i: (i, 0)
    fixed = lambda i: (0, 0)
    in_specs = [pl.BlockSpec((TM_PROJ, D_MODEL), row), pl.BlockSpec((1, D_MODEL), fixed)]
    in_specs += [pl.BlockSpec((D_MODEL, w), fixed) for w in widths[:6]]
    in_specs += [pl.BlockSpec((D_MODEL, LANES), fixed), pl.BlockSpec((LANES, GLA_KEY), fixed),
                 pl.BlockSpec((1, GLA_KEY), fixed)]
    return pl.pallas_call(
        _inproj_kernel,
        grid=(t // TM_PROJ,),
        in_specs=in_specs,
        out_specs=[pl.BlockSpec((TM_PROJ, w), row) for w in widths],
        out_shape=[jax.ShapeDtypeStruct((t, w), F32) for w in widths],
        compiler_params=pltpu.CompilerParams(dimension_semantics=("arbitrary",),
                                             vmem_limit_bytes=VMEM_LIMIT),
        name="inproj",
    )(x2, norm_mix.reshape(1, D_MODEL), *segs, wal, wup, b_alpha.reshape(1, GLA_KEY))


def _rglru_kernel(xr_ref, yr_ref, cw_ref, cb_ref, wa_ref, ba_ref, wx_ref, bx_ref, lam_ref, gn_ref,
                  o_ref, ext_ref, hc_ref):
    ts = xr_ref.shape[0]

    @pl.when(pl.program_id(1) == 0)
    def _():
        ext_ref[0:SUBLANES, :] = jnp.zeros((SUBLANES, RG_WIDTH), F32)
        hc_ref[...] = jnp.zeros_like(hc_ref)

    x = xr_ref[...]
    ext_ref[SUBLANES:SUBLANES + ts, :] = x
    cw = cw_ref[...]
    xc = cb_ref[...] + cw[CONV_WIDTH - 1:CONV_WIDTH, :] * x
    for s in range(1, CONV_WIDTH):
        xc = xc + cw[CONV_WIDTH - 1 - s:CONV_WIDTH - s, :] * ext_ref[SUBLANES - s:SUBLANES - s + ts, :]
    ext_ref[0:SUBLANES, :] = x[ts - SUBLANES:ts, :]

    xb = xc.astype(BF16)
    r = jax.nn.sigmoid(_dot(xb, wa_ref[...]) + ba_ref[...])
    gate_i = jax.nn.sigmoid(_dot(xb, wx_ref[...]) + bx_ref[...])
    log_a = (-C_RG) * r * _softplus(-lam_ref[...])
    a = jnp.exp(log_a)
    u = jnp.sqrt(jnp.tanh(-log_a) * (1.0 + a * a)) * (gate_i * xc)

    row = lax.broadcasted_iota(jnp.int32, (ts, RG_WIDTH), 0)
    d = 1
    while d < ts:
        keep = row >= d
        a_sh = jnp.where(keep, pltpu.roll(a, d, axis=0), 1.0)
        u_sh = jnp.where(keep, pltpu.roll(u, d, axis=0), 0.0)
        u = a * u_sh + u
        a = a * a_sh
        d *= 2
    h = u + a * hc_ref[0:1, :]
    hc_ref[...] = jnp.broadcast_to(h[ts - 1:ts, :], hc_ref.shape)

    y = h * jax.nn.gelu(yr_ref[...])
    o_ref[...] = _rms(y, gn_ref[...])


def _block_diag(w):
    eye = jnp.eye(RG_BLOCKS, dtype=w.dtype)
    return jnp.einsum('hij,hg->higj', w, eye).reshape(RG_WIDTH, RG_WIDTH)


def _rglru(xr, yr, conv_w, conv_b, w_a, b_a, w_x, b_x, lam, gn, bsz, seq):
    nt = seq // TS_RG
    row = lambda b, i: (b * nt + i, 0)
    fixed = lambda b, i: (0, 0)
    vec = lambda v: v.reshape(1, RG_WIDTH).astype(F32)
    return pl.pallas_call(
        _rglru_kernel,
        grid=(bsz, nt),
        in_specs=[pl.BlockSpec((TS_RG, RG_WIDTH), row), pl.BlockSpec((TS_RG, RG_WIDTH), row),
                  pl.BlockSpec((CONV_WIDTH, RG_WIDTH), fixed), pl.BlockSpec((1, RG_WIDTH), fixed),
                  pl.BlockSpec((RG_WIDTH, RG_WIDTH), fixed), pl.BlockSpec((1, RG_WIDTH), fixed),
                  pl.BlockSpec((RG_WIDTH, RG_WIDTH), fixed), pl.BlockSpec((1, RG_WIDTH), fixed),
                  pl.BlockSpec((1, RG_WIDTH), fixed), pl.BlockSpec((1, RG_WIDTH), fixed)],
        out_specs=pl.BlockSpec((TS_RG, RG_WIDTH), row),
        out_shape=jax.ShapeDtypeStruct((bsz * seq, RG_WIDTH), F32),
        scratch_shapes=[pltpu.VMEM((TS_RG + SUBLANES, RG_WIDTH), F32),
                        pltpu.VMEM((SUBLANES, RG_WIDTH), F32)],
        compiler_params=pltpu.CompilerParams(dimension_semantics=("arbitrary", "arbitrary"),
                                             vmem_limit_bytes=VMEM_LIMIT),
        name="rglru",
    )(xr, yr, conv_w.astype(F32), vec(conv_b), _block_diag(w_a).astype(BF16), vec(b_a),
      _block_diag(w_x).astype(BF16), vec(b_x), vec(lam), vec(gn))


def _gla_kernel(q_ref, k_ref, v_ref, g_ref, la_ref, gn_ref, o_ref, st_ref):
    ts = q_ref.shape[0]

    @pl.when(pl.program_id(1) == 0)
    def _():
        st_ref[...] = jnp.zeros_like(st_ref)

    ri = lax.broadcasted_iota(jnp.int32, (CHUNK, CHUNK), 0)
    ci = lax.broadcasted_iota(jnp.int32, (CHUNK, CHUNK), 1)
    causal = ri >= ci
    tril = causal.astype(F32)
    scale = GLA_DK ** -0.5
    gn = gn_ref[...]

    for c in range(ts // CHUNK):
        rows = slice(c * CHUNK, (c + 1) * CHUNK)
        la = la_ref[rows, :]
        b = jnp.dot(tril, la, precision=HIGHEST, preferred_element_type=F32)
        b_last = b[CHUNK - 1:CHUNK, :]
        q_s = (q_ref[rows, :] * scale) * jnp.exp(b)
        kk = k_ref[rows, :]
        k_s = kk * jnp.exp(-b)
        k_end = kk * jnp.exp(b_last - b)
        decay = jnp.exp(b_last)
        st = st_ref[...]
        new_parts = []
        for h in range(GLA_HEADS):
            ks = slice(h * GLA_DK, (h + 1) * GLA_DK)
            vs = slice(h * GLA_DV, (h + 1) * GLA_DV)
            qh = q_s[:, ks].astype(BF16)
            v_h = v_ref[rows, vs]
            vb = v_h.astype(BF16)
            att = lax.dot_general(qh, k_s[:, ks].astype(BF16), (((1,), (1,)), ((), ())),
                                  preferred_element_type=F32)
            att = jnp.where(causal, att, 0.0)
            o = _dot(att.astype(BF16), vb)
            o = o + lax.dot_general(qh, st[:, ks].astype(BF16), (((1,), (1,)), ((), ())),
                                    preferred_element_type=F32)
            new_parts.append(lax.dot_general(vb, k_end[:, ks].astype(BF16), (((0,), (0,)), ((), ())),
                                             preferred_element_type=F32))
            o = _rms(o, gn) * jax.nn.silu(g_ref[rows, vs])
            o_ref[rows, vs] = o
        st_ref[...] = decay * st + jnp.concatenate(new_parts, axis=1)


def _gla(q, k, v, g, la, gn, bsz, seq):
    nt = seq // TS_GLA
    row = lambda b, i: (b * nt + i, 0)
    fixed = lambda b, i: (0, 0)
    return pl.pallas_call(
        _gla_kernel,
        grid=(bsz, nt),
        in_specs=[pl.BlockSpec((TS_GLA, GLA_KEY), row), pl.BlockSpec((TS_GLA, GLA_KEY), row),
                  pl.BlockSpec((TS_GLA, GLA_VAL), row), pl.BlockSpec((TS_GLA, GLA_VAL), row),
                  pl.BlockSpec((TS_GLA, GLA_KEY), row), pl.BlockSpec((1, GLA_DV), fixed)],
        out_specs=pl.BlockSpec((TS_GLA, GLA_VAL), row),
        out_shape=jax.ShapeDtypeStruct((bsz * seq, GLA_VAL), F32),
        scratch_shapes=[pltpu.VMEM((GLA_DV, GLA_KEY), F32)],
        compiler_params=pltpu.CompilerParams(dimension_semantics=("arbitrary", "arbitrary"),
                                             vmem_limit_bytes=VMEM_LIMIT),
        name="gla",
    )(q, k, v, g, la, gn.reshape(1, GLA_DV).astype(F32))


def _outproj_kernel(yrg_ref, ygla_ref, x_ref, wo1_ref, wo2_ref, gn_ref, wr_ref, br_ref,
                    h_o, hn_o, rt_o, info_o):
    tm = x_ref.shape[0]
    h = x_ref[...] + _dot(yrg_ref[...].astype(BF16), wo1_ref[...]) \
        + _dot(ygla_ref[...].astype(BF16), wo2_ref[...])
    h_o[...] = h
    hn = _rms(h, gn_ref[...])
    _store_token_tiles(hn_o, hn)
    logits = jnp.dot(hn, wr_ref[...], precision=HIGHEST, preferred_element_type=F32) + br_ref[...]

    lane = lax.broadcasted_iota(jnp.int32, (tm, LANES), 1)
    neg = -jnp.inf
    glog = jnp.where(lane < N_GROUPS, logits, neg)
    gmax = jnp.max(glog, axis=-1, keepdims=True)
    gidx = jnp.min(jnp.where(glog == gmax, lane, LANES), axis=-1, keepdims=True)
    g_w = 1.0 / jnp.sum(jnp.exp(glog - gmax), axis=-1, keepdims=True)
    lo = N_GROUPS + gidx * EXPERTS_PER_GROUP
    in_group = jnp.logical_and(lane >= lo, lane < lo + EXPERTS_PER_GROUP)
    le = jnp.where(in_group, logits, neg)
    m1 = jnp.max(le, axis=-1, keepdims=True)
    i1 = jnp.min(jnp.where(le == m1, lane, LANES), axis=-1, keepdims=True)
    le2 = jnp.where(lane == i1, neg, le)
    m2 = jnp.max(le2, axis=-1, keepdims=True)
    i2 = jnp.min(jnp.where(le2 == m2, lane, LANES), axis=-1, keepdims=True)
    t2 = jnp.exp(m2 - m1)
    w1 = g_w / (1.0 + t2)
    w2 = g_w * t2 / (1.0 + t2)
    info = jnp.where(lane == 0, (i1 - N_GROUPS).astype(F32),
                     jnp.where(lane == 1, (i2 - N_GROUPS).astype(F32),
                               jnp.where(lane == 2, w1, jnp.where(lane == 3, w2, 0.0))))
    info_o[...] = info
    rt_o[...] = info.T[0:SUBLANES, :]


def _outproj(y_rg, y_gla, x2, w_out, norm_ffn, w_rg, b_rg, w_re, b_re):
    t = x2.shape[0]
    wo = w_out.astype(BF16)
    wr = jnp.pad(jnp.concatenate([w_rg, w_re], axis=1).astype(F32),
                 ((0, 0), (0, LANES - N_GROUPS - N_EXPERTS)))
    br = jnp.pad(jnp.concatenate([b_rg, b_re]).astype(F32), (0, LANES - N_GROUPS - N_EXPERTS))
    row = lambda i: (i, 0)
    fixed = lambda i: (0, 0)
    return pl.pallas_call(
        _outproj_kernel,
        grid=(t // TM_PROJ,),
        in_specs=[pl.BlockSpec((TM_PROJ, RG_WIDTH), row), pl.BlockSpec((TM_PROJ, GLA_VAL), row),
                  pl.BlockSpec((TM_PROJ, D_MODEL), row),
                  pl.BlockSpec((RG_WIDTH, D_MODEL), fixed), pl.BlockSpec((GLA_VAL, D_MODEL), fixed),
                  pl.BlockSpec((1, D_MODEL), fixed), pl.BlockSpec((D_MODEL, LANES), fixed),
                  pl.BlockSpec((1, LANES), fixed)],
        out_specs=[pl.BlockSpec((TM_PROJ, D_MODEL), row),
                   pl.BlockSpec((TM_PROJ * TILE_ROWS, LANES), row),
                   pl.BlockSpec((SUBLANES, TM_PROJ), lambda i: (0, i)),
                   pl.BlockSpec((TM_PROJ, LANES), row)],
        out_shape=[jax.ShapeDtypeStruct((t, D_MODEL), F32),
                   jax.ShapeDtypeStruct((t * TILE_ROWS, LANES), F32),
                   jax.ShapeDtypeStruct((SUBLANES, t), F32), jax.ShapeDtypeStruct((t, LANES), F32)],
        compiler_params=pltpu.CompilerParams(dimension_semantics=("arbitrary",),
                                             vmem_limit_bytes=VMEM_LIMIT),
        name="outproj",
    )(y_rg, y_gla, x2, wo[:RG_WIDTH], wo[RG_WIDTH:], norm_ffn.reshape(1, D_MODEL).astype(F32),
      wr, br.reshape(1, LANES))


def _plan_kernel(rt_ref, dest_o, meta_o, tri_ref, *, n_tok, nb_max):
    n_tiles = n_tok // TT_PLAN
    esub = lax.broadcasted_iota(jnp.int32, (N_EXPERTS, TT_PLAN), 0).astype(F32)

    def onehots(i):
        off = pl.multiple_of(i * TT_PLAN, TT_PLAN)
        e1 = rt_ref[0:1, pl.ds(off, TT_PLAN)]
        e2 = rt_ref[1:2, pl.ds(off, TT_PLAN)]
        m1 = jnp.where(esub == e1, 1.0, 0.0)
        m2 = jnp.where(esub == e2, 1.0, 0.0)
        return off, m1, m2

    def count_body(i, cnt):
        _, m1, m2 = onehots(i)
        return cnt + jnp.sum(m1 + m2, axis=1, keepdims=True)

    counts = lax.fori_loop(0, n_tiles, count_body, jnp.zeros((N_EXPERTS, 1), F32))
    nblk = jnp.floor((counts + (BM - 1)) * (1.0 / BM))
    ei = lax.broadcasted_iota(jnp.int32, (N_EXPERTS, N_EXPERTS), 0)
    ej = lax.broadcasted_iota(jnp.int32, (N_EXPERTS, N_EXPERTS), 1)
    nblk_row = jnp.sum(jnp.where(ei == ej, nblk, 0.0), axis=0, keepdims=True)
    bstart = jnp.sum(jnp.where(ej < ei, nblk_row, 0.0), axis=1, keepdims=True)
    bend = bstart + nblk
    n_used = jnp.sum(nblk, axis=0, keepdims=True)

    meta_w = meta_o.shape[1]
    blane = lax.broadcasted_iota(jnp.int32, (N_EXPERTS, meta_w), 1).astype(F32)
    owner = jnp.sum(jnp.where(bend <= blane, 1.0, 0.0), axis=0, keepdims=True)
    owner = jnp.minimum(owner, N_EXPERTS - 1.0)
    lane1 = lax.broadcasted_iota(jnp.int32, (1, meta_w), 1)
    meta_o[0:1, :] = jnp.where(lane1 == nb_max, n_used, owner).astype(jnp.int32)
    owned = jnp.logical_and(bend > blane, bstart <= blane)
    valid_end = jnp.sum(jnp.where(owned, bstart * float(BM) + counts, 0.0), axis=0, keepdims=True)
    n_valid = jnp.clip(valid_end - lane1.astype(F32) * float(BM), 0.0, float(BM))
    meta_o[1:2, :] = n_valid.astype(jnp.int32)

    ti = lax.broadcasted_iota(jnp.int32, (TT_PLAN, TT_PLAN), 0)
    tj = lax.broadcasted_iota(jnp.int32, (TT_PLAN, TT_PLAN), 1)
    tri_ref[...] = jnp.where(ti <= tj, 1.0, 0.0).astype(BF16)

    def dest_body(i, carry):
        off, m1, m2 = onehots(i)
        m = m1 + m2
        incl = _dot(m.astype(BF16), tri_ref[...])
        pos = carry + incl - m
        dest_o[0:1, pl.ds(off, TT_PLAN)] = jnp.sum(m1 * pos, axis=0, keepdims=True).astype(jnp.int32)
        dest_o[1:2, pl.ds(off, TT_PLAN)] = jnp.sum(m2 * pos, axis=0, keepdims=True).astype(jnp.int32)
        return carry + jnp.sum(m, axis=1, keepdims=True)

    lax.fori_loop(0, n_tiles, dest_body, bstart * float(BM))


def _plan(rt, n_tok, nb_max):
    meta_w = ((nb_max + 1 + LANES - 1) // LANES) * LANES
    return pl.pallas_call(
        functools.partial(_plan_kernel, n_tok=n_tok, nb_max=nb_max),
        out_shape=[jax.ShapeDtypeStruct((2, n_tok), jnp.int32),
                   jax.ShapeDtypeStruct((2, meta_w), jnp.int32)],
        scratch_shapes=[pltpu.VMEM((TT_PLAN, TT_PLAN), BF16)],
        compiler_params=pltpu.CompilerParams(vmem_limit_bytes=VMEM_LIMIT),
        name="plan",
    )(rt)


def _invert_kernel(dest_ref, init_hbm, slot_ref, sem, *, n_tok):
    fill = pltpu.make_async_copy(init_hbm, slot_ref, sem)
    fill.start()
    fill.wait()

    def body(t, _):
        slot_ref[dest_ref[t]] = t
        slot_ref[dest_ref[n_tok + t]] = n_tok + t
        return 0

    lax.fori_loop(0, n_tok, body, 0, unroll=8)


def _invert(dest_flat, n_tok, cap):
    init = jnp.zeros((cap,), jnp.int32)
    smem = pl.BlockSpec(memory_space=pltpu.SMEM)
    return pl.pallas_call(
        functools.partial(_invert_kernel, n_tok=n_tok),
        in_specs=[smem, pl.BlockSpec(memory_space=pl.ANY)],
        out_specs=smem,
        out_shape=jax.ShapeDtypeStruct((cap,), jnp.int32),
        scratch_shapes=[pltpu.SemaphoreType.DMA],
        name="invert",
    )(dest_flat, init)


def _experts_kernel(meta_ref, slot_ref, hn_hbm, wg_ref, wu_ref, wd_ref, y_hbm,
                    xbuf, ybuf, wbf, gsem, ssem, *, n_tok, nb_max, meta_w):
    b = pl.program_id(0)
    n_used = meta_ref[nb_max]
    slot = lax.rem(b, 2)
    other = 1 - slot
    dump_tok = 2 * n_tok

    def tile_rows(ref, t):
        return ref.at[pl.ds(pl.multiple_of(t * TILE_ROWS, TILE_ROWS), TILE_ROWS), :]

    def start_gather_row(blk, buf, j):
        tok = jnp.bitwise_and(slot_ref[blk * BM + j], n_tok - 1)
        pltpu.make_async_copy(tile_rows(hn_hbm, tok), tile_rows(xbuf.at[buf], j), gsem.at[buf]).start()

    def start_scatter_row(blk, n_valid, buf, j):
        dst = jnp.where(j < n_valid, slot_ref[blk * BM + j], dump_tok + j)
        pltpu.make_async_copy(tile_rows(ybuf.at[buf], j), tile_rows(y_hbm, dst), ssem.at[buf]).start()

    def wait_block(bufs, sems, buf):
        pltpu.make_async_copy(bufs.at[buf], bufs.at[buf], sems.at[buf]).wait()

    @pl.when(b < n_used)
    def _():
        @pl.when(b == 0)
        def _():
            xbuf[...] = jnp.zeros_like(xbuf)
            ybuf[...] = jnp.zeros_like(ybuf)
            lax.fori_loop(0, BM, lambda j, c: (start_gather_row(0, 0, j), c)[1], 0, unroll=8)

        @pl.when(jnp.logical_or(b == 0, meta_ref[b] != meta_ref[jnp.maximum(b - 1, 0)]))
        def _():
            wbf[0] = wg_ref[...].astype(BF16)
            wbf[1] = wu_ref[...].astype(BF16)
            wbf[2] = wd_ref[...].astype(BF16)

        wait_block(xbuf, gsem, slot)

        @pl.when(b >= 1)
        def _():
            wait_block(ybuf, ssem, slot)

        next_blk = jnp.minimum(b + 1, nb_max - 1)
        prev_blk = jnp.maximum(b - 1, 0)
        prev_valid = jnp.where(b >= 1, meta_ref[meta_w + prev_blk], 0)

        def move_rows(phase):
            for j in range(phase * ROWS_PER_PHASE, (phase + 1) * ROWS_PER_PHASE):
                start_gather_row(next_blk, other, j)
                start_scatter_row(prev_blk, prev_valid, other, j)

        x = _load_token_tiles(xbuf.at[slot], BM).astype(BF16)
        mids = []
        for n in range(N_COL_CHUNKS):
            cols = slice(n * COL_CHUNK, (n + 1) * COL_CHUNK)
            gate = _dot(x, wbf[0, :, cols])
            up = _dot(x, wbf[1, :, cols])
            mids.append((jax.nn.gelu(gate) * up).astype(BF16))
            move_rows(n)
        mid = jnp.concatenate(mids, axis=1)
        for n in range(N_COL_CHUNKS):
            y_n = _dot(mid, wbf[2, :, n * COL_CHUNK:(n + 1) * COL_CHUNK])
            for c in range(COL_CHUNK // LANES):
                tile_c = n * (COL_CHUNK // LANES) + c
                ybuf[slot, pl.ds(tile_c, BM, stride=TILE_ROWS), :] = y_n[:, c * LANES:(c + 1) * LANES]
            move_rows(N_COL_CHUNKS + n)

        @pl.when(b == n_used - 1)
        def _():
            wait_block(xbuf, gsem, other)
            wait_block(ybuf, ssem, other)
            n_valid = meta_ref[meta_w + b]
            lax.fori_loop(0, BM, lambda j, c: (start_scatter_row(b, n_valid, slot, j), c)[1], 0, unroll=8)
            wait_block(ybuf, ssem, slot)


def _experts(meta, slot_dst, hn, w_gate, w_up, w_down, n_tok, nb_max):
    meta_w = meta.shape[1]
    wmap = lambda b, meta_ref, slot_ref: (meta_ref[b], 0, 0)
    wspec = pl.BlockSpec((None, D_MODEL, D_MODEL), wmap)
    grid_spec = pltpu.PrefetchScalarGridSpec(
        num_scalar_prefetch=2,
        grid=(nb_max,),
        in_specs=[pl.BlockSpec(memory_space=pl.ANY), wspec, wspec, wspec],
        out_specs=pl.BlockSpec(memory_space=pl.ANY),
        scratch_shapes=[pltpu.VMEM((2, BM * TILE_ROWS, LANES), F32),
                        pltpu.VMEM((2, BM * TILE_ROWS, LANES), F32),
                        pltpu.VMEM((3, D_MODEL, D_MODEL), BF16),
                        pltpu.SemaphoreType.DMA((2,)), pltpu.SemaphoreType.DMA((2,))],
    )
    return pl.pallas_call(
        functools.partial(_experts_kernel, n_tok=n_tok, nb_max=nb_max, meta_w=meta_w),
        grid_spec=grid_spec,
        out_shape=jax.ShapeDtypeStruct(((2 * n_tok + BM) * TILE_ROWS, LANES), F32),
        compiler_params=pltpu.CompilerParams(dimension_semantics=("arbitrary",),
                                             vmem_limit_bytes=VMEM_LIMIT),
        name="experts",
    )(meta.reshape(-1), slot_dst, hn, w_gate, w_up, w_down)


def _final_kernel(h_ref, ya_ref, yb_ref, info_ref, gn_ref, o_ref):
    tm = h_ref.shape[0]
    wts = info_ref[...]
    h = h_ref[...] + wts[:, 2:3] * _load_token_tiles(ya_ref, tm) \
        + wts[:, 3:4] * _load_token_tiles(yb_ref, tm)
    o_ref[...] = _rms(h, gn_ref[...])


def _final(h, y2, info, norm_final, n_tok):
    nt = n_tok // TM_FINAL
    row = lambda i: (i, 0)
    return pl.pallas_call(
        _final_kernel,
        grid=(nt,),
        in_specs=[pl.BlockSpec((TM_FINAL, D_MODEL), row),
                  pl.BlockSpec((TM_FINAL * TILE_ROWS, LANES), row),
                  pl.BlockSpec((TM_FINAL * TILE_ROWS, LANES), lambda i: (i + nt, 0)),
                  pl.BlockSpec((TM_FINAL, LANES), row),
                  pl.BlockSpec((1, D_MODEL), lambda i: (0, 0))],
        out_specs=pl.BlockSpec((TM_FINAL, D_MODEL), row),
        out_shape=jax.ShapeDtypeStruct((n_tok, D_MODEL), F32),
        compiler_params=pltpu.CompilerParams(dimension_semantics=("arbitrary",),
                                             vmem_limit_bytes=VMEM_LIMIT),
        name="final",
    )(h, y2, y2, info, norm_final.reshape(1, D_MODEL).astype(F32))


def kernel(x, norm_mix, w_in, conv_w, conv_b, w_rg_a, b_rg_a, w_rg_x, b_rg_x, rg_lambda, rg_norm, w_alpha_up, b_alpha, gla_norm, w_out, norm_ffn, w_router_group, b_router_group, w_router_expert, b_router_expert, w_exp_gate, w_exp_up, w_exp_down, norm_final):
    bsz, seq, d = x.shape
    assert d == D_MODEL and norm_mix.shape[0] == 1, "single-layer model of width D_MODEL expected"
    n_tok = bsz * seq
    assert n_tok & (n_tok - 1) == 0, "token count must be a power of two (row id masking)"
    n_assign = 2 * n_tok
    nb_max = (n_assign + N_EXPERTS * (BM - 1)) // BM
    cap = nb_max * BM

    x2 = x.reshape(n_tok, d)
    xr, yr, q, k, v, g, la = _inproj(x2, norm_mix[0], w_in[0], w_alpha_up[0], b_alpha[0])
    y_rg = _rglru(xr, yr, conv_w[0], conv_b[0], w_rg_a[0], b_rg_a[0], w_rg_x[0], b_rg_x[0],
                  rg_lambda[0], rg_norm[0], bsz, seq)
    y_gla = _gla(q, k, v, g, la, gla_norm[0], bsz, seq)
    h, hn, rt, info = _outproj(y_rg, y_gla, x2, w_out[0], norm_ffn[0], w_router_group[0],
                         b_router_group[0], w_router_expert[0], b_router_expert[0])
    dest, meta = _plan(rt, n_tok, nb_max)
    slot_dst = _invert(dest.reshape(n_assign), n_tok, cap)
    y2 = _experts(meta, slot_dst, hn, w_exp_gate[0], w_exp_up[0], w_exp_down[0], n_tok, nb_max)
    out = _final(h, y2, info, norm_final, n_tok)
    return out.reshape(bsz, seq, d)
```

```python
import functools

import jax
import jax.numpy as jnp
from jax import lax
from jax.experimental import pallas as pl
from jax.experimental.pallas import tpu as pltpu

F32 = jnp.float32
BF16 = jnp.bfloat16
HIGHEST = lax.Precision.HIGHEST

D_MODEL = 1024
RG_WIDTH = 512
RG_BLOCKS = 8
RG_BLOCK = 64
CONV_WIDTH = 4
C_RG = 8.0
GLA_HEADS = 4
GLA_VAL = 512
GLA_KEY = 256
GLA_DK = 64
GLA_DV = 128
GATE_RANK = 16
GATE_NORM = 16.0
CHUNK = 64
N_GROUPS = 4
EXPERTS_PER_GROUP = 8
N_EXPERTS = 32
EPS = 1e-6

LANES = 128
SUBLANES = 8
VMEM_LIMIT = 56 * 1024 * 1024

TM_PROJ = 512
TS_RG = 256
TS_GLA = 256
TM_FINAL = 512
BM = 256
COL_CHUNK = 256
N_COL_CHUNKS = D_MODEL // COL_CHUNK
ROWS_PER_PHASE = BM // (2 * N_COL_CHUNKS)
TT_PLAN = 512


def _dot(a, b):
    return jnp.dot(a, b, preferred_element_type=F32)


def _softplus(z):
    return jnp.maximum(z, 0.0) + jnp.log1p(jnp.exp(-jnp.abs(z)))


def _rms(x, g):
    return x * lax.rsqrt(jnp.mean(x * x, axis=-1, keepdims=True) + EPS) * g


TILE_ROWS = D_MODEL // LANES


def _store_token_tiles(ref, val):
    n = val.shape[0]
    for c in range(TILE_ROWS):
        ref[pl.ds(c, n, stride=TILE_ROWS), :] = val[:, c * LANES:(c + 1) * LANES]


def _load_token_tiles(ref, n):
    return jnp.concatenate([ref[pl.ds(c, n, stride=TILE_ROWS), :] for c in range(TILE_ROWS)], axis=1)


def _inproj_kernel(x_ref, g_ref, wxr, wyr, wq, wk, wv, wg, wal, wup, bal,
                   xr_o, yr_o, q_o, k_o, v_o, g_o, la_o):
    hn = _rms(x_ref[...], g_ref[...]).astype(BF16)
    xr_o[...] = _dot(hn, wxr[...])
    yr_o[...] = _dot(hn, wyr[...])
    q_o[...] = _dot(hn, wq[...])
    k_o[...] = _dot(hn, wk[...])
    v_o[...] = _dot(hn, wv[...])
    g_o[...] = _dot(hn, wg[...])
    a_low = _dot(hn, wal[...])
    z = jnp.dot(a_low, wup[...], precision=HIGHEST, preferred_element_type=F32) + bal[...]
    log_sig = jnp.minimum(z, 0.0) - jnp.log1p(jnp.exp(-jnp.abs(z)))
    la_o[...] = log_sig * (1.0 / GATE_NORM)


def _inproj(x2, norm_mix, w_in, w_alpha_up, b_alpha):
    t = x2.shape[0]
    c = [0, 512, 1024, 1280, 1536, 2048, 2560, 2576]
    wb = w_in.astype(BF16)
    segs = [wb[:, c[i]:c[i + 1]] for i in range(6)]
    wal = jnp.pad(wb[:, c[6]:c[7]], ((0, 0), (0, LANES - GATE_RANK)))
    wup = jnp.pad(w_alpha_up.astype(F32), ((0, LANES - GATE_RANK), (0, 0)))
    widths = [512, 512, 256, 256, 512, 512, 256]
    row = lambda i: (i, 0)
    fixed = lambda i: (0, 0)
    in_specs = [pl.BlockSpec((TM_PROJ, D_MODEL), row), pl.BlockSpec((1, D_MODEL), fixed)]
    in_specs += [pl.BlockSpec((D_MODEL, w), fixed) for w in widths[:6]]
    in_specs += [pl.BlockSpec((D_MODEL, LANES), fixed), pl.BlockSpec((LANES, GLA_KEY), fixed),
                 pl.BlockSpec((1, GLA_KEY), fixed)]
    return pl.pallas_call(
        _inproj_kernel,
        grid=(t // TM_PROJ,),
        in_specs=in_specs,
        out_specs=[pl.BlockSpec((TM_PROJ, w), row) for w in widths],
        out_shape=[jax.ShapeDtypeStruct((t, w), F32) for w in widths],
        compiler_params=pltpu.CompilerParams(dimension_semantics=("arbitrary",),
                                             vmem_limit_bytes=VMEM_LIMIT),
        name="inproj",
    )(x2, norm_mix.reshape(1, D_MODEL), *segs, wal, wup, b_alpha.reshape(1, GLA_KEY))


def _rglru_kernel(xr_ref, yr_ref, cw_ref, cb_ref, wa_ref, ba_ref, wx_ref, bx_ref, lam_ref, gn_ref,
                  o_ref, ext_ref, hc_ref):
    ts = xr_ref.shape[0]

    @pl.when(pl.program_id(1) == 0)
    def _():
        ext_ref[0:SUBLANES, :] = jnp.zeros((SUBLANES, RG_WIDTH), F32)
        hc_ref[...] = jnp.zeros_like(hc_ref)

    x = xr_ref[...]
    ext_ref[SUBLANES:SUBLANES + ts, :] = x
    cw = cw_ref[...]
    xc = cb_ref[...] + cw[CONV_WIDTH - 1:CONV_WIDTH, :] * x
    for s in range(1, CONV_WIDTH):
        xc = xc + cw[CONV_WIDTH - 1 - s:CONV_WIDTH - s, :] * ext_ref[SUBLANES - s:SUBLANES - s + ts, :]
    ext_ref[0:SUBLANES, :] = x[ts - SUBLANES:ts, :]

    xb = xc.astype(BF16)
    r = jax.nn.sigmoid(_dot(xb, wa_ref[...]) + ba_ref[...])
    gate_i = jax.nn.sigmoid(_dot(xb, wx_ref[...]) + bx_ref[...])
    log_a = (-C_RG) * r * _softplus(-lam_ref[...])
    a = jnp.exp(log_a)
    u = jnp.sqrt(jnp.tanh(-log_a) * (1.0 + a * a)) * (gate_i * xc)

    row = lax.broadcasted_iota(jnp.int32, (ts, RG_WIDTH), 0)
    d = 1
    while d < ts:
        keep = row >= d
        a_sh = jnp.where(keep, pltpu.roll(a, d, axis=0), 1.0)
        u_sh = jnp.where(keep, pltpu.roll(u, d, axis=0), 0.0)
        u = a * u_sh + u
        a = a * a_sh
        d *= 2
    h = u + a * hc_ref[0:1, :]
    hc_ref[...] = jnp.broadcast_to(h[ts - 1:ts, :], hc_ref.shape)

    y = h * jax.nn.gelu(yr_ref[...])
    o_ref[...] = _rms(y, gn_ref[...])


def _block_diag(w):
    eye = jnp.eye(RG_BLOCKS, dtype=w.dtype)
    return jnp.einsum('hij,hg->higj', w, eye).reshape(RG_WIDTH, RG_WIDTH)


def _rglru(xr, yr, conv_w, conv_b, w_a, b_a, w_x, b_x, lam, gn, bsz, seq):
    nt = seq // TS_RG
    row = lambda b, i: (b * nt + i, 0)
    fixed = lambda b, i: (0, 0)
    vec = lambda v: v.reshape(1, RG_WIDTH).astype(F32)
    return pl.pallas_call(
        _rglru_kernel,
        grid=(bsz, nt),
        in_specs=[pl.BlockSpec((TS_RG, RG_WIDTH), row), pl.BlockSpec((TS_RG, RG_WIDTH), row),
                  pl.BlockSpec((CONV_WIDTH, RG_WIDTH), fixed), pl.BlockSpec((1, RG_WIDTH), fixed),
                  pl.BlockSpec((RG_WIDTH, RG_WIDTH), fixed), pl.BlockSpec((1, RG_WIDTH), fixed),
                  pl.BlockSpec((RG_WIDTH, RG_WIDTH), fixed), pl.BlockSpec((1, RG_WIDTH), fixed),
                  pl.BlockSpec((1, RG_WIDTH), fixed), pl.BlockSpec((1, RG_WIDTH), fixed)],
        out_specs=pl.BlockSpec((TS_RG, RG_WIDTH), row),
        out_shape=jax.ShapeDtypeStruct((bsz * seq, RG_WIDTH), F32),
        scratch_shapes=[pltpu.VMEM((TS_RG + SUBLANES, RG_WIDTH), F32),
                        pltpu.VMEM((SUBLANES, RG_WIDTH), F32)],
        compiler_params=pltpu.CompilerParams(dimension_semantics=("arbitrary", "arbitrary"),
                                             vmem_limit_bytes=VMEM_LIMIT),
        name="rglru",
    )(xr, yr, conv_w.astype(F32), vec(conv_b), _block_diag(w_a).astype(BF16), vec(b_a),
      _block_diag(w_x).astype(BF16), vec(b_x), vec(lam), vec(gn))


def _gla_kernel(q_ref, k_ref, v_ref, g_ref, la_ref, gn_ref, o_ref, st_ref):
    ts = q_ref.shape[0]

    @pl.when(pl.program_id(1) == 0)
    def _():
        st_ref[...] = jnp.zeros_like(st_ref)

    ri = lax.broadcasted_iota(jnp.int32, (CHUNK, CHUNK), 0)
    ci = lax.broadcasted_iota(jnp.int32, (CHUNK, CHUNK), 1)
    causal = ri >= ci
    tril = causal.astype(F32)
    scale = GLA_DK ** -0.5
    gn = gn_ref[...]

    for c in range(ts // CHUNK):
        rows = slice(c * CHUNK, (c + 1) * CHUNK)
        la = la_ref[rows, :]
        b = jnp.dot(tril, la, precision=HIGHEST, preferred_element_type=F32)
        b_last = b[CHUNK - 1:CHUNK, :]
        q_s = (q_ref[rows, :] * scale) * jnp.exp(b)
        kk = k_ref[rows, :]
        k_s = kk * jnp.exp(-b)
        k_end = kk * jnp.exp(b_last - b)
        decay = jnp.exp(b_last)
        st = st_ref[...]
        new_parts = []
        for h in range(GLA_HEADS):
            ks = slice(h * GLA_DK, (h + 1) * GLA_DK)
            vs = slice(h * GLA_DV, (h + 1) * GLA_DV)
            qh = q_s[:, ks].astype(BF16)
            v_h = v_ref[rows, vs]
            vb = v_h.astype(BF16)
            att = lax.dot_general(qh, k_s[:, ks].astype(BF16), (((1,), (1,)), ((), ())),
                                  preferred_element_type=F32)
            att = jnp.where(causal, att, 0.0)
            o = _dot(att.astype(BF16), vb)
            o = o + lax.dot_general(qh, st[:, ks].astype(BF16), (((1,), (1,)), ((), ())),
                                    preferred_element_type=F32)
            new_parts.append(lax.dot_general(vb, k_end[:, ks].astype(BF16), (((0,), (0,)), ((), ())),
                                             preferred_element_type=F32))
            o = _rms(o, gn) * jax.nn.silu(g_ref[rows, vs])
            o_ref[rows, vs] = o
        st_ref[...] = decay * st + jnp.concatenate(new_parts, axis=1)


def _gla(q, k, v, g, la, gn, bsz, seq):
    nt = seq // TS_GLA
    row = lambda b, i: (b * nt + i, 0)
    fixed = lambda b, i: (0, 0)
    return pl.pallas_call(
        _gla_kernel,
        grid=(bsz, nt),
        in_specs=[pl.BlockSpec((TS_GLA, GLA_KEY), row), pl.BlockSpec((TS_GLA, GLA_KEY), row),
                  pl.BlockSpec((TS_GLA, GLA_VAL), row), pl.BlockSpec((TS_GLA, GLA_VAL), row),
                  pl.BlockSpec((TS_GLA, GLA_KEY), row), pl.BlockSpec((1, GLA_DV), fixed)],
        out_specs=pl.BlockSpec((TS_GLA, GLA_VAL), row),
        out_shape=jax.ShapeDtypeStruct((bsz * seq, GLA_VAL), F32),
        scratch_shapes=[pltpu.VMEM((GLA_DV, GLA_KEY), F32)],
        compiler_params=pltpu.CompilerParams(dimension_semantics=("arbitrary", "arbitrary"),
                                             vmem_limit_bytes=VMEM_LIMIT),
        name="gla",
    )(q, k, v, g, la, gn.reshape(1, GLA_DV).astype(F32))


def _outproj_kernel(yrg_ref, ygla_ref, x_ref, wo1_ref, wo2_ref, gn_ref, wr_ref, br_ref,
                    h_o, hn_o, rt_o, info_o):
    tm = x_ref.shape[0]
    h = x_ref[...] + _dot(yrg_ref[...].astype(BF16), wo1_ref[...]) \
        + _dot(ygla_ref[...].astype(BF16), wo2_ref[...])
    h_o[...] = h
    hn = _rms(h, gn_ref[...])
    _store_token_tiles(hn_o, hn)
    logits = jnp.dot(hn, wr_ref[...], precision=HIGHEST, preferred_element_type=F32) + br_ref[...]

    lane = lax.broadcasted_iota(jnp.int32, (tm, LANES), 1)
    neg = -jnp.inf
    glog = jnp.where(lane < N_GROUPS, logits, neg)
    gmax = jnp.max(glog, axis=-1, keepdims=True)
    gidx = jnp.min(jnp.where(glog == gmax, lane, LANES), axis=-1, keepdims=True)
    g_w = 1.0 / jnp.sum(jnp.exp(glog - gmax), axis=-1, keepdims=True)
    lo = N_GROUPS + gidx * EXPERTS_PER_GROUP
    in_group = jnp.logical_and(lane >= lo, lane < lo + EXPERTS_PER_GROUP)
    le = jnp.where(in_group, logits, neg)
    m1 = jnp.max(le, axis=-1, keepdims=True)
    i1 = jnp.min(jnp.where(le == m1, lane, LANES), axis=-1, keepdims=True)
    le2 = jnp.where(lane == i1, neg, le)
    m2 = jnp.max(le2, axis=-1, keepdims=True)
    i2 = jnp.min(jnp.where(le2 == m2, lane, LANES), axis=-1, keepdims=True)
    t2 = jnp.exp(m2 - m1)
    w1 = g_w / (1.0 + t2)
    w2 = g_w * t2 / (1.0 + t2)
    info = jnp.where(lane == 0, (i1 - N_GROUPS).astype(F32),
                     jnp.where(lane == 1, (i2 - N_GROUPS).astype(F32),
                               jnp.where(lane == 2, w1, jnp.where(lane == 3, w2, 0.0))))
    info_o[...] = info
    rt_o[...] = info.T[0:SUBLANES, :]


def _outproj(y_rg, y_gla, x2, w_out, norm_ffn, w_rg, b_rg, w_re, b_re):
    t = x2.shape[0]
    wo = w_out.astype(BF16)
    wr = jnp.pad(jnp.concatenate([w_rg, w_re], axis=1).astype(F32),
                 ((0, 0), (0, LANES - N_GROUPS - N_EXPERTS)))
    br = jnp.pad(jnp.concatenate([b_rg, b_re]).astype(F32), (0, LANES - N_GROUPS - N_EXPERTS))
    row = lambda i: (i, 0)
    fixed = lambda i: (0, 0)
    return pl.pallas_call(
        _outproj_kernel,
        grid=(t // TM_PROJ,),
        in_specs=[pl.BlockSpec((TM_PROJ, RG_WIDTH), row), pl.BlockSpec((TM_PROJ, GLA_VAL), row),
                  pl.BlockSpec((TM_PROJ, D_MODEL), row),
                  pl.BlockSpec((RG_WIDTH, D_MODEL), fixed), pl.BlockSpec((GLA_VAL, D_MODEL), fixed),
                  pl.BlockSpec((1, D_MODEL), fixed), pl.BlockSpec((D_MODEL, LANES), fixed),
                  pl.BlockSpec((1, LANES), fixed)],
        out_specs=[pl.BlockSpec((TM_PROJ, D_MODEL), row),
                   pl.BlockSpec((TM_PROJ * TILE_ROWS, LANES), row),
                   pl.BlockSpec((SUBLANES, TM_PROJ), lambda i: (0, i)),
                   pl.BlockSpec((TM_PROJ, LANES), row)],
        out_shape=[jax.ShapeDtypeStruct((t, D_MODEL), F32),
                   jax.ShapeDtypeStruct((t * TILE_ROWS, LANES), F32),
                   jax.ShapeDtypeStruct((SUBLANES, t), F32), jax.ShapeDtypeStruct((t, LANES), F32)],
        compiler_params=pltpu.CompilerParams(dimension_semantics=("arbitrary",),
                                             vmem_limit_bytes=VMEM_LIMIT),
        name="outproj",
    )(y_rg, y_gla, x2, wo[:RG_WIDTH], wo[RG_WIDTH:], norm_ffn.reshape(1, D_MODEL).astype(F32),
      wr, br.reshape(1, LANES))


def _plan_kernel(rt_ref, dest_o, meta_o, tri_ref, *, n_tok, nb_max):
    n_tiles = n_tok // TT_PLAN
    esub = lax.broadcasted_iota(jnp.int32, (N_EXPERTS, TT_PLAN), 0).astype(F32)

    def onehots(i):
        off = pl.multiple_of(i * TT_PLAN, TT_PLAN)
        e1 = rt_ref[0:1, pl.ds(off, TT_PLAN)]
        e2 = rt_ref[1:2, pl.ds(off, TT_PLAN)]
        m1 = jnp.where(esub == e1, 1.0, 0.0)
        m2 = jnp.where(esub == e2, 1.0, 0.0)
        return off, m1, m2

    def count_body(i, cnt):
        _, m1, m2 = onehots(i)
        return cnt + jnp.sum(m1 + m2, axis=1, keepdims=True)

    counts = lax.fori_loop(0, n_tiles, count_body, jnp.zeros((N_EXPERTS, 1), F32))
    nblk = jnp.floor((counts + (BM - 1)) * (1.0 / BM))
    ei = lax.broadcasted_iota(jnp.int32, (N_EXPERTS, N_EXPERTS), 0)
    ej = lax.broadcasted_iota(jnp.int32, (N_EXPERTS, N_EXPERTS), 1)
    nblk_row = jnp.sum(jnp.where(ei == ej, nblk, 0.0), axis=0, keepdims=True)
    bstart = jnp.sum(jnp.where(ej < ei, nblk_row, 0.0), axis=1, keepdims=True)
    bend = bstart + nblk
    n_used = jnp.sum(nblk, axis=0, keepdims=True)

    meta_w = meta_o.shape[1]
    blane = lax.broadcasted_iota(jnp.int32, (N_EXPERTS, meta_w), 1).astype(F32)
    owner = jnp.sum(jnp.where(bend <= blane, 1.0, 0.0), axis=0, keepdims=True)
    owner = jnp.minimum(owner, N_EXPERTS - 1.0)
    lane1 = lax.broadcasted_iota(jnp.int32, (1, meta_w), 1)
    meta_o[0:1, :] = jnp.where(lane1 == nb_max, n_used, owner).astype(jnp.int32)
    owned = jnp.logical_and(bend > blane, bstart <= blane)
    valid_end = jnp.sum(jnp.where(owned, bstart * float(BM) + counts, 0.0), axis=0, keepdims=True)
    n_valid = jnp.clip(valid_end - lane1.astype(F32) * float(BM), 0.0, float(BM))
    meta_o[1:2, :] = n_valid.astype(jnp.int32)

    ti = lax.broadcasted_iota(jnp.int32, (TT_PLAN, TT_PLAN), 0)
    tj = lax.broadcasted_iota(jnp.int32, (TT_PLAN, TT_PLAN), 1)
    tri_ref[...] = jnp.where(ti <= tj, 1.0, 0.0).astype(BF16)

    def dest_body(i, carry):
        off, m1, m2 = onehots(i)
        m = m1 + m2
        incl = _dot(m.astype(BF16), tri_ref[...])
        pos = carry + incl - m
        dest_o[0:1, pl.ds(off, TT_PLAN)] = jnp.sum(m1 * pos, axis=0, keepdims=True).astype(jnp.int32)
        dest_o[1:2, pl.ds(off, TT_PLAN)] = jnp.sum(m2 * pos, axis=0, keepdims=True).astype(jnp.int32)
        return carry + jnp.sum(m, axis=1, keepdims=True)

    lax.fori_loop(0, n_tiles, dest_body, bstart * float(BM))


def _plan(rt, n_tok, nb_max):
    meta_w = ((nb_max + 1 + LANES - 1) // LANES) * LANES
    return pl.pallas_call(
        functools.partial(_plan_kernel, n_tok=n_tok, nb_max=nb_max),
        out_shape=[jax.ShapeDtypeStruct((2, n_tok), jnp.int32),
                   jax.ShapeDtypeStruct((2, meta_w), jnp.int32)],
        scratch_shapes=[pltpu.VMEM((TT_PLAN, TT_PLAN), BF16)],
        compiler_params=pltpu.CompilerParams(vmem_limit_bytes=VMEM_LIMIT),
        name="plan",
    )(rt)


def _invert_kernel(dest_ref, init_hbm, slot_ref, sem, *, n_tok):
    fill = pltpu.make_async_copy(init_hbm, slot_ref, sem)
    fill.start()
    fill.wait()

    def body(t, _):
        slot_ref[dest_ref[t]] = t
        slot_ref[dest_ref[n_tok + t]] = n_tok + t
        return 0

    lax.fori_loop(0, n_tok, body, 0, unroll=8)


def _invert(dest_flat, n_tok, cap):
    init = jnp.zeros((cap,), jnp.int32)
    smem = pl.BlockSpec(memory_space=pltpu.SMEM)
    return pl.pallas_call(
        functools.partial(_invert_kernel, n_tok=n_tok),
        in_specs=[smem, pl.BlockSpec(memory_space=pl.ANY)],
        out_specs=smem,
        out_shape=jax.ShapeDtypeStruct((cap,), jnp.int32),
        scratch_shapes=[pltpu.SemaphoreType.DMA],
        name="invert",
    )(dest_flat, init)


def _experts_kernel(meta_ref, slot_ref, hn_hbm, wg_ref, wu_ref, wd_ref, y_hbm,
                    xbuf, ybuf, wbf, gsem, ssem, *, n_tok, nb_max, meta_w):
    b = pl.program_id(0)
    n_used = meta_ref[nb_max]
    slot = lax.rem(b, 2)
    other = 1 - slot
    dump_tok = 2 * n_tok

    def tile_rows(ref, t):
        return ref.at[pl.ds(pl.multiple_of(t * TILE_ROWS, TILE_ROWS), TILE_ROWS), :]

    def start_gather_row(blk, buf, j, priority=0):
        tok = jnp.bitwise_and(slot_ref[blk * BM + j], n_tok - 1)
        pltpu.make_async_copy(tile_rows(hn_hbm, tok), tile_rows(xbuf.at[buf], j),
                              gsem.at[buf]).start(priority=priority)

    def start_scatter_row(blk, n_valid, buf, j, priority=0):
        dst = jnp.where(j < n_valid, slot_ref[blk * BM + j], dump_tok + j)
        pltpu.make_async_copy(tile_rows(ybuf.at[buf], j), tile_rows(y_hbm, dst),
                              ssem.at[buf]).start(priority=priority)

    def wait_block(bufs, sems, buf):
        pltpu.make_async_copy(bufs.at[buf], bufs.at[buf], sems.at[buf]).wait()

    @pl.when(b < n_used)
    def _():
        @pl.when(b == 0)
        def _():
            xbuf[...] = jnp.zeros_like(xbuf)
            ybuf[...] = jnp.zeros_like(ybuf)
            lax.fori_loop(0, BM, lambda j, c: (start_gather_row(0, 0, j), c)[1], 0, unroll=8)

        @pl.when(jnp.logical_or(b == 0, meta_ref[b] != meta_ref[jnp.maximum(b - 1, 0)]))
        def _():
            wbf[0] = wg_ref[...].astype(BF16)
            wbf[1] = wu_ref[...].astype(BF16)
            wbf[2] = wd_ref[...].astype(BF16)

        wait_block(xbuf, gsem, slot)

        @pl.when(b >= 1)
        def _():
            wait_block(ybuf, ssem, slot)

        next_blk = jnp.minimum(b + 1, nb_max - 1)
        prev_blk = jnp.maximum(b - 1, 0)
        prev_valid = jnp.where(b >= 1, meta_ref[meta_w + prev_blk], 0)

        def move_rows(phase):
            for j in range(phase * ROWS_PER_PHASE, (phase + 1) * ROWS_PER_PHASE):
                start_gather_row(next_blk, other, j, priority=j % 2)
                start_scatter_row(prev_blk, prev_valid, other, j, priority=j % 2)

        x = _load_token_tiles(xbuf.at[slot], BM).astype(BF16)
        mids = []
        for n in range(N_COL_CHUNKS):
            cols = slice(n * COL_CHUNK, (n + 1) * COL_CHUNK)
            gate = _dot(x, wbf[0, :, cols])
            up = _dot(x, wbf[1, :, cols])
            mids.append((jax.nn.gelu(gate) * up).astype(BF16))
            move_rows(n)
        mid = jnp.concatenate(mids, axis=1)
        for n in range(N_COL_CHUNKS):
            y_n = _dot(mid, wbf[2, :, n * COL_CHUNK:(n + 1) * COL_CHUNK])
            for c in range(COL_CHUNK // LANES):
                tile_c = n * (COL_CHUNK // LANES) + c
                ybuf[slot, pl.ds(tile_c, BM, stride=TILE_ROWS), :] = y_n[:, c * LANES:(c + 1) * LANES]
            move_rows(N_COL_CHUNKS + n)

        @pl.when(b == n_used - 1)
        def _():
            wait_block(xbuf, gsem, other)
            wait_block(ybuf, ssem, other)
            n_valid = meta_ref[meta_w + b]
            lax.fori_loop(0, BM, lambda j, c: (start_scatter_row(b, n_valid, slot, j), c)[1], 0, unroll=8)
            wait_block(ybuf, ssem, slot)


def _experts(meta, slot_dst, hn, w_gate, w_up, w_down, n_tok, nb_max):
    meta_w = meta.shape[1]
    wmap = lambda b, meta_ref, slot_ref: (meta_ref[b], 0, 0)
    wspec = pl.BlockSpec((None, D_MODEL, D_MODEL), wmap)
    grid_spec = pltpu.PrefetchScalarGridSpec(
        num_scalar_prefetch=2,
        grid=(nb_max,),
        in_specs=[pl.BlockSpec(memory_space=pl.ANY), wspec, wspec, wspec],
        out_specs=pl.BlockSpec(memory_space=pl.ANY),
        scratch_shapes=[pltpu.VMEM((2, BM * TILE_ROWS, LANES), F32),
                        pltpu.VMEM((2, BM * TILE_ROWS, LANES), F32),
                        pltpu.VMEM((3, D_MODEL, D_MODEL), BF16),
                        pltpu.SemaphoreType.DMA((2,)), pltpu.SemaphoreType.DMA((2,))],
    )
    return pl.pallas_call(
        functools.partial(_experts_kernel, n_tok=n_tok, nb_max=nb_max, meta_w=meta_w),
        grid_spec=grid_spec,
        out_shape=jax.ShapeDtypeStruct(((2 * n_tok + BM) * TILE_ROWS, LANES), F32),
        compiler_params=pltpu.CompilerParams(dimension_semantics=("arbitrary",),
                                             vmem_limit_bytes=VMEM_LIMIT),
        name="experts",
    )(meta.reshape(-1), slot_dst, hn, w_gate, w_up, w_down)


def _final_kernel(h_ref, ya_ref, yb_ref, info_ref, gn_ref, o_ref):
    tm = h_ref.shape[0]
    wts = info_ref[...]
    h = h_ref[...] + wts[:, 2:3] * _load_token_tiles(ya_ref, tm) \
        + wts[:, 3:4] * _load_token_tiles(yb_ref, tm)
    o_ref[...] = _rms(h, gn_ref[...])


def _final(h, y2, info, norm_final, n_tok):
    nt = n_tok // TM_FINAL
    row = lambda i: (i, 0)
    return pl.pallas_call(
        _final_kernel,
        grid=(nt,),
        in_specs=[pl.BlockSpec((TM_FINAL, D_MODEL), row),
                  pl.BlockSpec((TM_FINAL * TILE_ROWS, LANES), row),
                  pl.BlockSpec((TM_FINAL * TILE_ROWS, LANES), lambda i: (i + nt, 0)),
                  pl.BlockSpec((TM_FINAL, LANES), row),
                  pl.BlockSpec((1, D_MODEL), lambda i: (0, 0))],
        out_specs=pl.BlockSpec((TM_FINAL, D_MODEL), row),
        out_shape=jax.ShapeDtypeStruct((n_tok, D_MODEL), F32),
        compiler_params=pltpu.CompilerParams(dimension_semantics=("arbitrary",),
                                             vmem_limit_bytes=VMEM_LIMIT),
        name="final",
    )(h, y2, y2, info, norm_final.reshape(1, D_MODEL).astype(F32))


def kernel(x, norm_mix, w_in, conv_w, conv_b, w_rg_a, b_rg_a, w_rg_x, b_rg_x, rg_lambda, rg_norm, w_alpha_up, b_alpha, gla_norm, w_out, norm_ffn, w_router_group, b_router_group, w_router_expert, b_router_expert, w_exp_gate, w_exp_up, w_exp_down, norm_final):
    bsz, seq, d = x.shape
    assert d == D_MODEL and norm_mix.shape[0] == 1, "single-layer model of width D_MODEL expected"
    n_tok = bsz * seq
    assert n_tok & (n_tok - 1) == 0, "token count must be a power of two (row id masking)"
    n_assign = 2 * n_tok
    nb_max = (n_assign + N_EXPERTS * (BM - 1)) // BM
    cap = nb_max * BM

    x2 = x.reshape(n_tok, d)
    xr, yr, q, k, v, g, la = _inproj(x2, norm_mix[0], w_in[0], w_alpha_up[0], b_alpha[0])
    y_rg = _rglru(xr, yr, conv_w[0], conv_b[0], w_rg_a[0], b_rg_a[0], w_rg_x[0], b_rg_x[0],
                  rg_lambda[0], rg_norm[0], bsz, seq)
    y_gla = _gla(q, k, v, g, la, gla_norm[0], bsz, seq)
    h, hn, rt, info = _outproj(y_rg, y_gla, x2, w_out[0], norm_ffn[0], w_router_group[0],
                         b_router_group[0], w_router_expert[0], b_router_expert[0])
    dest, meta = _plan(rt, n_tok, nb_max)
    slot_dst = _invert(dest.reshape(n_assign), n_tok, cap)
    y2 = _experts(meta, slot_dst, hn, w_exp_gate[0], w_exp_up[0], w_exp_down[0], n_tok, nb_max)
    out = _final(h, y2, info, norm_final, n_tok)
    return out.reshape(bsz, seq, d)
```

```python
import functools

import jax
import jax.numpy as jnp
from jax import lax
from jax.experimental import pallas as pl
from jax.experimental.pallas import tpu as pltpu

F32 = jnp.float32
BF16 = jnp.bfloat16
HIGHEST = lax.Precision.HIGHEST

D_MODEL = 1024
RG_WIDTH = 512
RG_BLOCKS = 8
RG_BLOCK = 64
CONV_WIDTH = 4
C_RG = 8.0
GLA_HEADS = 4
GLA_VAL = 512
GLA_KEY = 256
GLA_DK = 64
GLA_DV = 128
GATE_RANK = 16
GATE_NORM = 16.0
CHUNK = 64
N_GROUPS = 4
EXPERTS_PER_GROUP = 8
N_EXPERTS = 32
EPS = 1e-6

LANES = 128
SUBLANES = 8
VMEM_LIMIT = 56 * 1024 * 1024

TM_PROJ = 512
TS_RG = 256
TS_GLA = 256
BM = 256
TT = 512
TILE_SLOTS = 2 * TT
RUN_CHUNK_LOG2 = 6
RUN_FIELDS = 4


def _dot(a, b):
    return jnp.dot(a, b, preferred_element_type=F32)


def _softplus(z):
    return jnp.maximum(z, 0.0) + jnp.log1p(jnp.exp(-jnp.abs(z)))


def _rms(x, g):
    return x * lax.rsqrt(jnp.mean(x * x, axis=-1, keepdims=True) + EPS) * g


TILE_ROWS = D_MODEL // LANES


def _copy_run(src, src_row, dst, dst_row, n_rows, sem):
    def piece(off, rows):
        s = src.at[pl.ds(pl.multiple_of((src_row + off) * TILE_ROWS, TILE_ROWS), rows * TILE_ROWS), :]
        d = dst.at[pl.ds(pl.multiple_of((dst_row + off) * TILE_ROWS, TILE_ROWS), rows * TILE_ROWS), :]
        pltpu.make_async_copy(s, d, sem).start()

    chunk = 1 << RUN_CHUNK_LOG2
    n_chunks = lax.shift_right_logical(n_rows, RUN_CHUNK_LOG2)
    lax.fori_loop(0, n_chunks, lambda c, carry: (piece(c * chunk, chunk), carry)[1], 0)
    off = n_chunks * chunk
    for k in reversed(range(RUN_CHUNK_LOG2)):
        bit = jnp.bitwise_and(n_rows, 1 << k)

        @pl.when(bit != 0)
        def _():
            piece(off, 1 << k)

        off = off + bit


def _wait_rows(ref, n_rows, sem):
    view = ref.at[pl.ds(0, n_rows * TILE_ROWS), :]
    pltpu.make_async_copy(view, view, sem).wait()


def _store_token_tiles(ref, val):
    n = val.shape[0]
    for c in range(TILE_ROWS):
        ref[pl.ds(c, n, stride=TILE_ROWS), :] = val[:, c * LANES:(c + 1) * LANES]


def _load_token_tiles(ref, n):
    return jnp.concatenate([ref[pl.ds(c, n, stride=TILE_ROWS), :] for c in range(TILE_ROWS)], axis=1)


def _inproj_kernel(x_ref, g_ref, wxr, wyr, wq, wk, wv, wg, wal, wup, bal,
                   xr_o, yr_o, q_o, k_o, v_o, g_o, la_o):
    hn = _rms(x_ref[...], g_ref[...]).astype(BF16)
    xr_o[...] = _dot(hn, wxr[...])
    yr_o[...] = _dot(hn, wyr[...])
    q_o[...] = _dot(hn, wq[...])
    k_o[...] = _dot(hn, wk[...])
    v_o[...] = _dot(hn, wv[...])
    g_o[...] = _dot(hn, wg[...])
    a_low = _dot(hn, wal[...])
    z = jnp.dot(a_low, wup[...], precision=HIGHEST, preferred_element_type=F32) + bal[...]
    log_sig = jnp.minimum(z, 0.0) - jnp.log1p(jnp.exp(-jnp.abs(z)))
    la_o[...] = log_sig * (1.0 / GATE_NORM)


def _inproj(x2, norm_mix, w_in, w_alpha_up, b_alpha):
    t = x2.shape[0]
    c = [0, 512, 1024, 1280, 1536, 2048, 2560, 2576]
    wb = w_in.astype(BF16)
    segs = [wb[:, c[i]:c[i + 1]] for i in range(6)]
    wal = jnp.pad(wb[:, c[6]:c[7]], ((0, 0), (0, LANES - GATE_RANK)))
    wup = jnp.pad(w_alpha_up.astype(F32), ((0, LANES - GATE_RANK), (0, 0)))
    widths = [512, 512, 256, 256, 512, 512, 256]
    row = lambda i: (i, 0)
    fixed = lambda i: (0, 0)
    in_specs = [pl.BlockSpec((TM_PROJ, D_MODEL), row), pl.BlockSpec((1, D_MODEL), fixed)]
    in_specs += [pl.BlockSpec((D_MODEL, w), fixed) for w in widths[:6]]
    in_specs += [pl.BlockSpec((D_MODEL, LANES), fixed), pl.BlockSpec((LANES, GLA_KEY), fixed),
                 pl.BlockSpec((1, GLA_KEY), fixed)]
    return pl.pallas_call(
        _inproj_kernel,
        grid=(t // TM_PROJ,),
        in_specs=in_specs,
        out_specs=[pl.BlockSpec((TM_PROJ, w), row) for w in widths],
        out_shape=[jax.ShapeDtypeStruct((t, w), F32) for w in widths],
        compiler_params=pltpu.CompilerParams(dimension_semantics=("arbitrary",),
                                             vmem_limit_bytes=VMEM_LIMIT),
        name="inproj",
    )(x2, norm_mix.reshape(1, D_MODEL), *segs, wal, wup, b_alpha.reshape(1, GLA_KEY))


def _rglru_kernel(xr_ref, yr_ref, cw_ref, cb_ref, wa_ref, ba_ref, wx_ref, bx_ref, lam_ref, gn_ref,
                  o_ref, ext_ref, hc_ref):
    ts = xr_ref.shape[0]

    @pl.when(pl.program_id(1) == 0)
    def _():
        ext_ref[0:SUBLANES, :] = jnp.zeros((SUBLANES, RG_WIDTH), F32)
        hc_ref[...] = jnp.zeros_like(hc_ref)

    x = xr_ref[...]
    ext_ref[SUBLANES:SUBLANES + ts, :] = x
    cw = cw_ref[...]
    xc = cb_ref[...] + cw[CONV_WIDTH - 1:CONV_WIDTH, :] * x
    for s in range(1, CONV_WIDTH):
        xc = xc + cw[CONV_WIDTH - 1 - s:CONV_WIDTH - s, :] * ext_ref[SUBLANES - s:SUBLANES - s + ts, :]
    ext_ref[0:SUBLANES, :] = x[ts - SUBLANES:ts, :]

    xb = xc.astype(BF16)
    r = jax.nn.sigmoid(_dot(xb, wa_ref[...]) + ba_ref[...])
    gate_i = jax.nn.sigmoid(_dot(xb, wx_ref[...]) + bx_ref[...])
    log_a = (-C_RG) * r * _softplus(-lam_ref[...])
    a = jnp.exp(log_a)
    u = jnp.sqrt(jnp.tanh(-log_a) * (1.0 + a * a)) * (gate_i * xc)

    row = lax.broadcasted_iota(jnp.int32, (ts, RG_WIDTH), 0)
    d = 1
    while d < ts:
        keep = row >= d
        a_sh = jnp.where(keep, pltpu.roll(a, d, axis=0), 1.0)
        u_sh = jnp.where(keep, pltpu.roll(u, d, axis=0), 0.0)
        u = a * u_sh + u
        a = a * a_sh
        d *= 2
    h = u + a * hc_ref[0:1, :]
    hc_ref[...] = jnp.broadcast_to(h[ts - 1:ts, :], hc_ref.shape)

    y = h * jax.nn.gelu(yr_ref[...])
    o_ref[...] = _rms(y, gn_ref[...])


def _block_diag(w):
    eye = jnp.eye(RG_BLOCKS, dtype=w.dtype)
    return jnp.einsum('hij,hg->higj', w, eye).reshape(RG_WIDTH, RG_WIDTH)


def _rglru(xr, yr, conv_w, conv_b, w_a, b_a, w_x, b_x, lam, gn, bsz, seq):
    nt = seq // TS_RG
    row = lambda b, i: (b * nt + i, 0)
    fixed = lambda b, i: (0, 0)
    vec = lambda v: v.reshape(1, RG_WIDTH).astype(F32)
    return pl.pallas_call(
        _rglru_kernel,
        grid=(bsz, nt),
        in_specs=[pl.BlockSpec((TS_RG, RG_WIDTH), row), pl.BlockSpec((TS_RG, RG_WIDTH), row),
                  pl.BlockSpec((CONV_WIDTH, RG_WIDTH), fixed), pl.BlockSpec((1, RG_WIDTH), fixed),
                  pl.BlockSpec((RG_WIDTH, RG_WIDTH), fixed), pl.BlockSpec((1, RG_WIDTH), fixed),
                  pl.BlockSpec((RG_WIDTH, RG_WIDTH), fixed), pl.BlockSpec((1, RG_WIDTH), fixed),
                  pl.BlockSpec((1, RG_WIDTH), fixed), pl.BlockSpec((1, RG_WIDTH), fixed)],
        out_specs=pl.BlockSpec((TS_RG, RG_WIDTH), row),
        out_shape=jax.ShapeDtypeStruct((bsz * seq, RG_WIDTH), F32),
        scratch_shapes=[pltpu.VMEM((TS_RG + SUBLANES, RG_WIDTH), F32),
                        pltpu.VMEM((SUBLANES, RG_WIDTH), F32)],
        compiler_params=pltpu.CompilerParams(dimension_semantics=("arbitrary", "arbitrary"),
                                             vmem_limit_bytes=VMEM_LIMIT),
        name="rglru",
    )(xr, yr, conv_w.astype(F32), vec(conv_b), _block_diag(w_a).astype(BF16), vec(b_a),
      _block_diag(w_x).astype(BF16), vec(b_x), vec(lam), vec(gn))


def _gla_kernel(q_ref, k_ref, v_ref, g_ref, la_ref, gn_ref, o_ref, st_ref):
    ts = q_ref.shape[0]

    @pl.when(pl.program_id(1) == 0)
    def _():
        st_ref[...] = jnp.zeros_like(st_ref)

    ri = lax.broadcasted_iota(jnp.int32, (CHUNK, CHUNK), 0)
    ci = lax.broadcasted_iota(jnp.int32, (CHUNK, CHUNK), 1)
    causal = ri >= ci
    tril = causal.astype(F32)
    scale = GLA_DK ** -0.5
    gn = gn_ref[...]

    for c in range(ts // CHUNK):
        rows = slice(c * CHUNK, (c + 1) * CHUNK)
        la = la_ref[rows, :]
        b = jnp.dot(tril, la, precision=HIGHEST, preferred_element_type=F32)
        b_last = b[CHUNK - 1:CHUNK, :]
        q_s = (q_ref[rows, :] * scale) * jnp.exp(b)
        kk = k_ref[rows, :]
        k_s = kk * jnp.exp(-b)
        k_end = kk * jnp.exp(b_last - b)
        decay = jnp.exp(b_last)
        st = st_ref[...]
        new_parts = []
        for h in range(GLA_HEADS):
            ks = slice(h * GLA_DK, (h + 1) * GLA_DK)
            vs = slice(h * GLA_DV, (h + 1) * GLA_DV)
            qh = q_s[:, ks].astype(BF16)
            v_h = v_ref[rows, vs]
            vb = v_h.astype(BF16)
            att = lax.dot_general(qh, k_s[:, ks].astype(BF16), (((1,), (1,)), ((), ())),
                                  preferred_element_type=F32)
            att = jnp.where(causal, att, 0.0)
            o = _dot(att.astype(BF16), vb)
            o = o + lax.dot_general(qh, st[:, ks].astype(BF16), (((1,), (1,)), ((), ())),
                                    preferred_element_type=F32)
            new_parts.append(lax.dot_general(vb, k_end[:, ks].astype(BF16), (((0,), (0,)), ((), ())),
                                             preferred_element_type=F32))
            o = _rms(o, gn) * jax.nn.silu(g_ref[rows, vs])
            o_ref[rows, vs] = o
        st_ref[...] = decay * st + jnp.concatenate(new_parts, axis=1)


def _gla(q, k, v, g, la, gn, bsz, seq):
    nt = seq // TS_GLA
    row = lambda b, i: (b * nt + i, 0)
    fixed = lambda b, i: (0, 0)
    return pl.pallas_call(
        _gla_kernel,
        grid=(bsz, nt),
        in_specs=[pl.BlockSpec((TS_GLA, GLA_KEY), row), pl.BlockSpec((TS_GLA, GLA_KEY), row),
                  pl.BlockSpec((TS_GLA, GLA_VAL), row), pl.BlockSpec((TS_GLA, GLA_VAL), row),
                  pl.BlockSpec((TS_GLA, GLA_KEY), row), pl.BlockSpec((1, GLA_DV), fixed)],
        out_specs=pl.BlockSpec((TS_GLA, GLA_VAL), row),
        out_shape=jax.ShapeDtypeStruct((bsz * seq, GLA_VAL), F32),
        scratch_shapes=[pltpu.VMEM((GLA_DV, GLA_KEY), F32)],
        compiler_params=pltpu.CompilerParams(dimension_semantics=("arbitrary", "arbitrary"),
                                             vmem_limit_bytes=VMEM_LIMIT),
        name="gla",
    )(q, k, v, g, la, gn.reshape(1, GLA_DV).astype(F32))


def _outproj_kernel(yrg_ref, ygla_ref, x_ref, wo1_ref, wo2_ref, gn_ref, wr_ref, br_ref,
                    h_o, hn_o, rt_o):
    tm = x_ref.shape[0]
    h = x_ref[...] + _dot(yrg_ref[...].astype(BF16), wo1_ref[...]) \
        + _dot(ygla_ref[...].astype(BF16), wo2_ref[...])
    h_o[...] = h
    hn = _rms(h, gn_ref[...])
    hn_o[...] = hn.astype(BF16)
    logits = jnp.dot(hn, wr_ref[...], precision=HIGHEST, preferred_element_type=F32) + br_ref[...]

    lane = lax.broadcasted_iota(jnp.int32, (tm, LANES), 1)
    neg = -jnp.inf
    glog = jnp.where(lane < N_GROUPS, logits, neg)
    gmax = jnp.max(glog, axis=-1, keepdims=True)
    gidx = jnp.min(jnp.where(glog == gmax, lane, LANES), axis=-1, keepdims=True)
    g_w = 1.0 / jnp.sum(jnp.exp(glog - gmax), axis=-1, keepdims=True)
    lo = N_GROUPS + gidx * EXPERTS_PER_GROUP
    in_group = jnp.logical_and(lane >= lo, lane < lo + EXPERTS_PER_GROUP)
    le = jnp.where(in_group, logits, neg)
    m1 = jnp.max(le, axis=-1, keepdims=True)
    i1 = jnp.min(jnp.where(le == m1, lane, LANES), axis=-1, keepdims=True)
    le2 = jnp.where(lane == i1, neg, le)
    m2 = jnp.max(le2, axis=-1, keepdims=True)
    i2 = jnp.min(jnp.where(le2 == m2, lane, LANES), axis=-1, keepdims=True)
    t2 = jnp.exp(m2 - m1)
    w1 = g_w / (1.0 + t2)
    w2 = g_w * t2 / (1.0 + t2)
    info = jnp.where(lane == 0, (i1 - N_GROUPS).astype(F32),
                     jnp.where(lane == 1, (i2 - N_GROUPS).astype(F32),
                               jnp.where(lane == 2, w1, jnp.where(lane == 3, w2, 0.0))))
    rt_o[...] = info.T[0:SUBLANES, :]


def _outproj(y_rg, y_gla, x2, w_out, norm_ffn, w_rg, b_rg, w_re, b_re):
    t = x2.shape[0]
    wo = w_out.astype(BF16)
    wr = jnp.pad(jnp.concatenate([w_rg, w_re], axis=1).astype(F32),
                 ((0, 0), (0, LANES - N_GROUPS - N_EXPERTS)))
    br = jnp.pad(jnp.concatenate([b_rg, b_re]).astype(F32), (0, LANES - N_GROUPS - N_EXPERTS))
    row = lambda i: (i, 0)
    fixed = lambda i: (0, 0)
    return pl.pallas_call(
        _outproj_kernel,
        grid=(t // TM_PROJ,),
        in_specs=[pl.BlockSpec((TM_PROJ, RG_WIDTH), row), pl.BlockSpec((TM_PROJ, GLA_VAL), row),
                  pl.BlockSpec((TM_PROJ, D_MODEL), row),
                  pl.BlockSpec((RG_WIDTH, D_MODEL), fixed), pl.BlockSpec((GLA_VAL, D_MODEL), fixed),
                  pl.BlockSpec((1, D_MODEL), fixed), pl.BlockSpec((D_MODEL, LANES), fixed),
                  pl.BlockSpec((1, LANES), fixed)],
        out_specs=[pl.BlockSpec((TM_PROJ, D_MODEL), row), pl.BlockSpec((TM_PROJ, D_MODEL), row),
                   pl.BlockSpec((SUBLANES, TM_PROJ), lambda i: (0, i))],
        out_shape=[jax.ShapeDtypeStruct((t, D_MODEL), F32), jax.ShapeDtypeStruct((t, D_MODEL), BF16),
                   jax.ShapeDtypeStruct((SUBLANES, t), F32)],
        compiler_params=pltpu.CompilerParams(dimension_semantics=("arbitrary",),
                                             vmem_limit_bytes=VMEM_LIMIT),
        name="outproj",
    )(y_rg, y_gla, x2, wo[:RG_WIDTH], wo[RG_WIDTH:], norm_ffn.reshape(1, D_MODEL).astype(F32),
      wr, br.reshape(1, LANES))


def _plan_kernel(rt_ref, lpos_o, meta_o, runs_o, tri_ref, *, n_tok, nb_max):
    n_tiles = n_tok // TT
    esub = lax.broadcasted_iota(jnp.int32, (N_EXPERTS, TT), 0).astype(F32)

    def onehots(i):
        off = pl.multiple_of(i * TT, TT)
        e1 = rt_ref[0:1, pl.ds(off, TT)]
        e2 = rt_ref[1:2, pl.ds(off, TT)]
        m1 = jnp.where(esub == e1, 1.0, 0.0)
        m2 = jnp.where(esub == e2, 1.0, 0.0)
        return off, m1, m2

    def count_body(i, cnt):
        _, m1, m2 = onehots(i)
        return cnt + jnp.sum(m1 + m2, axis=1, keepdims=True)

    counts = lax.fori_loop(0, n_tiles, count_body, jnp.zeros((N_EXPERTS, 1), F32))
    nblk = jnp.floor((counts + (BM - 1)) * (1.0 / BM))
    ei = lax.broadcasted_iota(jnp.int32, (N_EXPERTS, N_EXPERTS), 0)
    ej = lax.broadcasted_iota(jnp.int32, (N_EXPERTS, N_EXPERTS), 1)
    nblk_row = jnp.sum(jnp.where(ei == ej, nblk, 0.0), axis=0, keepdims=True)
    bstart = jnp.sum(jnp.where(ej < ei, nblk_row, 0.0), axis=1, keepdims=True)
    bend = bstart + nblk
    n_used = jnp.sum(nblk, axis=0, keepdims=True)

    meta_w = meta_o.shape[1]
    blane = lax.broadcasted_iota(jnp.int32, (N_EXPERTS, meta_w), 1).astype(F32)
    owner = jnp.sum(jnp.where(bend <= blane, 1.0, 0.0), axis=0, keepdims=True)
    owner = jnp.minimum(owner, N_EXPERTS - 1.0)
    lane1 = lax.broadcasted_iota(jnp.int32, (1, meta_w), 1)
    meta_o[...] = jnp.where(lane1 == nb_max, n_used, owner).astype(jnp.int32)

    ti = lax.broadcasted_iota(jnp.int32, (TT, TT), 0)
    tj = lax.broadcasted_iota(jnp.int32, (TT, TT), 1)
    tri_ref[...] = jnp.where(ti <= tj, 1.0, 0.0).astype(BF16)
    tlane = lax.broadcasted_iota(jnp.int32, (N_EXPERTS, LANES), 1)

    def tile_body(i, carry):
        first_slot, t_cnt, t_slot, t_rank = carry
        off, m1, m2 = onehots(i)
        m = m1 + m2
        incl = _dot(m.astype(BF16), tri_ref[...])
        cnt = incl[:, TT - 1:TT]
        cnt_row = jnp.sum(jnp.where(ei == ej, cnt, 0.0), axis=0, keepdims=True)
        first_rank = jnp.sum(jnp.where(ej < ei, cnt_row, 0.0), axis=1, keepdims=True)
        rank = first_rank + incl - m
        lpos_o[0:1, pl.ds(off, TT)] = jnp.sum(m1 * rank, axis=0, keepdims=True).astype(jnp.int32)
        lpos_o[1:2, pl.ds(off, TT)] = jnp.sum(m2 * rank, axis=0, keepdims=True).astype(jnp.int32)
        here = tlane == i
        return (first_slot + cnt, jnp.where(here, cnt, t_cnt), jnp.where(here, first_slot, t_slot),
                jnp.where(here, first_rank, t_rank))

    zeros = jnp.zeros((N_EXPERTS, LANES), F32)
    _, t_cnt, t_slot, t_rank = lax.fori_loop(0, n_tiles, tile_body,
                                             (bstart * float(BM), zeros, zeros, zeros))
    runs_o[0] = t_cnt.astype(jnp.int32)
    runs_o[1] = t_slot.astype(jnp.int32)
    runs_o[2] = t_rank.astype(jnp.int32)
    pad = jnp.where(tlane == 0, bstart * float(BM) + counts,
                    jnp.where(tlane == 1, nblk * float(BM) - counts, 0.0))
    runs_o[3] = pad.astype(jnp.int32)


def _plan(rt, n_tok, nb_max):
    assert n_tok // TT <= LANES, "run tables hold one token tile per lane"
    meta_w = ((nb_max + 1 + LANES - 1) // LANES) * LANES
    return pl.pallas_call(
        functools.partial(_plan_kernel, n_tok=n_tok, nb_max=nb_max),
        out_shape=[jax.ShapeDtypeStruct((2, n_tok), jnp.int32),
                   jax.ShapeDtypeStruct((1, meta_w), jnp.int32),
                   jax.ShapeDtypeStruct((RUN_FIELDS, N_EXPERTS, LANES), jnp.int32)],
        scratch_shapes=[pltpu.VMEM((TT, TT), BF16)],
        compiler_params=pltpu.CompilerParams(vmem_limit_bytes=VMEM_LIMIT),
        name="plan",
    )(rt)


def _run_entry(runs_ref, field, expert, tile):
    return runs_ref[(field * N_EXPERTS + expert) * LANES + tile]


def _dispatch_kernel(runs_ref, meta_ref, hn_ref, lpos_ref, xs_hbm, xsbuf, zbuf, sem, zsem,
                     *, n_tiles, nb_max, n_pad):
    i = pl.program_id(0)
    slot = lax.rem(i, 2)

    @pl.when(i == 0)
    def _():
        zbuf[...] = jnp.zeros_like(zbuf)
        for e in range(N_EXPERTS):
            _copy_run(zbuf, 0, xs_hbm, _run_entry(runs_ref, 3, e, 0), _run_entry(runs_ref, 3, e, 1),
                      zsem)

        def unused_block(blk, carry):
            dst = xs_hbm.at[pl.ds(pl.multiple_of(blk * (BM * TILE_ROWS), BM * TILE_ROWS),
                                  BM * TILE_ROWS), :]
            pltpu.make_async_copy(zbuf, dst, zsem).start()
            return carry

        lax.fori_loop(meta_ref[nb_max], nb_max, unused_block, 0)

    @pl.when(i >= 2)
    def _():
        _wait_rows(xsbuf.at[slot], TILE_SLOTS, sem.at[slot])

    rank = lax.broadcasted_iota(jnp.int32, (TILE_SLOTS, TT), 0)
    onehot = jnp.where(rank == lpos_ref[0:1, :], 1.0, jnp.where(rank == lpos_ref[1:2, :], 1.0, 0.0))
    _store_token_tiles(xsbuf.at[slot], _dot(onehot.astype(BF16), hn_ref[...]))
    for e in range(N_EXPERTS):
        _copy_run(xsbuf.at[slot], _run_entry(runs_ref, 2, e, i), xs_hbm, _run_entry(runs_ref, 1, e, i),
                  _run_entry(runs_ref, 0, e, i), sem.at[slot])

    @pl.when(i == n_tiles - 1)
    def _():
        _wait_rows(xsbuf.at[slot], TILE_SLOTS, sem.at[slot])
        if n_tiles > 1:
            _wait_rows(xsbuf.at[1 - slot], TILE_SLOTS, sem.at[1 - slot])
        _wait_rows(xs_hbm, n_pad, zsem)


def _dispatch(runs, meta, hn, lpos, n_tok, nb_max):
    n_tiles = n_tok // TT
    cap = nb_max * BM
    grid_spec = pltpu.PrefetchScalarGridSpec(
        num_scalar_prefetch=2,
        grid=(n_tiles,),
        in_specs=[pl.BlockSpec((TT, D_MODEL), lambda i, runs_ref, meta_ref: (i, 0)),
                  pl.BlockSpec((2, TT), lambda i, runs_ref, meta_ref: (0, i))],
        out_specs=pl.BlockSpec(memory_space=pl.ANY),
        scratch_shapes=[pltpu.VMEM((2, TILE_SLOTS * TILE_ROWS, LANES), F32),
                        pltpu.VMEM((BM * TILE_ROWS, LANES), F32),
                        pltpu.SemaphoreType.DMA((2,)), pltpu.SemaphoreType.DMA],
    )
    return pl.pallas_call(
        functools.partial(_dispatch_kernel, n_tiles=n_tiles, nb_max=nb_max, n_pad=cap - 2 * n_tok),
        grid_spec=grid_spec,
        out_shape=jax.ShapeDtypeStruct((cap * TILE_ROWS, LANES), F32),
        compiler_params=pltpu.CompilerParams(dimension_semantics=("arbitrary",),
                                             vmem_limit_bytes=VMEM_LIMIT),
        name="dispatch",
    )(runs.reshape(-1), meta.reshape(-1), hn, lpos)


def _experts_kernel(meta_ref, xs_ref, wg_ref, wu_ref, wd_ref, y_ref, wbf, *, nb_max):
    b = pl.program_id(0)
    n_used = meta_ref[nb_max]

    @pl.when(b < n_used)
    def _():
        @pl.when(jnp.logical_or(b == 0, meta_ref[b] != meta_ref[jnp.maximum(b - 1, 0)]))
        def _():
            wbf[0] = wg_ref[...].astype(BF16)
            wbf[1] = wu_ref[...].astype(BF16)
            wbf[2] = wd_ref[...].astype(BF16)

        x = _load_token_tiles(xs_ref, BM).astype(BF16)
        mid = (jax.nn.gelu(_dot(x, wbf[0])) * _dot(x, wbf[1])).astype(BF16)
        _store_token_tiles(y_ref, _dot(mid, wbf[2]))

    @pl.when(b >= n_used)
    def _():
        y_ref[...] = jnp.zeros_like(y_ref)


def _experts(meta, xs, w_gate, w_up, w_down, nb_max):
    wspec = pl.BlockSpec((None, D_MODEL, D_MODEL), lambda b, meta_ref: (meta_ref[b], 0, 0))
    rows = pl.BlockSpec((BM * TILE_ROWS, LANES), lambda b, meta_ref: (b, 0))
    grid_spec = pltpu.PrefetchScalarGridSpec(
        num_scalar_prefetch=1,
        grid=(nb_max,),
        in_specs=[rows, wspec, wspec, wspec],
        out_specs=rows,
        scratch_shapes=[pltpu.VMEM((3, D_MODEL, D_MODEL), BF16)],
    )
    return pl.pallas_call(
        functools.partial(_experts_kernel, nb_max=nb_max),
        grid_spec=grid_spec,
        out_shape=jax.ShapeDtypeStruct(xs.shape, F32),
        compiler_params=pltpu.CompilerParams(dimension_semantics=("arbitrary",),
                                             vmem_limit_bytes=VMEM_LIMIT),
        name="experts",
    )(meta.reshape(-1), xs, w_gate, w_up, w_down)


def _combine_kernel(runs_ref, h_ref, lpos_ref, rt_ref, gn_ref, y_hbm, o_ref, ysbuf, sem, *, n_tiles):
    i = pl.program_id(0)
    slot = lax.rem(i, 2)

    def fetch(tile, buf):
        for e in range(N_EXPERTS):
            _copy_run(y_hbm, _run_entry(runs_ref, 1, e, tile), ysbuf.at[buf],
                      _run_entry(runs_ref, 2, e, tile), _run_entry(runs_ref, 0, e, tile), sem.at[buf])

    @pl.when(i == 0)
    def _():
        fetch(0, 0)

    @pl.when(i + 1 < n_tiles)
    def _():
        fetch(i + 1, 1 - slot)

    _wait_rows(ysbuf.at[slot], TILE_SLOTS, sem.at[slot])

    rank = lax.broadcasted_iota(jnp.int32, (TILE_SLOTS, TT), 0)
    pw = jnp.where(rank == lpos_ref[0:1, :], rt_ref[2:3, :], 0.0) \
        + jnp.where(rank == lpos_ref[1:2, :], rt_ref[3:4, :], 0.0)
    pw_hi = pw.astype(BF16)
    pw_lo = (pw - pw_hi.astype(F32)).astype(BF16)
    ys = _load_token_tiles(ysbuf.at[slot], TILE_SLOTS)
    ys_hi = ys.astype(BF16)
    ys_lo = (ys - ys_hi.astype(F32)).astype(BF16)
    over_slots = (((0,), (0,)), ((), ()))
    moe = lax.dot_general(pw_hi, ys_hi, over_slots, preferred_element_type=F32) \
        + lax.dot_general(pw_hi, ys_lo, over_slots, preferred_element_type=F32) \
        + lax.dot_general(pw_lo, ys_hi, over_slots, preferred_element_type=F32)
    o_ref[...] = _rms(h_ref[...] + moe, gn_ref[...])


def _combine(runs, h, lpos, rt, norm_final, y, n_tok):
    n_tiles = n_tok // TT
    grid_spec = pltpu.PrefetchScalarGridSpec(
        num_scalar_prefetch=1,
        grid=(n_tiles,),
        in_specs=[pl.BlockSpec((TT, D_MODEL), lambda i, runs_ref: (i, 0)),
                  pl.BlockSpec((2, TT), lambda i, runs_ref: (0, i)),
                  pl.BlockSpec((SUBLANES, TT), lambda i, runs_ref: (0, i)),
                  pl.BlockSpec((1, D_MODEL), lambda i, runs_ref: (0, 0)),
                  pl.BlockSpec(memory_space=pl.ANY)],
        out_specs=pl.BlockSpec((TT, D_MODEL), lambda i, runs_ref: (i, 0)),
        scratch_shapes=[pltpu.VMEM((2, TILE_SLOTS * TILE_ROWS, LANES), F32),
                        pltpu.SemaphoreType.DMA((2,))],
    )
    return pl.pallas_call(
        functools.partial(_combine_kernel, n_tiles=n_tiles),
        grid_spec=grid_spec,
        out_shape=jax.ShapeDtypeStruct((n_tok, D_MODEL), F32),
        compiler_params=pltpu.CompilerParams(dimension_semantics=("arbitrary",),
                                             vmem_limit_bytes=VMEM_LIMIT),
        name="combine",
    )(runs.reshape(-1), h, lpos, rt, norm_final.reshape(1, D_MODEL).astype(F32), y)


def kernel(x, norm_mix, w_in, conv_w, conv_b, w_rg_a, b_rg_a, w_rg_x, b_rg_x, rg_lambda, rg_norm, w_alpha_up, b_alpha, gla_norm, w_out, norm_ffn, w_router_group, b_router_group, w_router_expert, b_router_expert, w_exp_gate, w_exp_up, w_exp_down, norm_final):
    bsz, seq, d = x.shape
    assert d == D_MODEL and norm_mix.shape[0] == 1, "single-layer model of width D_MODEL expected"
    n_tok = bsz * seq
    nb_max = (2 * n_tok + N_EXPERTS * (BM - 1)) // BM

    x2 = x.reshape(n_tok, d)
    xr, yr, q, k, v, g, la = _inproj(x2, norm_mix[0], w_in[0], w_alpha_up[0], b_alpha[0])
    y_rg = _rglru(xr, yr, conv_w[0], conv_b[0], w_rg_a[0], b_rg_a[0], w_rg_x[0], b_rg_x[0],
                  rg_lambda[0], rg_norm[0], bsz, seq)
    y_gla = _gla(q, k, v, g, la, gla_norm[0], bsz, seq)
    h, hn, rt = _outproj(y_rg, y_gla, x2, w_out[0], norm_ffn[0], w_router_group[0],
                         b_router_group[0], w_router_expert[0], b_router_expert[0])
    lpos, meta, runs = _plan(rt, n_tok, nb_max)
    xs = _dispatch(runs, meta, hn, lpos, n_tok, nb_max)
    y = _experts(meta, xs, w_exp_gate[0], w_exp_up[0], w_exp_down[0], nb_max)
    out = _combine(runs, h, lpos, rt, norm_final, y, n_tok)
    return out.reshape(bsz, seq, d)
```

```python
import functools

import jax
import jax.numpy as jnp
from jax import lax
from jax.experimental import pallas as pl
from jax.experimental.pallas import tpu as pltpu

F32 = jnp.float32
BF16 = jnp.bfloat16
HIGHEST = lax.Precision.HIGHEST

D_MODEL = 1024
RG_WIDTH = 512
RG_BLOCKS = 8
RG_BLOCK = 64
CONV_WIDTH = 4
C_RG = 8.0
GLA_HEADS = 4
GLA_VAL = 512
GLA_KEY = 256
GLA_DK = 64
GLA_DV = 128
GATE_RANK = 16
GATE_NORM = 16.0
CHUNK = 64
N_GROUPS = 4
EXPERTS_PER_GROUP = 8
N_EXPERTS = 32
EPS = 1e-6

LANES = 128
SUBLANES = 8
VMEM_LIMIT = 56 * 1024 * 1024

TM_PROJ = 512
TS_RG = 256
TS_GLA = 256
BM = 256
TT = 512
TILE_SLOTS = 2 * TT
RUN_CHUNK_LOG2 = 6
RUN_FIELDS = 4


def _dot(a, b):
    return jnp.dot(a, b, preferred_element_type=F32)


def _softplus(z):
    return jnp.maximum(z, 0.0) + jnp.log1p(jnp.exp(-jnp.abs(z)))


def _rms(x, g):
    return x * lax.rsqrt(jnp.mean(x * x, axis=-1, keepdims=True) + EPS) * g


TILE_ROWS = D_MODEL // LANES


def _copy_run(src, src_row, dst, dst_row, n_rows, sem):
    def piece(off, rows):
        s = src.at[pl.ds(pl.multiple_of((src_row + off) * TILE_ROWS, TILE_ROWS), rows * TILE_ROWS), :]
        d = dst.at[pl.ds(pl.multiple_of((dst_row + off) * TILE_ROWS, TILE_ROWS), rows * TILE_ROWS), :]
        pltpu.make_async_copy(s, d, sem).start()

    chunk = 1 << RUN_CHUNK_LOG2
    n_chunks = lax.shift_right_logical(n_rows, RUN_CHUNK_LOG2)
    lax.fori_loop(0, n_chunks, lambda c, carry: (piece(c * chunk, chunk), carry)[1], 0)
    off = n_chunks * chunk
    for k in reversed(range(RUN_CHUNK_LOG2)):
        bit = jnp.bitwise_and(n_rows, 1 << k)

        @pl.when(bit != 0)
        def _():
            piece(off, 1 << k)

        off = off + bit


def _wait_rows(ref, n_rows, sem):
    view = ref.at[pl.ds(0, n_rows * TILE_ROWS), :]
    pltpu.make_async_copy(view, view, sem).wait()


def _store_token_tiles(ref, val):
    n = val.shape[0]
    for c in range(TILE_ROWS):
        ref[pl.ds(c, n, stride=TILE_ROWS), :] = val[:, c * LANES:(c + 1) * LANES]


def _load_token_tiles(ref, n):
    return jnp.concatenate([ref[pl.ds(c, n, stride=TILE_ROWS), :] for c in range(TILE_ROWS)], axis=1)


def _inproj_kernel(x_ref, g_ref, wxr, wyr, wq, wk, wv, wg, wal, wup, bal,
                   xr_o, yr_o, q_o, k_o, v_o, g_o, la_o):
    hn = _rms(x_ref[...], g_ref[...]).astype(BF16)
    xr_o[...] = _dot(hn, wxr[...])
    yr_o[...] = _dot(hn, wyr[...])
    q_o[...] = _dot(hn, wq[...])
    k_o[...] = _dot(hn, wk[...])
    v_o[...] = _dot(hn, wv[...])
    g_o[...] = _dot(hn, wg[...])
    a_low = _dot(hn, wal[...])
    z = jnp.dot(a_low, wup[...], precision=HIGHEST, preferred_element_type=F32) + bal[...]
    log_sig = jnp.minimum(z, 0.0) - jnp.log1p(jnp.exp(-jnp.abs(z)))
    la_o[...] = log_sig * (1.0 / GATE_NORM)


def _inproj(x2, norm_mix, w_in, w_alpha_up, b_alpha):
    t = x2.shape[0]
    c = [0, 512, 1024, 1280, 1536, 2048, 2560, 2576]
    wb = w_in.astype(BF16)
    segs = [wb[:, c[i]:c[i + 1]] for i in range(6)]
    wal = jnp.pad(wb[:, c[6]:c[7]], ((0, 0), (0, LANES - GATE_RANK)))
    wup = jnp.pad(w_alpha_up.astype(F32), ((0, LANES - GATE_RANK), (0, 0)))
    widths = [512, 512, 256, 256, 512, 512, 256]
    row = lambda i: (i, 0)
    fixed = lambda i: (0, 0)
    in_specs = [pl.BlockSpec((TM_PROJ, D_MODEL), row), pl.BlockSpec((1, D_MODEL), fixed)]
    in_specs += [pl.BlockSpec((D_MODEL, w), fixed) for w in widths[:6]]
    in_specs += [pl.BlockSpec((D_MODEL, LANES), fixed), pl.BlockSpec((LANES, GLA_KEY), fixed),
                 pl.BlockSpec((1, GLA_KEY), fixed)]
    return pl.pallas_call(
        _inproj_kernel,
        grid=(t // TM_PROJ,),
        in_specs=in_specs,
        out_specs=[pl.BlockSpec((TM_PROJ, w), row) for w in widths],
        out_shape=[jax.ShapeDtypeStruct((t, w), F32) for w in widths],
        compiler_params=pltpu.CompilerParams(dimension_semantics=("arbitrary",),
                                             vmem_limit_bytes=VMEM_LIMIT),
        name="inproj",
    )(x2, norm_mix.reshape(1, D_MODEL), *segs, wal, wup, b_alpha.reshape(1, GLA_KEY))


def _rglru_kernel(xr_ref, yr_ref, cw_ref, cb_ref, wa_ref, ba_ref, wx_ref, bx_ref, lam_ref, gn_ref,
                  o_ref, ext_ref, hc_ref):
    ts = xr_ref.shape[0]

    @pl.when(pl.program_id(1) == 0)
    def _():
        ext_ref[0:SUBLANES, :] = jnp.zeros((SUBLANES, RG_WIDTH), F32)
        hc_ref[...] = jnp.zeros_like(hc_ref)

    x = xr_ref[...]
    ext_ref[SUBLANES:SUBLANES + ts, :] = x
    cw = cw_ref[...]
    xc = cb_ref[...] + cw[CONV_WIDTH - 1:CONV_WIDTH, :] * x
    for s in range(1, CONV_WIDTH):
        xc = xc + cw[CONV_WIDTH - 1 - s:CONV_WIDTH - s, :] * ext_ref[SUBLANES - s:SUBLANES - s + ts, :]
    ext_ref[0:SUBLANES, :] = x[ts - SUBLANES:ts, :]

    xb = xc.astype(BF16)
    r = jax.nn.sigmoid(_dot(xb, wa_ref[...]) + ba_ref[...])
    gate_i = jax.nn.sigmoid(_dot(xb, wx_ref[...]) + bx_ref[...])
    log_a = (-C_RG) * r * _softplus(-lam_ref[...])
    a = jnp.exp(log_a)
    u = jnp.sqrt(jnp.tanh(-log_a) * (1.0 + a * a)) * (gate_i * xc)

    row = lax.broadcasted_iota(jnp.int32, (ts, RG_WIDTH), 0)
    d = 1
    while d < ts:
        keep = row >= d
        a_sh = jnp.where(keep, pltpu.roll(a, d, axis=0), 1.0)
        u_sh = jnp.where(keep, pltpu.roll(u, d, axis=0), 0.0)
        u = a * u_sh + u
        a = a * a_sh
        d *= 2
    h = u + a * hc_ref[0:1, :]
    hc_ref[...] = jnp.broadcast_to(h[ts - 1:ts, :], hc_ref.shape)

    y = h * jax.nn.gelu(yr_ref[...])
    o_ref[...] = _rms(y, gn_ref[...])


def _block_diag(w):
    eye = jnp.eye(RG_BLOCKS, dtype=w.dtype)
    return jnp.einsum('hij,hg->higj', w, eye).reshape(RG_WIDTH, RG_WIDTH)


def _rglru(xr, yr, conv_w, conv_b, w_a, b_a, w_x, b_x, lam, gn, bsz, seq):
    nt = seq // TS_RG
    row = lambda b, i: (b * nt + i, 0)
    fixed = lambda b, i: (0, 0)
    vec = lambda v: v.reshape(1, RG_WIDTH).astype(F32)
    return pl.pallas_call(
        _rglru_kernel,
        grid=(bsz, nt),
        in_specs=[pl.BlockSpec((TS_RG, RG_WIDTH), row), pl.BlockSpec((TS_RG, RG_WIDTH), row),
                  pl.BlockSpec((CONV_WIDTH, RG_WIDTH), fixed), pl.BlockSpec((1, RG_WIDTH), fixed),
                  pl.BlockSpec((RG_WIDTH, RG_WIDTH), fixed), pl.BlockSpec((1, RG_WIDTH), fixed),
                  pl.BlockSpec((RG_WIDTH, RG_WIDTH), fixed), pl.BlockSpec((1, RG_WIDTH), fixed),
                  pl.BlockSpec((1, RG_WIDTH), fixed), pl.BlockSpec((1, RG_WIDTH), fixed)],
        out_specs=pl.BlockSpec((TS_RG, RG_WIDTH), row),
        out_shape=jax.ShapeDtypeStruct((bsz * seq, RG_WIDTH), F32),
        scratch_shapes=[pltpu.VMEM((TS_RG + SUBLANES, RG_WIDTH), F32),
                        pltpu.VMEM((SUBLANES, RG_WIDTH), F32)],
        compiler_params=pltpu.CompilerParams(dimension_semantics=("arbitrary", "arbitrary"),
                                             vmem_limit_bytes=VMEM_LIMIT),
        name="rglru",
    )(xr, yr, conv_w.astype(F32), vec(conv_b), _block_diag(w_a).astype(BF16), vec(b_a),
      _block_diag(w_x).astype(BF16), vec(b_x), vec(lam), vec(gn))


def _gla_kernel(q_ref, k_ref, v_ref, g_ref, la_ref, gn_ref, o_ref, st_ref):
    ts = q_ref.shape[0]

    @pl.when(pl.program_id(1) == 0)
    def _():
        st_ref[...] = jnp.zeros_like(st_ref)

    ri = lax.broadcasted_iota(jnp.int32, (CHUNK, CHUNK), 0)
    ci = lax.broadcasted_iota(jnp.int32, (CHUNK, CHUNK), 1)
    causal = ri >= ci
    tril = causal.astype(F32)
    scale = GLA_DK ** -0.5
    gn = gn_ref[...]

    for c in range(ts // CHUNK):
        rows = slice(c * CHUNK, (c + 1) * CHUNK)
        la = la_ref[rows, :]
        b = jnp.dot(tril, la, precision=HIGHEST, preferred_element_type=F32)
        b_last = b[CHUNK - 1:CHUNK, :]
        q_s = (q_ref[rows, :] * scale) * jnp.exp(b)
        kk = k_ref[rows, :]
        k_s = kk * jnp.exp(-b)
        k_end = kk * jnp.exp(b_last - b)
        decay = jnp.exp(b_last)
        st = st_ref[...]
        new_parts = []
        for h in range(GLA_HEADS):
            ks = slice(h * GLA_DK, (h + 1) * GLA_DK)
            vs = slice(h * GLA_DV, (h + 1) * GLA_DV)
            qh = q_s[:, ks].astype(BF16)
            v_h = v_ref[rows, vs]
            vb = v_h.astype(BF16)
            att = lax.dot_general(qh, k_s[:, ks].astype(BF16), (((1,), (1,)), ((), ())),
                                  preferred_element_type=F32)
            att = jnp.where(causal, att, 0.0)
            o = _dot(att.astype(BF16), vb)
            o = o + lax.dot_general(qh, st[:, ks].astype(BF16), (((1,), (1,)), ((), ())),
                                    preferred_element_type=F32)
            new_parts.append(lax.dot_general(vb, k_end[:, ks].astype(BF16), (((0,), (0,)), ((), ())),
                                             preferred_element_type=F32))
            o = _rms(o, gn) * jax.nn.silu(g_ref[rows, vs])
            o_ref[rows, vs] = o
        st_ref[...] = decay * st + jnp.concatenate(new_parts, axis=1)


def _gla(q, k, v, g, la, gn, bsz, seq):
    nt = seq // TS_GLA
    row = lambda b, i: (b * nt + i, 0)
    fixed = lambda b, i: (0, 0)
    return pl.pallas_call(
        _gla_kernel,
        grid=(bsz, nt),
        in_specs=[pl.BlockSpec((TS_GLA, GLA_KEY), row), pl.BlockSpec((TS_GLA, GLA_KEY), row),
                  pl.BlockSpec((TS_GLA, GLA_VAL), row), pl.BlockSpec((TS_GLA, GLA_VAL), row),
                  pl.BlockSpec((TS_GLA, GLA_KEY), row), pl.BlockSpec((1, GLA_DV), fixed)],
        out_specs=pl.BlockSpec((TS_GLA, GLA_VAL), row),
        out_shape=jax.ShapeDtypeStruct((bsz * seq, GLA_VAL), F32),
        scratch_shapes=[pltpu.VMEM((GLA_DV, GLA_KEY), F32)],
        compiler_params=pltpu.CompilerParams(dimension_semantics=("arbitrary", "arbitrary"),
                                             vmem_limit_bytes=VMEM_LIMIT),
        name="gla",
    )(q, k, v, g, la, gn.reshape(1, GLA_DV).astype(F32))


def _outproj_kernel(yrg_ref, ygla_ref, x_ref, wo1_ref, wo2_ref, gn_ref, wr_ref, br_ref,
                    h_o, hn_o, rt_o):
    tm = x_ref.shape[0]
    h = x_ref[...] + _dot(yrg_ref[...].astype(BF16), wo1_ref[...]) \
        + _dot(ygla_ref[...].astype(BF16), wo2_ref[...])
    h_o[...] = h
    hn = _rms(h, gn_ref[...])
    hn_hi = hn.astype(BF16)
    hn_o[...] = hn_hi
    hn_lo = (hn - hn_hi.astype(F32)).astype(BF16)
    hi_parts = _dot(hn_hi, wr_ref[...])
    logits = hi_parts[:, :LANES] + hi_parts[:, LANES:] + _dot(hn_lo, wr_ref[:, :LANES]) + br_ref[...]

    lane = lax.broadcasted_iota(jnp.int32, (tm, LANES), 1).astype(F32)
    neg = -jnp.inf
    glog = jnp.where(lane < N_GROUPS, logits, neg)
    gmax = jnp.max(glog, axis=-1, keepdims=True)
    gidx = jnp.min(jnp.where(glog == gmax, lane, float(LANES)), axis=-1, keepdims=True)
    g_w = 1.0 / jnp.sum(jnp.exp(glog - gmax), axis=-1, keepdims=True)
    lo = N_GROUPS + gidx * EXPERTS_PER_GROUP
    in_group = jnp.logical_and(lane >= lo, lane < lo + EXPERTS_PER_GROUP)
    le = jnp.where(in_group, logits, neg)
    m1 = jnp.max(le, axis=-1, keepdims=True)
    i1 = jnp.min(jnp.where(le == m1, lane, float(LANES)), axis=-1, keepdims=True)
    le2 = jnp.where(lane == i1, neg, le)
    m2 = jnp.max(le2, axis=-1, keepdims=True)
    i2 = jnp.min(jnp.where(le2 == m2, lane, float(LANES)), axis=-1, keepdims=True)
    t2 = jnp.exp(m2 - m1)
    w1 = g_w / (1.0 + t2)
    w2 = g_w * t2 / (1.0 + t2)
    info = jnp.where(lane == 0.0, i1 - N_GROUPS,
                     jnp.where(lane == 1.0, i2 - N_GROUPS,
                               jnp.where(lane == 2.0, w1, jnp.where(lane == 3.0, w2, 0.0))))
    rt_o[...] = info.T[0:SUBLANES, :]


def _outproj(y_rg, y_gla, x2, w_out, norm_ffn, w_rg, b_rg, w_re, b_re):
    t = x2.shape[0]
    wo = w_out.astype(BF16)
    wr = jnp.pad(jnp.concatenate([w_rg, w_re], axis=1).astype(F32),
                 ((0, 0), (0, LANES - N_GROUPS - N_EXPERTS)))
    wr_hi = wr.astype(BF16)
    wr = jnp.concatenate([wr_hi, (wr - wr_hi.astype(F32)).astype(BF16)], axis=1)
    br = jnp.pad(jnp.concatenate([b_rg, b_re]).astype(F32), (0, LANES - N_GROUPS - N_EXPERTS))
    row = lambda i: (i, 0)
    fixed = lambda i: (0, 0)
    return pl.pallas_call(
        _outproj_kernel,
        grid=(t // TM_PROJ,),
        in_specs=[pl.BlockSpec((TM_PROJ, RG_WIDTH), row), pl.BlockSpec((TM_PROJ, GLA_VAL), row),
                  pl.BlockSpec((TM_PROJ, D_MODEL), row),
                  pl.BlockSpec((RG_WIDTH, D_MODEL), fixed), pl.BlockSpec((GLA_VAL, D_MODEL), fixed),
                  pl.BlockSpec((1, D_MODEL), fixed), pl.BlockSpec((D_MODEL, 2 * LANES), fixed),
                  pl.BlockSpec((1, LANES), fixed)],
        out_specs=[pl.BlockSpec((TM_PROJ, D_MODEL), row), pl.BlockSpec((TM_PROJ, D_MODEL), row),
                   pl.BlockSpec((SUBLANES, TM_PROJ), lambda i: (0, i))],
        out_shape=[jax.ShapeDtypeStruct((t, D_MODEL), F32), jax.ShapeDtypeStruct((t, D_MODEL), BF16),
                   jax.ShapeDtypeStruct((SUBLANES, t), F32)],
        compiler_params=pltpu.CompilerParams(dimension_semantics=("arbitrary",),
                                             vmem_limit_bytes=VMEM_LIMIT),
        name="outproj",
    )(y_rg, y_gla, x2, wo[:RG_WIDTH], wo[RG_WIDTH:], norm_ffn.reshape(1, D_MODEL).astype(F32),
      wr, br.reshape(1, LANES))


def _plan_kernel(rt_ref, lpos_o, meta_o, runs_o, tri_ref, *, n_tok, nb_max):
    n_tiles = n_tok // TT
    esub = lax.broadcasted_iota(jnp.int32, (N_EXPERTS, TT), 0).astype(F32)

    def onehots(i):
        off = pl.multiple_of(i * TT, TT)
        e1 = rt_ref[0:1, pl.ds(off, TT)]
        e2 = rt_ref[1:2, pl.ds(off, TT)]
        m1 = jnp.where(esub == e1, 1.0, 0.0)
        m2 = jnp.where(esub == e2, 1.0, 0.0)
        return off, m1, m2

    def count_body(i, cnt):
        _, m1, m2 = onehots(i)
        return cnt + jnp.sum(m1 + m2, axis=1, keepdims=True)

    counts = lax.fori_loop(0, n_tiles, count_body, jnp.zeros((N_EXPERTS, 1), F32))
    nblk = jnp.floor((counts + (BM - 1)) * (1.0 / BM))
    ei = lax.broadcasted_iota(jnp.int32, (N_EXPERTS, N_EXPERTS), 0)
    ej = lax.broadcasted_iota(jnp.int32, (N_EXPERTS, N_EXPERTS), 1)
    nblk_row = jnp.sum(jnp.where(ei == ej, nblk, 0.0), axis=0, keepdims=True)
    bstart = jnp.sum(jnp.where(ej < ei, nblk_row, 0.0), axis=1, keepdims=True)
    bend = bstart + nblk
    n_used = jnp.sum(nblk, axis=0, keepdims=True)

    meta_w = meta_o.shape[1]
    blane = lax.broadcasted_iota(jnp.int32, (N_EXPERTS, meta_w), 1).astype(F32)
    owner = jnp.sum(jnp.where(bend <= blane, 1.0, 0.0), axis=0, keepdims=True)
    owner = jnp.minimum(owner, N_EXPERTS - 1.0)
    lane1 = lax.broadcasted_iota(jnp.int32, (1, meta_w), 1)
    meta_o[0:1, :] = jnp.where(lane1 == nb_max, n_used, owner).astype(jnp.int32)
    nonempty = jnp.where(nblk > 0.0, 1.0, 0.0)
    meta_o[1:2, :] = jnp.sum(jnp.where(bend <= blane, nonempty, 0.0), axis=0,
                             keepdims=True).astype(jnp.int32)
    owned = jnp.logical_and(bstart <= blane, blane < bend)
    next_start = jnp.sum(jnp.where(owned, bend, 0.0), axis=0, keepdims=True)
    next_owner = jnp.sum(jnp.where(bend <= next_start, 1.0, 0.0), axis=0, keepdims=True)
    meta_o[2:3, :] = jnp.where(next_start < n_used, next_owner, -1.0).astype(jnp.int32)

    ti = lax.broadcasted_iota(jnp.int32, (TT, TT), 0)
    tj = lax.broadcasted_iota(jnp.int32, (TT, TT), 1)
    tri_ref[...] = jnp.where(ti <= tj, 1.0, 0.0).astype(BF16)
    tlane = lax.broadcasted_iota(jnp.int32, (N_EXPERTS, LANES), 1)

    def tile_body(i, carry):
        first_slot, t_cnt, t_slot, t_rank = carry
        off, m1, m2 = onehots(i)
        m = m1 + m2
        incl = _dot(m.astype(BF16), tri_ref[...])
        cnt = incl[:, TT - 1:TT]
        cnt_row = jnp.sum(jnp.where(ei == ej, cnt, 0.0), axis=0, keepdims=True)
        first_rank = jnp.sum(jnp.where(ej < ei, cnt_row, 0.0), axis=1, keepdims=True)
        rank = first_rank + incl - m
        lpos_o[0:1, pl.ds(off, TT)] = jnp.sum(m1 * rank, axis=0, keepdims=True).astype(jnp.int32)
        lpos_o[1:2, pl.ds(off, TT)] = jnp.sum(m2 * rank, axis=0, keepdims=True).astype(jnp.int32)
        here = tlane == i
        return (first_slot + cnt, jnp.where(here, cnt, t_cnt), jnp.where(here, first_slot, t_slot),
                jnp.where(here, first_rank, t_rank))

    zeros = jnp.zeros((N_EXPERTS, LANES), F32)
    _, t_cnt, t_slot, t_rank = lax.fori_loop(0, n_tiles, tile_body,
                                             (bstart * float(BM), zeros, zeros, zeros))
    runs_o[0] = t_cnt.astype(jnp.int32)
    runs_o[1] = t_slot.astype(jnp.int32)
    runs_o[2] = t_rank.astype(jnp.int32)
    pad = jnp.where(tlane == 0, bstart * float(BM) + counts,
                    jnp.where(tlane == 1, nblk * float(BM) - counts, 0.0))
    runs_o[3] = pad.astype(jnp.int32)


def _plan(rt, n_tok, nb_max):
    assert n_tok // TT <= LANES, "run tables hold one token tile per lane"
    meta_w = ((nb_max + 1 + LANES - 1) // LANES) * LANES
    return pl.pallas_call(
        functools.partial(_plan_kernel, n_tok=n_tok, nb_max=nb_max),
        out_shape=[jax.ShapeDtypeStruct((2, n_tok), jnp.int32),
                   jax.ShapeDtypeStruct((3, meta_w), jnp.int32),
                   jax.ShapeDtypeStruct((RUN_FIELDS, N_EXPERTS, LANES), jnp.int32)],
        scratch_shapes=[pltpu.VMEM((TT, TT), BF16)],
        compiler_params=pltpu.CompilerParams(vmem_limit_bytes=VMEM_LIMIT),
        name="plan",
    )(rt)


def _run_entry(runs_ref, field, expert, tile):
    return runs_ref[(field * N_EXPERTS + expert) * LANES + tile]


def _dispatch_kernel(runs_ref, meta_ref, hn_ref, lpos_ref, xs_hbm, xsbuf, zbuf, sem, zsem,
                     *, n_tiles, nb_max, n_pad):
    i = pl.program_id(0)
    slot = lax.rem(i, 2)

    @pl.when(i == 0)
    def _():
        zbuf[...] = jnp.zeros_like(zbuf)
        for e in range(N_EXPERTS):
            _copy_run(zbuf, 0, xs_hbm, _run_entry(runs_ref, 3, e, 0), _run_entry(runs_ref, 3, e, 1),
                      zsem)

        def unused_block(blk, carry):
            dst = xs_hbm.at[pl.ds(pl.multiple_of(blk * (BM * TILE_ROWS), BM * TILE_ROWS),
                                  BM * TILE_ROWS), :]
            pltpu.make_async_copy(zbuf, dst, zsem).start()
            return carry

        lax.fori_loop(meta_ref[nb_max], nb_max, unused_block, 0)

    @pl.when(i >= 2)
    def _():
        _wait_rows(xsbuf.at[slot], TILE_SLOTS, sem.at[slot])

    rank = lax.broadcasted_iota(jnp.int32, (TILE_SLOTS, TT), 0)
    onehot = jnp.where(rank == lpos_ref[0:1, :], 1.0, jnp.where(rank == lpos_ref[1:2, :], 1.0, 0.0))
    _store_token_tiles(xsbuf.at[slot], _dot(onehot.astype(BF16), hn_ref[...]))
    for e in range(N_EXPERTS):
        _copy_run(xsbuf.at[slot], _run_entry(runs_ref, 2, e, i), xs_hbm, _run_entry(runs_ref, 1, e, i),
                  _run_entry(runs_ref, 0, e, i), sem.at[slot])

    @pl.when(i == n_tiles - 1)
    def _():
        _wait_rows(xsbuf.at[slot], TILE_SLOTS, sem.at[slot])
        if n_tiles > 1:
            _wait_rows(xsbuf.at[1 - slot], TILE_SLOTS, sem.at[1 - slot])
        _wait_rows(xs_hbm, n_pad, zsem)


def _dispatch(runs, meta, hn, lpos, n_tok, nb_max):
    n_tiles = n_tok // TT
    cap = nb_max * BM
    grid_spec = pltpu.PrefetchScalarGridSpec(
        num_scalar_prefetch=2,
        grid=(n_tiles,),
        in_specs=[pl.BlockSpec((TT, D_MODEL), lambda i, runs_ref, meta_ref: (i, 0)),
                  pl.BlockSpec((2, TT), lambda i, runs_ref, meta_ref: (0, i))],
        out_specs=pl.BlockSpec(memory_space=pl.ANY),
        scratch_shapes=[pltpu.VMEM((2, TILE_SLOTS * TILE_ROWS, LANES), F32),
                        pltpu.VMEM((BM * TILE_ROWS, LANES), F32),
                        pltpu.SemaphoreType.DMA((2,)), pltpu.SemaphoreType.DMA],
    )
    return pl.pallas_call(
        functools.partial(_dispatch_kernel, n_tiles=n_tiles, nb_max=nb_max, n_pad=cap - 2 * n_tok),
        grid_spec=grid_spec,
        out_shape=jax.ShapeDtypeStruct((cap * TILE_ROWS, LANES), F32),
        compiler_params=pltpu.CompilerParams(dimension_semantics=("arbitrary",),
                                             vmem_limit_bytes=VMEM_LIMIT),
        name="dispatch",
    )(runs.reshape(-1), meta.reshape(-1), hn, lpos)


def _experts_kernel(meta_ref, xs_ref, wg_hbm, wu_hbm, wd_hbm, y_ref, wf32, wbf, wsem, *, nb_max, meta_w):
    b = pl.program_id(0)
    n_used = meta_ref[nb_max]

    def start_weights(expert, buf):
        for j, w_hbm in enumerate((wg_hbm, wu_hbm, wd_hbm)):
            pltpu.make_async_copy(w_hbm.at[expert], wf32.at[buf, j], wsem.at[buf]).start()

    @pl.when(b < n_used)
    def _():
        owner = meta_ref[b]

        @pl.when(b == 0)
        def _():
            start_weights(owner, 0)

        @pl.when(jnp.logical_or(b == 0, owner != meta_ref[jnp.maximum(b - 1, 0)]))
        def _():
            buf = jnp.bitwise_and(meta_ref[meta_w + b], 1)
            pltpu.make_async_copy(wf32.at[buf], wf32.at[buf], wsem.at[buf]).wait()
            next_owner = meta_ref[2 * meta_w + b]

            @pl.when(next_owner >= 0)
            def _():
                start_weights(next_owner, 1 - buf)

            for j in range(3):
                wbf[j] = wf32[buf, j].astype(BF16)

        x = _load_token_tiles(xs_ref, BM).astype(BF16)
        mid = (jax.nn.gelu(_dot(x, wbf[0])) * _dot(x, wbf[1])).astype(BF16)
        _store_token_tiles(y_ref, _dot(mid, wbf[2]))

    @pl.when(b >= n_used)
    def _():
        y_ref[...] = jnp.zeros_like(y_ref)


def _experts(meta, xs, w_gate, w_up, w_down, nb_max):
    whole = pl.BlockSpec(memory_space=pl.ANY)
    rows = pl.BlockSpec((BM * TILE_ROWS, LANES), lambda b, meta_ref: (b, 0))
    grid_spec = pltpu.PrefetchScalarGridSpec(
        num_scalar_prefetch=1,
        grid=(nb_max,),
        in_specs=[rows, whole, whole, whole],
        out_specs=rows,
        scratch_shapes=[pltpu.VMEM((2, 3, D_MODEL, D_MODEL), F32),
                        pltpu.VMEM((3, D_MODEL, D_MODEL), BF16),
                        pltpu.SemaphoreType.DMA((2,))],
    )
    return pl.pallas_call(
        functools.partial(_experts_kernel, nb_max=nb_max, meta_w=meta.shape[1]),
        grid_spec=grid_spec,
        out_shape=jax.ShapeDtypeStruct(xs.shape, F32),
        compiler_params=pltpu.CompilerParams(dimension_semantics=("arbitrary",),
                                             vmem_limit_bytes=VMEM_LIMIT),
        name="experts",
    )(meta.reshape(-1), xs, w_gate, w_up, w_down)


def _combine_kernel(runs_ref, h_ref, lpos_ref, rt_ref, gn_ref, y_hbm, o_ref, ysbuf, sem, *, n_tiles):
    i = pl.program_id(0)
    slot = lax.rem(i, 2)

    def fetch(tile, buf):
        for e in range(N_EXPERTS):
            _copy_run(y_hbm, _run_entry(runs_ref, 1, e, tile), ysbuf.at[buf],
                      _run_entry(runs_ref, 2, e, tile), _run_entry(runs_ref, 0, e, tile), sem.at[buf])

    @pl.when(i == 0)
    def _():
        fetch(0, 0)

    @pl.when(i + 1 < n_tiles)
    def _():
        fetch(i + 1, 1 - slot)

    _wait_rows(ysbuf.at[slot], TILE_SLOTS, sem.at[slot])

    rank = lax.broadcasted_iota(jnp.int32, (TILE_SLOTS, TT), 0)
    first = rank == lpos_ref[0:1, :]
    second = rank == lpos_ref[1:2, :]
    slot_w = jnp.sum(jnp.where(first, rt_ref[2:3, :], 0.0) + jnp.where(second, rt_ref[3:4, :], 0.0),
                     axis=1, keepdims=True)
    onehot = jnp.where(first, 1.0, jnp.where(second, 1.0, 0.0)).astype(BF16)
    ys = _load_token_tiles(ysbuf.at[slot], TILE_SLOTS) * slot_w
    ys_hi = ys.astype(BF16)
    ys_lo = (ys - ys_hi.astype(F32)).astype(BF16)
    over_slots = (((0,), (0,)), ((), ()))
    moe = lax.dot_general(onehot, ys_hi, over_slots, preferred_element_type=F32) \
        + lax.dot_general(onehot, ys_lo, over_slots, preferred_element_type=F32)
    o_ref[...] = _rms(h_ref[...] + moe, gn_ref[...])


def _combine(runs, h, lpos, rt, norm_final, y, n_tok):
    n_tiles = n_tok // TT
    grid_spec = pltpu.PrefetchScalarGridSpec(
        num_scalar_prefetch=1,
        grid=(n_tiles,),
        in_specs=[pl.BlockSpec((TT, D_MODEL), lambda i, runs_ref: (i, 0)),
                  pl.BlockSpec((2, TT), lambda i, runs_ref: (0, i)),
                  pl.BlockSpec((SUBLANES, TT), lambda i, runs_ref: (0, i)),
                  pl.BlockSpec((1, D_MODEL), lambda i, runs_ref: (0, 0)),
                  pl.BlockSpec(memory_space=pl.ANY)],
        out_specs=pl.BlockSpec((TT, D_MODEL), lambda i, runs_ref: (i, 0)),
        scratch_shapes=[pltpu.VMEM((2, TILE_SLOTS * TILE_ROWS, LANES), F32),
                        pltpu.SemaphoreType.DMA((2,))],
    )
    return pl.pallas_call(
        functools.partial(_combine_kernel, n_tiles=n_tiles),
        grid_spec=grid_spec,
        out_shape=jax.ShapeDtypeStruct((n_tok, D_MODEL), F32),
        compiler_params=pltpu.CompilerParams(dimension_semantics=("arbitrary",),
                                             vmem_limit_bytes=VMEM_LIMIT),
        name="combine",
    )(runs.reshape(-1), h, lpos, rt, norm_final.reshape(1, D_MODEL).astype(F32), y)


def kernel(x, norm_mix, w_in, conv_w, conv_b, w_rg_a, b_rg_a, w_rg_x, b_rg_x, rg_lambda, rg_norm, w_alpha_up, b_alpha, gla_norm, w_out, norm_ffn, w_router_group, b_router_group, w_router_expert, b_router_expert, w_exp_gate, w_exp_up, w_exp_down, norm_final):
    bsz, seq, d = x.shape
    assert d == D_MODEL and norm_mix.shape[0] == 1, "single-layer model of width D_MODEL expected"
    n_tok = bsz * seq
    nb_max = (2 * n_tok + N_EXPERTS * (BM - 1)) // BM

    x2 = x.reshape(n_tok, d)
    xr, yr, q, k, v, g, la = _inproj(x2, norm_mix[0], w_in[0], w_alpha_up[0], b_alpha[0])
    y_rg = _rglru(xr, yr, conv_w[0], conv_b[0], w_rg_a[0], b_rg_a[0], w_rg_x[0], b_rg_x[0],
                  rg_lambda[0], rg_norm[0], bsz, seq)
    y_gla = _gla(q, k, v, g, la, gla_norm[0], bsz, seq)
    h, hn, rt = _outproj(y_rg, y_gla, x2, w_out[0], norm_ffn[0], w_router_group[0],
                         b_router_group[0], w_router_expert[0], b_router_expert[0])
    lpos, meta, runs = _plan(rt, n_tok, nb_max)
    xs = _dispatch(runs, meta, hn, lpos, n_tok, nb_max)
    y = _experts(meta, xs, w_exp_gate[0], w_exp_up[0], w_exp_down[0], nb_max)
    out = _combine(runs, h, lpos, rt, norm_final, y, n_tok)
    return out.reshape(bsz, seq, d)
```

```python
import functools

import jax
import jax.numpy as jnp
from jax import lax
from jax.experimental import pallas as pl
from jax.experimental.pallas import tpu as pltpu

F32 = jnp.float32
BF16 = jnp.bfloat16
HIGHEST = lax.Precision.HIGHEST

D_MODEL = 1024
RG_WIDTH = 512
RG_BLOCKS = 8
RG_BLOCK = 64
CONV_WIDTH = 4
C_RG = 8.0
GLA_HEADS = 4
GLA_VAL = 512
GLA_KEY = 256
GLA_DK = 64
GLA_DV = 128
GATE_RANK = 16
GATE_NORM = 16.0
CHUNK = 64
N_GROUPS = 4
EXPERTS_PER_GROUP = 8
N_EXPERTS = 32
EPS = 1e-6

LANES = 128
SUBLANES = 8
VMEM_LIMIT = 56 * 1024 * 1024

TM_PROJ = 512
TM_INPROJ = 1024
TS_RG = 256
TS_GLA = 256
BM = 256
TT = 512
TILE_SLOTS = 2 * TT
RUN_CHUNK_LOG2 = 6
RUN_FIELDS = 4


def _dot(a, b):
    return jnp.dot(a, b, preferred_element_type=F32)


def _softplus(z):
    return jnp.maximum(z, 0.0) + jnp.log1p(jnp.exp(-jnp.abs(z)))


def _rms(x, g):
    return x * lax.rsqrt(jnp.mean(x * x, axis=-1, keepdims=True) + EPS) * g


TILE_ROWS = D_MODEL // LANES


def _copy_run(src, src_row, dst, dst_row, n_rows, sem):
    def piece(off, rows):
        s = src.at[pl.ds(pl.multiple_of((src_row + off) * TILE_ROWS, TILE_ROWS), rows * TILE_ROWS), :]
        d = dst.at[pl.ds(pl.multiple_of((dst_row + off) * TILE_ROWS, TILE_ROWS), rows * TILE_ROWS), :]
        pltpu.make_async_copy(s, d, sem).start()

    chunk = 1 << RUN_CHUNK_LOG2
    n_chunks = lax.shift_right_logical(n_rows, RUN_CHUNK_LOG2)
    lax.fori_loop(0, n_chunks, lambda c, carry: (piece(c * chunk, chunk), carry)[1], 0)
    off = n_chunks * chunk
    for k in reversed(range(RUN_CHUNK_LOG2)):
        bit = jnp.bitwise_and(n_rows, 1 << k)

        @pl.when(bit != 0)
        def _():
            piece(off, 1 << k)

        off = off + bit


def _wait_rows(ref, n_rows, sem):
    view = ref.at[pl.ds(0, n_rows * TILE_ROWS), :]
    pltpu.make_async_copy(view, view, sem).wait()


def _store_token_tiles(ref, val):
    n = val.shape[0]
    for c in range(TILE_ROWS):
        ref[pl.ds(c, n, stride=TILE_ROWS), :] = val[:, c * LANES:(c + 1) * LANES]


def _load_token_tiles(ref, n):
    return jnp.concatenate([ref[pl.ds(c, n, stride=TILE_ROWS), :] for c in range(TILE_ROWS)], axis=1)


def _inproj_kernel(x_ref, g_ref, wxr, wyr, wq, wk, wv, wg, wal, wup, bal,
                   xr_o, yr_o, q_o, k_o, v_o, g_o, la_o):
    hn = _rms(x_ref[...], g_ref[...]).astype(BF16)
    xr_o[...] = _dot(hn, wxr[...])
    yr_o[...] = _dot(hn, wyr[...])
    q_o[...] = _dot(hn, wq[...])
    k_o[...] = _dot(hn, wk[...])
    v_o[...] = _dot(hn, wv[...])
    g_o[...] = _dot(hn, wg[...])
    a_low = _dot(hn, wal[...])
    z = jnp.dot(a_low, wup[...], precision=HIGHEST, preferred_element_type=F32) + bal[...]
    log_sig = jnp.minimum(z, 0.0) - jnp.log1p(jnp.exp(-jnp.abs(z)))
    la_o[...] = log_sig * (1.0 / GATE_NORM)


def _inproj(x2, norm_mix, w_in, w_alpha_up, b_alpha):
    t = x2.shape[0]
    c = [0, 512, 1024, 1280, 1536, 2048, 2560, 2576]
    wb = w_in.astype(BF16)
    segs = [wb[:, c[i]:c[i + 1]] for i in range(6)]
    wal = jnp.pad(wb[:, c[6]:c[7]], ((0, 0), (0, LANES - GATE_RANK)))
    wup = jnp.pad(w_alpha_up.astype(F32), ((0, LANES - GATE_RANK), (0, 0)))
    widths = [512, 512, 256, 256, 512, 512, 256]
    row = lambda i: (i, 0)
    fixed = lambda i: (0, 0)
    in_specs = [pl.BlockSpec((TM_INPROJ, D_MODEL), row), pl.BlockSpec((1, D_MODEL), fixed)]
    in_specs += [pl.BlockSpec((D_MODEL, w), fixed) for w in widths[:6]]
    in_specs += [pl.BlockSpec((D_MODEL, LANES), fixed), pl.BlockSpec((LANES, GLA_KEY), fixed),
                 pl.BlockSpec((1, GLA_KEY), fixed)]
    return pl.pallas_call(
        _inproj_kernel,
        grid=(t // TM_INPROJ,),
        in_specs=in_specs,
        out_specs=[pl.BlockSpec((TM_INPROJ, w), row) for w in widths],
        out_shape=[jax.ShapeDtypeStruct((t, w), F32) for w in widths],
        compiler_params=pltpu.CompilerParams(dimension_semantics=("arbitrary",),
                                             vmem_limit_bytes=VMEM_LIMIT),
        name="inproj",
    )(x2, norm_mix.reshape(1, D_MODEL), *segs, wal, wup, b_alpha.reshape(1, GLA_KEY))


def _rglru_kernel(xr_ref, yr_ref, cw_ref, cb_ref, wa_ref, ba_ref, wx_ref, bx_ref, lam_ref, gn_ref,
                  o_ref, tail_ref, hc_ref):
    ts = xr_ref.shape[0]
    n_groups = ts // SUBLANES
    grouped = (n_groups, SUBLANES, RG_WIDTH)
    sub = lax.broadcasted_iota(jnp.int32, grouped, 1)

    @pl.when(pl.program_id(1) == 0)
    def _():
        tail_ref[...] = jnp.zeros_like(tail_ref)
        hc_ref[...] = jnp.zeros_like(hc_ref)

    x = xr_ref[...]
    x_grp = x.reshape(grouped)
    x_prev = jnp.concatenate([tail_ref[...], x[:ts - SUBLANES, :]], axis=0).reshape(grouped)
    tail_ref[...] = x[ts - SUBLANES:ts, :]
    cw = cw_ref[...]
    xc = cb_ref[...] + cw[CONV_WIDTH - 1:CONV_WIDTH, :] * x
    for s in range(1, CONV_WIDTH):
        mixed = jnp.where(sub >= SUBLANES - s, x_prev, x_grp)
        shifted = pltpu.roll(mixed, s, axis=1)
        xc = xc + cw[CONV_WIDTH - 1 - s:CONV_WIDTH - s, :] * shifted.reshape(ts, RG_WIDTH)

    xb = xc.astype(BF16)
    r = jax.nn.sigmoid(_dot(xb, wa_ref[...]) + ba_ref[...])
    gate_i = jax.nn.sigmoid(_dot(xb, wx_ref[...]) + bx_ref[...])
    log_a = (-C_RG) * r * _softplus(-lam_ref[...])
    a = jnp.exp(log_a)
    u = jnp.sqrt(jnp.tanh(-log_a) * (1.0 + a * a)) * (gate_i * xc)

    a = a.reshape(grouped)
    u = u.reshape(grouped)
    d = 1
    while d < SUBLANES:
        keep = sub >= d
        a_sh = jnp.where(keep, pltpu.roll(a, d, axis=1), 1.0)
        u_sh = jnp.where(keep, pltpu.roll(u, d, axis=1), 0.0)
        u = a * u_sh + u
        a = a * a_sh
        d *= 2
    carry = hc_ref[0:1, :]
    groups = []
    for j in range(n_groups):
        h_j = u[j] + a[j] * carry
        groups.append(h_j)
        carry = h_j[SUBLANES - 1:SUBLANES, :]
    h = jnp.concatenate(groups, axis=0)
    hc_ref[...] = jnp.broadcast_to(carry, hc_ref.shape)

    y = h * jax.nn.gelu(yr_ref[...])
    o_ref[...] = _rms(y, gn_ref[...])


def _block_diag(w):
    eye = jnp.eye(RG_BLOCKS, dtype=w.dtype)
    return jnp.einsum('hij,hg->higj', w, eye).reshape(RG_WIDTH, RG_WIDTH)


def _rglru(xr, yr, conv_w, conv_b, w_a, b_a, w_x, b_x, lam, gn, bsz, seq):
    nt = seq // TS_RG
    row = lambda b, i: (b * nt + i, 0)
    fixed = lambda b, i: (0, 0)
    vec = lambda v: v.reshape(1, RG_WIDTH).astype(F32)
    return pl.pallas_call(
        _rglru_kernel,
        grid=(bsz, nt),
        in_specs=[pl.BlockSpec((TS_RG, RG_WIDTH), row), pl.BlockSpec((TS_RG, RG_WIDTH), row),
                  pl.BlockSpec((CONV_WIDTH, RG_WIDTH), fixed), pl.BlockSpec((1, RG_WIDTH), fixed),
                  pl.BlockSpec((RG_WIDTH, RG_WIDTH), fixed), pl.BlockSpec((1, RG_WIDTH), fixed),
                  pl.BlockSpec((RG_WIDTH, RG_WIDTH), fixed), pl.BlockSpec((1, RG_WIDTH), fixed),
                  pl.BlockSpec((1, RG_WIDTH), fixed), pl.BlockSpec((1, RG_WIDTH), fixed)],
        out_specs=pl.BlockSpec((TS_RG, RG_WIDTH), row),
        out_shape=jax.ShapeDtypeStruct((bsz * seq, RG_WIDTH), F32),
        scratch_shapes=[pltpu.VMEM((SUBLANES, RG_WIDTH), F32),
                        pltpu.VMEM((SUBLANES, RG_WIDTH), F32)],
        compiler_params=pltpu.CompilerParams(dimension_semantics=("arbitrary", "arbitrary"),
                                             vmem_limit_bytes=VMEM_LIMIT),
        name="rglru",
    )(xr, yr, conv_w.astype(F32), vec(conv_b), _block_diag(w_a).astype(BF16), vec(b_a),
      _block_diag(w_x).astype(BF16), vec(b_x), vec(lam), vec(gn))


def _gla_kernel(q_ref, k_ref, v_ref, g_ref, la_ref, gn_ref, o_ref, st_ref):
    bsz, ts = q_ref.shape[0], q_ref.shape[1]
    n_chunks = ts // CHUNK

    @pl.when(pl.program_id(0) == 0)
    def _():
        st_ref[...] = jnp.zeros_like(st_ref)

    ri = lax.broadcasted_iota(jnp.int32, (ts, ts), 0)
    ci = lax.broadcasted_iota(jnp.int32, (ts, ts), 1)
    chunk_bits = CHUNK.bit_length() - 1
    same_chunk = lax.shift_right_logical(ri, chunk_bits) == lax.shift_right_logical(ci, chunk_bits)
    prefix = jnp.where(jnp.logical_and(same_chunk, ri >= ci), 1.0, 0.0).astype(BF16)
    total = jnp.where(same_chunk, 1.0, 0.0).astype(BF16)
    causal = (ri >= ci)[:CHUNK, :CHUNK]
    scale = GLA_DK ** -0.5
    gn = gn_ref[...]
    nt_dims = (((1,), (1,)), ((), ()))
    tn_dims = (((0,), (0,)), ((), ()))

    for bi in range(bsz):
        la = la_ref[bi]
        la_hi = la.astype(BF16)
        la_lo = (la - la_hi.astype(F32)).astype(BF16)
        b = _dot(prefix, la_hi) + _dot(prefix, la_lo)
        b_tot = _dot(total, la_hi) + _dot(total, la_lo)
        kk = k_ref[bi]
        q_s = (q_ref[bi] * scale) * jnp.exp(b)
        k_s = kk * jnp.exp(-b)
        k_end = kk * jnp.exp(b_tot - b)
        decay = jnp.exp(b_tot)

        o_intra, kv, q_heads = [], [], []
        for c in range(n_chunks):
            rows = slice(c * CHUNK, (c + 1) * CHUNK)
            o_c, kv_c, q_c = [], [], []
            for h in range(GLA_HEADS):
                ks = slice(h * GLA_DK, (h + 1) * GLA_DK)
                vs = slice(h * GLA_DV, (h + 1) * GLA_DV)
                qh = q_s[rows, ks].astype(BF16)
                vb = v_ref[bi, rows, vs].astype(BF16)
                att = lax.dot_general(qh, k_s[rows, ks].astype(BF16), nt_dims, preferred_element_type=F32)
                att = jnp.where(causal, att, 0.0).astype(BF16)
                o_c.append(_dot(att, vb))
                kv_c.append(lax.dot_general(vb, k_end[rows, ks].astype(BF16), tn_dims,
                                            preferred_element_type=F32))
                q_c.append(qh)
            o_intra.append(o_c)
            q_heads.append(q_c)
            kv.append(jnp.concatenate(kv_c, axis=1))

        st = st_ref[bi]
        for c in range(n_chunks):
            rows = slice(c * CHUNK, (c + 1) * CHUNK)
            for h in range(GLA_HEADS):
                ks = slice(h * GLA_DK, (h + 1) * GLA_DK)
                vs = slice(h * GLA_DV, (h + 1) * GLA_DV)
                o = o_intra[c][h] + lax.dot_general(q_heads[c][h], st[:, ks].astype(BF16), nt_dims,
                                                    preferred_element_type=F32)
                o_ref[bi, rows, vs] = _rms(o, gn) * jax.nn.silu(g_ref[bi, rows, vs])
            st = decay[c * CHUNK:c * CHUNK + 1, :] * st + kv[c]
        st_ref[bi] = st


def _gla(q, k, v, g, la, gn, bsz, seq):
    nt = seq // TS_GLA
    blk = lambda w: pl.BlockSpec((bsz, TS_GLA, w), lambda i: (0, i, 0))
    r3 = lambda a: a.reshape(bsz, seq, a.shape[-1])
    out = pl.pallas_call(
        _gla_kernel,
        grid=(nt,),
        in_specs=[blk(GLA_KEY), blk(GLA_KEY), blk(GLA_VAL), blk(GLA_VAL), blk(GLA_KEY),
                  pl.BlockSpec((1, GLA_DV), lambda i: (0, 0))],
        out_specs=blk(GLA_VAL),
        out_shape=jax.ShapeDtypeStruct((bsz, seq, GLA_VAL), F32),
        scratch_shapes=[pltpu.VMEM((bsz, GLA_DV, GLA_KEY), F32)],
        compiler_params=pltpu.CompilerParams(dimension_semantics=("arbitrary",),
                                             vmem_limit_bytes=VMEM_LIMIT),
        name="gla",
    )(r3(q), r3(k), r3(v), r3(g), r3(la), gn.reshape(1, GLA_DV).astype(F32))
    return out.reshape(bsz * seq, GLA_VAL)


def _outproj_kernel(yrg_ref, ygla_ref, x_ref, wo1_ref, wo2_ref, gn_ref, wr_ref, br_ref,
                    h_o, hn_o, rt_o):
    tm = x_ref.shape[0]
    h = x_ref[...] + _dot(yrg_ref[...].astype(BF16), wo1_ref[...]) \
        + _dot(ygla_ref[...].astype(BF16), wo2_ref[...])
    h_o[...] = h
    hn = _rms(h, gn_ref[...])
    hn_hi = hn.astype(BF16)
    hn_o[...] = hn_hi
    hn_lo = (hn - hn_hi.astype(F32)).astype(BF16)
    hi_parts = _dot(hn_hi, wr_ref[...])
    logits = hi_parts[:, :LANES] + hi_parts[:, LANES:] + _dot(hn_lo, wr_ref[:, :LANES]) + br_ref[...]

    lane = lax.broadcasted_iota(jnp.int32, (tm, LANES), 1).astype(F32)
    neg = -jnp.inf
    glog = jnp.where(lane < N_GROUPS, logits, neg)
    gmax = jnp.max(glog, axis=-1, keepdims=True)
    gidx = jnp.min(jnp.where(glog == gmax, lane, float(LANES)), axis=-1, keepdims=True)
    g_w = 1.0 / jnp.sum(jnp.exp(glog - gmax), axis=-1, keepdims=True)
    lo = N_GROUPS + gidx * EXPERTS_PER_GROUP
    in_group = jnp.logical_and(lane >= lo, lane < lo + EXPERTS_PER_GROUP)
    le = jnp.where(in_group, logits, neg)
    m1 = jnp.max(le, axis=-1, keepdims=True)
    i1 = jnp.min(jnp.where(le == m1, lane, float(LANES)), axis=-1, keepdims=True)
    le2 = jnp.where(lane == i1, neg, le)
    m2 = jnp.max(le2, axis=-1, keepdims=True)
    i2 = jnp.min(jnp.where(le2 == m2, lane, float(LANES)), axis=-1, keepdims=True)
    t2 = jnp.exp(m2 - m1)
    w1 = g_w / (1.0 + t2)
    w2 = g_w * t2 / (1.0 + t2)
    info = jnp.where(lane == 0.0, i1 - N_GROUPS,
                     jnp.where(lane == 1.0, i2 - N_GROUPS,
                               jnp.where(lane == 2.0, w1, jnp.where(lane == 3.0, w2, 0.0))))
    rt_o[...] = info.T[0:SUBLANES, :]


def _outproj(y_rg, y_gla, x2, w_out, norm_ffn, w_rg, b_rg, w_re, b_re):
    t = x2.shape[0]
    wo = w_out.astype(BF16)
    wr = jnp.pad(jnp.concatenate([w_rg, w_re], axis=1).astype(F32),
                 ((0, 0), (0, LANES - N_GROUPS - N_EXPERTS)))
    wr_hi = wr.astype(BF16)
    wr = jnp.concatenate([wr_hi, (wr - wr_hi.astype(F32)).astype(BF16)], axis=1)
    br = jnp.pad(jnp.concatenate([b_rg, b_re]).astype(F32), (0, LANES - N_GROUPS - N_EXPERTS))
    row = lambda i: (i, 0)
    fixed = lambda i: (0, 0)
    return pl.pallas_call(
        _outproj_kernel,
        grid=(t // TM_PROJ,),
        in_specs=[pl.BlockSpec((TM_PROJ, RG_WIDTH), row), pl.BlockSpec((TM_PROJ, GLA_VAL), row),
                  pl.BlockSpec((TM_PROJ, D_MODEL), row),
                  pl.BlockSpec((RG_WIDTH, D_MODEL), fixed), pl.BlockSpec((GLA_VAL, D_MODEL), fixed),
                  pl.BlockSpec((1, D_MODEL), fixed), pl.BlockSpec((D_MODEL, 2 * LANES), fixed),
                  pl.BlockSpec((1, LANES), fixed)],
        out_specs=[pl.BlockSpec((TM_PROJ, D_MODEL), row), pl.BlockSpec((TM_PROJ, D_MODEL), row),
                   pl.BlockSpec((SUBLANES, TM_PROJ), lambda i: (0, i))],
        out_shape=[jax.ShapeDtypeStruct((t, D_MODEL), F32), jax.ShapeDtypeStruct((t, D_MODEL), BF16),
                   jax.ShapeDtypeStruct((SUBLANES, t), F32)],
        compiler_params=pltpu.CompilerParams(dimension_semantics=("arbitrary",),
                                             vmem_limit_bytes=VMEM_LIMIT),
        name="outproj",
    )(y_rg, y_gla, x2, wo[:RG_WIDTH], wo[RG_WIDTH:], norm_ffn.reshape(1, D_MODEL).astype(F32),
      wr, br.reshape(1, LANES))


def _plan_kernel(rt_ref, lpos_o, meta_o, runs_o, tri_ref, *, n_tok, nb_max):
    n_tiles = n_tok // TT
    esub = lax.broadcasted_iota(jnp.int32, (N_EXPERTS, TT), 0).astype(F32)

    def onehots(i):
        off = pl.multiple_of(i * TT, TT)
        e1 = rt_ref[0:1, pl.ds(off, TT)]
        e2 = rt_ref[1:2, pl.ds(off, TT)]
        m1 = jnp.where(esub == e1, 1.0, 0.0)
        m2 = jnp.where(esub == e2, 1.0, 0.0)
        return off, m1, m2

    def count_body(i, cnt):
        _, m1, m2 = onehots(i)
        return cnt + jnp.sum(m1 + m2, axis=1, keepdims=True)

    counts = lax.fori_loop(0, n_tiles, count_body, jnp.zeros((N_EXPERTS, 1), F32))
    nblk = jnp.floor((counts + (BM - 1)) * (1.0 / BM))
    ei = lax.broadcasted_iota(jnp.int32, (N_EXPERTS, N_EXPERTS), 0)
    ej = lax.broadcasted_iota(jnp.int32, (N_EXPERTS, N_EXPERTS), 1)
    nblk_row = jnp.sum(jnp.where(ei == ej, nblk, 0.0), axis=0, keepdims=True)
    bstart = jnp.sum(jnp.where(ej < ei, nblk_row, 0.0), axis=1, keepdims=True)
    bend = bstart + nblk
    n_used = jnp.sum(nblk, axis=0, keepdims=True)

    meta_w = meta_o.shape[1]
    blane = lax.broadcasted_iota(jnp.int32, (N_EXPERTS, meta_w), 1).astype(F32)
    owner = jnp.sum(jnp.where(bend <= blane, 1.0, 0.0), axis=0, keepdims=True)
    owner = jnp.minimum(owner, N_EXPERTS - 1.0)
    lane1 = lax.broadcasted_iota(jnp.int32, (1, meta_w), 1)
    meta_o[0:1, :] = jnp.where(lane1 == nb_max, n_used, owner).astype(jnp.int32)
    nonempty = jnp.where(nblk > 0.0, 1.0, 0.0)
    meta_o[1:2, :] = jnp.sum(jnp.where(bend <= blane, nonempty, 0.0), axis=0,
                             keepdims=True).astype(jnp.int32)
    owned = jnp.logical_and(bstart <= blane, blane < bend)
    next_start = jnp.sum(jnp.where(owned, bend, 0.0), axis=0, keepdims=True)
    next_owner = jnp.sum(jnp.where(bend <= next_start, 1.0, 0.0), axis=0, keepdims=True)
    meta_o[2:3, :] = jnp.where(next_start < n_used, next_owner, -1.0).astype(jnp.int32)

    ti = lax.broadcasted_iota(jnp.int32, (TT, TT), 0)
    tj = lax.broadcasted_iota(jnp.int32, (TT, TT), 1)
    tri_ref[...] = jnp.where(ti <= tj, 1.0, 0.0).astype(BF16)
    tlane = lax.broadcasted_iota(jnp.int32, (N_EXPERTS, LANES), 1)

    def tile_body(i, carry):
        first_slot, t_cnt, t_slot, t_rank = carry
        off, m1, m2 = onehots(i)
        m = m1 + m2
        incl = _dot(m.astype(BF16), tri_ref[...])
        cnt = incl[:, TT - 1:TT]
        cnt_row = jnp.sum(jnp.where(ei == ej, cnt, 0.0), axis=0, keepdims=True)
        first_rank = jnp.sum(jnp.where(ej < ei, cnt_row, 0.0), axis=1, keepdims=True)
        rank = first_rank + incl - m
        lpos_o[0:1, pl.ds(off, TT)] = jnp.sum(m1 * rank, axis=0, keepdims=True).astype(jnp.int32)
        lpos_o[1:2, pl.ds(off, TT)] = jnp.sum(m2 * rank, axis=0, keepdims=True).astype(jnp.int32)
        here = tlane == i
        return (first_slot + cnt, jnp.where(here, cnt, t_cnt), jnp.where(here, first_slot, t_slot),
                jnp.where(here, first_rank, t_rank))

    zeros = jnp.zeros((N_EXPERTS, LANES), F32)
    _, t_cnt, t_slot, t_rank = lax.fori_loop(0, n_tiles, tile_body,
                                             (bstart * float(BM), zeros, zeros, zeros))
    runs_o[0] = t_cnt.astype(jnp.int32)
    runs_o[1] = t_slot.astype(jnp.int32)
    runs_o[2] = t_rank.astype(jnp.int32)
    pad = jnp.where(tlane == 0, bstart * float(BM) + counts,
                    jnp.where(tlane == 1, nblk * float(BM) - counts, 0.0))
    runs_o[3] = pad.astype(jnp.int32)


def _plan(rt, n_tok, nb_max):
    assert n_tok // TT <= LANES, "run tables hold one token tile per lane"
    meta_w = ((nb_max + 1 + LANES - 1) // LANES) * LANES
    return pl.pallas_call(
        functools.partial(_plan_kernel, n_tok=n_tok, nb_max=nb_max),
        out_shape=[jax.ShapeDtypeStruct((2, n_tok), jnp.int32),
                   jax.ShapeDtypeStruct((3, meta_w), jnp.int32),
                   jax.ShapeDtypeStruct((RUN_FIELDS, N_EXPERTS, LANES), jnp.int32)],
        scratch_shapes=[pltpu.VMEM((TT, TT), BF16)],
        compiler_params=pltpu.CompilerParams(vmem_limit_bytes=VMEM_LIMIT),
        name="plan",
    )(rt)


def _run_entry(runs_ref, field, expert, tile):
    return runs_ref[(field * N_EXPERTS + expert) * LANES + tile]


def _dispatch_kernel(runs_ref, meta_ref, hn_ref, lpos_ref, xs_hbm, xsbuf, zbuf, sem, zsem,
                     *, n_tiles, nb_max, n_pad):
    i = pl.program_id(0)
    slot = lax.rem(i, 2)

    @pl.when(i == 0)
    def _():
        zbuf[...] = jnp.zeros_like(zbuf)
        for e in range(N_EXPERTS):
            _copy_run(zbuf, 0, xs_hbm, _run_entry(runs_ref, 3, e, 0), _run_entry(runs_ref, 3, e, 1),
                      zsem)

        def unused_block(blk, carry):
            dst = xs_hbm.at[pl.ds(pl.multiple_of(blk * (BM * TILE_ROWS), BM * TILE_ROWS),
                                  BM * TILE_ROWS), :]
            pltpu.make_async_copy(zbuf, dst, zsem).start()
            return carry

        lax.fori_loop(meta_ref[nb_max], nb_max, unused_block, 0)

    @pl.when(i >= 2)
    def _():
        _wait_rows(xsbuf.at[slot], TILE_SLOTS, sem.at[slot])

    rank = lax.broadcasted_iota(jnp.int32, (TILE_SLOTS, TT), 0)
    onehot = jnp.where(rank == lpos_ref[0:1, :], 1.0, jnp.where(rank == lpos_ref[1:2, :], 1.0, 0.0))
    _store_token_tiles(xsbuf.at[slot], _dot(onehot.astype(BF16), hn_ref[...]))
    for e in range(N_EXPERTS):
        _copy_run(xsbuf.at[slot], _run_entry(runs_ref, 2, e, i), xs_hbm, _run_entry(runs_ref, 1, e, i),
                  _run_entry(runs_ref, 0, e, i), sem.at[slot])

    @pl.when(i == n_tiles - 1)
    def _():
        _wait_rows(xsbuf.at[slot], TILE_SLOTS, sem.at[slot])
        if n_tiles > 1:
            _wait_rows(xsbuf.at[1 - slot], TILE_SLOTS, sem.at[1 - slot])
        _wait_rows(xs_hbm, n_pad, zsem)


def _dispatch(runs, meta, hn, lpos, n_tok, nb_max):
    n_tiles = n_tok // TT
    cap = nb_max * BM
    grid_spec = pltpu.PrefetchScalarGridSpec(
        num_scalar_prefetch=2,
        grid=(n_tiles,),
        in_specs=[pl.BlockSpec((TT, D_MODEL), lambda i, runs_ref, meta_ref: (i, 0)),
                  pl.BlockSpec((2, TT), lambda i, runs_ref, meta_ref: (0, i))],
        out_specs=pl.BlockSpec(memory_space=pl.ANY),
        scratch_shapes=[pltpu.VMEM((2, TILE_SLOTS * TILE_ROWS, LANES), F32),
                        pltpu.VMEM((BM * TILE_ROWS, LANES), F32),
                        pltpu.SemaphoreType.DMA((2,)), pltpu.SemaphoreType.DMA],
    )
    return pl.pallas_call(
        functools.partial(_dispatch_kernel, n_tiles=n_tiles, nb_max=nb_max, n_pad=cap - 2 * n_tok),
        grid_spec=grid_spec,
        out_shape=jax.ShapeDtypeStruct((cap * TILE_ROWS, LANES), F32),
        compiler_params=pltpu.CompilerParams(dimension_semantics=("arbitrary",),
                                             vmem_limit_bytes=VMEM_LIMIT),
        name="dispatch",
    )(runs.reshape(-1), meta.reshape(-1), hn, lpos)


def _experts_kernel(meta_ref, xs_ref, wg_hbm, wu_hbm, wd_hbm, y_ref, wf32, wbf, wsem, *, nb_max, meta_w):
    b = pl.program_id(0)
    n_used = meta_ref[nb_max]

    def start_weights(expert, buf):
        for j, w_hbm in enumerate((wg_hbm, wu_hbm, wd_hbm)):
            pltpu.make_async_copy(w_hbm.at[expert], wf32.at[buf, j], wsem.at[buf]).start(priority=1)

    @pl.when(b < n_used)
    def _():
        owner = meta_ref[b]

        @pl.when(b == 0)
        def _():
            start_weights(owner, 0)

        @pl.when(jnp.logical_or(b == 0, owner != meta_ref[jnp.maximum(b - 1, 0)]))
        def _():
            buf = jnp.bitwise_and(meta_ref[meta_w + b], 1)
            pltpu.make_async_copy(wf32.at[buf], wf32.at[buf], wsem.at[buf]).wait()
            next_owner = meta_ref[2 * meta_w + b]

            @pl.when(next_owner >= 0)
            def _():
                start_weights(next_owner, 1 - buf)

            for j in range(3):
                wbf[j] = wf32[buf, j].astype(BF16)

        x = _load_token_tiles(xs_ref, BM).astype(BF16)
        mid = (jax.nn.gelu(_dot(x, wbf[0])) * _dot(x, wbf[1])).astype(BF16)
        _store_token_tiles(y_ref, _dot(mid, wbf[2]))

    @pl.when(b >= n_used)
    def _():
        y_ref[...] = jnp.zeros_like(y_ref)


def _experts(meta, xs, w_gate, w_up, w_down, nb_max):
    whole = pl.BlockSpec(memory_space=pl.ANY)
    rows = pl.BlockSpec((BM * TILE_ROWS, LANES), lambda b, meta_ref: (b, 0))
    grid_spec = pltpu.PrefetchScalarGridSpec(
        num_scalar_prefetch=1,
        grid=(nb_max,),
        in_specs=[rows, whole, whole, whole],
        out_specs=rows,
        scratch_shapes=[pltpu.VMEM((2, 3, D_MODEL, D_MODEL), F32),
                        pltpu.VMEM((3, D_MODEL, D_MODEL), BF16),
                        pltpu.SemaphoreType.DMA((2,))],
    )
    return pl.pallas_call(
        functools.partial(_experts_kernel, nb_max=nb_max, meta_w=meta.shape[1]),
        grid_spec=grid_spec,
        out_shape=jax.ShapeDtypeStruct(xs.shape, F32),
        compiler_params=pltpu.CompilerParams(dimension_semantics=("arbitrary",),
                                             vmem_limit_bytes=VMEM_LIMIT),
        name="experts",
    )(meta.reshape(-1), xs, w_gate, w_up, w_down)


def _combine_kernel(runs_ref, h_ref, lpos_ref, rt_ref, gn_ref, y_hbm, o_ref, ysbuf, sem, *, n_tiles):
    i = pl.program_id(0)
    slot = lax.rem(i, 2)

    def fetch(tile, buf):
        for e in range(N_EXPERTS):
            _copy_run(y_hbm, _run_entry(runs_ref, 1, e, tile), ysbuf.at[buf],
                      _run_entry(runs_ref, 2, e, tile), _run_entry(runs_ref, 0, e, tile), sem.at[buf])

    @pl.when(i == 0)
    def _():
        fetch(0, 0)

    @pl.when(i + 1 < n_tiles)
    def _():
        fetch(i + 1, 1 - slot)

    _wait_rows(ysbuf.at[slot], TILE_SLOTS, sem.at[slot])

    rank = lax.broadcasted_iota(jnp.int32, (TILE_SLOTS, TT), 0)
    first = rank == lpos_ref[0:1, :]
    second = rank == lpos_ref[1:2, :]
    slot_w = jnp.sum(jnp.where(first, rt_ref[2:3, :], 0.0) + jnp.where(second, rt_ref[3:4, :], 0.0),
                     axis=1, keepdims=True)
    onehot = jnp.where(first, 1.0, jnp.where(second, 1.0, 0.0)).astype(BF16)
    ys = _load_token_tiles(ysbuf.at[slot], TILE_SLOTS) * slot_w
    ys_hi = ys.astype(BF16)
    ys_lo = (ys - ys_hi.astype(F32)).astype(BF16)
    over_slots = (((0,), (0,)), ((), ()))
    moe = lax.dot_general(onehot, ys_hi, over_slots, preferred_element_type=F32) \
        + lax.dot_general(onehot, ys_lo, over_slots, preferred_element_type=F32)
    o_ref[...] = _rms(h_ref[...] + moe, gn_ref[...])


def _combine(runs, h, lpos, rt, norm_final, y, n_tok):
    n_tiles = n_tok // TT
    grid_spec = pltpu.PrefetchScalarGridSpec(
        num_scalar_prefetch=1,
        grid=(n_tiles,),
        in_specs=[pl.BlockSpec((TT, D_MODEL), lambda i, runs_ref: (i, 0)),
                  pl.BlockSpec((2, TT), lambda i, runs_ref: (0, i)),
                  pl.BlockSpec((SUBLANES, TT), lambda i, runs_ref: (0, i)),
                  pl.BlockSpec((1, D_MODEL), lambda i, runs_ref: (0, 0)),
                  pl.BlockSpec(memory_space=pl.ANY)],
        out_specs=pl.BlockSpec((TT, D_MODEL), lambda i, runs_ref: (i, 0)),
        scratch_shapes=[pltpu.VMEM((2, TILE_SLOTS * TILE_ROWS, LANES), F32),
                        pltpu.SemaphoreType.DMA((2,))],
    )
    return pl.pallas_call(
        functools.partial(_combine_kernel, n_tiles=n_tiles),
        grid_spec=grid_spec,
        out_shape=jax.ShapeDtypeStruct((n_tok, D_MODEL), F32),
        compiler_params=pltpu.CompilerParams(dimension_semantics=("arbitrary",),
                                             vmem_limit_bytes=VMEM_LIMIT),
        name="combine",
    )(runs.reshape(-1), h, lpos, rt, norm_final.reshape(1, D_MODEL).astype(F32), y)


def kernel(x, norm_mix, w_in, conv_w, conv_b, w_rg_a, b_rg_a, w_rg_x, b_rg_x, rg_lambda, rg_norm, w_alpha_up, b_alpha, gla_norm, w_out, norm_ffn, w_router_group, b_router_group, w_router_expert, b_router_expert, w_exp_gate, w_exp_up, w_exp_down, norm_final):
    bsz, seq, d = x.shape
    assert d == D_MODEL and norm_mix.shape[0] == 1, "single-layer model of width D_MODEL expected"
    n_tok = bsz * seq
    nb_max = (2 * n_tok + N_EXPERTS * (BM - 1)) // BM

    x2 = x.reshape(n_tok, d)
    xr, yr, q, k, v, g, la = _inproj(x2, norm_mix[0], w_in[0], w_alpha_up[0], b_alpha[0])
    y_rg = _rglru(xr, yr, conv_w[0], conv_b[0], w_rg_a[0], b_rg_a[0], w_rg_x[0], b_rg_x[0],
                  rg_lambda[0], rg_norm[0], bsz, seq)
    y_gla = _gla(q, k, v, g, la, gla_norm[0], bsz, seq)
    h, hn, rt = _outproj(y_rg, y_gla, x2, w_out[0], norm_ffn[0], w_router_group[0],
                         b_router_group[0], w_router_expert[0], b_router_expert[0])
    lpos, meta, runs = _plan(rt, n_tok, nb_max)
    xs = _dispatch(runs, meta, hn, lpos, n_tok, nb_max)
    y = _experts(meta, xs, w_exp_gate[0], w_exp_up[0], w_exp_down[0], nb_max)
    out = _combine(runs, h, lpos, rt, norm_final, y, n_tok)
    return out.reshape(bsz, seq, d)
```

```python
import functools

import jax
import jax.numpy as jnp
from jax import lax
from jax.experimental import pallas as pl
from jax.experimental.pallas import tpu as pltpu

F32 = jnp.float32
BF16 = jnp.bfloat16

D_MODEL = 1024
RG_WIDTH = 512
RG_BLOCKS = 8
RG_BLOCK = 64
CONV_WIDTH = 4
C_RG = 8.0
GLA_HEADS = 4
GLA_VAL = 512
GLA_KEY = 256
GLA_DK = 64
GLA_DV = 128
GATE_RANK = 16
GATE_NORM = 16.0
CHUNK = 64
N_GROUPS = 4
EXPERTS_PER_GROUP = 8
N_EXPERTS = 32
EPS = 1e-6

LANES = 128
SUBLANES = 8
VMEM_LIMIT = 56 * 1024 * 1024

TM_PROJ = 512
TM_INPROJ = 1024
TS_RG = 256
TS_GLA = 256
BM = 256
TT = 512
TILE_SLOTS = 2 * TT
RUN_CHUNK_LOG2 = 6
RUN_FIELDS = 4


def _dot(a, b):
    return jnp.dot(a, b, preferred_element_type=F32)


def _softplus(z):
    return jnp.maximum(z, 0.0) + jnp.log1p(jnp.exp(-jnp.abs(z)))


def _rms(x, g):
    return x * lax.rsqrt(jnp.mean(x * x, axis=-1, keepdims=True) + EPS) * g


TILE_ROWS = D_MODEL // LANES


def _copy_run(src, src_row, dst, dst_row, n_rows, sem):
    def piece(off, rows):
        s = src.at[pl.ds(pl.multiple_of((src_row + off) * TILE_ROWS, TILE_ROWS), rows * TILE_ROWS), :]
        d = dst.at[pl.ds(pl.multiple_of((dst_row + off) * TILE_ROWS, TILE_ROWS), rows * TILE_ROWS), :]
        pltpu.make_async_copy(s, d, sem).start()

    chunk = 1 << RUN_CHUNK_LOG2
    n_chunks = lax.shift_right_logical(n_rows, RUN_CHUNK_LOG2)
    lax.fori_loop(0, n_chunks, lambda c, carry: (piece(c * chunk, chunk), carry)[1], 0)
    off = n_chunks * chunk
    for k in reversed(range(RUN_CHUNK_LOG2)):
        bit = jnp.bitwise_and(n_rows, 1 << k)

        @pl.when(bit != 0)
        def _():
            piece(off, 1 << k)

        off = off + bit


def _wait_rows(ref, n_rows, sem):
    view = ref.at[pl.ds(0, n_rows * TILE_ROWS), :]
    pltpu.make_async_copy(view, view, sem).wait()


def _store_token_tiles(ref, val):
    n = val.shape[0]
    for c in range(TILE_ROWS):
        ref[pl.ds(c, n, stride=TILE_ROWS), :] = val[:, c * LANES:(c + 1) * LANES]


def _load_token_tiles(ref, n):
    return jnp.concatenate([ref[pl.ds(c, n, stride=TILE_ROWS), :] for c in range(TILE_ROWS)], axis=1)


def _inproj_kernel(x_ref, g_ref, wxr, wyr, wq, wk, wv, wg, wal, wup, bal,
                   xr_o, yr_o, q_o, k_o, v_o, g_o, la_o):
    hn = _rms(x_ref[...], g_ref[...]).astype(BF16)
    xr_o[...] = _dot(hn, wxr[...])
    yr_o[...] = _dot(hn, wyr[...])
    q_o[...] = _dot(hn, wq[...])
    k_o[...] = _dot(hn, wk[...])
    v_o[...] = _dot(hn, wv[...])
    g_o[...] = _dot(hn, wg[...])
    a3 = _dot(hn, wal[...])
    a3_hi = a3.astype(BF16).astype(F32)
    lane = lax.broadcasted_iota(jnp.int32, a3.shape, 1)
    use_low = jnp.logical_and(lane >= GATE_RANK, lane < 2 * GATE_RANK)
    z = _dot(jnp.where(use_low, a3 - a3_hi, a3_hi).astype(BF16), wup[...]) + bal[...]
    log_sig = jnp.minimum(z, 0.0) - jnp.log1p(jnp.exp(-jnp.abs(z)))
    la_o[...] = log_sig * (1.0 / GATE_NORM)


def _inproj(x2, norm_mix, w_in, w_alpha_up, b_alpha):
    t = x2.shape[0]
    c = [0, 512, 1024, 1280, 1536, 2048, 2560, 2576]
    wb = w_in.astype(BF16)
    segs = [wb[:, c[i]:c[i + 1]] for i in range(6)]
    w_low = wb[:, c[6]:c[7]]
    wal = jnp.pad(jnp.concatenate([w_low, w_low, w_low], axis=1), ((0, 0), (0, LANES - 3 * GATE_RANK)))
    up = w_alpha_up.astype(F32)
    up_hi = up.astype(BF16)
    up_lo = (up - up_hi.astype(F32)).astype(BF16)
    wup = jnp.pad(jnp.concatenate([up_hi, up_hi, up_lo], axis=0), ((0, LANES - 3 * GATE_RANK), (0, 0)))
    widths = [512, 512, 256, 256, 512, 512, 256]
    row = lambda i: (i, 0)
    fixed = lambda i: (0, 0)
    in_specs = [pl.BlockSpec((TM_INPROJ, D_MODEL), row), pl.BlockSpec((1, D_MODEL), fixed)]
    in_specs += [pl.BlockSpec((D_MODEL, w), fixed) for w in widths[:6]]
    in_specs += [pl.BlockSpec((D_MODEL, LANES), fixed), pl.BlockSpec((LANES, GLA_KEY), fixed),
                 pl.BlockSpec((1, GLA_KEY), fixed)]
    return pl.pallas_call(
        _inproj_kernel,
        grid=(t // TM_INPROJ,),
        in_specs=in_specs,
        out_specs=[pl.BlockSpec((TM_INPROJ, w), row) for w in widths],
        out_shape=[jax.ShapeDtypeStruct((t, w), F32) for w in widths],
        compiler_params=pltpu.CompilerParams(dimension_semantics=("arbitrary",),
                                             vmem_limit_bytes=VMEM_LIMIT),
        name="inproj",
    )(x2, norm_mix.reshape(1, D_MODEL), *segs, wal, wup, b_alpha.reshape(1, GLA_KEY))


def _rglru_kernel(xr_ref, yr_ref, cw_ref, cb_ref, wa_ref, ba_ref, wx_ref, bx_ref, lam_ref, gn_ref,
                  o_ref, tail_ref, hc_ref):
    ts = xr_ref.shape[0]
    n_groups = ts // SUBLANES
    grouped = (n_groups, SUBLANES, RG_WIDTH)
    sub = lax.broadcasted_iota(jnp.int32, grouped, 1)

    @pl.when(pl.program_id(1) == 0)
    def _():
        tail_ref[...] = jnp.zeros_like(tail_ref)
        hc_ref[...] = jnp.zeros_like(hc_ref)

    x = xr_ref[...]
    x_grp = x.reshape(grouped)
    x_prev = jnp.concatenate([tail_ref[...], x[:ts - SUBLANES, :]], axis=0).reshape(grouped)
    tail_ref[...] = x[ts - SUBLANES:ts, :]
    cw = cw_ref[...]
    xc = cb_ref[...] + cw[CONV_WIDTH - 1:CONV_WIDTH, :] * x
    for s in range(1, CONV_WIDTH):
        mixed = jnp.where(sub >= SUBLANES - s, x_prev, x_grp)
        shifted = pltpu.roll(mixed, s, axis=1)
        xc = xc + cw[CONV_WIDTH - 1 - s:CONV_WIDTH - s, :] * shifted.reshape(ts, RG_WIDTH)

    xb = xc.astype(BF16)
    r = jax.nn.sigmoid(_dot(xb, wa_ref[...]) + ba_ref[...])
    gate_i = jax.nn.sigmoid(_dot(xb, wx_ref[...]) + bx_ref[...])
    log_a = (-C_RG) * r * _softplus(-lam_ref[...])
    a = jnp.exp(log_a)
    u = jnp.sqrt(jnp.tanh(-log_a) * (1.0 + a * a)) * (gate_i * xc)

    a = a.reshape(grouped)
    u = u.reshape(grouped)
    d = 1
    while d < SUBLANES:
        keep = sub >= d
        a_sh = jnp.where(keep, pltpu.roll(a, d, axis=1), 1.0)
        u_sh = jnp.where(keep, pltpu.roll(u, d, axis=1), 0.0)
        u = a * u_sh + u
        a = a * a_sh
        d *= 2
    carry = hc_ref[0:1, :]
    groups = []
    for j in range(n_groups):
        h_j = u[j] + a[j] * carry
        groups.append(h_j)
        carry = h_j[SUBLANES - 1:SUBLANES, :]
    h = jnp.concatenate(groups, axis=0)
    hc_ref[...] = jnp.broadcast_to(carry, hc_ref.shape)

    y = h * jax.nn.gelu(yr_ref[...])
    o_ref[...] = _rms(y, gn_ref[...])


def _block_diag(w):
    eye = jnp.eye(RG_BLOCKS, dtype=w.dtype)
    return jnp.einsum('hij,hg->higj', w, eye).reshape(RG_WIDTH, RG_WIDTH)


def _rglru(xr, yr, conv_w, conv_b, w_a, b_a, w_x, b_x, lam, gn, bsz, seq):
    nt = seq // TS_RG
    row = lambda b, i: (b * nt + i, 0)
    fixed = lambda b, i: (0, 0)
    vec = lambda v: v.reshape(1, RG_WIDTH).astype(F32)
    return pl.pallas_call(
        _rglru_kernel,
        grid=(bsz, nt),
        in_specs=[pl.BlockSpec((TS_RG, RG_WIDTH), row), pl.BlockSpec((TS_RG, RG_WIDTH), row),
                  pl.BlockSpec((CONV_WIDTH, RG_WIDTH), fixed), pl.BlockSpec((1, RG_WIDTH), fixed),
                  pl.BlockSpec((RG_WIDTH, RG_WIDTH), fixed), pl.BlockSpec((1, RG_WIDTH), fixed),
                  pl.BlockSpec((RG_WIDTH, RG_WIDTH), fixed), pl.BlockSpec((1, RG_WIDTH), fixed),
                  pl.BlockSpec((1, RG_WIDTH), fixed), pl.BlockSpec((1, RG_WIDTH), fixed)],
        out_specs=pl.BlockSpec((TS_RG, RG_WIDTH), row),
        out_shape=jax.ShapeDtypeStruct((bsz * seq, RG_WIDTH), F32),
        scratch_shapes=[pltpu.VMEM((SUBLANES, RG_WIDTH), F32),
                        pltpu.VMEM((SUBLANES, RG_WIDTH), F32)],
        compiler_params=pltpu.CompilerParams(dimension_semantics=("arbitrary", "arbitrary"),
                                             vmem_limit_bytes=VMEM_LIMIT),
        name="rglru",
    )(xr, yr, conv_w.astype(F32), vec(conv_b), _block_diag(w_a).astype(BF16), vec(b_a),
      _block_diag(w_x).astype(BF16), vec(b_x), vec(lam), vec(gn))


def _gla_kernel(q_ref, k_ref, v_ref, g_ref, la_ref, gn_ref, o_ref, st_ref):
    bsz, ts = q_ref.shape[0], q_ref.shape[1]
    n_chunks = ts // CHUNK

    @pl.when(pl.program_id(0) == 0)
    def _():
        st_ref[...] = jnp.zeros_like(st_ref)

    ri = lax.broadcasted_iota(jnp.int32, (ts, ts), 0)
    ci = lax.broadcasted_iota(jnp.int32, (ts, ts), 1)
    chunk_bits = CHUNK.bit_length() - 1
    same_chunk = lax.shift_right_logical(ri, chunk_bits) == lax.shift_right_logical(ci, chunk_bits)
    prefix = jnp.where(jnp.logical_and(same_chunk, ri >= ci), 1.0, 0.0).astype(BF16)
    total = jnp.where(same_chunk, 1.0, 0.0).astype(BF16)
    causal = (ri >= ci)[:CHUNK, :CHUNK]
    scale = GLA_DK ** -0.5
    gn = gn_ref[...]
    nt_dims = (((1,), (1,)), ((), ()))
    tn_dims = (((0,), (0,)), ((), ()))

    for bi in range(bsz):
        la = la_ref[bi]
        la_hi = la.astype(BF16)
        la_lo = (la - la_hi.astype(F32)).astype(BF16)
        b = _dot(prefix, la_hi) + _dot(prefix, la_lo)
        b_tot = _dot(total, la_hi) + _dot(total, la_lo)
        kk = k_ref[bi]
        q_s = (q_ref[bi] * scale) * jnp.exp(b)
        k_s = kk * jnp.exp(-b)
        k_end = kk * jnp.exp(b_tot - b)
        decay = jnp.exp(b_tot)

        o_intra, kv, q_heads = [], [], []
        for c in range(n_chunks):
            rows = slice(c * CHUNK, (c + 1) * CHUNK)
            o_c, kv_c, q_c = [], [], []
            for h in range(GLA_HEADS):
                ks = slice(h * GLA_DK, (h + 1) * GLA_DK)
                vs = slice(h * GLA_DV, (h + 1) * GLA_DV)
                qh = q_s[rows, ks].astype(BF16)
                vb = v_ref[bi, rows, vs].astype(BF16)
                att = lax.dot_general(qh, k_s[rows, ks].astype(BF16), nt_dims, preferred_element_type=F32)
                att = jnp.where(causal, att, 0.0).astype(BF16)
                o_c.append(_dot(att, vb))
                kv_c.append(lax.dot_general(vb, k_end[rows, ks].astype(BF16), tn_dims,
                                            preferred_element_type=F32))
                q_c.append(qh)
            o_intra.append(o_c)
            q_heads.append(q_c)
            kv.append(jnp.concatenate(kv_c, axis=1))

        st = st_ref[bi]
        for c in range(n_chunks):
            rows = slice(c * CHUNK, (c + 1) * CHUNK)
            for h in range(GLA_HEADS):
                ks = slice(h * GLA_DK, (h + 1) * GLA_DK)
                vs = slice(h * GLA_DV, (h + 1) * GLA_DV)
                o = o_intra[c][h] + lax.dot_general(q_heads[c][h], st[:, ks].astype(BF16), nt_dims,
                                                    preferred_element_type=F32)
                o_ref[bi, rows, vs] = _rms(o, gn) * jax.nn.silu(g_ref[bi, rows, vs])
            st = decay[c * CHUNK:c * CHUNK + 1, :] * st + kv[c]
        st_ref[bi] = st


def _gla(q, k, v, g, la, gn, bsz, seq):
    nt = seq // TS_GLA
    blk = lambda w: pl.BlockSpec((bsz, TS_GLA, w), lambda i: (0, i, 0))
    r3 = lambda a: a.reshape(bsz, seq, a.shape[-1])
    out = pl.pallas_call(
        _gla_kernel,
        grid=(nt,),
        in_specs=[blk(GLA_KEY), blk(GLA_KEY), blk(GLA_VAL), blk(GLA_VAL), blk(GLA_KEY),
                  pl.BlockSpec((1, GLA_DV), lambda i: (0, 0))],
        out_specs=blk(GLA_VAL),
        out_shape=jax.ShapeDtypeStruct((bsz, seq, GLA_VAL), F32),
        scratch_shapes=[pltpu.VMEM((bsz, GLA_DV, GLA_KEY), F32)],
        compiler_params=pltpu.CompilerParams(dimension_semantics=("arbitrary",),
                                             vmem_limit_bytes=VMEM_LIMIT),
        name="gla",
    )(r3(q), r3(k), r3(v), r3(g), r3(la), gn.reshape(1, GLA_DV).astype(F32))
    return out.reshape(bsz * seq, GLA_VAL)


def _outproj_kernel(yrg_ref, ygla_ref, x_ref, wo1_ref, wo2_ref, gn_ref, wr_ref, br_ref,
                    h_o, hn_o, rt_o):
    tm = x_ref.shape[0]
    h = x_ref[...] + _dot(yrg_ref[...].astype(BF16), wo1_ref[...]) \
        + _dot(ygla_ref[...].astype(BF16), wo2_ref[...])
    h_o[...] = h
    hn = _rms(h, gn_ref[...])
    hn_hi = hn.astype(BF16)
    hn_o[...] = hn_hi
    hn_lo = (hn - hn_hi.astype(F32)).astype(BF16)
    hi_parts = _dot(hn_hi, wr_ref[...])
    logits = hi_parts[:, :LANES] + hi_parts[:, LANES:] + _dot(hn_lo, wr_ref[:, :LANES]) + br_ref[...]

    lane = lax.broadcasted_iota(jnp.int32, (tm, LANES), 1).astype(F32)
    neg = -jnp.inf
    glog = jnp.where(lane < N_GROUPS, logits, neg)
    gmax = jnp.max(glog, axis=-1, keepdims=True)
    gidx = jnp.min(jnp.where(glog == gmax, lane, float(LANES)), axis=-1, keepdims=True)
    g_w = 1.0 / jnp.sum(jnp.exp(glog - gmax), axis=-1, keepdims=True)
    lo = N_GROUPS + gidx * EXPERTS_PER_GROUP
    in_group = jnp.logical_and(lane >= lo, lane < lo + EXPERTS_PER_GROUP)
    le = jnp.where(in_group, logits, neg)
    m1 = jnp.max(le, axis=-1, keepdims=True)
    i1 = jnp.min(jnp.where(le == m1, lane, float(LANES)), axis=-1, keepdims=True)
    le2 = jnp.where(lane == i1, neg, le)
    m2 = jnp.max(le2, axis=-1, keepdims=True)
    i2 = jnp.min(jnp.where(le2 == m2, lane, float(LANES)), axis=-1, keepdims=True)
    t2 = jnp.exp(m2 - m1)
    w1 = g_w / (1.0 + t2)
    w2 = g_w * t2 / (1.0 + t2)
    info = jnp.where(lane == 0.0, i1 - N_GROUPS,
                     jnp.where(lane == 1.0, i2 - N_GROUPS,
                               jnp.where(lane == 2.0, w1, jnp.where(lane == 3.0, w2, 0.0))))
    rt_o[...] = info.T[0:SUBLANES, :]


def _outproj(y_rg, y_gla, x2, w_out, norm_ffn, w_rg, b_rg, w_re, b_re):
    t = x2.shape[0]
    wo = w_out.astype(BF16)
    wr = jnp.pad(jnp.concatenate([w_rg, w_re], axis=1).astype(F32),
                 ((0, 0), (0, LANES - N_GROUPS - N_EXPERTS)))
    wr_hi = wr.astype(BF16)
    wr = jnp.concatenate([wr_hi, (wr - wr_hi.astype(F32)).astype(BF16)], axis=1)
    br = jnp.pad(jnp.concatenate([b_rg, b_re]).astype(F32), (0, LANES - N_GROUPS - N_EXPERTS))
    row = lambda i: (i, 0)
    fixed = lambda i: (0, 0)
    return pl.pallas_call(
        _outproj_kernel,
        grid=(t // TM_PROJ,),
        in_specs=[pl.BlockSpec((TM_PROJ, RG_WIDTH), row), pl.BlockSpec((TM_PROJ, GLA_VAL), row),
                  pl.BlockSpec((TM_PROJ, D_MODEL), row),
                  pl.BlockSpec((RG_WIDTH, D_MODEL), fixed), pl.BlockSpec((GLA_VAL, D_MODEL), fixed),
                  pl.BlockSpec((1, D_MODEL), fixed), pl.BlockSpec((D_MODEL, 2 * LANES), fixed),
                  pl.BlockSpec((1, LANES), fixed)],
        out_specs=[pl.BlockSpec((TM_PROJ, D_MODEL), row), pl.BlockSpec((TM_PROJ, D_MODEL), row),
                   pl.BlockSpec((SUBLANES, TM_PROJ), lambda i: (0, i))],
        out_shape=[jax.ShapeDtypeStruct((t, D_MODEL), F32), jax.ShapeDtypeStruct((t, D_MODEL), BF16),
                   jax.ShapeDtypeStruct((SUBLANES, t), F32)],
        compiler_params=pltpu.CompilerParams(dimension_semantics=("arbitrary",),
                                             vmem_limit_bytes=VMEM_LIMIT),
        name="outproj",
    )(y_rg, y_gla, x2, wo[:RG_WIDTH], wo[RG_WIDTH:], norm_ffn.reshape(1, D_MODEL).astype(F32),
      wr, br.reshape(1, LANES))


def _plan_kernel(rt_ref, lpos_o, meta_o, runs_o, tri_ref, *, n_tok, nb_max):
    n_tiles = n_tok // TT
    esub = lax.broadcasted_iota(jnp.int32, (N_EXPERTS, TT), 0).astype(F32)

    def onehots(i):
        off = pl.multiple_of(i * TT, TT)
        e1 = rt_ref[0:1, pl.ds(off, TT)]
        e2 = rt_ref[1:2, pl.ds(off, TT)]
        m1 = jnp.where(esub == e1, 1.0, 0.0)
        m2 = jnp.where(esub == e2, 1.0, 0.0)
        return off, m1, m2

    def count_body(i, cnt):
        _, m1, m2 = onehots(i)
        return cnt + jnp.sum(m1 + m2, axis=1, keepdims=True)

    counts = lax.fori_loop(0, n_tiles, count_body, jnp.zeros((N_EXPERTS, 1), F32))
    nblk = jnp.floor((counts + (BM - 1)) * (1.0 / BM))
    ei = lax.broadcasted_iota(jnp.int32, (N_EXPERTS, N_EXPERTS), 0)
    ej = lax.broadcasted_iota(jnp.int32, (N_EXPERTS, N_EXPERTS), 1)
    nblk_row = jnp.sum(jnp.where(ei == ej, nblk, 0.0), axis=0, keepdims=True)
    bstart = jnp.sum(jnp.where(ej < ei, nblk_row, 0.0), axis=1, keepdims=True)
    bend = bstart + nblk
    n_used = jnp.sum(nblk, axis=0, keepdims=True)

    meta_w = meta_o.shape[1]
    blane = lax.broadcasted_iota(jnp.int32, (N_EXPERTS, meta_w), 1).astype(F32)
    owner = jnp.sum(jnp.where(bend <= blane, 1.0, 0.0), axis=0, keepdims=True)
    owner = jnp.minimum(owner, N_EXPERTS - 1.0)
    lane1 = lax.broadcasted_iota(jnp.int32, (1, meta_w), 1)
    meta_o[0:1, :] = jnp.where(lane1 == nb_max, n_used, owner).astype(jnp.int32)
    nonempty = jnp.where(nblk > 0.0, 1.0, 0.0)
    meta_o[1:2, :] = jnp.sum(jnp.where(bend <= blane, nonempty, 0.0), axis=0,
                             keepdims=True).astype(jnp.int32)
    owned = jnp.logical_and(bstart <= blane, blane < bend)
    next_start = jnp.sum(jnp.where(owned, bend, 0.0), axis=0, keepdims=True)
    next_owner = jnp.sum(jnp.where(bend <= next_start, 1.0, 0.0), axis=0, keepdims=True)
    meta_o[2:3, :] = jnp.where(next_start < n_used, next_owner, -1.0).astype(jnp.int32)

    ti = lax.broadcasted_iota(jnp.int32, (TT, TT), 0)
    tj = lax.broadcasted_iota(jnp.int32, (TT, TT), 1)
    tri_ref[...] = jnp.where(ti <= tj, 1.0, 0.0).astype(BF16)
    tlane = lax.broadcasted_iota(jnp.int32, (N_EXPERTS, LANES), 1)

    def tile_body(i, carry):
        first_slot, t_cnt, t_slot, t_rank = carry
        off, m1, m2 = onehots(i)
        m = m1 + m2
        incl = _dot(m.astype(BF16), tri_ref[...])
        cnt = incl[:, TT - 1:TT]
        cnt_row = jnp.sum(jnp.where(ei == ej, cnt, 0.0), axis=0, keepdims=True)
        first_rank = jnp.sum(jnp.where(ej < ei, cnt_row, 0.0), axis=1, keepdims=True)
        rank = first_rank + incl - m
        lpos_o[0:1, pl.ds(off, TT)] = jnp.sum(m1 * rank, axis=0, keepdims=True).astype(jnp.int32)
        lpos_o[1:2, pl.ds(off, TT)] = jnp.sum(m2 * rank, axis=0, keepdims=True).astype(jnp.int32)
        here = tlane == i
        return (first_slot + cnt, jnp.where(here, cnt, t_cnt), jnp.where(here, first_slot, t_slot),
                jnp.where(here, first_rank, t_rank))

    zeros = jnp.zeros((N_EXPERTS, LANES), F32)
    _, t_cnt, t_slot, t_rank = lax.fori_loop(0, n_tiles, tile_body,
                                             (bstart * float(BM), zeros, zeros, zeros))
    runs_o[0] = t_cnt.astype(jnp.int32)
    runs_o[1] = t_slot.astype(jnp.int32)
    runs_o[2] = t_rank.astype(jnp.int32)
    pad = jnp.where(tlane == 0, bstart * float(BM) + counts,
                    jnp.where(tlane == 1, nblk * float(BM) - counts, 0.0))
    runs_o[3] = pad.astype(jnp.int32)


def _plan(rt, n_tok, nb_max):
    assert n_tok // TT <= LANES, "run tables hold one token tile per lane"
    meta_w = ((nb_max + 1 + LANES - 1) // LANES) * LANES
    return pl.pallas_call(
        functools.partial(_plan_kernel, n_tok=n_tok, nb_max=nb_max),
        out_shape=[jax.ShapeDtypeStruct((2, n_tok), jnp.int32),
                   jax.ShapeDtypeStruct((3, meta_w), jnp.int32),
                   jax.ShapeDtypeStruct((RUN_FIELDS, N_EXPERTS, LANES), jnp.int32)],
        scratch_shapes=[pltpu.VMEM((TT, TT), BF16)],
        compiler_params=pltpu.CompilerParams(vmem_limit_bytes=VMEM_LIMIT),
        name="plan",
    )(rt)


def _run_entry(runs_ref, field, expert, tile):
    return runs_ref[(field * N_EXPERTS + expert) * LANES + tile]


def _dispatch_kernel(runs_ref, meta_ref, hn_ref, lpos_ref, xs_hbm, xsbuf, zbuf, sem, zsem,
                     *, n_tiles, nb_max, n_pad):
    i = pl.program_id(0)
    slot = lax.rem(i, 2)

    @pl.when(i == 0)
    def _():
        zbuf[...] = jnp.zeros_like(zbuf)
        for e in range(N_EXPERTS):
            _copy_run(zbuf, 0, xs_hbm, _run_entry(runs_ref, 3, e, 0), _run_entry(runs_ref, 3, e, 1),
                      zsem)

        def unused_block(blk, carry):
            dst = xs_hbm.at[pl.ds(pl.multiple_of(blk * (BM * TILE_ROWS), BM * TILE_ROWS),
                                  BM * TILE_ROWS), :]
            pltpu.make_async_copy(zbuf, dst, zsem).start()
            return carry

        lax.fori_loop(meta_ref[nb_max], nb_max, unused_block, 0)

    @pl.when(i >= 2)
    def _():
        _wait_rows(xsbuf.at[slot], TILE_SLOTS, sem.at[slot])

    rank = lax.broadcasted_iota(jnp.int32, (TILE_SLOTS, TT), 0)
    onehot = jnp.where(rank == lpos_ref[0:1, :], 1.0, jnp.where(rank == lpos_ref[1:2, :], 1.0, 0.0))
    _store_token_tiles(xsbuf.at[slot], _dot(onehot.astype(BF16), hn_ref[...]))
    for e in range(N_EXPERTS):
        _copy_run(xsbuf.at[slot], _run_entry(runs_ref, 2, e, i), xs_hbm, _run_entry(runs_ref, 1, e, i),
                  _run_entry(runs_ref, 0, e, i), sem.at[slot])

    @pl.when(i == n_tiles - 1)
    def _():
        _wait_rows(xsbuf.at[slot], TILE_SLOTS, sem.at[slot])
        if n_tiles > 1:
            _wait_rows(xsbuf.at[1 - slot], TILE_SLOTS, sem.at[1 - slot])
        _wait_rows(xs_hbm, n_pad, zsem)


def _dispatch(runs, meta, hn, lpos, n_tok, nb_max):
    n_tiles = n_tok // TT
    cap = nb_max * BM
    grid_spec = pltpu.PrefetchScalarGridSpec(
        num_scalar_prefetch=2,
        grid=(n_tiles,),
        in_specs=[pl.BlockSpec((TT, D_MODEL), lambda i, runs_ref, meta_ref: (i, 0)),
                  pl.BlockSpec((2, TT), lambda i, runs_ref, meta_ref: (0, i))],
        out_specs=pl.BlockSpec(memory_space=pl.ANY),
        scratch_shapes=[pltpu.VMEM((2, TILE_SLOTS * TILE_ROWS, LANES), F32),
                        pltpu.VMEM((BM * TILE_ROWS, LANES), F32),
                        pltpu.SemaphoreType.DMA((2,)), pltpu.SemaphoreType.DMA],
    )
    return pl.pallas_call(
        functools.partial(_dispatch_kernel, n_tiles=n_tiles, nb_max=nb_max, n_pad=cap - 2 * n_tok),
        grid_spec=grid_spec,
        out_shape=jax.ShapeDtypeStruct((cap * TILE_ROWS, LANES), F32),
        compiler_params=pltpu.CompilerParams(dimension_semantics=("arbitrary",),
                                             vmem_limit_bytes=VMEM_LIMIT),
        name="dispatch",
    )(runs.reshape(-1), meta.reshape(-1), hn, lpos)


def _experts_kernel(meta_ref, xs_ref, wg_hbm, wu_hbm, wd_hbm, y_ref, wf32, wbf, wsem, *, nb_max, meta_w):
    b = pl.program_id(0)
    n_used = meta_ref[nb_max]

    def start_weights(expert, buf):
        for j, w_hbm in enumerate((wg_hbm, wu_hbm, wd_hbm)):
            pltpu.make_async_copy(w_hbm.at[expert], wf32.at[buf, j], wsem.at[buf]).start(priority=1)

    @pl.when(b < n_used)
    def _():
        owner = meta_ref[b]

        @pl.when(b == 0)
        def _():
            start_weights(owner, 0)

        @pl.when(jnp.logical_or(b == 0, owner != meta_ref[jnp.maximum(b - 1, 0)]))
        def _():
            buf = jnp.bitwise_and(meta_ref[meta_w + b], 1)
            pltpu.make_async_copy(wf32.at[buf], wf32.at[buf], wsem.at[buf]).wait()
            next_owner = meta_ref[2 * meta_w + b]

            @pl.when(next_owner >= 0)
            def _():
                start_weights(next_owner, 1 - buf)

            for j in range(3):
                wbf[j] = wf32[buf, j].astype(BF16)

        x = _load_token_tiles(xs_ref, BM).astype(BF16)
        mid = (jax.nn.gelu(_dot(x, wbf[0])) * _dot(x, wbf[1])).astype(BF16)
        _store_token_tiles(y_ref, _dot(mid, wbf[2]))

    @pl.when(b >= n_used)
    def _():
        y_ref[...] = jnp.zeros_like(y_ref)


def _experts(meta, xs, w_gate, w_up, w_down, nb_max):
    whole = pl.BlockSpec(memory_space=pl.ANY)
    rows = pl.BlockSpec((BM * TILE_ROWS, LANES), lambda b, meta_ref: (b, 0))
    grid_spec = pltpu.PrefetchScalarGridSpec(
        num_scalar_prefetch=1,
        grid=(nb_max,),
        in_specs=[rows, whole, whole, whole],
        out_specs=rows,
        scratch_shapes=[pltpu.VMEM((2, 3, D_MODEL, D_MODEL), F32),
                        pltpu.VMEM((3, D_MODEL, D_MODEL), BF16),
                        pltpu.SemaphoreType.DMA((2,))],
    )
    return pl.pallas_call(
        functools.partial(_experts_kernel, nb_max=nb_max, meta_w=meta.shape[1]),
        grid_spec=grid_spec,
        out_shape=jax.ShapeDtypeStruct(xs.shape, F32),
        compiler_params=pltpu.CompilerParams(dimension_semantics=("arbitrary",),
                                             vmem_limit_bytes=VMEM_LIMIT),
        name="experts",
    )(meta.reshape(-1), xs, w_gate, w_up, w_down)


def _combine_kernel(runs_ref, h_ref, lpos_ref, rt_ref, gn_ref, y_hbm, o_ref, ysbuf, sem, *, n_tiles):
    i = pl.program_id(0)
    slot = lax.rem(i, 2)

    def fetch(tile, buf):
        for e in range(N_EXPERTS):
            _copy_run(y_hbm, _run_entry(runs_ref, 1, e, tile), ysbuf.at[buf],
                      _run_entry(runs_ref, 2, e, tile), _run_entry(runs_ref, 0, e, tile), sem.at[buf])

    @pl.when(i == 0)
    def _():
        fetch(0, 0)

    @pl.when(i + 1 < n_tiles)
    def _():
        fetch(i + 1, 1 - slot)

    _wait_rows(ysbuf.at[slot], TILE_SLOTS, sem.at[slot])

    rank = lax.broadcasted_iota(jnp.int32, (TILE_SLOTS, TT), 0)
    first = rank == lpos_ref[0:1, :]
    second = rank == lpos_ref[1:2, :]
    slot_w = jnp.sum(jnp.where(first, rt_ref[2:3, :], 0.0) + jnp.where(second, rt_ref[3:4, :], 0.0),
                     axis=1, keepdims=True)
    onehot = jnp.where(first, 1.0, jnp.where(second, 1.0, 0.0)).astype(BF16)
    ys = _load_token_tiles(ysbuf.at[slot], TILE_SLOTS) * slot_w
    ys_hi = ys.astype(BF16)
    ys_lo = (ys - ys_hi.astype(F32)).astype(BF16)
    over_slots = (((0,), (0,)), ((), ()))
    moe = lax.dot_general(onehot, ys_hi, over_slots, preferred_element_type=F32) \
        + lax.dot_general(onehot, ys_lo, over_slots, preferred_element_type=F32)
    o_ref[...] = _rms(h_ref[...] + moe, gn_ref[...])


def _combine(runs, h, lpos, rt, norm_final, y, n_tok):
    n_tiles = n_tok // TT
    grid_spec = pltpu.PrefetchScalarGridSpec(
        num_scalar_prefetch=1,
        grid=(n_tiles,),
        in_specs=[pl.BlockSpec((TT, D_MODEL), lambda i, runs_ref: (i, 0)),
                  pl.BlockSpec((2, TT), lambda i, runs_ref: (0, i)),
                  pl.BlockSpec((SUBLANES, TT), lambda i, runs_ref: (0, i)),
                  pl.BlockSpec((1, D_MODEL), lambda i, runs_ref: (0, 0)),
                  pl.BlockSpec(memory_space=pl.ANY)],
        out_specs=pl.BlockSpec((TT, D_MODEL), lambda i, runs_ref: (i, 0)),
        scratch_shapes=[pltpu.VMEM((2, TILE_SLOTS * TILE_ROWS, LANES), F32),
                        pltpu.SemaphoreType.DMA((2,))],
    )
    return pl.pallas_call(
        functools.partial(_combine_kernel, n_tiles=n_tiles),
        grid_spec=grid_spec,
        out_shape=jax.ShapeDtypeStruct((n_tok, D_MODEL), F32),
        compiler_params=pltpu.CompilerParams(dimension_semantics=("arbitrary",),
                                             vmem_limit_bytes=VMEM_LIMIT),
        name="combine",
    )(runs.reshape(-1), h, lpos, rt, norm_final.reshape(1, D_MODEL).astype(F32), y)


def kernel(x, norm_mix, w_in, conv_w, conv_b, w_rg_a, b_rg_a, w_rg_x, b_rg_x, rg_lambda, rg_norm, w_alpha_up, b_alpha, gla_norm, w_out, norm_ffn, w_router_group, b_router_group, w_router_expert, b_router_expert, w_exp_gate, w_exp_up, w_exp_down, norm_final):
    bsz, seq, d = x.shape
    assert d == D_MODEL and norm_mix.shape[0] == 1, "single-layer model of width D_MODEL expected"
    n_tok = bsz * seq
    nb_max = (2 * n_tok + N_EXPERTS * (BM - 1)) // BM

    x2 = x.reshape(n_tok, d)
    xr, yr, q, k, v, g, la = _inproj(x2, norm_mix[0], w_in[0], w_alpha_up[0], b_alpha[0])
    y_rg = _rglru(xr, yr, conv_w[0], conv_b[0], w_rg_a[0], b_rg_a[0], w_rg_x[0], b_rg_x[0],
                  rg_lambda[0], rg_norm[0], bsz, seq)
    y_gla = _gla(q, k, v, g, la, gla_norm[0], bsz, seq)
    h, hn, rt = _outproj(y_rg, y_gla, x2, w_out[0], norm_ffn[0], w_router_group[0],
                         b_router_group[0], w_router_expert[0], b_router_expert[0])
    lpos, meta, runs = _plan(rt, n_tok, nb_max)
    xs = _dispatch(runs, meta, hn, lpos, n_tok, nb_max)
    y = _experts(meta, xs, w_exp_gate[0], w_exp_up[0], w_exp_down[0], nb_max)
    out = _combine(runs, h, lpos, rt, norm_final, y, n_tok)
    return out.reshape(bsz, seq, d)
```

```python
import functools

import jax
import jax.numpy as jnp
from jax import lax
from jax.experimental import pallas as pl
from jax.experimental.pallas import tpu as pltpu

F32 = jnp.float32
BF16 = jnp.bfloat16

D_MODEL = 1024
RG_WIDTH = 512
RG_BLOCKS = 8
RG_BLOCK = 64
CONV_WIDTH = 4
C_RG = 8.0
GLA_HEADS = 4
GLA_VAL = 512
GLA_KEY = 256
GLA_DK = 64
GLA_DV = 128
GATE_RANK = 16
GATE_NORM = 16.0
CHUNK = 64
N_GROUPS = 4
EXPERTS_PER_GROUP = 8
N_EXPERTS = 32
EPS = 1e-6

LANES = 128
SUBLANES = 8
VMEM_LIMIT = 56 * 1024 * 1024

TM_PROJ = 1024
TM_INPROJ = 1024
TS_RG = 512
TS_GLA = 512
BM = 256
TT = 512
TILE_SLOTS = 2 * TT
RUN_CHUNK_LOG2 = 6
RUN_FIELDS = 4


def _dot(a, b):
    return jnp.dot(a, b, preferred_element_type=F32)


def _softplus(z):
    return jnp.maximum(z, 0.0) + jnp.log1p(jnp.exp(-jnp.abs(z)))


def _rms(x, g):
    return x * lax.rsqrt(jnp.mean(x * x, axis=-1, keepdims=True) + EPS) * g


TILE_ROWS = D_MODEL // LANES


def _copy_run(src, src_row, dst, dst_row, n_rows, sem):
    def piece(off, rows):
        s = src.at[pl.ds(pl.multiple_of((src_row + off) * TILE_ROWS, TILE_ROWS), rows * TILE_ROWS), :]
        d = dst.at[pl.ds(pl.multiple_of((dst_row + off) * TILE_ROWS, TILE_ROWS), rows * TILE_ROWS), :]
        pltpu.make_async_copy(s, d, sem).start()

    chunk = 1 << RUN_CHUNK_LOG2
    n_chunks = lax.shift_right_logical(n_rows, RUN_CHUNK_LOG2)
    lax.fori_loop(0, n_chunks, lambda c, carry: (piece(c * chunk, chunk), carry)[1], 0)
    off = n_chunks * chunk
    for k in reversed(range(RUN_CHUNK_LOG2)):
        bit = jnp.bitwise_and(n_rows, 1 << k)

        @pl.when(bit != 0)
        def _():
            piece(off, 1 << k)

        off = off + bit


def _wait_rows(ref, n_rows, sem):
    view = ref.at[pl.ds(0, n_rows * TILE_ROWS), :]
    pltpu.make_async_copy(view, view, sem).wait()


def _store_token_tiles(ref, val):
    n = val.shape[0]
    for c in range(TILE_ROWS):
        ref[pl.ds(c, n, stride=TILE_ROWS), :] = val[:, c * LANES:(c + 1) * LANES]


def _load_token_tiles(ref, n):
    return jnp.concatenate([ref[pl.ds(c, n, stride=TILE_ROWS), :] for c in range(TILE_ROWS)], axis=1)


def _inproj_kernel(x_ref, g_ref, wxr, wyr, wq, wk, wv, wg, wal, wup, bal,
                   xr_o, yr_o, q_o, k_o, v_o, g_o, la_o):
    hn = _rms(x_ref[...], g_ref[...]).astype(BF16)
    xr_o[...] = _dot(hn, wxr[...])
    yr_o[...] = _dot(hn, wyr[...])
    q_o[...] = _dot(hn, wq[...])
    k_o[...] = _dot(hn, wk[...])
    v_o[...] = _dot(hn, wv[...])
    g_o[...] = _dot(hn, wg[...])
    a3 = _dot(hn, wal[...])
    a3_hi = a3.astype(BF16).astype(F32)
    lane = lax.broadcasted_iota(jnp.int32, a3.shape, 1)
    use_low = jnp.logical_and(lane >= GATE_RANK, lane < 2 * GATE_RANK)
    z = _dot(jnp.where(use_low, a3 - a3_hi, a3_hi).astype(BF16), wup[...]) + bal[...]
    log_sig = jnp.minimum(z, 0.0) - jnp.log1p(jnp.exp(-jnp.abs(z)))
    la_o[...] = log_sig * (1.0 / GATE_NORM)


def _inproj(x2, norm_mix, w_in, w_alpha_up, b_alpha):
    t = x2.shape[0]
    c = [0, 512, 1024, 1280, 1536, 2048, 2560, 2576]
    wb = w_in.astype(BF16)
    segs = [wb[:, c[i]:c[i + 1]] for i in range(6)]
    w_low = wb[:, c[6]:c[7]]
    wal = jnp.pad(jnp.concatenate([w_low, w_low, w_low], axis=1), ((0, 0), (0, LANES - 3 * GATE_RANK)))
    up = w_alpha_up.astype(F32)
    up_hi = up.astype(BF16)
    up_lo = (up - up_hi.astype(F32)).astype(BF16)
    wup = jnp.pad(jnp.concatenate([up_hi, up_hi, up_lo], axis=0), ((0, LANES - 3 * GATE_RANK), (0, 0)))
    widths = [512, 512, 256, 256, 512, 512, 256]
    row = lambda i: (i, 0)
    fixed = lambda i: (0, 0)
    in_specs = [pl.BlockSpec((TM_INPROJ, D_MODEL), row), pl.BlockSpec((1, D_MODEL), fixed)]
    in_specs += [pl.BlockSpec((D_MODEL, w), fixed) for w in widths[:6]]
    in_specs += [pl.BlockSpec((D_MODEL, LANES), fixed), pl.BlockSpec((LANES, GLA_KEY), fixed),
                 pl.BlockSpec((1, GLA_KEY), fixed)]
    return pl.pallas_call(
        _inproj_kernel,
        grid=(t // TM_INPROJ,),
        in_specs=in_specs,
        out_specs=[pl.BlockSpec((TM_INPROJ, w), row) for w in widths],
        out_shape=[jax.ShapeDtypeStruct((t, w), F32) for w in widths],
        compiler_params=pltpu.CompilerParams(dimension_semantics=("arbitrary",),
                                             vmem_limit_bytes=VMEM_LIMIT),
        name="inproj",
    )(x2, norm_mix.reshape(1, D_MODEL), *segs, wal, wup, b_alpha.reshape(1, GLA_KEY))


def _rglru_kernel(xr_ref, yr_ref, cw_ref, cb_ref, wa_ref, ba_ref, wx_ref, bx_ref, lam_ref, gn_ref,
                  o_ref, tail_ref, hc_ref):
    ts = xr_ref.shape[0]
    n_groups = ts // SUBLANES
    grouped = (n_groups, SUBLANES, RG_WIDTH)
    sub = lax.broadcasted_iota(jnp.int32, grouped, 1)

    @pl.when(pl.program_id(1) == 0)
    def _():
        tail_ref[...] = jnp.zeros_like(tail_ref)
        hc_ref[...] = jnp.zeros_like(hc_ref)

    x = xr_ref[...]
    x_grp = x.reshape(grouped)
    x_prev = jnp.concatenate([tail_ref[...], x[:ts - SUBLANES, :]], axis=0).reshape(grouped)
    tail_ref[...] = x[ts - SUBLANES:ts, :]
    cw = cw_ref[...]
    xc = cb_ref[...] + cw[CONV_WIDTH - 1:CONV_WIDTH, :] * x
    for s in range(1, CONV_WIDTH):
        mixed = jnp.where(sub >= SUBLANES - s, x_prev, x_grp)
        shifted = pltpu.roll(mixed, s, axis=1)
        xc = xc + cw[CONV_WIDTH - 1 - s:CONV_WIDTH - s, :] * shifted.reshape(ts, RG_WIDTH)

    xb = xc.astype(BF16)
    r = jax.nn.sigmoid(_dot(xb, wa_ref[...]) + ba_ref[...])
    gate_i = jax.nn.sigmoid(_dot(xb, wx_ref[...]) + bx_ref[...])
    log_a = (-C_RG) * r * _softplus(-lam_ref[...])
    a = jnp.exp(log_a)
    u = jnp.sqrt(jnp.tanh(-log_a) * (1.0 + a * a)) * (gate_i * xc)

    a = a.reshape(grouped)
    u = u.reshape(grouped)
    d = 1
    while d < SUBLANES:
        keep = sub >= d
        a_sh = jnp.where(keep, pltpu.roll(a, d, axis=1), 1.0)
        u_sh = jnp.where(keep, pltpu.roll(u, d, axis=1), 0.0)
        u = a * u_sh + u
        a = a * a_sh
        d *= 2
    carry = hc_ref[0:1, :]
    groups = []
    for j in range(n_groups):
        h_j = u[j] + a[j] * carry
        groups.append(h_j)
        carry = h_j[SUBLANES - 1:SUBLANES, :]
    h = jnp.concatenate(groups, axis=0)
    hc_ref[...] = jnp.broadcast_to(carry, hc_ref.shape)

    y = h * jax.nn.gelu(yr_ref[...])
    o_ref[...] = _rms(y, gn_ref[...])


def _block_diag(w):
    eye = jnp.eye(RG_BLOCKS, dtype=w.dtype)
    return jnp.einsum('hij,hg->higj', w, eye).reshape(RG_WIDTH, RG_WIDTH)


def _rglru(xr, yr, conv_w, conv_b, w_a, b_a, w_x, b_x, lam, gn, bsz, seq):
    nt = seq // TS_RG
    row = lambda b, i: (b * nt + i, 0)
    fixed = lambda b, i: (0, 0)
    vec = lambda v: v.reshape(1, RG_WIDTH).astype(F32)
    return pl.pallas_call(
        _rglru_kernel,
        grid=(bsz, nt),
        in_specs=[pl.BlockSpec((TS_RG, RG_WIDTH), row), pl.BlockSpec((TS_RG, RG_WIDTH), row),
                  pl.BlockSpec((CONV_WIDTH, RG_WIDTH), fixed), pl.BlockSpec((1, RG_WIDTH), fixed),
                  pl.BlockSpec((RG_WIDTH, RG_WIDTH), fixed), pl.BlockSpec((1, RG_WIDTH), fixed),
                  pl.BlockSpec((RG_WIDTH, RG_WIDTH), fixed), pl.BlockSpec((1, RG_WIDTH), fixed),
                  pl.BlockSpec((1, RG_WIDTH), fixed), pl.BlockSpec((1, RG_WIDTH), fixed)],
        out_specs=pl.BlockSpec((TS_RG, RG_WIDTH), row),
        out_shape=jax.ShapeDtypeStruct((bsz * seq, RG_WIDTH), F32),
        scratch_shapes=[pltpu.VMEM((SUBLANES, RG_WIDTH), F32),
                        pltpu.VMEM((SUBLANES, RG_WIDTH), F32)],
        compiler_params=pltpu.CompilerParams(dimension_semantics=("arbitrary", "arbitrary"),
                                             vmem_limit_bytes=VMEM_LIMIT),
        name="rglru",
    )(xr, yr, conv_w.astype(F32), vec(conv_b), _block_diag(w_a).astype(BF16), vec(b_a),
      _block_diag(w_x).astype(BF16), vec(b_x), vec(lam), vec(gn))


def _gla_kernel(q_ref, k_ref, v_ref, g_ref, la_ref, gn_ref, o_ref, st_ref):
    bsz, ts = q_ref.shape[0], q_ref.shape[1]
    n_chunks = ts // CHUNK

    @pl.when(pl.program_id(0) == 0)
    def _():
        st_ref[...] = jnp.zeros_like(st_ref)

    ri = lax.broadcasted_iota(jnp.int32, (ts, ts), 0)
    ci = lax.broadcasted_iota(jnp.int32, (ts, ts), 1)
    chunk_bits = CHUNK.bit_length() - 1
    same_chunk = lax.shift_right_logical(ri, chunk_bits) == lax.shift_right_logical(ci, chunk_bits)
    prefix = jnp.where(jnp.logical_and(same_chunk, ri >= ci), 1.0, 0.0).astype(BF16)
    total = jnp.where(same_chunk, 1.0, 0.0).astype(BF16)
    causal = (ri >= ci)[:CHUNK, :CHUNK]
    scale = GLA_DK ** -0.5
    gn = gn_ref[...]
    nt_dims = (((1,), (1,)), ((), ()))
    tn_dims = (((0,), (0,)), ((), ()))

    for bi in range(bsz):
        la = la_ref[bi]
        la_hi = la.astype(BF16)
        la_lo = (la - la_hi.astype(F32)).astype(BF16)
        b = _dot(prefix, la_hi) + _dot(prefix, la_lo)
        b_tot = _dot(total, la_hi) + _dot(total, la_lo)
        kk = k_ref[bi]
        q_s = (q_ref[bi] * scale) * jnp.exp(b)
        k_s = kk * jnp.exp(-b)
        k_end = kk * jnp.exp(b_tot - b)
        decay = jnp.exp(b_tot)

        o_intra, kv, q_heads = [], [], []
        for c in range(n_chunks):
            rows = slice(c * CHUNK, (c + 1) * CHUNK)
            o_c, kv_c, q_c = [], [], []
            for h in range(GLA_HEADS):
                ks = slice(h * GLA_DK, (h + 1) * GLA_DK)
                vs = slice(h * GLA_DV, (h + 1) * GLA_DV)
                qh = q_s[rows, ks].astype(BF16)
                vb = v_ref[bi, rows, vs].astype(BF16)
                att = lax.dot_general(qh, k_s[rows, ks].astype(BF16), nt_dims, preferred_element_type=F32)
                att = jnp.where(causal, att, 0.0).astype(BF16)
                o_c.append(_dot(att, vb))
                kv_c.append(lax.dot_general(vb, k_end[rows, ks].astype(BF16), tn_dims,
                                            preferred_element_type=F32))
                q_c.append(qh)
            o_intra.append(o_c)
            q_heads.append(q_c)
            kv.append(jnp.concatenate(kv_c, axis=1))

        st = st_ref[bi]
        for c in range(n_chunks):
            rows = slice(c * CHUNK, (c + 1) * CHUNK)
            for h in range(GLA_HEADS):
                ks = slice(h * GLA_DK, (h + 1) * GLA_DK)
                vs = slice(h * GLA_DV, (h + 1) * GLA_DV)
                o = o_intra[c][h] + lax.dot_general(q_heads[c][h], st[:, ks].astype(BF16), nt_dims,
                                                    preferred_element_type=F32)
                o_ref[bi, rows, vs] = _rms(o, gn) * jax.nn.silu(g_ref[bi, rows, vs])
            st = decay[c * CHUNK:c * CHUNK + 1, :] * st + kv[c]
        st_ref[bi] = st


def _gla(q, k, v, g, la, gn, bsz, seq):
    nt = seq // TS_GLA
    blk = lambda w: pl.BlockSpec((bsz, TS_GLA, w), lambda i: (0, i, 0))
    r3 = lambda a: a.reshape(bsz, seq, a.shape[-1])
    out = pl.pallas_call(
        _gla_kernel,
        grid=(nt,),
        in_specs=[blk(GLA_KEY), blk(GLA_KEY), blk(GLA_VAL), blk(GLA_VAL), blk(GLA_KEY),
                  pl.BlockSpec((1, GLA_DV), lambda i: (0, 0))],
        out_specs=blk(GLA_VAL),
        out_shape=jax.ShapeDtypeStruct((bsz, seq, GLA_VAL), F32),
        scratch_shapes=[pltpu.VMEM((bsz, GLA_DV, GLA_KEY), F32)],
        compiler_params=pltpu.CompilerParams(dimension_semantics=("arbitrary",),
                                             vmem_limit_bytes=VMEM_LIMIT),
        name="gla",
    )(r3(q), r3(k), r3(v), r3(g), r3(la), gn.reshape(1, GLA_DV).astype(F32))
    return out.reshape(bsz * seq, GLA_VAL)


def _outproj_kernel(yrg_ref, ygla_ref, x_ref, wo1_ref, wo2_ref, gn_ref, wr_ref, br_ref,
                    h_o, hn_o, rt_o):
    tm = x_ref.shape[0]
    h = x_ref[...] + _dot(yrg_ref[...].astype(BF16), wo1_ref[...]) \
        + _dot(ygla_ref[...].astype(BF16), wo2_ref[...])
    h_o[...] = h
    hn = _rms(h, gn_ref[...])
    hn_hi = hn.astype(BF16)
    hn_o[...] = hn_hi
    hn_lo = (hn - hn_hi.astype(F32)).astype(BF16)
    hi_parts = _dot(hn_hi, wr_ref[...])
    logits = hi_parts[:, :LANES] + hi_parts[:, LANES:] + _dot(hn_lo, wr_ref[:, :LANES]) + br_ref[...]

    lane = lax.broadcasted_iota(jnp.int32, (tm, LANES), 1).astype(F32)
    neg = -jnp.inf
    glog = jnp.where(lane < N_GROUPS, logits, neg)
    gmax = jnp.max(glog, axis=-1, keepdims=True)
    gidx = jnp.min(jnp.where(glog == gmax, lane, float(LANES)), axis=-1, keepdims=True)
    g_w = 1.0 / jnp.sum(jnp.exp(glog - gmax), axis=-1, keepdims=True)
    lo = N_GROUPS + gidx * EXPERTS_PER_GROUP
    in_group = jnp.logical_and(lane >= lo, lane < lo + EXPERTS_PER_GROUP)
    le = jnp.where(in_group, logits, neg)
    m1 = jnp.max(le, axis=-1, keepdims=True)
    i1 = jnp.min(jnp.where(le == m1, lane, float(LANES)), axis=-1, keepdims=True)
    le2 = jnp.where(lane == i1, neg, le)
    m2 = jnp.max(le2, axis=-1, keepdims=True)
    i2 = jnp.min(jnp.where(le2 == m2, lane, float(LANES)), axis=-1, keepdims=True)
    t2 = jnp.exp(m2 - m1)
    w1 = g_w / (1.0 + t2)
    w2 = g_w * t2 / (1.0 + t2)
    info = jnp.where(lane == 0.0, i1 - N_GROUPS,
                     jnp.where(lane == 1.0, i2 - N_GROUPS,
                               jnp.where(lane == 2.0, w1, jnp.where(lane == 3.0, w2, 0.0))))
    rt_o[...] = info.T[0:SUBLANES, :]


def _outproj(y_rg, y_gla, x2, w_out, norm_ffn, w_rg, b_rg, w_re, b_re):
    t = x2.shape[0]
    wo = w_out.astype(BF16)
    wr = jnp.pad(jnp.concatenate([w_rg, w_re], axis=1).astype(F32),
                 ((0, 0), (0, LANES - N_GROUPS - N_EXPERTS)))
    wr_hi = wr.astype(BF16)
    wr = jnp.concatenate([wr_hi, (wr - wr_hi.astype(F32)).astype(BF16)], axis=1)
    br = jnp.pad(jnp.concatenate([b_rg, b_re]).astype(F32), (0, LANES - N_GROUPS - N_EXPERTS))
    row = lambda i: (i, 0)
    fixed = lambda i: (0, 0)
    return pl.pallas_call(
        _outproj_kernel,
        grid=(t // TM_PROJ,),
        in_specs=[pl.BlockSpec((TM_PROJ, RG_WIDTH), row), pl.BlockSpec((TM_PROJ, GLA_VAL), row),
                  pl.BlockSpec((TM_PROJ, D_MODEL), row),
                  pl.BlockSpec((RG_WIDTH, D_MODEL), fixed), pl.BlockSpec((GLA_VAL, D_MODEL), fixed),
                  pl.BlockSpec((1, D_MODEL), fixed), pl.BlockSpec((D_MODEL, 2 * LANES), fixed),
                  pl.BlockSpec((1, LANES), fixed)],
        out_specs=[pl.BlockSpec((TM_PROJ, D_MODEL), row), pl.BlockSpec((TM_PROJ, D_MODEL), row),
                   pl.BlockSpec((SUBLANES, TM_PROJ), lambda i: (0, i))],
        out_shape=[jax.ShapeDtypeStruct((t, D_MODEL), F32), jax.ShapeDtypeStruct((t, D_MODEL), BF16),
                   jax.ShapeDtypeStruct((SUBLANES, t), F32)],
        compiler_params=pltpu.CompilerParams(dimension_semantics=("arbitrary",),
                                             vmem_limit_bytes=VMEM_LIMIT),
        name="outproj",
    )(y_rg, y_gla, x2, wo[:RG_WIDTH], wo[RG_WIDTH:], norm_ffn.reshape(1, D_MODEL).astype(F32),
      wr, br.reshape(1, LANES))


def _plan_kernel(rt_ref, lpos_o, meta_o, runs_o, tri_ref, *, n_tok, nb_max):
    n_tiles = n_tok // TT
    esub = lax.broadcasted_iota(jnp.int32, (N_EXPERTS, TT), 0).astype(F32)

    def onehots(i):
        off = pl.multiple_of(i * TT, TT)
        e1 = rt_ref[0:1, pl.ds(off, TT)]
        e2 = rt_ref[1:2, pl.ds(off, TT)]
        m1 = jnp.where(esub == e1, 1.0, 0.0)
        m2 = jnp.where(esub == e2, 1.0, 0.0)
        return off, m1, m2

    def count_body(i, cnt):
        _, m1, m2 = onehots(i)
        return cnt + jnp.sum(m1 + m2, axis=1, keepdims=True)

    counts = lax.fori_loop(0, n_tiles, count_body, jnp.zeros((N_EXPERTS, 1), F32))
    nblk = jnp.floor((counts + (BM - 1)) * (1.0 / BM))
    ei = lax.broadcasted_iota(jnp.int32, (N_EXPERTS, N_EXPERTS), 0)
    ej = lax.broadcasted_iota(jnp.int32, (N_EXPERTS, N_EXPERTS), 1)
    nblk_row = jnp.sum(jnp.where(ei == ej, nblk, 0.0), axis=0, keepdims=True)
    bstart = jnp.sum(jnp.where(ej < ei, nblk_row, 0.0), axis=1, keepdims=True)
    bend = bstart + nblk
    n_used = jnp.sum(nblk, axis=0, keepdims=True)

    meta_w = meta_o.shape[1]
    blane = lax.broadcasted_iota(jnp.int32, (N_EXPERTS, meta_w), 1).astype(F32)
    owner = jnp.sum(jnp.where(bend <= blane, 1.0, 0.0), axis=0, keepdims=True)
    owner = jnp.minimum(owner, N_EXPERTS - 1.0)
    lane1 = lax.broadcasted_iota(jnp.int32, (1, meta_w), 1)
    meta_o[0:1, :] = jnp.where(lane1 == nb_max, n_used, owner).astype(jnp.int32)
    nonempty = jnp.where(nblk > 0.0, 1.0, 0.0)
    meta_o[1:2, :] = jnp.sum(jnp.where(bend <= blane, nonempty, 0.0), axis=0,
                             keepdims=True).astype(jnp.int32)
    owned = jnp.logical_and(bstart <= blane, blane < bend)
    next_start = jnp.sum(jnp.where(owned, bend, 0.0), axis=0, keepdims=True)
    next_owner = jnp.sum(jnp.where(bend <= next_start, 1.0, 0.0), axis=0, keepdims=True)
    meta_o[2:3, :] = jnp.where(next_start < n_used, next_owner, -1.0).astype(jnp.int32)
    valid_end = jnp.sum(jnp.where(owned, bstart * float(BM) + counts, 0.0), axis=0, keepdims=True)
    meta_o[3:4, :] = jnp.clip(valid_end - lane1.astype(F32) * float(BM), 0.0, float(BM)).astype(jnp.int32)

    ti = lax.broadcasted_iota(jnp.int32, (TT, TT), 0)
    tj = lax.broadcasted_iota(jnp.int32, (TT, TT), 1)
    tri_ref[...] = jnp.where(ti <= tj, 1.0, 0.0).astype(BF16)
    tlane = lax.broadcasted_iota(jnp.int32, (N_EXPERTS, LANES), 1)

    def tile_body(i, carry):
        first_slot, t_cnt, t_slot, t_rank = carry
        off, m1, m2 = onehots(i)
        m = m1 + m2
        incl = _dot(m.astype(BF16), tri_ref[...])
        cnt = incl[:, TT - 1:TT]
        cnt_row = jnp.sum(jnp.where(ei == ej, cnt, 0.0), axis=0, keepdims=True)
        first_rank = jnp.sum(jnp.where(ej < ei, cnt_row, 0.0), axis=1, keepdims=True)
        rank = first_rank + incl - m
        lpos_o[0:1, pl.ds(off, TT)] = jnp.sum(m1 * rank, axis=0, keepdims=True).astype(jnp.int32)
        lpos_o[1:2, pl.ds(off, TT)] = jnp.sum(m2 * rank, axis=0, keepdims=True).astype(jnp.int32)
        here = tlane == i
        return (first_slot + cnt, jnp.where(here, cnt, t_cnt), jnp.where(here, first_slot, t_slot),
                jnp.where(here, first_rank, t_rank))

    zeros = jnp.zeros((N_EXPERTS, LANES), F32)
    _, t_cnt, t_slot, t_rank = lax.fori_loop(0, n_tiles, tile_body,
                                             (bstart * float(BM), zeros, zeros, zeros))
    runs_o[0] = t_cnt.astype(jnp.int32)
    runs_o[1] = t_slot.astype(jnp.int32)
    runs_o[2] = t_rank.astype(jnp.int32)
    pad = jnp.where(tlane == 0, bstart * float(BM) + counts,
                    jnp.where(tlane == 1, nblk * float(BM) - counts, 0.0))
    runs_o[3] = pad.astype(jnp.int32)


def _plan(rt, n_tok, nb_max):
    assert n_tok // TT <= LANES, "run tables hold one token tile per lane"
    meta_w = ((nb_max + 1 + LANES - 1) // LANES) * LANES
    return pl.pallas_call(
        functools.partial(_plan_kernel, n_tok=n_tok, nb_max=nb_max),
        out_shape=[jax.ShapeDtypeStruct((2, n_tok), jnp.int32),
                   jax.ShapeDtypeStruct((4, meta_w), jnp.int32),
                   jax.ShapeDtypeStruct((RUN_FIELDS, N_EXPERTS, LANES), jnp.int32)],
        scratch_shapes=[pltpu.VMEM((TT, TT), BF16)],
        compiler_params=pltpu.CompilerParams(vmem_limit_bytes=VMEM_LIMIT),
        name="plan",
    )(rt)


def _run_entry(runs_ref, field, expert, tile):
    return runs_ref[(field * N_EXPERTS + expert) * LANES + tile]


def _dispatch_kernel(runs_ref, meta_ref, hn_ref, lpos_ref, xs_hbm, xsbuf, zbuf, sem, zsem,
                     *, n_tiles, nb_max, n_pad):
    i = pl.program_id(0)
    slot = lax.rem(i, 2)

    @pl.when(i == 0)
    def _():
        zbuf[...] = jnp.zeros_like(zbuf)
        for e in range(N_EXPERTS):
            _copy_run(zbuf, 0, xs_hbm, _run_entry(runs_ref, 3, e, 0), _run_entry(runs_ref, 3, e, 1),
                      zsem)

        def unused_block(blk, carry):
            dst = xs_hbm.at[pl.ds(pl.multiple_of(blk * (BM * TILE_ROWS), BM * TILE_ROWS),
                                  BM * TILE_ROWS), :]
            pltpu.make_async_copy(zbuf, dst, zsem).start()
            return carry

        lax.fori_loop(meta_ref[nb_max], nb_max, unused_block, 0)

    @pl.when(i >= 2)
    def _():
        _wait_rows(xsbuf.at[slot], TILE_SLOTS, sem.at[slot])

    rank = lax.broadcasted_iota(jnp.int32, (TILE_SLOTS, TT), 0)
    onehot = jnp.where(rank == lpos_ref[0:1, :], 1.0, jnp.where(rank == lpos_ref[1:2, :], 1.0, 0.0))
    _store_token_tiles(xsbuf.at[slot], _dot(onehot.astype(BF16), hn_ref[...]))
    for e in range(N_EXPERTS):
        _copy_run(xsbuf.at[slot], _run_entry(runs_ref, 2, e, i), xs_hbm, _run_entry(runs_ref, 1, e, i),
                  _run_entry(runs_ref, 0, e, i), sem.at[slot])

    @pl.when(i == n_tiles - 1)
    def _():
        _wait_rows(xsbuf.at[slot], TILE_SLOTS, sem.at[slot])
        if n_tiles > 1:
            _wait_rows(xsbuf.at[1 - slot], TILE_SLOTS, sem.at[1 - slot])
        _wait_rows(xs_hbm, n_pad, zsem)


def _dispatch(runs, meta, hn, lpos, n_tok, nb_max):
    n_tiles = n_tok // TT
    cap = nb_max * BM
    grid_spec = pltpu.PrefetchScalarGridSpec(
        num_scalar_prefetch=2,
        grid=(n_tiles,),
        in_specs=[pl.BlockSpec((TT, D_MODEL), lambda i, runs_ref, meta_ref: (i, 0)),
                  pl.BlockSpec((2, TT), lambda i, runs_ref, meta_ref: (0, i))],
        out_specs=pl.BlockSpec(memory_space=pl.ANY),
        scratch_shapes=[pltpu.VMEM((2, TILE_SLOTS * TILE_ROWS, LANES), F32),
                        pltpu.VMEM((BM * TILE_ROWS, LANES), F32),
                        pltpu.SemaphoreType.DMA((2,)), pltpu.SemaphoreType.DMA],
    )
    return pl.pallas_call(
        functools.partial(_dispatch_kernel, n_tiles=n_tiles, nb_max=nb_max, n_pad=cap - 2 * n_tok),
        grid_spec=grid_spec,
        out_shape=jax.ShapeDtypeStruct((cap * TILE_ROWS, LANES), F32),
        compiler_params=pltpu.CompilerParams(dimension_semantics=("arbitrary",),
                                             vmem_limit_bytes=VMEM_LIMIT),
        name="dispatch",
    )(runs.reshape(-1), meta.reshape(-1), hn, lpos)


def _experts_kernel(meta_ref, xs_ref, wg_hbm, wu_hbm, wd_hbm, y_ref, wf32, wbf, wsem, *, nb_max, meta_w):
    b = pl.program_id(0)
    n_used = meta_ref[nb_max]

    def start_weights(expert, buf):
        for j, w_hbm in enumerate((wg_hbm, wu_hbm, wd_hbm)):
            pltpu.make_async_copy(w_hbm.at[expert], wf32.at[buf, j], wsem.at[buf]).start(priority=1)

    @pl.when(b < n_used)
    def _():
        owner = meta_ref[b]

        @pl.when(b == 0)
        def _():
            start_weights(owner, 0)

        @pl.when(jnp.logical_or(b == 0, owner != meta_ref[jnp.maximum(b - 1, 0)]))
        def _():
            buf = jnp.bitwise_and(meta_ref[meta_w + b], 1)
            pltpu.make_async_copy(wf32.at[buf], wf32.at[buf], wsem.at[buf]).wait()
            next_owner = meta_ref[2 * meta_w + b]

            @pl.when(next_owner >= 0)
            def _():
                start_weights(next_owner, 1 - buf)

            for j in range(3):
                wbf[j] = wf32[buf, j].astype(BF16)

        def geglu(n_rows):
            rows = pl.ds(0, n_rows * TILE_ROWS)
            x = _load_token_tiles(xs_ref.at[rows, :], n_rows).astype(BF16)
            mid = (jax.nn.gelu(_dot(x, wbf[0])) * _dot(x, wbf[1])).astype(BF16)
            _store_token_tiles(y_ref.at[rows, :], _dot(mid, wbf[2]))

        half = BM // 2
        mostly_padding = meta_ref[3 * meta_w + b] <= half

        @pl.when(jnp.logical_not(mostly_padding))
        def _():
            geglu(BM)

        @pl.when(mostly_padding)
        def _():
            geglu(half)
            y_ref[pl.ds(half * TILE_ROWS, half * TILE_ROWS), :] = jnp.zeros((half * TILE_ROWS, LANES), F32)

    @pl.when(b >= n_used)
    def _():
        y_ref[...] = jnp.zeros_like(y_ref)


def _experts(meta, xs, w_gate, w_up, w_down, nb_max):
    whole = pl.BlockSpec(memory_space=pl.ANY)
    rows = pl.BlockSpec((BM * TILE_ROWS, LANES), lambda b, meta_ref: (b, 0))
    grid_spec = pltpu.PrefetchScalarGridSpec(
        num_scalar_prefetch=1,
        grid=(nb_max,),
        in_specs=[rows, whole, whole, whole],
        out_specs=rows,
        scratch_shapes=[pltpu.VMEM((2, 3, D_MODEL, D_MODEL), F32),
                        pltpu.VMEM((3, D_MODEL, D_MODEL), BF16),
                        pltpu.SemaphoreType.DMA((2,))],
    )
    return pl.pallas_call(
        functools.partial(_experts_kernel, nb_max=nb_max, meta_w=meta.shape[1]),
        grid_spec=grid_spec,
        out_shape=jax.ShapeDtypeStruct(xs.shape, F32),
        compiler_params=pltpu.CompilerParams(dimension_semantics=("arbitrary",),
                                             vmem_limit_bytes=VMEM_LIMIT),
        name="experts",
    )(meta.reshape(-1), xs, w_gate, w_up, w_down)


def _combine_kernel(runs_ref, h_ref, lpos_ref, rt_ref, gn_ref, y_hbm, o_ref, ysbuf, sem, *, n_tiles):
    i = pl.program_id(0)
    slot = lax.rem(i, 2)

    def fetch(tile, buf):
        for e in range(N_EXPERTS):
            _copy_run(y_hbm, _run_entry(runs_ref, 1, e, tile), ysbuf.at[buf],
                      _run_entry(runs_ref, 2, e, tile), _run_entry(runs_ref, 0, e, tile), sem.at[buf])

    @pl.when(i == 0)
    def _():
        fetch(0, 0)

    @pl.when(i + 1 < n_tiles)
    def _():
        fetch(i + 1, 1 - slot)

    _wait_rows(ysbuf.at[slot], TILE_SLOTS, sem.at[slot])

    rank = lax.broadcasted_iota(jnp.int32, (TILE_SLOTS, TT), 0)
    first = rank == lpos_ref[0:1, :]
    second = rank == lpos_ref[1:2, :]
    slot_w = jnp.sum(jnp.where(first, rt_ref[2:3, :], 0.0) + jnp.where(second, rt_ref[3:4, :], 0.0),
                     axis=1, keepdims=True)
    onehot = jnp.where(first, 1.0, jnp.where(second, 1.0, 0.0)).astype(BF16)
    ys = (_load_token_tiles(ysbuf.at[slot], TILE_SLOTS) * slot_w).astype(BF16)
    moe = lax.dot_general(onehot, ys, (((0,), (0,)), ((), ())), preferred_element_type=F32)
    o_ref[...] = _rms(h_ref[...] + moe, gn_ref[...])


def _combine(runs, h, lpos, rt, norm_final, y, n_tok):
    n_tiles = n_tok // TT
    grid_spec = pltpu.PrefetchScalarGridSpec(
        num_scalar_prefetch=1,
        grid=(n_tiles,),
        in_specs=[pl.BlockSpec((TT, D_MODEL), lambda i, runs_ref: (i, 0)),
                  pl.BlockSpec((2, TT), lambda i, runs_ref: (0, i)),
                  pl.BlockSpec((SUBLANES, TT), lambda i, runs_ref: (0, i)),
                  pl.BlockSpec((1, D_MODEL), lambda i, runs_ref: (0, 0)),
                  pl.BlockSpec(memory_space=pl.ANY)],
        out_specs=pl.BlockSpec((TT, D_MODEL), lambda i, runs_ref: (i, 0)),
        scratch_shapes=[pltpu.VMEM((2, TILE_SLOTS * TILE_ROWS, LANES), F32),
                        pltpu.SemaphoreType.DMA((2,))],
    )
    return pl.pallas_call(
        functools.partial(_combine_kernel, n_tiles=n_tiles),
        grid_spec=grid_spec,
        out_shape=jax.ShapeDtypeStruct((n_tok, D_MODEL), F32),
        compiler_params=pltpu.CompilerParams(dimension_semantics=("arbitrary",),
                                             vmem_limit_bytes=VMEM_LIMIT),
        name="combine",
    )(runs.reshape(-1), h, lpos, rt, norm_final.reshape(1, D_MODEL).astype(F32), y)


def kernel(x, norm_mix, w_in, conv_w, conv_b, w_rg_a, b_rg_a, w_rg_x, b_rg_x, rg_lambda, rg_norm, w_alpha_up, b_alpha, gla_norm, w_out, norm_ffn, w_router_group, b_router_group, w_router_expert, b_router_expert, w_exp_gate, w_exp_up, w_exp_down, norm_final):
    bsz, seq, d = x.shape
    assert d == D_MODEL and norm_mix.shape[0] == 1, "single-layer model of width D_MODEL expected"
    n_tok = bsz * seq
    nb_max = (2 * n_tok + N_EXPERTS * (BM - 1)) // BM

    x2 = x.reshape(n_tok, d)
    xr, yr, q, k, v, g, la = _inproj(x2, norm_mix[0], w_in[0], w_alpha_up[0], b_alpha[0])
    y_rg = _rglru(xr, yr, conv_w[0], conv_b[0], w_rg_a[0], b_rg_a[0], w_rg_x[0], b_rg_x[0],
                  rg_lambda[0], rg_norm[0], bsz, seq)
    y_gla = _gla(q, k, v, g, la, gla_norm[0], bsz, seq)
    h, hn, rt = _outproj(y_rg, y_gla, x2, w_out[0], norm_ffn[0], w_router_group[0],
                         b_router_group[0], w_router_expert[0], b_router_expert[0])
    lpos, meta, runs = _plan(rt, n_tok, nb_max)
    xs = _dispatch(runs, meta, hn, lpos, n_tok, nb_max)
    y = _experts(meta, xs, w_exp_gate[0], w_exp_up[0], w_exp_down[0], nb_max)
    out = _combine(runs, h, lpos, rt, norm_final, y, n_tok)
    return out.reshape(bsz, seq, d)
```

```python
import functools

import jax
import jax.numpy as jnp
from jax import lax
from jax.experimental import pallas as pl
from jax.experimental.pallas import tpu as pltpu

F32 = jnp.float32
BF16 = jnp.bfloat16

D_MODEL = 1024
RG_WIDTH = 512
RG_BLOCKS = 8
RG_BLOCK = 64
CONV_WIDTH = 4
C_RG = 8.0
GLA_HEADS = 4
GLA_VAL = 512
GLA_KEY = 256
GLA_DK = 64
GLA_DV = 128
GATE_RANK = 16
GATE_NORM = 16.0
CHUNK = 64
N_GROUPS = 4
EXPERTS_PER_GROUP = 8
N_EXPERTS = 32
EPS = 1e-6

LANES = 128
SUBLANES = 8
VMEM_LIMIT = 56 * 1024 * 1024

TM_PROJ = 1024
TM_INPROJ = 1024
TS_RG = 512
TS_GLA = 512
BM = 256
TT = 512
TILE_SLOTS = 2 * TT
RUN_CHUNK_LOG2 = 6
RUN_FIELDS = 4


def _dot(a, b):
    return jnp.dot(a, b, preferred_element_type=F32)


def _softplus(z):
    return jnp.maximum(z, 0.0) + jnp.log1p(jnp.exp(-jnp.abs(z)))


def _rms(x, g):
    return x * lax.rsqrt(jnp.mean(x * x, axis=-1, keepdims=True) + EPS) * g


U32 = jnp.uint32
TILE_ROWS = D_MODEL // 2 // LANES


def _copy_run(src, src_row, dst, dst_row, n_rows, sem):
    def piece(off, rows):
        s = src.at[pl.ds(pl.multiple_of((src_row + off) * TILE_ROWS, TILE_ROWS), rows * TILE_ROWS), :]
        d = dst.at[pl.ds(pl.multiple_of((dst_row + off) * TILE_ROWS, TILE_ROWS), rows * TILE_ROWS), :]
        pltpu.make_async_copy(s, d, sem).start()

    chunk = 1 << RUN_CHUNK_LOG2
    n_chunks = lax.shift_right_logical(n_rows, RUN_CHUNK_LOG2)
    lax.fori_loop(0, n_chunks, lambda c, carry: (piece(c * chunk, chunk), carry)[1], 0)
    off = n_chunks * chunk
    for k in reversed(range(RUN_CHUNK_LOG2)):
        bit = jnp.bitwise_and(n_rows, 1 << k)

        @pl.when(bit != 0)
        def _():
            piece(off, 1 << k)

        off = off + bit


def _wait_rows(ref, n_rows, sem):
    view = ref.at[pl.ds(0, n_rows * TILE_ROWS), :]
    pltpu.make_async_copy(view, view, sem).wait()


def _store_token_tiles(ref, val, already_bf16=False):
    n = val.shape[0]
    if not already_bf16:
        val = val.astype(BF16).astype(F32)
    bits = lax.bitcast_convert_type(val, U32)
    packed = jnp.bitwise_or(bits[:, :D_MODEL // 2], jnp.right_shift(bits[:, D_MODEL // 2:], 16))
    for c in range(TILE_ROWS):
        ref[pl.ds(c, n, stride=TILE_ROWS), :] = packed[:, c * LANES:(c + 1) * LANES]


def _load_token_tiles(ref, n):
    words = [ref[pl.ds(c, n, stride=TILE_ROWS), :] for c in range(TILE_ROWS)]
    high = [lax.bitcast_convert_type(jnp.bitwise_and(w, jnp.uint32(0xFFFF0000)), F32) for w in words]
    low = [lax.bitcast_convert_type(jnp.left_shift(w, 16), F32) for w in words]
    return jnp.concatenate(high + low, axis=1)


def _inproj_kernel(x_ref, g_ref, wxr, wyr, wq, wk, wv, wg, wal, wup, bal,
                   xr_o, yr_o, q_o, k_o, v_o, g_o, la_o):
    hn = _rms(x_ref[...], g_ref[...]).astype(BF16)
    xr_o[...] = _dot(hn, wxr[...])
    yr_o[...] = _dot(hn, wyr[...])
    q_o[...] = _dot(hn, wq[...])
    k_o[...] = _dot(hn, wk[...])
    v_o[...] = _dot(hn, wv[...])
    g_o[...] = _dot(hn, wg[...])
    a3 = _dot(hn, wal[...])
    a3_hi = a3.astype(BF16).astype(F32)
    lane = lax.broadcasted_iota(jnp.int32, a3.shape, 1)
    use_low = jnp.logical_and(lane >= GATE_RANK, lane < 2 * GATE_RANK)
    z = _dot(jnp.where(use_low, a3 - a3_hi, a3_hi).astype(BF16), wup[...]) + bal[...]
    log_sig = jnp.minimum(z, 0.0) - jnp.log1p(jnp.exp(-jnp.abs(z)))
    la_o[...] = log_sig * (1.0 / GATE_NORM)


def _inproj(x2, norm_mix, w_in, w_alpha_up, b_alpha):
    t = x2.shape[0]
    c = [0, 512, 1024, 1280, 1536, 2048, 2560, 2576]
    wb = w_in.astype(BF16)
    segs = [wb[:, c[i]:c[i + 1]] for i in range(6)]
    w_low = wb[:, c[6]:c[7]]
    wal = jnp.pad(jnp.concatenate([w_low, w_low, w_low], axis=1), ((0, 0), (0, LANES - 3 * GATE_RANK)))
    up = w_alpha_up.astype(F32)
    up_hi = up.astype(BF16)
    up_lo = (up - up_hi.astype(F32)).astype(BF16)
    wup = jnp.pad(jnp.concatenate([up_hi, up_hi, up_lo], axis=0), ((0, LANES - 3 * GATE_RANK), (0, 0)))
    widths = [512, 512, 256, 256, 512, 512, 256]
    row = lambda i: (i, 0)
    fixed = lambda i: (0, 0)
    in_specs = [pl.BlockSpec((TM_INPROJ, D_MODEL), row), pl.BlockSpec((1, D_MODEL), fixed)]
    in_specs += [pl.BlockSpec((D_MODEL, w), fixed) for w in widths[:6]]
    in_specs += [pl.BlockSpec((D_MODEL, LANES), fixed), pl.BlockSpec((LANES, GLA_KEY), fixed),
                 pl.BlockSpec((1, GLA_KEY), fixed)]
    return pl.pallas_call(
        _inproj_kernel,
        grid=(t // TM_INPROJ,),
        in_specs=in_specs,
        out_specs=[pl.BlockSpec((TM_INPROJ, w), row) for w in widths],
        out_shape=[jax.ShapeDtypeStruct((t, w), F32) for w in widths],
        compiler_params=pltpu.CompilerParams(dimension_semantics=("arbitrary",),
                                             vmem_limit_bytes=VMEM_LIMIT),
        name="inproj",
    )(x2, norm_mix.reshape(1, D_MODEL), *segs, wal, wup, b_alpha.reshape(1, GLA_KEY))


def _rglru_kernel(xr_ref, yr_ref, cw_ref, cb_ref, wa_ref, ba_ref, wx_ref, bx_ref, lam_ref, gn_ref,
                  o_ref, tail_ref, hc_ref):
    ts = xr_ref.shape[0]
    n_groups = ts // SUBLANES
    grouped = (n_groups, SUBLANES, RG_WIDTH)
    sub = lax.broadcasted_iota(jnp.int32, grouped, 1)

    @pl.when(pl.program_id(1) == 0)
    def _():
        tail_ref[...] = jnp.zeros_like(tail_ref)
        hc_ref[...] = jnp.zeros_like(hc_ref)

    x = xr_ref[...]
    x_grp = x.reshape(grouped)
    x_prev = jnp.concatenate([tail_ref[...], x[:ts - SUBLANES, :]], axis=0).reshape(grouped)
    tail_ref[...] = x[ts - SUBLANES:ts, :]
    cw = cw_ref[...]
    xc = cb_ref[...] + cw[CONV_WIDTH - 1:CONV_WIDTH, :] * x
    for s in range(1, CONV_WIDTH):
        mixed = jnp.where(sub >= SUBLANES - s, x_prev, x_grp)
        shifted = pltpu.roll(mixed, s, axis=1)
        xc = xc + cw[CONV_WIDTH - 1 - s:CONV_WIDTH - s, :] * shifted.reshape(ts, RG_WIDTH)

    xb = xc.astype(BF16)
    r = jax.nn.sigmoid(_dot(xb, wa_ref[...]) + ba_ref[...])
    gate_i = jax.nn.sigmoid(_dot(xb, wx_ref[...]) + bx_ref[...])
    log_a = (-C_RG) * r * _softplus(-lam_ref[...])
    a = jnp.exp(log_a)
    u = jnp.sqrt(jnp.tanh(-log_a) * (1.0 + a * a)) * (gate_i * xc)

    a = a.reshape(grouped)
    u = u.reshape(grouped)
    d = 1
    while d < SUBLANES:
        keep = sub >= d
        a_sh = jnp.where(keep, pltpu.roll(a, d, axis=1), 1.0)
        u_sh = jnp.where(keep, pltpu.roll(u, d, axis=1), 0.0)
        u = a * u_sh + u
        a = a * a_sh
        d *= 2
    carry = hc_ref[0:1, :]
    groups = []
    for j in range(n_groups):
        h_j = u[j] + a[j] * carry
        groups.append(h_j)
        carry = h_j[SUBLANES - 1:SUBLANES, :]
    h = jnp.concatenate(groups, axis=0)
    hc_ref[...] = jnp.broadcast_to(carry, hc_ref.shape)

    y = h * jax.nn.gelu(yr_ref[...])
    o_ref[...] = _rms(y, gn_ref[...])


def _block_diag(w):
    eye = jnp.eye(RG_BLOCKS, dtype=w.dtype)
    return jnp.einsum('hij,hg->higj', w, eye).reshape(RG_WIDTH, RG_WIDTH)


def _rglru(xr, yr, conv_w, conv_b, w_a, b_a, w_x, b_x, lam, gn, bsz, seq):
    nt = seq // TS_RG
    row = lambda b, i: (b * nt + i, 0)
    fixed = lambda b, i: (0, 0)
    vec = lambda v: v.reshape(1, RG_WIDTH).astype(F32)
    return pl.pallas_call(
        _rglru_kernel,
        grid=(bsz, nt),
        in_specs=[pl.BlockSpec((TS_RG, RG_WIDTH), row), pl.BlockSpec((TS_RG, RG_WIDTH), row),
                  pl.BlockSpec((CONV_WIDTH, RG_WIDTH), fixed), pl.BlockSpec((1, RG_WIDTH), fixed),
                  pl.BlockSpec((RG_WIDTH, RG_WIDTH), fixed), pl.BlockSpec((1, RG_WIDTH), fixed),
                  pl.BlockSpec((RG_WIDTH, RG_WIDTH), fixed), pl.BlockSpec((1, RG_WIDTH), fixed),
                  pl.BlockSpec((1, RG_WIDTH), fixed), pl.BlockSpec((1, RG_WIDTH), fixed)],
        out_specs=pl.BlockSpec((TS_RG, RG_WIDTH), row),
        out_shape=jax.ShapeDtypeStruct((bsz * seq, RG_WIDTH), F32),
        scratch_shapes=[pltpu.VMEM((SUBLANES, RG_WIDTH), F32),
                        pltpu.VMEM((SUBLANES, RG_WIDTH), F32)],
        compiler_params=pltpu.CompilerParams(dimension_semantics=("arbitrary", "arbitrary"),
                                             vmem_limit_bytes=VMEM_LIMIT),
        name="rglru",
    )(xr, yr, conv_w.astype(F32), vec(conv_b), _block_diag(w_a).astype(BF16), vec(b_a),
      _block_diag(w_x).astype(BF16), vec(b_x), vec(lam), vec(gn))


def _gla_kernel(q_ref, k_ref, v_ref, g_ref, la_ref, gn_ref, o_ref, st_ref):
    bsz, ts = q_ref.shape[0], q_ref.shape[1]
    n_chunks = ts // CHUNK

    @pl.when(pl.program_id(0) == 0)
    def _():
        st_ref[...] = jnp.zeros_like(st_ref)

    ri = lax.broadcasted_iota(jnp.int32, (ts, ts), 0)
    ci = lax.broadcasted_iota(jnp.int32, (ts, ts), 1)
    chunk_bits = CHUNK.bit_length() - 1
    same_chunk = lax.shift_right_logical(ri, chunk_bits) == lax.shift_right_logical(ci, chunk_bits)
    prefix = jnp.where(jnp.logical_and(same_chunk, ri >= ci), 1.0, 0.0).astype(BF16)
    total = jnp.where(same_chunk, 1.0, 0.0).astype(BF16)
    causal = (ri >= ci)[:CHUNK, :CHUNK]
    scale = GLA_DK ** -0.5
    gn = gn_ref[...]
    nt_dims = (((1,), (1,)), ((), ()))
    tn_dims = (((0,), (0,)), ((), ()))

    for bi in range(bsz):
        la = la_ref[bi]
        la_hi = la.astype(BF16)
        la_lo = (la - la_hi.astype(F32)).astype(BF16)
        b = _dot(prefix, la_hi) + _dot(prefix, la_lo)
        b_tot = _dot(total, la_hi) + _dot(total, la_lo)
        kk = k_ref[bi]
        q_s = (q_ref[bi] * scale) * jnp.exp(b)
        k_s = kk * jnp.exp(-b)
        k_end = kk * jnp.exp(b_tot - b)
        decay = jnp.exp(b_tot)

        o_intra, kv, q_heads = [], [], []
        for c in range(n_chunks):
            rows = slice(c * CHUNK, (c + 1) * CHUNK)
            o_c, kv_c, q_c = [], [], []
            for h in range(GLA_HEADS):
                ks = slice(h * GLA_DK, (h + 1) * GLA_DK)
                vs = slice(h * GLA_DV, (h + 1) * GLA_DV)
                qh = q_s[rows, ks].astype(BF16)
                vb = v_ref[bi, rows, vs].astype(BF16)
                att = lax.dot_general(qh, k_s[rows, ks].astype(BF16), nt_dims, preferred_element_type=F32)
                att = jnp.where(causal, att, 0.0).astype(BF16)
                o_c.append(_dot(att, vb))
                kv_c.append(lax.dot_general(vb, k_end[rows, ks].astype(BF16), tn_dims,
                                            preferred_element_type=F32))
                q_c.append(qh)
            o_intra.append(o_c)
            q_heads.append(q_c)
            kv.append(jnp.concatenate(kv_c, axis=1))

        st = st_ref[bi]
        for c in range(n_chunks):
            rows = slice(c * CHUNK, (c + 1) * CHUNK)
            for h in range(GLA_HEADS):
                ks = slice(h * GLA_DK, (h + 1) * GLA_DK)
                vs = slice(h * GLA_DV, (h + 1) * GLA_DV)
                o = o_intra[c][h] + lax.dot_general(q_heads[c][h], st[:, ks].astype(BF16), nt_dims,
                                                    preferred_element_type=F32)
                o_ref[bi, rows, vs] = _rms(o, gn) * jax.nn.silu(g_ref[bi, rows, vs])
            st = decay[c * CHUNK:c * CHUNK + 1, :] * st + kv[c]
        st_ref[bi] = st


def _gla(q, k, v, g, la, gn, bsz, seq):
    nt = seq // TS_GLA
    blk = lambda w: pl.BlockSpec((bsz, TS_GLA, w), lambda i: (0, i, 0))
    r3 = lambda a: a.reshape(bsz, seq, a.shape[-1])
    out = pl.pallas_call(
        _gla_kernel,
        grid=(nt,),
        in_specs=[blk(GLA_KEY), blk(GLA_KEY), blk(GLA_VAL), blk(GLA_VAL), blk(GLA_KEY),
                  pl.BlockSpec((1, GLA_DV), lambda i: (0, 0))],
        out_specs=blk(GLA_VAL),
        out_shape=jax.ShapeDtypeStruct((bsz, seq, GLA_VAL), F32),
        scratch_shapes=[pltpu.VMEM((bsz, GLA_DV, GLA_KEY), F32)],
        compiler_params=pltpu.CompilerParams(dimension_semantics=("arbitrary",),
                                             vmem_limit_bytes=VMEM_LIMIT),
        name="gla",
    )(r3(q), r3(k), r3(v), r3(g), r3(la), gn.reshape(1, GLA_DV).astype(F32))
    return out.reshape(bsz * seq, GLA_VAL)


def _outproj_kernel(yrg_ref, ygla_ref, x_ref, wo1_ref, wo2_ref, gn_ref, wr_ref, br_ref,
                    h_o, hn_o, rt_o):
    tm = x_ref.shape[0]
    h = x_ref[...] + _dot(yrg_ref[...].astype(BF16), wo1_ref[...]) \
        + _dot(ygla_ref[...].astype(BF16), wo2_ref[...])
    h_o[...] = h
    hn = _rms(h, gn_ref[...])
    hn_hi = hn.astype(BF16)
    hn_o[...] = hn_hi
    hn_lo = (hn - hn_hi.astype(F32)).astype(BF16)
    hi_parts = _dot(hn_hi, wr_ref[...])
    logits = hi_parts[:, :LANES] + hi_parts[:, LANES:] + _dot(hn_lo, wr_ref[:, :LANES]) + br_ref[...]

    lane = lax.broadcasted_iota(jnp.int32, (tm, LANES), 1).astype(F32)
    neg = -jnp.inf
    glog = jnp.where(lane < N_GROUPS, logits, neg)
    gmax = jnp.max(glog, axis=-1, keepdims=True)
    gidx = jnp.min(jnp.where(glog == gmax, lane, float(LANES)), axis=-1, keepdims=True)
    g_w = 1.0 / jnp.sum(jnp.exp(glog - gmax), axis=-1, keepdims=True)
    lo = N_GROUPS + gidx * EXPERTS_PER_GROUP
    in_group = jnp.logical_and(lane >= lo, lane < lo + EXPERTS_PER_GROUP)
    le = jnp.where(in_group, logits, neg)
    m1 = jnp.max(le, axis=-1, keepdims=True)
    i1 = jnp.min(jnp.where(le == m1, lane, float(LANES)), axis=-1, keepdims=True)
    le2 = jnp.where(lane == i1, neg, le)
    m2 = jnp.max(le2, axis=-1, keepdims=True)
    i2 = jnp.min(jnp.where(le2 == m2, lane, float(LANES)), axis=-1, keepdims=True)
    t2 = jnp.exp(m2 - m1)
    w1 = g_w / (1.0 + t2)
    w2 = g_w * t2 / (1.0 + t2)
    info = jnp.where(lane == 0.0, i1 - N_GROUPS,
                     jnp.where(lane == 1.0, i2 - N_GROUPS,
                               jnp.where(lane == 2.0, w1, jnp.where(lane == 3.0, w2, 0.0))))
    rt_o[...] = info.T[0:SUBLANES, :]


def _outproj(y_rg, y_gla, x2, w_out, norm_ffn, w_rg, b_rg, w_re, b_re):
    t = x2.shape[0]
    wo = w_out.astype(BF16)
    wr = jnp.pad(jnp.concatenate([w_rg, w_re], axis=1).astype(F32),
                 ((0, 0), (0, LANES - N_GROUPS - N_EXPERTS)))
    wr_hi = wr.astype(BF16)
    wr = jnp.concatenate([wr_hi, (wr - wr_hi.astype(F32)).astype(BF16)], axis=1)
    br = jnp.pad(jnp.concatenate([b_rg, b_re]).astype(F32), (0, LANES - N_GROUPS - N_EXPERTS))
    row = lambda i: (i, 0)
    fixed = lambda i: (0, 0)
    return pl.pallas_call(
        _outproj_kernel,
        grid=(t // TM_PROJ,),
        in_specs=[pl.BlockSpec((TM_PROJ, RG_WIDTH), row), pl.BlockSpec((TM_PROJ, GLA_VAL), row),
                  pl.BlockSpec((TM_PROJ, D_MODEL), row),
                  pl.BlockSpec((RG_WIDTH, D_MODEL), fixed), pl.BlockSpec((GLA_VAL, D_MODEL), fixed),
                  pl.BlockSpec((1, D_MODEL), fixed), pl.BlockSpec((D_MODEL, 2 * LANES), fixed),
                  pl.BlockSpec((1, LANES), fixed)],
        out_specs=[pl.BlockSpec((TM_PROJ, D_MODEL), row), pl.BlockSpec((TM_PROJ, D_MODEL), row),
                   pl.BlockSpec((SUBLANES, TM_PROJ), lambda i: (0, i))],
        out_shape=[jax.ShapeDtypeStruct((t, D_MODEL), F32), jax.ShapeDtypeStruct((t, D_MODEL), BF16),
                   jax.ShapeDtypeStruct((SUBLANES, t), F32)],
        compiler_params=pltpu.CompilerParams(dimension_semantics=("arbitrary",),
                                             vmem_limit_bytes=VMEM_LIMIT),
        name="outproj",
    )(y_rg, y_gla, x2, wo[:RG_WIDTH], wo[RG_WIDTH:], norm_ffn.reshape(1, D_MODEL).astype(F32),
      wr, br.reshape(1, LANES))


def _plan_kernel(rt_ref, lpos_o, meta_o, runs_o, tri_ref, *, n_tok, nb_max):
    n_tiles = n_tok // TT
    esub = lax.broadcasted_iota(jnp.int32, (N_EXPERTS, TT), 0).astype(F32)

    def onehots(i):
        off = pl.multiple_of(i * TT, TT)
        e1 = rt_ref[0:1, pl.ds(off, TT)]
        e2 = rt_ref[1:2, pl.ds(off, TT)]
        m1 = jnp.where(esub == e1, 1.0, 0.0)
        m2 = jnp.where(esub == e2, 1.0, 0.0)
        return off, m1, m2

    def count_body(i, cnt):
        _, m1, m2 = onehots(i)
        return cnt + jnp.sum(m1 + m2, axis=1, keepdims=True)

    counts = lax.fori_loop(0, n_tiles, count_body, jnp.zeros((N_EXPERTS, 1), F32))
    nblk = jnp.floor((counts + (BM - 1)) * (1.0 / BM))
    ei = lax.broadcasted_iota(jnp.int32, (N_EXPERTS, N_EXPERTS), 0)
    ej = lax.broadcasted_iota(jnp.int32, (N_EXPERTS, N_EXPERTS), 1)
    nblk_row = jnp.sum(jnp.where(ei == ej, nblk, 0.0), axis=0, keepdims=True)
    bstart = jnp.sum(jnp.where(ej < ei, nblk_row, 0.0), axis=1, keepdims=True)
    bend = bstart + nblk
    n_used = jnp.sum(nblk, axis=0, keepdims=True)

    meta_w = meta_o.shape[1]
    blane = lax.broadcasted_iota(jnp.int32, (N_EXPERTS, meta_w), 1).astype(F32)
    owner = jnp.sum(jnp.where(bend <= blane, 1.0, 0.0), axis=0, keepdims=True)
    owner = jnp.minimum(owner, N_EXPERTS - 1.0)
    lane1 = lax.broadcasted_iota(jnp.int32, (1, meta_w), 1)
    meta_o[0:1, :] = jnp.where(lane1 == nb_max, n_used, owner).astype(jnp.int32)
    nonempty = jnp.where(nblk > 0.0, 1.0, 0.0)
    meta_o[1:2, :] = jnp.sum(jnp.where(bend <= blane, nonempty, 0.0), axis=0,
                             keepdims=True).astype(jnp.int32)
    owned = jnp.logical_and(bstart <= blane, blane < bend)
    next_start = jnp.sum(jnp.where(owned, bend, 0.0), axis=0, keepdims=True)
    next_owner = jnp.sum(jnp.where(bend <= next_start, 1.0, 0.0), axis=0, keepdims=True)
    meta_o[2:3, :] = jnp.where(next_start < n_used, next_owner, -1.0).astype(jnp.int32)
    valid_end = jnp.sum(jnp.where(owned, bstart * float(BM) + counts, 0.0), axis=0, keepdims=True)
    meta_o[3:4, :] = jnp.clip(valid_end - lane1.astype(F32) * float(BM), 0.0, float(BM)).astype(jnp.int32)

    ti = lax.broadcasted_iota(jnp.int32, (TT, TT), 0)
    tj = lax.broadcasted_iota(jnp.int32, (TT, TT), 1)
    tri_ref[...] = jnp.where(ti <= tj, 1.0, 0.0).astype(BF16)
    tlane = lax.broadcasted_iota(jnp.int32, (N_EXPERTS, LANES), 1)

    def tile_body(i, carry):
        first_slot, t_cnt, t_slot, t_rank = carry
        off, m1, m2 = onehots(i)
        m = m1 + m2
        incl = _dot(m.astype(BF16), tri_ref[...])
        cnt = incl[:, TT - 1:TT]
        cnt_row = jnp.sum(jnp.where(ei == ej, cnt, 0.0), axis=0, keepdims=True)
        first_rank = jnp.sum(jnp.where(ej < ei, cnt_row, 0.0), axis=1, keepdims=True)
        rank = first_rank + incl - m
        lpos_o[0:1, pl.ds(off, TT)] = jnp.sum(m1 * rank, axis=0, keepdims=True).astype(jnp.int32)
        lpos_o[1:2, pl.ds(off, TT)] = jnp.sum(m2 * rank, axis=0, keepdims=True).astype(jnp.int32)
        here = tlane == i
        return (first_slot + cnt, jnp.where(here, cnt, t_cnt), jnp.where(here, first_slot, t_slot),
                jnp.where(here, first_rank, t_rank))

    zeros = jnp.zeros((N_EXPERTS, LANES), F32)
    _, t_cnt, t_slot, t_rank = lax.fori_loop(0, n_tiles, tile_body,
                                             (bstart * float(BM), zeros, zeros, zeros))
    runs_o[0] = t_cnt.astype(jnp.int32)
    runs_o[1] = t_slot.astype(jnp.int32)
    runs_o[2] = t_rank.astype(jnp.int32)
    pad = jnp.where(tlane == 0, bstart * float(BM) + counts,
                    jnp.where(tlane == 1, nblk * float(BM) - counts, 0.0))
    runs_o[3] = pad.astype(jnp.int32)


def _plan(rt, n_tok, nb_max):
    assert n_tok // TT <= LANES, "run tables hold one token tile per lane"
    meta_w = ((nb_max + 1 + LANES - 1) // LANES) * LANES
    return pl.pallas_call(
        functools.partial(_plan_kernel, n_tok=n_tok, nb_max=nb_max),
        out_shape=[jax.ShapeDtypeStruct((2, n_tok), jnp.int32),
                   jax.ShapeDtypeStruct((4, meta_w), jnp.int32),
                   jax.ShapeDtypeStruct((RUN_FIELDS, N_EXPERTS, LANES), jnp.int32)],
        scratch_shapes=[pltpu.VMEM((TT, TT), BF16)],
        compiler_params=pltpu.CompilerParams(vmem_limit_bytes=VMEM_LIMIT),
        name="plan",
    )(rt)


def _run_entry(runs_ref, field, expert, tile):
    return runs_ref[(field * N_EXPERTS + expert) * LANES + tile]


def _dispatch_kernel(runs_ref, meta_ref, hn_ref, lpos_ref, xs_hbm, xsbuf, zbuf, sem, zsem,
                     *, n_tiles, nb_max, n_pad):
    i = pl.program_id(0)
    slot = lax.rem(i, 2)

    @pl.when(i == 0)
    def _():
        zbuf[...] = jnp.zeros_like(zbuf)
        for e in range(N_EXPERTS):
            _copy_run(zbuf, 0, xs_hbm, _run_entry(runs_ref, 3, e, 0), _run_entry(runs_ref, 3, e, 1),
                      zsem)

        def unused_block(blk, carry):
            dst = xs_hbm.at[pl.ds(pl.multiple_of(blk * (BM * TILE_ROWS), BM * TILE_ROWS),
                                  BM * TILE_ROWS), :]
            pltpu.make_async_copy(zbuf, dst, zsem).start()
            return carry

        lax.fori_loop(meta_ref[nb_max], nb_max, unused_block, 0)

    @pl.when(i >= 2)
    def _():
        _wait_rows(xsbuf.at[slot], TILE_SLOTS, sem.at[slot])

    rank = lax.broadcasted_iota(jnp.int32, (TILE_SLOTS, TT), 0)
    onehot = jnp.where(rank == lpos_ref[0:1, :], 1.0, jnp.where(rank == lpos_ref[1:2, :], 1.0, 0.0))
    _store_token_tiles(xsbuf.at[slot], _dot(onehot.astype(BF16), hn_ref[...]), already_bf16=True)
    for e in range(N_EXPERTS):
        _copy_run(xsbuf.at[slot], _run_entry(runs_ref, 2, e, i), xs_hbm, _run_entry(runs_ref, 1, e, i),
                  _run_entry(runs_ref, 0, e, i), sem.at[slot])

    @pl.when(i == n_tiles - 1)
    def _():
        _wait_rows(xsbuf.at[slot], TILE_SLOTS, sem.at[slot])
        if n_tiles > 1:
            _wait_rows(xsbuf.at[1 - slot], TILE_SLOTS, sem.at[1 - slot])
        _wait_rows(xs_hbm, n_pad, zsem)


def _dispatch(runs, meta, hn, lpos, n_tok, nb_max):
    n_tiles = n_tok // TT
    cap = nb_max * BM
    grid_spec = pltpu.PrefetchScalarGridSpec(
        num_scalar_prefetch=2,
        grid=(n_tiles,),
        in_specs=[pl.BlockSpec((TT, D_MODEL), lambda i, runs_ref, meta_ref: (i, 0)),
                  pl.BlockSpec((2, TT), lambda i, runs_ref, meta_ref: (0, i))],
        out_specs=pl.BlockSpec(memory_space=pl.ANY),
        scratch_shapes=[pltpu.VMEM((2, TILE_SLOTS * TILE_ROWS, LANES), U32),
                        pltpu.VMEM((BM * TILE_ROWS, LANES), U32),
                        pltpu.SemaphoreType.DMA((2,)), pltpu.SemaphoreType.DMA],
    )
    return pl.pallas_call(
        functools.partial(_dispatch_kernel, n_tiles=n_tiles, nb_max=nb_max, n_pad=cap - 2 * n_tok),
        grid_spec=grid_spec,
        out_shape=jax.ShapeDtypeStruct((cap * TILE_ROWS, LANES), U32),
        compiler_params=pltpu.CompilerParams(dimension_semantics=("arbitrary",),
                                             vmem_limit_bytes=VMEM_LIMIT),
        name="dispatch",
    )(runs.reshape(-1), meta.reshape(-1), hn, lpos)


def _experts_kernel(meta_ref, xs_ref, wg_hbm, wu_hbm, wd_hbm, y_ref, wf32, wbf, wsem, *, nb_max, meta_w):
    b = pl.program_id(0)
    n_used = meta_ref[nb_max]

    def start_weights(expert, buf):
        for j, w_hbm in enumerate((wg_hbm, wu_hbm, wd_hbm)):
            pltpu.make_async_copy(w_hbm.at[expert], wf32.at[buf, j], wsem.at[buf]).start(priority=1)

    @pl.when(b < n_used)
    def _():
        owner = meta_ref[b]

        @pl.when(b == 0)
        def _():
            start_weights(owner, 0)

        @pl.when(jnp.logical_or(b == 0, owner != meta_ref[jnp.maximum(b - 1, 0)]))
        def _():
            buf = jnp.bitwise_and(meta_ref[meta_w + b], 1)
            pltpu.make_async_copy(wf32.at[buf], wf32.at[buf], wsem.at[buf]).wait()
            next_owner = meta_ref[2 * meta_w + b]

            @pl.when(next_owner >= 0)
            def _():
                start_weights(next_owner, 1 - buf)

            for j in range(3):
                wbf[j] = wf32[buf, j].astype(BF16)

        def geglu(n_rows):
            rows = pl.ds(0, n_rows * TILE_ROWS)
            x = _load_token_tiles(xs_ref.at[rows, :], n_rows).astype(BF16)
            mid = (jax.nn.gelu(_dot(x, wbf[0])) * _dot(x, wbf[1])).astype(BF16)
            _store_token_tiles(y_ref.at[rows, :], _dot(mid, wbf[2]))

        half = BM // 2
        mostly_padding = meta_ref[3 * meta_w + b] <= half

        @pl.when(jnp.logical_not(mostly_padding))
        def _():
            geglu(BM)

        @pl.when(mostly_padding)
        def _():
            geglu(half)
            y_ref[pl.ds(half * TILE_ROWS, half * TILE_ROWS), :] = jnp.zeros((half * TILE_ROWS, LANES), U32)

    @pl.when(b >= n_used)
    def _():
        y_ref[...] = jnp.zeros_like(y_ref)


def _experts(meta, xs, w_gate, w_up, w_down, nb_max):
    whole = pl.BlockSpec(memory_space=pl.ANY)
    rows = pl.BlockSpec((BM * TILE_ROWS, LANES), lambda b, meta_ref: (b, 0))
    grid_spec = pltpu.PrefetchScalarGridSpec(
        num_scalar_prefetch=1,
        grid=(nb_max,),
        in_specs=[rows, whole, whole, whole],
        out_specs=rows,
        scratch_shapes=[pltpu.VMEM((2, 3, D_MODEL, D_MODEL), F32),
                        pltpu.VMEM((3, D_MODEL, D_MODEL), BF16),
                        pltpu.SemaphoreType.DMA((2,))],
    )
    return pl.pallas_call(
        functools.partial(_experts_kernel, nb_max=nb_max, meta_w=meta.shape[1]),
        grid_spec=grid_spec,
        out_shape=jax.ShapeDtypeStruct(xs.shape, U32),
        compiler_params=pltpu.CompilerParams(dimension_semantics=("arbitrary",),
                                             vmem_limit_bytes=VMEM_LIMIT),
        name="experts",
    )(meta.reshape(-1), xs, w_gate, w_up, w_down)


def _combine_kernel(runs_ref, h_ref, lpos_ref, rt_ref, gn_ref, y_hbm, o_ref, ysbuf, sem, *, n_tiles):
    i = pl.program_id(0)
    slot = lax.rem(i, 2)

    def fetch(tile, buf):
        for e in range(N_EXPERTS):
            _copy_run(y_hbm, _run_entry(runs_ref, 1, e, tile), ysbuf.at[buf],
                      _run_entry(runs_ref, 2, e, tile), _run_entry(runs_ref, 0, e, tile), sem.at[buf])

    @pl.when(i == 0)
    def _():
        fetch(0, 0)

    @pl.when(i + 1 < n_tiles)
    def _():
        fetch(i + 1, 1 - slot)

    _wait_rows(ysbuf.at[slot], TILE_SLOTS, sem.at[slot])

    rank = lax.broadcasted_iota(jnp.int32, (TILE_SLOTS, TT), 0)
    first = rank == lpos_ref[0:1, :]
    second = rank == lpos_ref[1:2, :]
    slot_w = jnp.sum(jnp.where(first, rt_ref[2:3, :], 0.0) + jnp.where(second, rt_ref[3:4, :], 0.0),
                     axis=1, keepdims=True)
    onehot = jnp.where(first, 1.0, jnp.where(second, 1.0, 0.0)).astype(BF16)
    ys = (_load_token_tiles(ysbuf.at[slot], TILE_SLOTS) * slot_w).astype(BF16)
    moe = lax.dot_general(onehot, ys, (((0,), (0,)), ((), ())), preferred_element_type=F32)
    o_ref[...] = _rms(h_ref[...] + moe, gn_ref[...])


def _combine(runs, h, lpos, rt, norm_final, y, n_tok):
    n_tiles = n_tok // TT
    grid_spec = pltpu.PrefetchScalarGridSpec(
        num_scalar_prefetch=1,
        grid=(n_tiles,),
        in_specs=[pl.BlockSpec((TT, D_MODEL), lambda i, runs_ref: (i, 0)),
                  pl.BlockSpec((2, TT), lambda i, runs_ref: (0, i)),
                  pl.BlockSpec((SUBLANES, TT), lambda i, runs_ref: (0, i)),
                  pl.BlockSpec((1, D_MODEL), lambda i, runs_ref: (0, 0)),
                  pl.BlockSpec(memory_space=pl.ANY)],
        out_specs=pl.BlockSpec((TT, D_MODEL), lambda i, runs_ref: (i, 0)),
        scratch_shapes=[pltpu.VMEM((2, TILE_SLOTS * TILE_ROWS, LANES), U32),
                        pltpu.SemaphoreType.DMA((2,))],
    )
    return pl.pallas_call(
        functools.partial(_combine_kernel, n_tiles=n_tiles),
        grid_spec=grid_spec,
        out_shape=jax.ShapeDtypeStruct((n_tok, D_MODEL), F32),
        compiler_params=pltpu.CompilerParams(dimension_semantics=("arbitrary",),
                                             vmem_limit_bytes=VMEM_LIMIT),
        name="combine",
    )(runs.reshape(-1), h, lpos, rt, norm_final.reshape(1, D_MODEL).astype(F32), y)


def kernel(x, norm_mix, w_in, conv_w, conv_b, w_rg_a, b_rg_a, w_rg_x, b_rg_x, rg_lambda, rg_norm, w_alpha_up, b_alpha, gla_norm, w_out, norm_ffn, w_router_group, b_router_group, w_router_expert, b_router_expert, w_exp_gate, w_exp_up, w_exp_down, norm_final):
    bsz, seq, d = x.shape
    assert d == D_MODEL and norm_mix.shape[0] == 1, "single-layer model of width D_MODEL expected"
    n_tok = bsz * seq
    nb_max = (2 * n_tok + N_EXPERTS * (BM - 1)) // BM

    x2 = x.reshape(n_tok, d)
    xr, yr, q, k, v, g, la = _inproj(x2, norm_mix[0], w_in[0], w_alpha_up[0], b_alpha[0])
    y_rg = _rglru(xr, yr, conv_w[0], conv_b[0], w_rg_a[0], b_rg_a[0], w_rg_x[0], b_rg_x[0],
                  rg_lambda[0], rg_norm[0], bsz, seq)
    y_gla = _gla(q, k, v, g, la, gla_norm[0], bsz, seq)
    h, hn, rt = _outproj(y_rg, y_gla, x2, w_out[0], norm_ffn[0], w_router_group[0],
                         b_router_group[0], w_router_expert[0], b_router_expert[0])
    lpos, meta, runs = _plan(rt, n_tok, nb_max)
    xs = _dispatch(runs, meta, hn, lpos, n_tok, nb_max)
    y = _experts(meta, xs, w_exp_gate[0], w_exp_up[0], w_exp_down[0], nb_max)
    out = _combine(runs, h, lpos, rt, norm_final, y, n_tok)
    return out.reshape(bsz, seq, d)
```

```python
import functools

import jax
import jax.numpy as jnp
from jax import lax
from jax.experimental import pallas as pl
from jax.experimental.pallas import tpu as pltpu

F32 = jnp.float32
BF16 = jnp.bfloat16

D_MODEL = 1024
RG_WIDTH = 512
RG_BLOCKS = 8
RG_BLOCK = 64
CONV_WIDTH = 4
C_RG = 8.0
GLA_HEADS = 4
GLA_VAL = 512
GLA_KEY = 256
GLA_DK = 64
GLA_DV = 128
GATE_RANK = 16
GATE_NORM = 16.0
CHUNK = 64
N_GROUPS = 4
EXPERTS_PER_GROUP = 8
N_EXPERTS = 32
EPS = 1e-6

LANES = 128
SUBLANES = 8
VMEM_LIMIT = 56 * 1024 * 1024

TM_PROJ = 1024
TM_INPROJ = 1024
TS_RG = 512
TS_GLA = 512
BM = 512
SKIP_ROWS = 128
TT = 512
TILE_SLOTS = 2 * TT
RUN_CHUNK_LOG2 = 6
RUN_FIELDS = 4


def _dot(a, b):
    return jnp.dot(a, b, preferred_element_type=F32)


def _softplus(z):
    return jnp.maximum(z, 0.0) + jnp.log1p(jnp.exp(-jnp.abs(z)))


def _rms(x, g):
    return x * lax.rsqrt(jnp.mean(x * x, axis=-1, keepdims=True) + EPS) * g


U32 = jnp.uint32
TILE_ROWS = D_MODEL // 2 // LANES


def _copy_run(src, src_row, dst, dst_row, n_rows, sem):
    def piece(off, rows):
        s = src.at[pl.ds(pl.multiple_of((src_row + off) * TILE_ROWS, TILE_ROWS), rows * TILE_ROWS), :]
        d = dst.at[pl.ds(pl.multiple_of((dst_row + off) * TILE_ROWS, TILE_ROWS), rows * TILE_ROWS), :]
        pltpu.make_async_copy(s, d, sem).start()

    chunk = 1 << RUN_CHUNK_LOG2
    n_chunks = lax.shift_right_logical(n_rows, RUN_CHUNK_LOG2)
    lax.fori_loop(0, n_chunks, lambda c, carry: (piece(c * chunk, chunk), carry)[1], 0)
    off = n_chunks * chunk
    for k in reversed(range(RUN_CHUNK_LOG2)):
        bit = jnp.bitwise_and(n_rows, 1 << k)

        @pl.when(bit != 0)
        def _():
            piece(off, 1 << k)

        off = off + bit


def _wait_rows(ref, n_rows, sem):
    view = ref.at[pl.ds(0, n_rows * TILE_ROWS), :]
    pltpu.make_async_copy(view, view, sem).wait()


def _store_token_tiles(ref, val, already_bf16=False):
    n = val.shape[0]
    if not already_bf16:
        val = val.astype(BF16).astype(F32)
    bits = lax.bitcast_convert_type(val, U32)
    packed = jnp.bitwise_or(bits[:, :D_MODEL // 2], jnp.right_shift(bits[:, D_MODEL // 2:], 16))
    for c in range(TILE_ROWS):
        ref[pl.ds(c, n, stride=TILE_ROWS), :] = packed[:, c * LANES:(c + 1) * LANES]


def _load_token_tiles(ref, n):
    words = [ref[pl.ds(c, n, stride=TILE_ROWS), :] for c in range(TILE_ROWS)]
    high = [lax.bitcast_convert_type(jnp.bitwise_and(w, jnp.uint32(0xFFFF0000)), F32) for w in words]
    low = [lax.bitcast_convert_type(jnp.left_shift(w, 16), F32) for w in words]
    return jnp.concatenate(high + low, axis=1)


def _inproj_kernel(x_ref, g_ref, wxr, wyr, wq, wk, wv, wg, wal, wup, bal,
                   xr_o, yr_o, q_o, k_o, v_o, g_o, la_o):
    hn = _rms(x_ref[...], g_ref[...]).astype(BF16)
    xr_o[...] = _dot(hn, wxr[...])
    yr_o[...] = _dot(hn, wyr[...])
    q_o[...] = _dot(hn, wq[...])
    k_o[...] = _dot(hn, wk[...])
    v_o[...] = _dot(hn, wv[...])
    g_o[...] = _dot(hn, wg[...])
    a3 = _dot(hn, wal[...])
    a3_hi = a3.astype(BF16).astype(F32)
    lane = lax.broadcasted_iota(jnp.int32, a3.shape, 1)
    use_low = jnp.logical_and(lane >= GATE_RANK, lane < 2 * GATE_RANK)
    z = _dot(jnp.where(use_low, a3 - a3_hi, a3_hi).astype(BF16), wup[...]) + bal[...]
    log_sig = jnp.minimum(z, 0.0) - jnp.log1p(jnp.exp(-jnp.abs(z)))
    la_o[...] = log_sig * (1.0 / GATE_NORM)


def _inproj(x2, norm_mix, w_in, w_alpha_up, b_alpha):
    t = x2.shape[0]
    c = [0, 512, 1024, 1280, 1536, 2048, 2560, 2576]
    wb = w_in.astype(BF16)
    segs = [wb[:, c[i]:c[i + 1]] for i in range(6)]
    w_low = wb[:, c[6]:c[7]]
    wal = jnp.pad(jnp.concatenate([w_low, w_low, w_low], axis=1), ((0, 0), (0, LANES - 3 * GATE_RANK)))
    up = w_alpha_up.astype(F32)
    up_hi = up.astype(BF16)
    up_lo = (up - up_hi.astype(F32)).astype(BF16)
    wup = jnp.pad(jnp.concatenate([up_hi, up_hi, up_lo], axis=0), ((0, LANES - 3 * GATE_RANK), (0, 0)))
    widths = [512, 512, 256, 256, 512, 512, 256]
    row = lambda i: (i, 0)
    fixed = lambda i: (0, 0)
    in_specs = [pl.BlockSpec((TM_INPROJ, D_MODEL), row), pl.BlockSpec((1, D_MODEL), fixed)]
    in_specs += [pl.BlockSpec((D_MODEL, w), fixed) for w in widths[:6]]
    in_specs += [pl.BlockSpec((D_MODEL, LANES), fixed), pl.BlockSpec((LANES, GLA_KEY), fixed),
                 pl.BlockSpec((1, GLA_KEY), fixed)]
    return pl.pallas_call(
        _inproj_kernel,
        grid=(t // TM_INPROJ,),
        in_specs=in_specs,
        out_specs=[pl.BlockSpec((TM_INPROJ, w), row) for w in widths],
        out_shape=[jax.ShapeDtypeStruct((t, w), F32) for w in widths],
        compiler_params=pltpu.CompilerParams(dimension_semantics=("arbitrary",),
                                             vmem_limit_bytes=VMEM_LIMIT),
        name="inproj",
    )(x2, norm_mix.reshape(1, D_MODEL), *segs, wal, wup, b_alpha.reshape(1, GLA_KEY))


def _rglru_kernel(xr_ref, yr_ref, cw_ref, cb_ref, wa_ref, ba_ref, wx_ref, bx_ref, lam_ref, gn_ref,
                  o_ref, tail_ref, hc_ref):
    ts = xr_ref.shape[0]
    n_groups = ts // SUBLANES
    grouped = (n_groups, SUBLANES, RG_WIDTH)
    sub = lax.broadcasted_iota(jnp.int32, grouped, 1)

    @pl.when(pl.program_id(1) == 0)
    def _():
        tail_ref[...] = jnp.zeros_like(tail_ref)
        hc_ref[...] = jnp.zeros_like(hc_ref)

    x = xr_ref[...]
    x_grp = x.reshape(grouped)
    x_prev = jnp.concatenate([tail_ref[...], x[:ts - SUBLANES, :]], axis=0).reshape(grouped)
    tail_ref[...] = x[ts - SUBLANES:ts, :]
    cw = cw_ref[...]
    xc = cb_ref[...] + cw[CONV_WIDTH - 1:CONV_WIDTH, :] * x
    for s in range(1, CONV_WIDTH):
        mixed = jnp.where(sub >= SUBLANES - s, x_prev, x_grp)
        shifted = pltpu.roll(mixed, s, axis=1)
        xc = xc + cw[CONV_WIDTH - 1 - s:CONV_WIDTH - s, :] * shifted.reshape(ts, RG_WIDTH)

    xb = xc.astype(BF16)
    r = jax.nn.sigmoid(_dot(xb, wa_ref[...]) + ba_ref[...])
    gate_i = jax.nn.sigmoid(_dot(xb, wx_ref[...]) + bx_ref[...])
    log_a = (-C_RG) * r * _softplus(-lam_ref[...])
    a = jnp.exp(log_a)
    u = jnp.sqrt(jnp.tanh(-log_a) * (1.0 + a * a)) * (gate_i * xc)

    a = a.reshape(grouped)
    u = u.reshape(grouped)
    d = 1
    while d < SUBLANES:
        keep = sub >= d
        a_sh = jnp.where(keep, pltpu.roll(a, d, axis=1), 1.0)
        u_sh = jnp.where(keep, pltpu.roll(u, d, axis=1), 0.0)
        u = a * u_sh + u
        a = a * a_sh
        d *= 2
    carry = hc_ref[0:1, :]
    groups = []
    for j in range(n_groups):
        h_j = u[j] + a[j] * carry
        groups.append(h_j)
        carry = h_j[SUBLANES - 1:SUBLANES, :]
    h = jnp.concatenate(groups, axis=0)
    hc_ref[...] = jnp.broadcast_to(carry, hc_ref.shape)

    y = h * jax.nn.gelu(yr_ref[...])
    o_ref[...] = _rms(y, gn_ref[...])


def _block_diag(w):
    eye = jnp.eye(RG_BLOCKS, dtype=w.dtype)
    return jnp.einsum('hij,hg->higj', w, eye).reshape(RG_WIDTH, RG_WIDTH)


def _rglru(xr, yr, conv_w, conv_b, w_a, b_a, w_x, b_x, lam, gn, bsz, seq):
    nt = seq // TS_RG
    row = lambda b, i: (b * nt + i, 0)
    fixed = lambda b, i: (0, 0)
    vec = lambda v: v.reshape(1, RG_WIDTH).astype(F32)
    return pl.pallas_call(
        _rglru_kernel,
        grid=(bsz, nt),
        in_specs=[pl.BlockSpec((TS_RG, RG_WIDTH), row), pl.BlockSpec((TS_RG, RG_WIDTH), row),
                  pl.BlockSpec((CONV_WIDTH, RG_WIDTH), fixed), pl.BlockSpec((1, RG_WIDTH), fixed),
                  pl.BlockSpec((RG_WIDTH, RG_WIDTH), fixed), pl.BlockSpec((1, RG_WIDTH), fixed),
                  pl.BlockSpec((RG_WIDTH, RG_WIDTH), fixed), pl.BlockSpec((1, RG_WIDTH), fixed),
                  pl.BlockSpec((1, RG_WIDTH), fixed), pl.BlockSpec((1, RG_WIDTH), fixed)],
        out_specs=pl.BlockSpec((TS_RG, RG_WIDTH), row),
        out_shape=jax.ShapeDtypeStruct((bsz * seq, RG_WIDTH), F32),
        scratch_shapes=[pltpu.VMEM((SUBLANES, RG_WIDTH), F32),
                        pltpu.VMEM((SUBLANES, RG_WIDTH), F32)],
        compiler_params=pltpu.CompilerParams(dimension_semantics=("arbitrary", "arbitrary"),
                                             vmem_limit_bytes=VMEM_LIMIT),
        name="rglru",
    )(xr, yr, conv_w.astype(F32), vec(conv_b), _block_diag(w_a).astype(BF16), vec(b_a),
      _block_diag(w_x).astype(BF16), vec(b_x), vec(lam), vec(gn))


def _gla_kernel(q_ref, k_ref, v_ref, g_ref, la_ref, gn_ref, o_ref, st_ref):
    bsz, ts = q_ref.shape[0], q_ref.shape[1]
    n_chunks = ts // CHUNK

    @pl.when(pl.program_id(0) == 0)
    def _():
        st_ref[...] = jnp.zeros_like(st_ref)

    ri = lax.broadcasted_iota(jnp.int32, (ts, ts), 0)
    ci = lax.broadcasted_iota(jnp.int32, (ts, ts), 1)
    chunk_bits = CHUNK.bit_length() - 1
    same_chunk = lax.shift_right_logical(ri, chunk_bits) == lax.shift_right_logical(ci, chunk_bits)
    prefix = jnp.where(jnp.logical_and(same_chunk, ri >= ci), 1.0, 0.0).astype(BF16)
    total = jnp.where(same_chunk, 1.0, 0.0).astype(BF16)
    causal = (ri >= ci)[:CHUNK, :CHUNK]
    scale = GLA_DK ** -0.5
    gn = gn_ref[...]
    nt_dims = (((1,), (1,)), ((), ()))
    tn_dims = (((0,), (0,)), ((), ()))

    for bi in range(bsz):
        la = la_ref[bi]
        la_hi = la.astype(BF16)
        la_lo = (la - la_hi.astype(F32)).astype(BF16)
        b = _dot(prefix, la_hi) + _dot(prefix, la_lo)
        b_tot = _dot(total, la_hi) + _dot(total, la_lo)
        kk = k_ref[bi]
        q_s = (q_ref[bi] * scale) * jnp.exp(b)
        k_s = kk * jnp.exp(-b)
        k_end = kk * jnp.exp(b_tot - b)
        decay = jnp.exp(b_tot)

        o_intra, kv, q_heads = [], [], []
        for c in range(n_chunks):
            rows = slice(c * CHUNK, (c + 1) * CHUNK)
            o_c, kv_c, q_c = [], [], []
            for h in range(GLA_HEADS):
                ks = slice(h * GLA_DK, (h + 1) * GLA_DK)
                vs = slice(h * GLA_DV, (h + 1) * GLA_DV)
                qh = q_s[rows, ks].astype(BF16)
                vb = v_ref[bi, rows, vs].astype(BF16)
                att = lax.dot_general(qh, k_s[rows, ks].astype(BF16), nt_dims, preferred_element_type=F32)
                att = jnp.where(causal, att, 0.0).astype(BF16)
                o_c.append(_dot(att, vb))
                kv_c.append(lax.dot_general(vb, k_end[rows, ks].astype(BF16), tn_dims,
                                            preferred_element_type=F32))
                q_c.append(qh)
            o_intra.append(o_c)
            q_heads.append(q_c)
            kv.append(jnp.concatenate(kv_c, axis=1))

        st = st_ref[bi]
        for c in range(n_chunks):
            rows = slice(c * CHUNK, (c + 1) * CHUNK)
            for h in range(GLA_HEADS):
                ks = slice(h * GLA_DK, (h + 1) * GLA_DK)
                vs = slice(h * GLA_DV, (h + 1) * GLA_DV)
                o = o_intra[c][h] + lax.dot_general(q_heads[c][h], st[:, ks].astype(BF16), nt_dims,
                                                    preferred_element_type=F32)
                o_ref[bi, rows, vs] = _rms(o, gn) * jax.nn.silu(g_ref[bi, rows, vs])
            st = decay[c * CHUNK:c * CHUNK + 1, :] * st + kv[c]
        st_ref[bi] = st


def _gla(q, k, v, g, la, gn, bsz, seq):
    nt = seq // TS_GLA
    blk = lambda w: pl.BlockSpec((bsz, TS_GLA, w), lambda i: (0, i, 0))
    r3 = lambda a: a.reshape(bsz, seq, a.shape[-1])
    out = pl.pallas_call(
        _gla_kernel,
        grid=(nt,),
        in_specs=[blk(GLA_KEY), blk(GLA_KEY), blk(GLA_VAL), blk(GLA_VAL), blk(GLA_KEY),
                  pl.BlockSpec((1, GLA_DV), lambda i: (0, 0))],
        out_specs=blk(GLA_VAL),
        out_shape=jax.ShapeDtypeStruct((bsz, seq, GLA_VAL), F32),
        scratch_shapes=[pltpu.VMEM((bsz, GLA_DV, GLA_KEY), F32)],
        compiler_params=pltpu.CompilerParams(dimension_semantics=("arbitrary",),
                                             vmem_limit_bytes=VMEM_LIMIT),
        name="gla",
    )(r3(q), r3(k), r3(v), r3(g), r3(la), gn.reshape(1, GLA_DV).astype(F32))
    return out.reshape(bsz * seq, GLA_VAL)


def _outproj_kernel(yrg_ref, ygla_ref, x_ref, wo1_ref, wo2_ref, gn_ref, wr_ref, br_ref,
                    h_o, hn_o, rt_o):
    tm = x_ref.shape[0]
    h = x_ref[...] + _dot(yrg_ref[...].astype(BF16), wo1_ref[...]) \
        + _dot(ygla_ref[...].astype(BF16), wo2_ref[...])
    h_o[...] = h
    hn = _rms(h, gn_ref[...])
    hn_hi = hn.astype(BF16)
    hn_o[...] = hn_hi
    hn_lo = (hn - hn_hi.astype(F32)).astype(BF16)
    hi_parts = _dot(hn_hi, wr_ref[...])
    logits = hi_parts[:, :LANES] + hi_parts[:, LANES:] + _dot(hn_lo, wr_ref[:, :LANES]) + br_ref[...]

    lane = lax.broadcasted_iota(jnp.int32, (tm, LANES), 1).astype(F32)
    neg = -jnp.inf
    glog = jnp.where(lane < N_GROUPS, logits, neg)
    gmax = jnp.max(glog, axis=-1, keepdims=True)
    gidx = jnp.min(jnp.where(glog == gmax, lane, float(LANES)), axis=-1, keepdims=True)
    g_w = 1.0 / jnp.sum(jnp.exp(glog - gmax), axis=-1, keepdims=True)
    lo = N_GROUPS + gidx * EXPERTS_PER_GROUP
    in_group = jnp.logical_and(lane >= lo, lane < lo + EXPERTS_PER_GROUP)
    le = jnp.where(in_group, logits, neg)
    m1 = jnp.max(le, axis=-1, keepdims=True)
    i1 = jnp.min(jnp.where(le == m1, lane, float(LANES)), axis=-1, keepdims=True)
    le2 = jnp.where(lane == i1, neg, le)
    m2 = jnp.max(le2, axis=-1, keepdims=True)
    i2 = jnp.min(jnp.where(le2 == m2, lane, float(LANES)), axis=-1, keepdims=True)
    t2 = jnp.exp(m2 - m1)
    w1 = g_w / (1.0 + t2)
    w2 = g_w * t2 / (1.0 + t2)
    info = jnp.where(lane == 0.0, i1 - N_GROUPS,
                     jnp.where(lane == 1.0, i2 - N_GROUPS,
                               jnp.where(lane == 2.0, w1, jnp.where(lane == 3.0, w2, 0.0))))
    rt_o[...] = info.T[0:SUBLANES, :]


def _outproj(y_rg, y_gla, x2, w_out, norm_ffn, w_rg, b_rg, w_re, b_re):
    t = x2.shape[0]
    wo = w_out.astype(BF16)
    wr = jnp.pad(jnp.concatenate([w_rg, w_re], axis=1).astype(F32),
                 ((0, 0), (0, LANES - N_GROUPS - N_EXPERTS)))
    wr_hi = wr.astype(BF16)
    wr = jnp.concatenate([wr_hi, (wr - wr_hi.astype(F32)).astype(BF16)], axis=1)
    br = jnp.pad(jnp.concatenate([b_rg, b_re]).astype(F32), (0, LANES - N_GROUPS - N_EXPERTS))
    row = lambda i: (i, 0)
    fixed = lambda i: (0, 0)
    return pl.pallas_call(
        _outproj_kernel,
        grid=(t // TM_PROJ,),
        in_specs=[pl.BlockSpec((TM_PROJ, RG_WIDTH), row), pl.BlockSpec((TM_PROJ, GLA_VAL), row),
                  pl.BlockSpec((TM_PROJ, D_MODEL), row),
                  pl.BlockSpec((RG_WIDTH, D_MODEL), fixed), pl.BlockSpec((GLA_VAL, D_MODEL), fixed),
                  pl.BlockSpec((1, D_MODEL), fixed), pl.BlockSpec((D_MODEL, 2 * LANES), fixed),
                  pl.BlockSpec((1, LANES), fixed)],
        out_specs=[pl.BlockSpec((TM_PROJ, D_MODEL), row), pl.BlockSpec((TM_PROJ, D_MODEL), row),
                   pl.BlockSpec((SUBLANES, TM_PROJ), lambda i: (0, i))],
        out_shape=[jax.ShapeDtypeStruct((t, D_MODEL), F32), jax.ShapeDtypeStruct((t, D_MODEL), BF16),
                   jax.ShapeDtypeStruct((SUBLANES, t), F32)],
        compiler_params=pltpu.CompilerParams(dimension_semantics=("arbitrary",),
                                             vmem_limit_bytes=VMEM_LIMIT),
        name="outproj",
    )(y_rg, y_gla, x2, wo[:RG_WIDTH], wo[RG_WIDTH:], norm_ffn.reshape(1, D_MODEL).astype(F32),
      wr, br.reshape(1, LANES))


def _plan_kernel(rt_ref, lpos_o, meta_o, runs_o, tri_ref, *, n_tok, nb_max):
    n_tiles = n_tok // TT
    esub = lax.broadcasted_iota(jnp.int32, (N_EXPERTS, TT), 0).astype(F32)

    def onehots(i):
        off = pl.multiple_of(i * TT, TT)
        e1 = rt_ref[0:1, pl.ds(off, TT)]
        e2 = rt_ref[1:2, pl.ds(off, TT)]
        m1 = jnp.where(esub == e1, 1.0, 0.0)
        m2 = jnp.where(esub == e2, 1.0, 0.0)
        return off, m1, m2

    def count_body(i, cnt):
        _, m1, m2 = onehots(i)
        return cnt + jnp.sum(m1 + m2, axis=1, keepdims=True)

    counts = lax.fori_loop(0, n_tiles, count_body, jnp.zeros((N_EXPERTS, 1), F32))
    nblk = jnp.floor((counts + (BM - 1)) * (1.0 / BM))
    ei = lax.broadcasted_iota(jnp.int32, (N_EXPERTS, N_EXPERTS), 0)
    ej = lax.broadcasted_iota(jnp.int32, (N_EXPERTS, N_EXPERTS), 1)
    nblk_row = jnp.sum(jnp.where(ei == ej, nblk, 0.0), axis=0, keepdims=True)
    bstart = jnp.sum(jnp.where(ej < ei, nblk_row, 0.0), axis=1, keepdims=True)
    bend = bstart + nblk
    n_used = jnp.sum(nblk, axis=0, keepdims=True)

    meta_w = meta_o.shape[1]
    blane = lax.broadcasted_iota(jnp.int32, (N_EXPERTS, meta_w), 1).astype(F32)
    owner = jnp.sum(jnp.where(bend <= blane, 1.0, 0.0), axis=0, keepdims=True)
    owner = jnp.minimum(owner, N_EXPERTS - 1.0)
    lane1 = lax.broadcasted_iota(jnp.int32, (1, meta_w), 1)
    meta_o[0:1, :] = jnp.where(lane1 == nb_max, n_used, owner).astype(jnp.int32)
    nonempty = jnp.where(nblk > 0.0, 1.0, 0.0)
    meta_o[1:2, :] = jnp.sum(jnp.where(bend <= blane, nonempty, 0.0), axis=0,
                             keepdims=True).astype(jnp.int32)
    owned = jnp.logical_and(bstart <= blane, blane < bend)
    next_start = jnp.sum(jnp.where(owned, bend, 0.0), axis=0, keepdims=True)
    next_owner = jnp.sum(jnp.where(bend <= next_start, 1.0, 0.0), axis=0, keepdims=True)
    meta_o[2:3, :] = jnp.where(next_start < n_used, next_owner, -1.0).astype(jnp.int32)
    valid_end = jnp.sum(jnp.where(owned, bstart * float(BM) + counts, 0.0), axis=0, keepdims=True)
    meta_o[3:4, :] = jnp.clip(valid_end - lane1.astype(F32) * float(BM), 0.0, float(BM)).astype(jnp.int32)

    ti = lax.broadcasted_iota(jnp.int32, (TT, TT), 0)
    tj = lax.broadcasted_iota(jnp.int32, (TT, TT), 1)
    tri_ref[...] = jnp.where(ti <= tj, 1.0, 0.0).astype(BF16)
    tlane = lax.broadcasted_iota(jnp.int32, (N_EXPERTS, LANES), 1)

    def tile_body(i, carry):
        first_slot, t_cnt, t_slot, t_rank = carry
        off, m1, m2 = onehots(i)
        m = m1 + m2
        incl = _dot(m.astype(BF16), tri_ref[...])
        cnt = incl[:, TT - 1:TT]
        cnt_row = jnp.sum(jnp.where(ei == ej, cnt, 0.0), axis=0, keepdims=True)
        first_rank = jnp.sum(jnp.where(ej < ei, cnt_row, 0.0), axis=1, keepdims=True)
        rank = first_rank + incl - m
        lpos_o[0:1, pl.ds(off, TT)] = jnp.sum(m1 * rank, axis=0, keepdims=True).astype(jnp.int32)
        lpos_o[1:2, pl.ds(off, TT)] = jnp.sum(m2 * rank, axis=0, keepdims=True).astype(jnp.int32)
        here = tlane == i
        return (first_slot + cnt, jnp.where(here, cnt, t_cnt), jnp.where(here, first_slot, t_slot),
                jnp.where(here, first_rank, t_rank))

    zeros = jnp.zeros((N_EXPERTS, LANES), F32)
    _, t_cnt, t_slot, t_rank = lax.fori_loop(0, n_tiles, tile_body,
                                             (bstart * float(BM), zeros, zeros, zeros))
    runs_o[0] = t_cnt.astype(jnp.int32)
    runs_o[1] = t_slot.astype(jnp.int32)
    runs_o[2] = t_rank.astype(jnp.int32)
    pad = jnp.where(tlane == 0, bstart * float(BM) + counts,
                    jnp.where(tlane == 1, nblk * float(BM) - counts, 0.0))
    runs_o[3] = pad.astype(jnp.int32)


def _plan(rt, n_tok, nb_max):
    assert n_tok // TT <= LANES, "run tables hold one token tile per lane"
    meta_w = ((nb_max + 1 + LANES - 1) // LANES) * LANES
    return pl.pallas_call(
        functools.partial(_plan_kernel, n_tok=n_tok, nb_max=nb_max),
        out_shape=[jax.ShapeDtypeStruct((2, n_tok), jnp.int32),
                   jax.ShapeDtypeStruct((4, meta_w), jnp.int32),
                   jax.ShapeDtypeStruct((RUN_FIELDS, N_EXPERTS, LANES), jnp.int32)],
        scratch_shapes=[pltpu.VMEM((TT, TT), BF16)],
        compiler_params=pltpu.CompilerParams(vmem_limit_bytes=VMEM_LIMIT),
        name="plan",
    )(rt)


def _run_entry(runs_ref, field, expert, tile):
    return runs_ref[(field * N_EXPERTS + expert) * LANES + tile]


def _dispatch_kernel(runs_ref, meta_ref, hn_ref, lpos_ref, xs_hbm, xsbuf, zbuf, sem, zsem,
                     *, n_tiles, nb_max, n_pad):
    i = pl.program_id(0)
    slot = lax.rem(i, 2)

    @pl.when(i == 0)
    def _():
        zbuf[...] = jnp.zeros_like(zbuf)
        for e in range(N_EXPERTS):
            _copy_run(zbuf, 0, xs_hbm, _run_entry(runs_ref, 3, e, 0), _run_entry(runs_ref, 3, e, 1),
                      zsem)

        def unused_block(blk, carry):
            dst = xs_hbm.at[pl.ds(pl.multiple_of(blk * (BM * TILE_ROWS), BM * TILE_ROWS),
                                  BM * TILE_ROWS), :]
            pltpu.make_async_copy(zbuf, dst, zsem).start()
            return carry

        lax.fori_loop(meta_ref[nb_max], nb_max, unused_block, 0)

    @pl.when(i >= 2)
    def _():
        _wait_rows(xsbuf.at[slot], TILE_SLOTS, sem.at[slot])

    rank = lax.broadcasted_iota(jnp.int32, (TILE_SLOTS, TT), 0)
    onehot = jnp.where(rank == lpos_ref[0:1, :], 1.0, jnp.where(rank == lpos_ref[1:2, :], 1.0, 0.0))
    _store_token_tiles(xsbuf.at[slot], _dot(onehot.astype(BF16), hn_ref[...]), already_bf16=True)
    for e in range(N_EXPERTS):
        _copy_run(xsbuf.at[slot], _run_entry(runs_ref, 2, e, i), xs_hbm, _run_entry(runs_ref, 1, e, i),
                  _run_entry(runs_ref, 0, e, i), sem.at[slot])

    @pl.when(i == n_tiles - 1)
    def _():
        _wait_rows(xsbuf.at[slot], TILE_SLOTS, sem.at[slot])
        if n_tiles > 1:
            _wait_rows(xsbuf.at[1 - slot], TILE_SLOTS, sem.at[1 - slot])
        _wait_rows(xs_hbm, n_pad, zsem)


def _dispatch(runs, meta, hn, lpos, n_tok, nb_max):
    n_tiles = n_tok // TT
    cap = nb_max * BM
    grid_spec = pltpu.PrefetchScalarGridSpec(
        num_scalar_prefetch=2,
        grid=(n_tiles,),
        in_specs=[pl.BlockSpec((TT, D_MODEL), lambda i, runs_ref, meta_ref: (i, 0)),
                  pl.BlockSpec((2, TT), lambda i, runs_ref, meta_ref: (0, i))],
        out_specs=pl.BlockSpec(memory_space=pl.ANY),
        scratch_shapes=[pltpu.VMEM((2, TILE_SLOTS * TILE_ROWS, LANES), U32),
                        pltpu.VMEM((BM * TILE_ROWS, LANES), U32),
                        pltpu.SemaphoreType.DMA((2,)), pltpu.SemaphoreType.DMA],
    )
    return pl.pallas_call(
        functools.partial(_dispatch_kernel, n_tiles=n_tiles, nb_max=nb_max, n_pad=cap - 2 * n_tok),
        grid_spec=grid_spec,
        out_shape=jax.ShapeDtypeStruct((cap * TILE_ROWS, LANES), U32),
        compiler_params=pltpu.CompilerParams(dimension_semantics=("arbitrary",),
                                             vmem_limit_bytes=VMEM_LIMIT),
        name="dispatch",
    )(runs.reshape(-1), meta.reshape(-1), hn, lpos)


def _experts_kernel(meta_ref, xs_ref, wg_hbm, wu_hbm, wd_hbm, y_ref, wf32, wbf, wsem, *, nb_max, meta_w):
    b = pl.program_id(0)
    n_used = meta_ref[nb_max]

    def start_weights(expert, buf):
        for j, w_hbm in enumerate((wg_hbm, wu_hbm, wd_hbm)):
            pltpu.make_async_copy(w_hbm.at[expert], wf32.at[buf, j], wsem.at[buf]).start(priority=1)

    @pl.when(b < n_used)
    def _():
        owner = meta_ref[b]

        @pl.when(b == 0)
        def _():
            start_weights(owner, 0)

        @pl.when(jnp.logical_or(b == 0, owner != meta_ref[jnp.maximum(b - 1, 0)]))
        def _():
            buf = jnp.bitwise_and(meta_ref[meta_w + b], 1)
            pltpu.make_async_copy(wf32.at[buf], wf32.at[buf], wsem.at[buf]).wait()
            next_owner = meta_ref[2 * meta_w + b]

            @pl.when(next_owner >= 0)
            def _():
                start_weights(next_owner, 1 - buf)

            for j in range(3):
                wbf[j] = wf32[buf, j].astype(BF16)

        def geglu(n_rows):
            rows = pl.ds(0, n_rows * TILE_ROWS)
            x = _load_token_tiles(xs_ref.at[rows, :], n_rows).astype(BF16)
            mid = (jax.nn.gelu(_dot(x, wbf[0])) * _dot(x, wbf[1])).astype(BF16)
            _store_token_tiles(y_ref.at[rows, :], _dot(mid, wbf[2]))

        n_groups = lax.shift_right_logical(meta_ref[3 * meta_w + b] + (SKIP_ROWS - 1),
                                           SKIP_ROWS.bit_length() - 1)
        for g in range(1, BM // SKIP_ROWS + 1):
            @pl.when(n_groups == g)
            def _(g=g):
                geglu(g * SKIP_ROWS)
                if g * SKIP_ROWS < BM:
                    rest = (BM - g * SKIP_ROWS) * TILE_ROWS
                    y_ref[pl.ds(g * SKIP_ROWS * TILE_ROWS, rest), :] = jnp.zeros((rest, LANES), U32)

    @pl.when(b >= n_used)
    def _():
        y_ref[...] = jnp.zeros_like(y_ref)


def _experts(meta, xs, w_gate, w_up, w_down, nb_max):
    whole = pl.BlockSpec(memory_space=pl.ANY)
    rows = pl.BlockSpec((BM * TILE_ROWS, LANES), lambda b, meta_ref: (b, 0))
    grid_spec = pltpu.PrefetchScalarGridSpec(
        num_scalar_prefetch=1,
        grid=(nb_max,),
        in_specs=[rows, whole, whole, whole],
        out_specs=rows,
        scratch_shapes=[pltpu.VMEM((2, 3, D_MODEL, D_MODEL), F32),
                        pltpu.VMEM((3, D_MODEL, D_MODEL), BF16),
                        pltpu.SemaphoreType.DMA((2,))],
    )
    return pl.pallas_call(
        functools.partial(_experts_kernel, nb_max=nb_max, meta_w=meta.shape[1]),
        grid_spec=grid_spec,
        out_shape=jax.ShapeDtypeStruct(xs.shape, U32),
        compiler_params=pltpu.CompilerParams(dimension_semantics=("arbitrary",),
                                             vmem_limit_bytes=VMEM_LIMIT),
        name="experts",
    )(meta.reshape(-1), xs, w_gate, w_up, w_down)


def _combine_kernel(runs_ref, h_ref, lpos_ref, rt_ref, gn_ref, y_hbm, o_ref, ysbuf, sem, *, n_tiles):
    i = pl.program_id(0)
    slot = lax.rem(i, 2)

    def fetch(tile, buf):
        for e in range(N_EXPERTS):
            _copy_run(y_hbm, _run_entry(runs_ref, 1, e, tile), ysbuf.at[buf],
                      _run_entry(runs_ref, 2, e, tile), _run_entry(runs_ref, 0, e, tile), sem.at[buf])

    @pl.when(i == 0)
    def _():
        fetch(0, 0)

    @pl.when(i + 1 < n_tiles)
    def _():
        fetch(i + 1, 1 - slot)

    _wait_rows(ysbuf.at[slot], TILE_SLOTS, sem.at[slot])

    rank = lax.broadcasted_iota(jnp.int32, (TILE_SLOTS, TT), 0)
    first = rank == lpos_ref[0:1, :]
    second = rank == lpos_ref[1:2, :]
    slot_w = jnp.sum(jnp.where(first, rt_ref[2:3, :], 0.0) + jnp.where(second, rt_ref[3:4, :], 0.0),
                     axis=1, keepdims=True)
    onehot = jnp.where(first, 1.0, jnp.where(second, 1.0, 0.0)).astype(BF16)
    ys = (_load_token_tiles(ysbuf.at[slot], TILE_SLOTS) * slot_w).astype(BF16)
    moe = lax.dot_general(onehot, ys, (((0,), (0,)), ((), ())), preferred_element_type=F32)
    o_ref[...] = _rms(h_ref[...] + moe, gn_ref[...])


def _combine(runs, h, lpos, rt, norm_final, y, n_tok):
    n_tiles = n_tok // TT
    grid_spec = pltpu.PrefetchScalarGridSpec(
        num_scalar_prefetch=1,
        grid=(n_tiles,),
        in_specs=[pl.BlockSpec((TT, D_MODEL), lambda i, runs_ref: (i, 0)),
                  pl.BlockSpec((2, TT), lambda i, runs_ref: (0, i)),
                  pl.BlockSpec((SUBLANES, TT), lambda i, runs_ref: (0, i)),
                  pl.BlockSpec((1, D_MODEL), lambda i, runs_ref: (0, 0)),
                  pl.BlockSpec(memory_space=pl.ANY)],
        out_specs=pl.BlockSpec((TT, D_MODEL), lambda i, runs_ref: (i, 0)),
        scratch_shapes=[pltpu.VMEM((2, TILE_SLOTS * TILE_ROWS, LANES), U32),
                        pltpu.SemaphoreType.DMA((2,))],
    )
    return pl.pallas_call(
        functools.partial(_combine_kernel, n_tiles=n_tiles),
        grid_spec=grid_spec,
        out_shape=jax.ShapeDtypeStruct((n_tok, D_MODEL), F32),
        compiler_params=pltpu.CompilerParams(dimension_semantics=("arbitrary",),
                                             vmem_limit_bytes=VMEM_LIMIT),
        name="combine",
    )(runs.reshape(-1), h, lpos, rt, norm_final.reshape(1, D_MODEL).astype(F32), y)


def kernel(x, norm_mix, w_in, conv_w, conv_b, w_rg_a, b_rg_a, w_rg_x, b_rg_x, rg_lambda, rg_norm, w_alpha_up, b_alpha, gla_norm, w_out, norm_ffn, w_router_group, b_router_group, w_router_expert, b_router_expert, w_exp_gate, w_exp_up, w_exp_down, norm_final):
    bsz, seq, d = x.shape
    assert d == D_MODEL and norm_mix.shape[0] == 1, "single-layer model of width D_MODEL expected"
    n_tok = bsz * seq
    nb_max = (2 * n_tok + N_EXPERTS * (BM - 1)) // BM

    x2 = x.reshape(n_tok, d)
    xr, yr, q, k, v, g, la = _inproj(x2, norm_mix[0], w_in[0], w_alpha_up[0], b_alpha[0])
    y_rg = _rglru(xr, yr, conv_w[0], conv_b[0], w_rg_a[0], b_rg_a[0], w_rg_x[0], b_rg_x[0],
                  rg_lambda[0], rg_norm[0], bsz, seq)
    y_gla = _gla(q, k, v, g, la, gla_norm[0], bsz, seq)
    h, hn, rt = _outproj(y_rg, y_gla, x2, w_out[0], norm_ffn[0], w_router_group[0],
                         b_router_group[0], w_router_expert[0], b_router_expert[0])
    lpos, meta, runs = _plan(rt, n_tok, nb_max)
    xs = _dispatch(runs, meta, hn, lpos, n_tok, nb_max)
    y = _experts(meta, xs, w_exp_gate[0], w_exp_up[0], w_exp_down[0], nb_max)
    out = _combine(runs, h, lpos, rt, norm_final, y, n_tok)
    return out.reshape(bsz, seq, d)
```

```python
import functools

import jax
import jax.numpy as jnp
from jax import lax
from jax.experimental import pallas as pl
from jax.experimental.pallas import tpu as pltpu

F32 = jnp.float32
BF16 = jnp.bfloat16

D_MODEL = 1024
RG_WIDTH = 512
RG_BLOCKS = 8
RG_BLOCK = 64
CONV_WIDTH = 4
C_RG = 8.0
GLA_HEADS = 4
GLA_VAL = 512
GLA_KEY = 256
GLA_DK = 64
GLA_DV = 128
GATE_RANK = 16
GATE_NORM = 16.0
CHUNK = 64
N_GROUPS = 4
EXPERTS_PER_GROUP = 8
N_EXPERTS = 32
EPS = 1e-6

LANES = 128
SUBLANES = 8
VMEM_LIMIT = 56 * 1024 * 1024

TM_PROJ = 1024
TM_INPROJ = 1024
TS_RG = 512
TS_GLA = 512
BM = 512
SKIP_ROWS = 128
TT = 512
TILE_SLOTS = 2 * TT
RUN_FIELDS = 4
RUN_LONG = 64


def _dot(a, b):
    return jnp.dot(a, b, preferred_element_type=F32)


def _softplus(z):
    return jnp.maximum(z, 0.0) + jnp.log1p(jnp.exp(-jnp.abs(z)))


def _rms(x, g):
    return x * lax.rsqrt(jnp.mean(x * x, axis=-1, keepdims=True) + EPS) * g


U32 = jnp.uint32
TILE_ROWS = D_MODEL // 2 // LANES


def _copy_run(src, src_row, dst, dst_row, n_rows, sem, max_rows):
    def pieces(bits):
        for k in bits:
            rows = 1 << k
            off = jnp.bitwise_and(n_rows, -2 * rows)

            @pl.when(jnp.bitwise_and(n_rows, rows) != 0)
            def _():
                s = src.at[pl.ds(pl.multiple_of((src_row + off) * TILE_ROWS, TILE_ROWS), rows * TILE_ROWS), :]
                d = dst.at[pl.ds(pl.multiple_of((dst_row + off) * TILE_ROWS, TILE_ROWS), rows * TILE_ROWS), :]
                pltpu.make_async_copy(s, d, sem).start()

    n_bits, long_bit = max_rows.bit_length(), RUN_LONG.bit_length() - 1
    if n_bits > long_bit:
        @pl.when(n_rows >= RUN_LONG)
        def _():
            pieces(range(long_bit, n_bits))
    pieces(range(min(long_bit, n_bits)))


def _wait_rows(ref, n_rows, sem):
    view = ref.at[pl.ds(0, n_rows * TILE_ROWS), :]
    pltpu.make_async_copy(view, view, sem).wait()


def _store_token_tiles(ref, val, already_bf16=False):
    n = val.shape[0]
    if not already_bf16:
        val = val.astype(BF16).astype(F32)
    bits = lax.bitcast_convert_type(val, U32)
    packed = jnp.bitwise_or(bits[:, :D_MODEL // 2], jnp.right_shift(bits[:, D_MODEL // 2:], 16))
    for c in range(TILE_ROWS):
        ref[pl.ds(c, n, stride=TILE_ROWS), :] = packed[:, c * LANES:(c + 1) * LANES]


def _load_token_tiles(ref, n):
    words = [ref[pl.ds(c, n, stride=TILE_ROWS), :] for c in range(TILE_ROWS)]
    high = [lax.bitcast_convert_type(jnp.bitwise_and(w, jnp.uint32(0xFFFF0000)), F32) for w in words]
    low = [lax.bitcast_convert_type(jnp.left_shift(w, 16), F32) for w in words]
    return jnp.concatenate(high + low, axis=1)


def _inproj_kernel(x_ref, g_ref, wxr, wyr, wq, wk, wv, wg, wal, wup, bal,
                   xr_o, yr_o, q_o, k_o, v_o, g_o, la_o):
    hn = _rms(x_ref[...], g_ref[...]).astype(BF16)
    xr_o[...] = _dot(hn, wxr[...])
    yr_o[...] = _dot(hn, wyr[...])
    q_o[...] = _dot(hn, wq[...])
    k_o[...] = _dot(hn, wk[...])
    v_o[...] = _dot(hn, wv[...])
    g_o[...] = _dot(hn, wg[...])
    a3 = _dot(hn, wal[...])
    a3_hi = a3.astype(BF16).astype(F32)
    lane = lax.broadcasted_iota(jnp.int32, a3.shape, 1)
    use_low = jnp.logical_and(lane >= GATE_RANK, lane < 2 * GATE_RANK)
    z = _dot(jnp.where(use_low, a3 - a3_hi, a3_hi).astype(BF16), wup[...]) + bal[...]
    log_sig = jnp.minimum(z, 0.0) - jnp.log1p(jnp.exp(-jnp.abs(z)))
    la_o[...] = log_sig * (1.0 / GATE_NORM)


def _inproj(x2, norm_mix, w_in, w_alpha_up, b_alpha):
    t = x2.shape[0]
    c = [0, 512, 1024, 1280, 1536, 2048, 2560, 2576]
    wb = w_in.astype(BF16)
    segs = [wb[:, c[i]:c[i + 1]] for i in range(6)]
    w_low = wb[:, c[6]:c[7]]
    wal = jnp.pad(jnp.concatenate([w_low, w_low, w_low], axis=1), ((0, 0), (0, LANES - 3 * GATE_RANK)))
    up = w_alpha_up.astype(F32)
    up_hi = up.astype(BF16)
    up_lo = (up - up_hi.astype(F32)).astype(BF16)
    wup = jnp.pad(jnp.concatenate([up_hi, up_hi, up_lo], axis=0), ((0, LANES - 3 * GATE_RANK), (0, 0)))
    widths = [512, 512, 256, 256, 512, 512, 256]
    row = lambda i: (i, 0)
    fixed = lambda i: (0, 0)
    in_specs = [pl.BlockSpec((TM_INPROJ, D_MODEL), row), pl.BlockSpec((1, D_MODEL), fixed)]
    in_specs += [pl.BlockSpec((D_MODEL, w), fixed) for w in widths[:6]]
    in_specs += [pl.BlockSpec((D_MODEL, LANES), fixed), pl.BlockSpec((LANES, GLA_KEY), fixed),
                 pl.BlockSpec((1, GLA_KEY), fixed)]
    return pl.pallas_call(
        _inproj_kernel,
        grid=(t // TM_INPROJ,),
        in_specs=in_specs,
        out_specs=[pl.BlockSpec((TM_INPROJ, w), row) for w in widths],
        out_shape=[jax.ShapeDtypeStruct((t, w), F32) for w in widths],
        compiler_params=pltpu.CompilerParams(dimension_semantics=("arbitrary",),
                                             vmem_limit_bytes=VMEM_LIMIT),
        name="inproj",
    )(x2, norm_mix.reshape(1, D_MODEL), *segs, wal, wup, b_alpha.reshape(1, GLA_KEY))


def _rglru_kernel(xr_ref, yr_ref, cw_ref, cb_ref, wa_ref, ba_ref, wx_ref, bx_ref, lam_ref, gn_ref,
                  o_ref, tail_ref, hc_ref):
    ts = xr_ref.shape[0]
    n_groups = ts // SUBLANES
    grouped = (n_groups, SUBLANES, RG_WIDTH)
    sub = lax.broadcasted_iota(jnp.int32, grouped, 1)

    @pl.when(pl.program_id(1) == 0)
    def _():
        tail_ref[...] = jnp.zeros_like(tail_ref)
        hc_ref[...] = jnp.zeros_like(hc_ref)

    x = xr_ref[...]
    x_grp = x.reshape(grouped)
    x_prev = jnp.concatenate([tail_ref[...], x[:ts - SUBLANES, :]], axis=0).reshape(grouped)
    tail_ref[...] = x[ts - SUBLANES:ts, :]
    cw = cw_ref[...]
    xc = cb_ref[...] + cw[CONV_WIDTH - 1:CONV_WIDTH, :] * x
    for s in range(1, CONV_WIDTH):
        mixed = jnp.where(sub >= SUBLANES - s, x_prev, x_grp)
        shifted = pltpu.roll(mixed, s, axis=1)
        xc = xc + cw[CONV_WIDTH - 1 - s:CONV_WIDTH - s, :] * shifted.reshape(ts, RG_WIDTH)

    xb = xc.astype(BF16)
    r = jax.nn.sigmoid(_dot(xb, wa_ref[...]) + ba_ref[...])
    gate_i = jax.nn.sigmoid(_dot(xb, wx_ref[...]) + bx_ref[...])
    log_a = (-C_RG) * r * _softplus(-lam_ref[...])
    a = jnp.exp(log_a)
    u = jnp.sqrt(jnp.tanh(-log_a) * (1.0 + a * a)) * (gate_i * xc)

    a = a.reshape(grouped)
    u = u.reshape(grouped)
    d = 1
    while d < SUBLANES:
        keep = sub >= d
        a_sh = jnp.where(keep, pltpu.roll(a, d, axis=1), 1.0)
        u_sh = jnp.where(keep, pltpu.roll(u, d, axis=1), 0.0)
        u = a * u_sh + u
        a = a * a_sh
        d *= 2
    carry = hc_ref[0:1, :]
    groups = []
    for j in range(n_groups):
        h_j = u[j] + a[j] * carry
        groups.append(h_j)
        carry = h_j[SUBLANES - 1:SUBLANES, :]
    h = jnp.concatenate(groups, axis=0)
    hc_ref[...] = jnp.broadcast_to(carry, hc_ref.shape)

    y = h * jax.nn.gelu(yr_ref[...])
    o_ref[...] = _rms(y, gn_ref[...])


def _block_diag(w):
    eye = jnp.eye(RG_BLOCKS, dtype=w.dtype)
    return jnp.einsum('hij,hg->higj', w, eye).reshape(RG_WIDTH, RG_WIDTH)


def _rglru(xr, yr, conv_w, conv_b, w_a, b_a, w_x, b_x, lam, gn, bsz, seq):
    nt = seq // TS_RG
    row = lambda b, i: (b * nt + i, 0)
    fixed = lambda b, i: (0, 0)
    vec = lambda v: v.reshape(1, RG_WIDTH).astype(F32)
    return pl.pallas_call(
        _rglru_kernel,
        grid=(bsz, nt),
        in_specs=[pl.BlockSpec((TS_RG, RG_WIDTH), row), pl.BlockSpec((TS_RG, RG_WIDTH), row),
                  pl.BlockSpec((CONV_WIDTH, RG_WIDTH), fixed), pl.BlockSpec((1, RG_WIDTH), fixed),
                  pl.BlockSpec((RG_WIDTH, RG_WIDTH), fixed), pl.BlockSpec((1, RG_WIDTH), fixed),
                  pl.BlockSpec((RG_WIDTH, RG_WIDTH), fixed), pl.BlockSpec((1, RG_WIDTH), fixed),
                  pl.BlockSpec((1, RG_WIDTH), fixed), pl.BlockSpec((1, RG_WIDTH), fixed)],
        out_specs=pl.BlockSpec((TS_RG, RG_WIDTH), row),
        out_shape=jax.ShapeDtypeStruct((bsz * seq, RG_WIDTH), F32),
        scratch_shapes=[pltpu.VMEM((SUBLANES, RG_WIDTH), F32),
                        pltpu.VMEM((SUBLANES, RG_WIDTH), F32)],
        compiler_params=pltpu.CompilerParams(dimension_semantics=("arbitrary", "arbitrary"),
                                             vmem_limit_bytes=VMEM_LIMIT),
        name="rglru",
    )(xr, yr, conv_w.astype(F32), vec(conv_b), _block_diag(w_a).astype(BF16), vec(b_a),
      _block_diag(w_x).astype(BF16), vec(b_x), vec(lam), vec(gn))


def _gla_kernel(q_ref, k_ref, v_ref, g_ref, la_ref, gn_ref, o_ref, st_ref):
    bsz, ts = q_ref.shape[0], q_ref.shape[1]
    n_chunks = ts // CHUNK

    @pl.when(pl.program_id(0) == 0)
    def _():
        st_ref[...] = jnp.zeros_like(st_ref)

    ri = lax.broadcasted_iota(jnp.int32, (ts, ts), 0)
    ci = lax.broadcasted_iota(jnp.int32, (ts, ts), 1)
    chunk_bits = CHUNK.bit_length() - 1
    same_chunk = lax.shift_right_logical(ri, chunk_bits) == lax.shift_right_logical(ci, chunk_bits)
    prefix = jnp.where(jnp.logical_and(same_chunk, ri >= ci), 1.0, 0.0).astype(BF16)
    total = jnp.where(same_chunk, 1.0, 0.0).astype(BF16)
    causal = (ri >= ci)[:CHUNK, :CHUNK]
    scale = GLA_DK ** -0.5
    gn = gn_ref[...]
    nt_dims = (((1,), (1,)), ((), ()))
    tn_dims = (((0,), (0,)), ((), ()))

    for bi in range(bsz):
        la = la_ref[bi]
        la_hi = la.astype(BF16)
        la_lo = (la - la_hi.astype(F32)).astype(BF16)
        b = _dot(prefix, la_hi) + _dot(prefix, la_lo)
        b_tot = _dot(total, la_hi) + _dot(total, la_lo)
        kk = k_ref[bi]
        q_s = (q_ref[bi] * scale) * jnp.exp(b)
        k_s = kk * jnp.exp(-b)
        k_end = kk * jnp.exp(b_tot - b)
        decay = jnp.exp(b_tot)

        o_intra, kv, q_heads = [], [], []
        for c in range(n_chunks):
            rows = slice(c * CHUNK, (c + 1) * CHUNK)
            o_c, kv_c, q_c = [], [], []
            for h in range(GLA_HEADS):
                ks = slice(h * GLA_DK, (h + 1) * GLA_DK)
                vs = slice(h * GLA_DV, (h + 1) * GLA_DV)
                qh = q_s[rows, ks].astype(BF16)
                vb = v_ref[bi, rows, vs].astype(BF16)
                att = lax.dot_general(qh, k_s[rows, ks].astype(BF16), nt_dims, preferred_element_type=F32)
                att = jnp.where(causal, att, 0.0).astype(BF16)
                o_c.append(_dot(att, vb))
                kv_c.append(lax.dot_general(vb, k_end[rows, ks].astype(BF16), tn_dims,
                                            preferred_element_type=F32))
                q_c.append(qh)
            o_intra.append(o_c)
            q_heads.append(q_c)
            kv.append(jnp.concatenate(kv_c, axis=1))

        st = st_ref[bi]
        for c in range(n_chunks):
            rows = slice(c * CHUNK, (c + 1) * CHUNK)
            for h in range(GLA_HEADS):
                ks = slice(h * GLA_DK, (h + 1) * GLA_DK)
                vs = slice(h * GLA_DV, (h + 1) * GLA_DV)
                o = o_intra[c][h] + lax.dot_general(q_heads[c][h], st[:, ks].astype(BF16), nt_dims,
                                                    preferred_element_type=F32)
                o_ref[bi, rows, vs] = _rms(o, gn) * jax.nn.silu(g_ref[bi, rows, vs])
            st = decay[c * CHUNK:c * CHUNK + 1, :] * st + kv[c]
        st_ref[bi] = st


def _gla(q, k, v, g, la, gn, bsz, seq):
    nt = seq // TS_GLA
    blk = lambda w: pl.BlockSpec((bsz, TS_GLA, w), lambda i: (0, i, 0))
    r3 = lambda a: a.reshape(bsz, seq, a.shape[-1])
    out = pl.pallas_call(
        _gla_kernel,
        grid=(nt,),
        in_specs=[blk(GLA_KEY), blk(GLA_KEY), blk(GLA_VAL), blk(GLA_VAL), blk(GLA_KEY),
                  pl.BlockSpec((1, GLA_DV), lambda i: (0, 0))],
        out_specs=blk(GLA_VAL),
        out_shape=jax.ShapeDtypeStruct((bsz, seq, GLA_VAL), F32),
        scratch_shapes=[pltpu.VMEM((bsz, GLA_DV, GLA_KEY), F32)],
        compiler_params=pltpu.CompilerParams(dimension_semantics=("arbitrary",),
                                             vmem_limit_bytes=VMEM_LIMIT),
        name="gla",
    )(r3(q), r3(k), r3(v), r3(g), r3(la), gn.reshape(1, GLA_DV).astype(F32))
    return out.reshape(bsz * seq, GLA_VAL)


def _outproj_kernel(yrg_ref, ygla_ref, x_ref, wo1_ref, wo2_ref, gn_ref, wr_ref, br_ref,
                    h_o, hn_o, rt_o):
    tm = x_ref.shape[0]
    h = x_ref[...] + _dot(yrg_ref[...].astype(BF16), wo1_ref[...]) \
        + _dot(ygla_ref[...].astype(BF16), wo2_ref[...])
    h_o[...] = h
    hn = _rms(h, gn_ref[...])
    hn_hi = hn.astype(BF16)
    hn_o[...] = hn_hi
    hn_lo = (hn - hn_hi.astype(F32)).astype(BF16)
    hi_parts = _dot(hn_hi, wr_ref[...])
    logits = hi_parts[:, :LANES] + hi_parts[:, LANES:] + _dot(hn_lo, wr_ref[:, :LANES]) + br_ref[...]

    lane = lax.broadcasted_iota(jnp.int32, (tm, LANES), 1).astype(F32)
    neg = -jnp.inf
    glog = jnp.where(lane < N_GROUPS, logits, neg)
    gmax = jnp.max(glog, axis=-1, keepdims=True)
    gidx = jnp.min(jnp.where(glog == gmax, lane, float(LANES)), axis=-1, keepdims=True)
    g_w = 1.0 / jnp.sum(jnp.exp(glog - gmax), axis=-1, keepdims=True)
    lo = N_GROUPS + gidx * EXPERTS_PER_GROUP
    in_group = jnp.logical_and(lane >= lo, lane < lo + EXPERTS_PER_GROUP)
    le = jnp.where(in_group, logits, neg)
    m1 = jnp.max(le, axis=-1, keepdims=True)
    i1 = jnp.min(jnp.where(le == m1, lane, float(LANES)), axis=-1, keepdims=True)
    le2 = jnp.where(lane == i1, neg, le)
    m2 = jnp.max(le2, axis=-1, keepdims=True)
    i2 = jnp.min(jnp.where(le2 == m2, lane, float(LANES)), axis=-1, keepdims=True)
    t2 = jnp.exp(m2 - m1)
    w1 = g_w / (1.0 + t2)
    w2 = g_w * t2 / (1.0 + t2)
    info = jnp.where(lane == 0.0, i1 - N_GROUPS,
                     jnp.where(lane == 1.0, i2 - N_GROUPS,
                               jnp.where(lane == 2.0, w1, jnp.where(lane == 3.0, w2, 0.0))))
    rt_o[...] = info.T[0:SUBLANES, :]


def _outproj(y_rg, y_gla, x2, w_out, norm_ffn, w_rg, b_rg, w_re, b_re):
    t = x2.shape[0]
    wo = w_out.astype(BF16)
    wr = jnp.pad(jnp.concatenate([w_rg, w_re], axis=1).astype(F32),
                 ((0, 0), (0, LANES - N_GROUPS - N_EXPERTS)))
    wr_hi = wr.astype(BF16)
    wr = jnp.concatenate([wr_hi, (wr - wr_hi.astype(F32)).astype(BF16)], axis=1)
    br = jnp.pad(jnp.concatenate([b_rg, b_re]).astype(F32), (0, LANES - N_GROUPS - N_EXPERTS))
    row = lambda i: (i, 0)
    fixed = lambda i: (0, 0)
    return pl.pallas_call(
        _outproj_kernel,
        grid=(t // TM_PROJ,),
        in_specs=[pl.BlockSpec((TM_PROJ, RG_WIDTH), row), pl.BlockSpec((TM_PROJ, GLA_VAL), row),
                  pl.BlockSpec((TM_PROJ, D_MODEL), row),
                  pl.BlockSpec((RG_WIDTH, D_MODEL), fixed), pl.BlockSpec((GLA_VAL, D_MODEL), fixed),
                  pl.BlockSpec((1, D_MODEL), fixed), pl.BlockSpec((D_MODEL, 2 * LANES), fixed),
                  pl.BlockSpec((1, LANES), fixed)],
        out_specs=[pl.BlockSpec((TM_PROJ, D_MODEL), row), pl.BlockSpec((TM_PROJ, D_MODEL), row),
                   pl.BlockSpec((SUBLANES, TM_PROJ), lambda i: (0, i))],
        out_shape=[jax.ShapeDtypeStruct((t, D_MODEL), F32), jax.ShapeDtypeStruct((t, D_MODEL), BF16),
                   jax.ShapeDtypeStruct((SUBLANES, t), F32)],
        compiler_params=pltpu.CompilerParams(dimension_semantics=("arbitrary",),
                                             vmem_limit_bytes=VMEM_LIMIT),
        name="outproj",
    )(y_rg, y_gla, x2, wo[:RG_WIDTH], wo[RG_WIDTH:], norm_ffn.reshape(1, D_MODEL).astype(F32),
      wr, br.reshape(1, LANES))


def _plan_kernel(rt_ref, lpos_o, meta_o, runs_o, tri_ref, *, n_tok, nb_max):
    n_tiles = n_tok // TT
    esub = lax.broadcasted_iota(jnp.int32, (N_EXPERTS, TT), 0).astype(F32)

    def onehots(i):
        off = pl.multiple_of(i * TT, TT)
        e1 = rt_ref[0:1, pl.ds(off, TT)]
        e2 = rt_ref[1:2, pl.ds(off, TT)]
        m1 = jnp.where(esub == e1, 1.0, 0.0)
        m2 = jnp.where(esub == e2, 1.0, 0.0)
        return off, m1, m2

    def count_body(i, cnt):
        _, m1, m2 = onehots(i)
        return cnt + jnp.sum(m1 + m2, axis=1, keepdims=True)

    counts = lax.fori_loop(0, n_tiles, count_body, jnp.zeros((N_EXPERTS, 1), F32))
    nblk = jnp.floor((counts + (BM - 1)) * (1.0 / BM))
    ei = lax.broadcasted_iota(jnp.int32, (N_EXPERTS, N_EXPERTS), 0)
    ej = lax.broadcasted_iota(jnp.int32, (N_EXPERTS, N_EXPERTS), 1)
    nblk_row = jnp.sum(jnp.where(ei == ej, nblk, 0.0), axis=0, keepdims=True)
    bstart = jnp.sum(jnp.where(ej < ei, nblk_row, 0.0), axis=1, keepdims=True)
    bend = bstart + nblk
    n_used = jnp.sum(nblk, axis=0, keepdims=True)

    meta_w = meta_o.shape[1]
    blane = lax.broadcasted_iota(jnp.int32, (N_EXPERTS, meta_w), 1).astype(F32)
    owner = jnp.sum(jnp.where(bend <= blane, 1.0, 0.0), axis=0, keepdims=True)
    owner = jnp.minimum(owner, N_EXPERTS - 1.0)
    lane1 = lax.broadcasted_iota(jnp.int32, (1, meta_w), 1)
    meta_o[0:1, :] = jnp.where(lane1 == nb_max, n_used, owner).astype(jnp.int32)
    nonempty = jnp.where(nblk > 0.0, 1.0, 0.0)
    meta_o[1:2, :] = jnp.sum(jnp.where(bend <= blane, nonempty, 0.0), axis=0,
                             keepdims=True).astype(jnp.int32)
    owned = jnp.logical_and(bstart <= blane, blane < bend)
    next_start = jnp.sum(jnp.where(owned, bend, 0.0), axis=0, keepdims=True)
    next_owner = jnp.sum(jnp.where(bend <= next_start, 1.0, 0.0), axis=0, keepdims=True)
    meta_o[2:3, :] = jnp.where(next_start < n_used, next_owner, -1.0).astype(jnp.int32)
    valid_end = jnp.sum(jnp.where(owned, bstart * float(BM) + counts, 0.0), axis=0, keepdims=True)
    meta_o[3:4, :] = jnp.clip(valid_end - lane1.astype(F32) * float(BM), 0.0, float(BM)).astype(jnp.int32)

    ti = lax.broadcasted_iota(jnp.int32, (TT, TT), 0)
    tj = lax.broadcasted_iota(jnp.int32, (TT, TT), 1)
    tri_ref[...] = jnp.where(ti <= tj, 1.0, 0.0).astype(BF16)
    tlane = lax.broadcasted_iota(jnp.int32, (N_EXPERTS, LANES), 1)

    def tile_body(i, carry):
        first_slot, t_cnt, t_slot, t_rank = carry
        off, m1, m2 = onehots(i)
        m = m1 + m2
        incl = _dot(m.astype(BF16), tri_ref[...])
        cnt = incl[:, TT - 1:TT]
        cnt_row = jnp.sum(jnp.where(ei == ej, cnt, 0.0), axis=0, keepdims=True)
        first_rank = jnp.sum(jnp.where(ej < ei, cnt_row, 0.0), axis=1, keepdims=True)
        rank = first_rank + incl - m
        lpos_o[0:1, pl.ds(off, TT)] = jnp.sum(m1 * rank, axis=0, keepdims=True).astype(jnp.int32)
        lpos_o[1:2, pl.ds(off, TT)] = jnp.sum(m2 * rank, axis=0, keepdims=True).astype(jnp.int32)
        here = tlane == i
        return (first_slot + cnt, jnp.where(here, cnt, t_cnt), jnp.where(here, first_slot, t_slot),
                jnp.where(here, first_rank, t_rank))

    zeros = jnp.zeros((N_EXPERTS, LANES), F32)
    _, t_cnt, t_slot, t_rank = lax.fori_loop(0, n_tiles, tile_body,
                                             (bstart * float(BM), zeros, zeros, zeros))
    runs_o[0] = t_cnt.astype(jnp.int32)
    runs_o[1] = t_slot.astype(jnp.int32)
    runs_o[2] = t_rank.astype(jnp.int32)
    pad = jnp.where(tlane == 0, bstart * float(BM) + counts,
                    jnp.where(tlane == 1, nblk * float(BM) - counts, 0.0))
    runs_o[3] = pad.astype(jnp.int32)


def _plan(rt, n_tok, nb_max):
    assert n_tok // TT <= LANES, "run tables hold one token tile per lane"
    meta_w = ((nb_max + 1 + LANES - 1) // LANES) * LANES
    return pl.pallas_call(
        functools.partial(_plan_kernel, n_tok=n_tok, nb_max=nb_max),
        out_shape=[jax.ShapeDtypeStruct((2, n_tok), jnp.int32),
                   jax.ShapeDtypeStruct((4, meta_w), jnp.int32),
                   jax.ShapeDtypeStruct((RUN_FIELDS, N_EXPERTS, LANES), jnp.int32)],
        scratch_shapes=[pltpu.VMEM((TT, TT), BF16)],
        compiler_params=pltpu.CompilerParams(vmem_limit_bytes=VMEM_LIMIT),
        name="plan",
    )(rt)


def _run_entry(runs_ref, field, expert, tile):
    return runs_ref[(field * N_EXPERTS + expert) * LANES + tile]


def _dispatch_kernel(runs_ref, meta_ref, hn_ref, lpos_ref, xs_hbm, xsbuf, zbuf, sem, zsem,
                     *, n_tiles, nb_max, n_pad):
    i = pl.program_id(0)
    slot = lax.rem(i, 2)

    @pl.when(i == 0)
    def _():
        zbuf[...] = jnp.zeros_like(zbuf)
        for e in range(N_EXPERTS):
            _copy_run(zbuf, 0, xs_hbm, _run_entry(runs_ref, 3, e, 0), _run_entry(runs_ref, 3, e, 1),
                      zsem, BM - 1)

        def unused_block(blk, carry):
            dst = xs_hbm.at[pl.ds(pl.multiple_of(blk * (BM * TILE_ROWS), BM * TILE_ROWS),
                                  BM * TILE_ROWS), :]
            pltpu.make_async_copy(zbuf, dst, zsem).start()
            return carry

        lax.fori_loop(meta_ref[nb_max], nb_max, unused_block, 0)

    @pl.when(i >= 2)
    def _():
        _wait_rows(xsbuf.at[slot], TILE_SLOTS, sem.at[slot])

    rank = lax.broadcasted_iota(jnp.int32, (TILE_SLOTS, TT), 0)
    onehot = jnp.where(rank == lpos_ref[0:1, :], 1.0, jnp.where(rank == lpos_ref[1:2, :], 1.0, 0.0))
    _store_token_tiles(xsbuf.at[slot], _dot(onehot.astype(BF16), hn_ref[...]), already_bf16=True)
    for e in range(N_EXPERTS):
        _copy_run(xsbuf.at[slot], _run_entry(runs_ref, 2, e, i), xs_hbm, _run_entry(runs_ref, 1, e, i),
                  _run_entry(runs_ref, 0, e, i), sem.at[slot], TT)

    @pl.when(i == n_tiles - 1)
    def _():
        _wait_rows(xsbuf.at[slot], TILE_SLOTS, sem.at[slot])
        if n_tiles > 1:
            _wait_rows(xsbuf.at[1 - slot], TILE_SLOTS, sem.at[1 - slot])
        _wait_rows(xs_hbm, n_pad, zsem)


def _dispatch(runs, meta, hn, lpos, n_tok, nb_max):
    n_tiles = n_tok // TT
    cap = nb_max * BM
    grid_spec = pltpu.PrefetchScalarGridSpec(
        num_scalar_prefetch=2,
        grid=(n_tiles,),
        in_specs=[pl.BlockSpec((TT, D_MODEL), lambda i, runs_ref, meta_ref: (i, 0)),
                  pl.BlockSpec((2, TT), lambda i, runs_ref, meta_ref: (0, i))],
        out_specs=pl.BlockSpec(memory_space=pl.ANY),
        scratch_shapes=[pltpu.VMEM((2, TILE_SLOTS * TILE_ROWS, LANES), U32),
                        pltpu.VMEM((BM * TILE_ROWS, LANES), U32),
                        pltpu.SemaphoreType.DMA((2,)), pltpu.SemaphoreType.DMA],
    )
    return pl.pallas_call(
        functools.partial(_dispatch_kernel, n_tiles=n_tiles, nb_max=nb_max, n_pad=cap - 2 * n_tok),
        grid_spec=grid_spec,
        out_shape=jax.ShapeDtypeStruct((cap * TILE_ROWS, LANES), U32),
        compiler_params=pltpu.CompilerParams(dimension_semantics=("arbitrary",),
                                             vmem_limit_bytes=VMEM_LIMIT),
        name="dispatch",
    )(runs.reshape(-1), meta.reshape(-1), hn, lpos)


def _experts_kernel(meta_ref, xs_ref, wg_hbm, wu_hbm, wd_hbm, y_ref, wf32, wbf, wsem, *, nb_max, meta_w):
    b = pl.program_id(0)
    n_used = meta_ref[nb_max]

    def start_weights(expert, buf):
        for j, w_hbm in enumerate((wg_hbm, wu_hbm, wd_hbm)):
            pltpu.make_async_copy(w_hbm.at[expert], wf32.at[buf, j], wsem.at[buf]).start(priority=1)

    @pl.when(b < n_used)
    def _():
        owner = meta_ref[b]

        @pl.when(b == 0)
        def _():
            start_weights(owner, 0)

        @pl.when(jnp.logical_or(b == 0, owner != meta_ref[jnp.maximum(b - 1, 0)]))
        def _():
            buf = jnp.bitwise_and(meta_ref[meta_w + b], 1)
            pltpu.make_async_copy(wf32.at[buf], wf32.at[buf], wsem.at[buf]).wait()
            next_owner = meta_ref[2 * meta_w + b]

            @pl.when(next_owner >= 0)
            def _():
                start_weights(next_owner, 1 - buf)

            for j in range(3):
                wbf[j] = wf32[buf, j].astype(BF16)

        def geglu(n_rows):
            rows = pl.ds(0, n_rows * TILE_ROWS)
            x = _load_token_tiles(xs_ref.at[rows, :], n_rows).astype(BF16)
            mid = (jax.nn.gelu(_dot(x, wbf[0])) * _dot(x, wbf[1])).astype(BF16)
            _store_token_tiles(y_ref.at[rows, :], _dot(mid, wbf[2]))

        n_groups = lax.shift_right_logical(meta_ref[3 * meta_w + b] + (SKIP_ROWS - 1),
                                           SKIP_ROWS.bit_length() - 1)
        for g in range(1, BM // SKIP_ROWS + 1):
            @pl.when(n_groups == g)
            def _(g=g):
                geglu(g * SKIP_ROWS)
                if g * SKIP_ROWS < BM:
                    rest = (BM - g * SKIP_ROWS) * TILE_ROWS
                    y_ref[pl.ds(g * SKIP_ROWS * TILE_ROWS, rest), :] = jnp.zeros((rest, LANES), U32)

    @pl.when(b >= n_used)
    def _():
        y_ref[...] = jnp.zeros_like(y_ref)


def _experts(meta, xs, w_gate, w_up, w_down, nb_max):
    whole = pl.BlockSpec(memory_space=pl.ANY)
    rows = pl.BlockSpec((BM * TILE_ROWS, LANES), lambda b, meta_ref: (b, 0))
    grid_spec = pltpu.PrefetchScalarGridSpec(
        num_scalar_prefetch=1,
        grid=(nb_max,),
        in_specs=[rows, whole, whole, whole],
        out_specs=rows,
        scratch_shapes=[pltpu.VMEM((2, 3, D_MODEL, D_MODEL), F32),
                        pltpu.VMEM((3, D_MODEL, D_MODEL), BF16),
                        pltpu.SemaphoreType.DMA((2,))],
    )
    return pl.pallas_call(
        functools.partial(_experts_kernel, nb_max=nb_max, meta_w=meta.shape[1]),
        grid_spec=grid_spec,
        out_shape=jax.ShapeDtypeStruct(xs.shape, U32),
        compiler_params=pltpu.CompilerParams(dimension_semantics=("arbitrary",),
                                             vmem_limit_bytes=VMEM_LIMIT),
        name="experts",
    )(meta.reshape(-1), xs, w_gate, w_up, w_down)


def _combine_kernel(runs_ref, h_ref, lpos_ref, rt_ref, gn_ref, y_hbm, o_ref, ysbuf, sem, *, n_tiles):
    i = pl.program_id(0)
    slot = lax.rem(i, 2)

    def fetch(tile, buf):
        for e in range(N_EXPERTS):
            _copy_run(y_hbm, _run_entry(runs_ref, 1, e, tile), ysbuf.at[buf],
                      _run_entry(runs_ref, 2, e, tile), _run_entry(runs_ref, 0, e, tile), sem.at[buf], TT)

    @pl.when(i == 0)
    def _():
        fetch(0, 0)

    @pl.when(i + 1 < n_tiles)
    def _():
        fetch(i + 1, 1 - slot)

    _wait_rows(ysbuf.at[slot], TILE_SLOTS, sem.at[slot])

    rank = lax.broadcasted_iota(jnp.int32, (TILE_SLOTS, TT), 0)
    first = rank == lpos_ref[0:1, :]
    second = rank == lpos_ref[1:2, :]
    slot_w = jnp.sum(jnp.where(first, rt_ref[2:3, :], 0.0) + jnp.where(second, rt_ref[3:4, :], 0.0),
                     axis=1, keepdims=True)
    onehot = jnp.where(first, 1.0, jnp.where(second, 1.0, 0.0)).astype(BF16)
    ys = (_load_token_tiles(ysbuf.at[slot], TILE_SLOTS) * slot_w).astype(BF16)
    moe = lax.dot_general(onehot, ys, (((0,), (0,)), ((), ())), preferred_element_type=F32)
    o_ref[...] = _rms(h_ref[...] + moe, gn_ref[...])


def _combine(runs, h, lpos, rt, norm_final, y, n_tok):
    n_tiles = n_tok // TT
    grid_spec = pltpu.PrefetchScalarGridSpec(
        num_scalar_prefetch=1,
        grid=(n_tiles,),
        in_specs=[pl.BlockSpec((TT, D_MODEL), lambda i, runs_ref: (i, 0)),
                  pl.BlockSpec((2, TT), lambda i, runs_ref: (0, i)),
                  pl.BlockSpec((SUBLANES, TT), lambda i, runs_ref: (0, i)),
                  pl.BlockSpec((1, D_MODEL), lambda i, runs_ref: (0, 0)),
                  pl.BlockSpec(memory_space=pl.ANY)],
        out_specs=pl.BlockSpec((TT, D_MODEL), lambda i, runs_ref: (i, 0)),
        scratch_shapes=[pltpu.VMEM((2, TILE_SLOTS * TILE_ROWS, LANES), U32),
                        pltpu.SemaphoreType.DMA((2,))],
    )
    return pl.pallas_call(
        functools.partial(_combine_kernel, n_tiles=n_tiles),
        grid_spec=grid_spec,
        out_shape=jax.ShapeDtypeStruct((n_tok, D_MODEL), F32),
        compiler_params=pltpu.CompilerParams(dimension_semantics=("arbitrary",),
                                             vmem_limit_bytes=VMEM_LIMIT),
        name="combine",
    )(runs.reshape(-1), h, lpos, rt, norm_final.reshape(1, D_MODEL).astype(F32), y)


def kernel(x, norm_mix, w_in, conv_w, conv_b, w_rg_a, b_rg_a, w_rg_x, b_rg_x, rg_lambda, rg_norm, w_alpha_up, b_alpha, gla_norm, w_out, norm_ffn, w_router_group, b_router_group, w_router_expert, b_router_expert, w_exp_gate, w_exp_up, w_exp_down, norm_final):
    bsz, seq, d = x.shape
    assert d == D_MODEL and norm_mix.shape[0] == 1, "single-layer model of width D_MODEL expected"
    n_tok = bsz * seq
    nb_max = (2 * n_tok + N_EXPERTS * (BM - 1)) // BM

    x2 = x.reshape(n_tok, d)
    xr, yr, q, k, v, g, la = _inproj(x2, norm_mix[0], w_in[0], w_alpha_up[0], b_alpha[0])
    y_rg = _rglru(xr, yr, conv_w[0], conv_b[0], w_rg_a[0], b_rg_a[0], w_rg_x[0], b_rg_x[0],
                  rg_lambda[0], rg_norm[0], bsz, seq)
    y_gla = _gla(q, k, v, g, la, gla_norm[0], bsz, seq)
    h, hn, rt = _outproj(y_rg, y_gla, x2, w_out[0], norm_ffn[0], w_router_group[0],
                         b_router_group[0], w_router_expert[0], b_router_expert[0])
    lpos, meta, runs = _plan(rt, n_tok, nb_max)
    xs = _dispatch(runs, meta, hn, lpos, n_tok, nb_max)
    y = _experts(meta, xs, w_exp_gate[0], w_exp_up[0], w_exp_down[0], nb_max)
    out = _combine(runs, h, lpos, rt, norm_final, y, n_tok)
    return out.reshape(bsz, seq, d)
```

```python
import functools

import jax
import jax.numpy as jnp
from jax import lax
from jax.experimental import pallas as pl
from jax.experimental.pallas import tpu as pltpu

F32 = jnp.float32
BF16 = jnp.bfloat16

D_MODEL = 1024
RG_WIDTH = 512
RG_BLOCKS = 8
RG_BLOCK = 64
CONV_WIDTH = 4
C_RG = 8.0
GLA_HEADS = 4
GLA_VAL = 512
GLA_KEY = 256
GLA_DK = 64
GLA_DV = 128
GATE_RANK = 16
GATE_NORM = 16.0
CHUNK = 64
N_GROUPS = 4
EXPERTS_PER_GROUP = 8
N_EXPERTS = 32
EPS = 1e-6

LANES = 128
SUBLANES = 8
VMEM_LIMIT = 56 * 1024 * 1024

TM_INPROJ = 1024
TS_MIX = 512
BM = 512
SKIP_ROWS = 128
TT = 512
TILE_SLOTS = 2 * TT
RUN_CHUNK_LOG2 = 6
RUN_FIELDS = 4


def _dot(a, b):
    return jnp.dot(a, b, preferred_element_type=F32)


def _softplus(z):
    return jnp.maximum(z, 0.0) + jnp.log1p(jnp.exp(-jnp.abs(z)))


def _rms(x, g):
    return x * lax.rsqrt(jnp.mean(x * x, axis=-1, keepdims=True) + EPS) * g


U32 = jnp.uint32
TILE_ROWS = D_MODEL // 2 // LANES


def _copy_run(src, src_row, dst, dst_row, n_rows, sem):
    def piece(off, rows):
        s = src.at[pl.ds(pl.multiple_of((src_row + off) * TILE_ROWS, TILE_ROWS), rows * TILE_ROWS), :]
        d = dst.at[pl.ds(pl.multiple_of((dst_row + off) * TILE_ROWS, TILE_ROWS), rows * TILE_ROWS), :]
        pltpu.make_async_copy(s, d, sem).start()

    chunk = 1 << RUN_CHUNK_LOG2
    n_chunks = lax.shift_right_logical(n_rows, RUN_CHUNK_LOG2)
    lax.fori_loop(0, n_chunks, lambda c, carry: (piece(c * chunk, chunk), carry)[1], 0)
    off = n_chunks * chunk
    for k in reversed(range(RUN_CHUNK_LOG2)):
        bit = jnp.bitwise_and(n_rows, 1 << k)

        @pl.when(bit != 0)
        def _():
            piece(off, 1 << k)

        off = off + bit


def _wait_rows(ref, n_rows, sem):
    view = ref.at[pl.ds(0, n_rows * TILE_ROWS), :]
    pltpu.make_async_copy(view, view, sem).wait()


def _store_token_tiles(ref, val, already_bf16=False):
    n = val.shape[0]
    if not already_bf16:
        val = val.astype(BF16).astype(F32)
    bits = lax.bitcast_convert_type(val, U32)
    packed = jnp.bitwise_or(bits[:, :D_MODEL // 2], jnp.right_shift(bits[:, D_MODEL // 2:], 16))
    for c in range(TILE_ROWS):
        ref[pl.ds(c, n, stride=TILE_ROWS), :] = packed[:, c * LANES:(c + 1) * LANES]


def _load_token_tiles(ref, n):
    words = [ref[pl.ds(c, n, stride=TILE_ROWS), :] for c in range(TILE_ROWS)]
    high = [lax.bitcast_convert_type(jnp.bitwise_and(w, jnp.uint32(0xFFFF0000)), F32) for w in words]
    low = [lax.bitcast_convert_type(jnp.left_shift(w, 16), F32) for w in words]
    return jnp.concatenate(high + low, axis=1)


def _inproj_kernel(x_ref, g_ref, wxr, wyr, wq, wk, wv, wg, wal, wup, bal,
                   xr_o, yr_o, q_o, k_o, v_o, g_o, la_o):
    hn = _rms(x_ref[...], g_ref[...]).astype(BF16)
    xr_o[...] = _dot(hn, wxr[...])
    yr_o[...] = _dot(hn, wyr[...])
    q_o[...] = _dot(hn, wq[...])
    k_o[...] = _dot(hn, wk[...])
    v_o[...] = _dot(hn, wv[...])
    g_o[...] = _dot(hn, wg[...])
    a3 = _dot(hn, wal[...])
    a3_hi = a3.astype(BF16).astype(F32)
    lane = lax.broadcasted_iota(jnp.int32, a3.shape, 1)
    use_low = jnp.logical_and(lane >= GATE_RANK, lane < 2 * GATE_RANK)
    z = _dot(jnp.where(use_low, a3 - a3_hi, a3_hi).astype(BF16), wup[...]) + bal[...]
    log_sig = jnp.minimum(z, 0.0) - jnp.log1p(jnp.exp(-jnp.abs(z)))
    la_o[...] = log_sig * (1.0 / GATE_NORM)


def _inproj(x2, norm_mix, w_in, w_alpha_up, b_alpha):
    t = x2.shape[0]
    c = [0, 512, 1024, 1280, 1536, 2048, 2560, 2576]
    wb = w_in.astype(BF16)
    segs = [wb[:, c[i]:c[i + 1]] for i in range(6)]
    w_low = wb[:, c[6]:c[7]]
    wal = jnp.pad(jnp.concatenate([w_low, w_low, w_low], axis=1), ((0, 0), (0, LANES - 3 * GATE_RANK)))
    up = w_alpha_up.astype(F32)
    up_hi = up.astype(BF16)
    up_lo = (up - up_hi.astype(F32)).astype(BF16)
    wup = jnp.pad(jnp.concatenate([up_hi, up_hi, up_lo], axis=0), ((0, LANES - 3 * GATE_RANK), (0, 0)))
    widths = [512, 512, 256, 256, 512, 512, 256]
    row = lambda i: (i, 0)
    fixed = lambda i: (0, 0)
    in_specs = [pl.BlockSpec((TM_INPROJ, D_MODEL), row), pl.BlockSpec((1, D_MODEL), fixed)]
    in_specs += [pl.BlockSpec((D_MODEL, w), fixed) for w in widths[:6]]
    in_specs += [pl.BlockSpec((D_MODEL, LANES), fixed), pl.BlockSpec((LANES, GLA_KEY), fixed),
                 pl.BlockSpec((1, GLA_KEY), fixed)]
    return pl.pallas_call(
        _inproj_kernel,
        grid=(t // TM_INPROJ,),
        in_specs=in_specs,
        out_specs=[pl.BlockSpec((TM_INPROJ, w), row) for w in widths],
        out_shape=[jax.ShapeDtypeStruct((t, w), F32) for w in widths],
        compiler_params=pltpu.CompilerParams(dimension_semantics=("arbitrary",),
                                             vmem_limit_bytes=VMEM_LIMIT),
        name="inproj",
    )(x2, norm_mix.reshape(1, D_MODEL), *segs, wal, wup, b_alpha.reshape(1, GLA_KEY))


def _rglru_part(bi, xr_ref, yr_ref, cw_ref, cb_ref, wa_ref, ba_ref, wx_ref, bx_ref, lam_ref, gn_ref,
                o_ref, tail_ref, hc_ref):
    ts = xr_ref.shape[1]
    n_groups = ts // SUBLANES
    grouped = (n_groups, SUBLANES, RG_WIDTH)
    sub = lax.broadcasted_iota(jnp.int32, grouped, 1)

    x = xr_ref[bi]
    x_grp = x.reshape(grouped)
    x_prev = jnp.concatenate([tail_ref[bi], x[:ts - SUBLANES, :]], axis=0).reshape(grouped)
    tail_ref[bi] = x[ts - SUBLANES:ts, :]
    cw = cw_ref[...]
    xc = cb_ref[...] + cw[CONV_WIDTH - 1:CONV_WIDTH, :] * x
    for s in range(1, CONV_WIDTH):
        mixed = jnp.where(sub >= SUBLANES - s, x_prev, x_grp)
        shifted = pltpu.roll(mixed, s, axis=1)
        xc = xc + cw[CONV_WIDTH - 1 - s:CONV_WIDTH - s, :] * shifted.reshape(ts, RG_WIDTH)

    xb = xc.astype(BF16)
    r = jax.nn.sigmoid(_dot(xb, wa_ref[...]) + ba_ref[...])
    gate_i = jax.nn.sigmoid(_dot(xb, wx_ref[...]) + bx_ref[...])
    log_a = (-C_RG) * r * _softplus(-lam_ref[...])
    a = jnp.exp(log_a)
    z = jnp.tanh(-log_a) * (1.0 + a * a)
    u = (z * lax.rsqrt(jnp.maximum(z, jnp.finfo(F32).tiny))) * (gate_i * xc)

    a = a.reshape(grouped)
    u = u.reshape(grouped)
    d = 1
    while d < SUBLANES:
        keep = sub >= d
        a_sh = jnp.where(keep, pltpu.roll(a, d, axis=1), 1.0)
        u_sh = jnp.where(keep, pltpu.roll(u, d, axis=1), 0.0)
        u = a * u_sh + u
        a = a * a_sh
        d *= 2
    carry = hc_ref[bi, 0:1, :]
    groups = []
    for j in range(n_groups):
        h_j = u[j] + a[j] * carry
        groups.append(h_j)
        carry = h_j[SUBLANES - 1:SUBLANES, :]
    h = jnp.concatenate(groups, axis=0)
    hc_ref[bi] = jnp.broadcast_to(carry, hc_ref.shape[1:])

    y = h * jax.nn.gelu(yr_ref[bi])
    o_ref[bi, :, 0:RG_WIDTH] = _rms(y, gn_ref[...]).astype(o_ref.dtype)


def _block_diag(w):
    eye = jnp.eye(RG_BLOCKS, dtype=w.dtype)
    return jnp.einsum('hij,hg->higj', w, eye).reshape(RG_WIDTH, RG_WIDTH)


def _mixer_kernel(xr_ref, yr_ref, q_ref, k_ref, v_ref, g_ref, la_ref, x_ref,
                  cw_ref, cb_ref, wa_ref, ba_ref, wx_ref, bx_ref, lam_ref, rgn_ref, gn_ref,
                  wo_ref, fgn_ref, wr_ref, br_ref,
                  h_o, hn_o, rt_o, o_ref, tail_ref, hc_ref, st_ref):
    bsz, ts = q_ref.shape[0], q_ref.shape[1]
    n_chunks = ts // CHUNK

    @pl.when(pl.program_id(0) == 0)
    def _():
        tail_ref[...] = jnp.zeros_like(tail_ref)
        hc_ref[...] = jnp.zeros_like(hc_ref)
        st_ref[...] = jnp.zeros_like(st_ref)

    ri = lax.broadcasted_iota(jnp.int32, (ts, ts), 0)
    ci = lax.broadcasted_iota(jnp.int32, (ts, ts), 1)
    chunk_bits = CHUNK.bit_length() - 1
    same_chunk = lax.shift_right_logical(ri, chunk_bits) == lax.shift_right_logical(ci, chunk_bits)
    prefix = jnp.where(jnp.logical_and(same_chunk, ri >= ci), 1.0, 0.0).astype(BF16)
    total = jnp.where(same_chunk, 1.0, 0.0).astype(BF16)
    causal = (ri >= ci)[:CHUNK, :CHUNK]
    scale = GLA_DK ** -0.5
    gn = gn_ref[...]
    nt_dims = (((1,), (1,)), ((), ()))
    tn_dims = (((0,), (0,)), ((), ()))

    for bi in range(bsz):
        _rglru_part(bi, xr_ref, yr_ref, cw_ref, cb_ref, wa_ref, ba_ref, wx_ref, bx_ref, lam_ref, rgn_ref,
                    o_ref, tail_ref, hc_ref)

        la = la_ref[bi]
        la_hi = la.astype(BF16)
        la_lo = (la - la_hi.astype(F32)).astype(BF16)
        b = _dot(prefix, la_hi) + _dot(prefix, la_lo)
        b_tot = _dot(total, la_hi) + _dot(total, la_lo)
        kk = k_ref[bi]
        q_s = (q_ref[bi] * scale) * jnp.exp(b)
        k_s = kk * jnp.exp(-b)
        k_end = kk * jnp.exp(b_tot - b)
        decay = jnp.exp(b_tot)

        o_intra, kv, q_heads = [], [], []
        for c in range(n_chunks):
            rows = slice(c * CHUNK, (c + 1) * CHUNK)
            o_c, kv_c, q_c = [], [], []
            for h in range(GLA_HEADS):
                ks = slice(h * GLA_DK, (h + 1) * GLA_DK)
                vs = slice(h * GLA_DV, (h + 1) * GLA_DV)
                qh = q_s[rows, ks].astype(BF16)
                vb = v_ref[bi, rows, vs].astype(BF16)
                att = lax.dot_general(qh, k_s[rows, ks].astype(BF16), nt_dims, preferred_element_type=F32)
                att = jnp.where(causal, att, 0.0).astype(BF16)
                o_c.append(_dot(att, vb))
                kv_c.append(lax.dot_general(vb, k_end[rows, ks].astype(BF16), tn_dims,
                                            preferred_element_type=F32))
                q_c.append(qh)
            o_intra.append(o_c)
            q_heads.append(q_c)
            kv.append(jnp.concatenate(kv_c, axis=1))

        st = st_ref[bi]
        for c in range(n_chunks):
            rows = slice(c * CHUNK, (c + 1) * CHUNK)
            for h in range(GLA_HEADS):
                ks = slice(h * GLA_DK, (h + 1) * GLA_DK)
                vs = slice(h * GLA_DV, (h + 1) * GLA_DV)
                o = o_intra[c][h] + lax.dot_general(q_heads[c][h], st[:, ks].astype(BF16), nt_dims,
                                                    preferred_element_type=F32)
                gated = _rms(o, gn) * jax.nn.silu(g_ref[bi, rows, vs])
                o_ref[bi, rows, RG_WIDTH + h * GLA_DV:RG_WIDTH + (h + 1) * GLA_DV] = gated.astype(o_ref.dtype)
            st = decay[c * CHUNK:c * CHUNK + 1, :] * st + kv[c]
        st_ref[bi] = st

        _outproj_part(bi, o_ref, x_ref, wo_ref, fgn_ref, wr_ref, br_ref, h_o, hn_o, rt_o)


def _mixer(xr, yr, q, k, v, g, la, x, conv_w, conv_b, w_a, b_a, w_x, b_x, lam, rg_norm, gla_norm,
           w_out, norm_ffn, w_rg, b_rg, w_re, b_re):
    bsz, seq, _ = x.shape
    nt = seq // TS_MIX
    blk = lambda w: pl.BlockSpec((bsz, TS_MIX, w), lambda i: (0, i, 0))
    fixed = lambda r, c: pl.BlockSpec((r, c), lambda i: (0, 0))
    r3 = lambda a: a.reshape(bsz, seq, a.shape[-1])
    vec = lambda a: a.reshape(1, -1).astype(F32)
    wr = jnp.pad(jnp.concatenate([w_rg, w_re], axis=1).astype(F32),
                 ((0, 0), (0, LANES - N_GROUPS - N_EXPERTS)))
    wr_hi = wr.astype(BF16)
    wr = jnp.concatenate([wr_hi, (wr - wr_hi.astype(F32)).astype(BF16)], axis=1)
    br = jnp.pad(jnp.concatenate([b_rg, b_re]).astype(F32), (0, LANES - N_GROUPS - N_EXPERTS))
    h, hn, rt = pl.pallas_call(
        _mixer_kernel,
        grid=(nt,),
        in_specs=[blk(RG_WIDTH), blk(RG_WIDTH), blk(GLA_KEY), blk(GLA_KEY), blk(GLA_VAL), blk(GLA_VAL),
                  blk(GLA_KEY), blk(D_MODEL),
                  fixed(CONV_WIDTH, RG_WIDTH), fixed(1, RG_WIDTH), fixed(RG_WIDTH, RG_WIDTH),
                  fixed(1, RG_WIDTH), fixed(RG_WIDTH, RG_WIDTH), fixed(1, RG_WIDTH), fixed(1, RG_WIDTH),
                  fixed(1, RG_WIDTH), fixed(1, GLA_DV),
                  fixed(D_MODEL, D_MODEL), fixed(1, D_MODEL), fixed(D_MODEL, 2 * LANES), fixed(1, LANES)],
        out_specs=[blk(D_MODEL), blk(D_MODEL),
                   pl.BlockSpec((bsz, SUBLANES, TS_MIX), lambda i: (0, 0, i))],
        out_shape=[jax.ShapeDtypeStruct((bsz, seq, D_MODEL), F32),
                   jax.ShapeDtypeStruct((bsz, seq, D_MODEL), BF16),
                   jax.ShapeDtypeStruct((bsz, SUBLANES, seq), F32)],
        scratch_shapes=[pltpu.VMEM((bsz, TS_MIX, D_MODEL), BF16),
                        pltpu.VMEM((bsz, SUBLANES, RG_WIDTH), F32),
                        pltpu.VMEM((bsz, SUBLANES, RG_WIDTH), F32),
                        pltpu.VMEM((bsz, GLA_DV, GLA_KEY), F32)],
        compiler_params=pltpu.CompilerParams(dimension_semantics=("arbitrary",),
                                             vmem_limit_bytes=VMEM_LIMIT),
        name="mixer",
    )(r3(xr), r3(yr), r3(q), r3(k), r3(v), r3(g), r3(la), x,
      conv_w.astype(F32), vec(conv_b), _block_diag(w_a).astype(BF16), vec(b_a),
      _block_diag(w_x).astype(BF16), vec(b_x), vec(lam), vec(rg_norm), vec(gla_norm),
      w_out.astype(BF16), vec(norm_ffn), wr, br.reshape(1, LANES))
    n_tok = bsz * seq
    rt = rt.transpose(1, 0, 2).reshape(SUBLANES, n_tok)
    return h.reshape(n_tok, D_MODEL), hn.reshape(n_tok, D_MODEL), rt


def _outproj_part(bi, y_ref, x_ref, wo_ref, gn_ref, wr_ref, br_ref, h_o, hn_o, rt_o):
    tm = x_ref.shape[1]
    h = x_ref[bi] + _dot(y_ref[bi], wo_ref[...])
    h_o[bi] = h
    hn = _rms(h, gn_ref[...])
    hn_hi = hn.astype(BF16)
    hn_o[bi] = hn_hi
    hn_lo = (hn - hn_hi.astype(F32)).astype(BF16)
    hi_parts = _dot(hn_hi, wr_ref[...])
    logits = hi_parts[:, :LANES] + hi_parts[:, LANES:] + _dot(hn_lo, wr_ref[:, :LANES]) + br_ref[...]

    lane = lax.broadcasted_iota(jnp.int32, (tm, LANES), 1).astype(F32)
    neg = -jnp.inf
    glog = jnp.where(lane < N_GROUPS, logits, neg)
    gmax = jnp.max(glog, axis=-1, keepdims=True)
    gidx = jnp.min(jnp.where(glog == gmax, lane, float(LANES)), axis=-1, keepdims=True)
    g_w = 1.0 / jnp.sum(jnp.exp(glog - gmax), axis=-1, keepdims=True)
    lo = N_GROUPS + gidx * EXPERTS_PER_GROUP
    in_group = jnp.logical_and(lane >= lo, lane < lo + EXPERTS_PER_GROUP)
    le = jnp.where(in_group, logits, neg)
    m1 = jnp.max(le, axis=-1, keepdims=True)
    i1 = jnp.min(jnp.where(le == m1, lane, float(LANES)), axis=-1, keepdims=True)
    le2 = jnp.where(lane == i1, neg, le)
    m2 = jnp.max(le2, axis=-1, keepdims=True)
    i2 = jnp.min(jnp.where(le2 == m2, lane, float(LANES)), axis=-1, keepdims=True)
    t2 = jnp.exp(m2 - m1)
    w1 = g_w / (1.0 + t2)
    w2 = g_w * t2 / (1.0 + t2)
    info = jnp.where(lane == 0.0, i1 - N_GROUPS,
                     jnp.where(lane == 1.0, i2 - N_GROUPS,
                               jnp.where(lane == 2.0, w1, jnp.where(lane == 3.0, w2, 0.0))))
    rt_o[bi] = info.T[0:SUBLANES, :]


def _plan_kernel(rt_ref, lpos_o, meta_o, runs_o, tri_ref, *, n_tok, nb_max):
    n_tiles = n_tok // TT
    esub = lax.broadcasted_iota(jnp.int32, (N_EXPERTS, TT), 0).astype(F32)

    def onehots(i):
        off = pl.multiple_of(i * TT, TT)
        e1 = rt_ref[0:1, pl.ds(off, TT)]
        e2 = rt_ref[1:2, pl.ds(off, TT)]
        m1 = jnp.where(esub == e1, 1.0, 0.0)
        m2 = jnp.where(esub == e2, 1.0, 0.0)
        return off, m1, m2

    def count_body(i, cnt):
        _, m1, m2 = onehots(i)
        return cnt + jnp.sum(m1 + m2, axis=1, keepdims=True)

    counts = lax.fori_loop(0, n_tiles, count_body, jnp.zeros((N_EXPERTS, 1), F32))
    nblk = jnp.floor((counts + (BM - 1)) * (1.0 / BM))
    ei = lax.broadcasted_iota(jnp.int32, (N_EXPERTS, N_EXPERTS), 0)
    ej = lax.broadcasted_iota(jnp.int32, (N_EXPERTS, N_EXPERTS), 1)
    nblk_row = jnp.sum(jnp.where(ei == ej, nblk, 0.0), axis=0, keepdims=True)
    bstart = jnp.sum(jnp.where(ej < ei, nblk_row, 0.0), axis=1, keepdims=True)
    bend = bstart + nblk
    n_used = jnp.sum(nblk, axis=0, keepdims=True)

    meta_w = meta_o.shape[1]
    blane = lax.broadcasted_iota(jnp.int32, (N_EXPERTS, meta_w), 1).astype(F32)
    owner = jnp.sum(jnp.where(bend <= blane, 1.0, 0.0), axis=0, keepdims=True)
    owner = jnp.minimum(owner, N_EXPERTS - 1.0)
    lane1 = lax.broadcasted_iota(jnp.int32, (1, meta_w), 1)
    meta_o[0:1, :] = jnp.where(lane1 == nb_max, n_used, owner).astype(jnp.int32)
    nonempty = jnp.where(nblk > 0.0, 1.0, 0.0)
    meta_o[1:2, :] = jnp.sum(jnp.where(bend <= blane, nonempty, 0.0), axis=0,
                             keepdims=True).astype(jnp.int32)
    owned = jnp.logical_and(bstart <= blane, blane < bend)
    next_start = jnp.sum(jnp.where(owned, bend, 0.0), axis=0, keepdims=True)
    next_owner = jnp.sum(jnp.where(bend <= next_start, 1.0, 0.0), axis=0, keepdims=True)
    meta_o[2:3, :] = jnp.where(next_start < n_used, next_owner, -1.0).astype(jnp.int32)
    valid_end = jnp.sum(jnp.where(owned, bstart * float(BM) + counts, 0.0), axis=0, keepdims=True)
    meta_o[3:4, :] = jnp.clip(valid_end - lane1.astype(F32) * float(BM), 0.0, float(BM)).astype(jnp.int32)

    ti = lax.broadcasted_iota(jnp.int32, (TT, TT), 0)
    tj = lax.broadcasted_iota(jnp.int32, (TT, TT), 1)
    tri_ref[...] = jnp.where(ti <= tj, 1.0, 0.0).astype(BF16)
    tlane = lax.broadcasted_iota(jnp.int32, (N_EXPERTS, LANES), 1)

    def tile_body(i, carry):
        first_slot, t_cnt, t_slot, t_rank = carry
        off, m1, m2 = onehots(i)
        m = m1 + m2
        incl = _dot(m.astype(BF16), tri_ref[...])
        cnt = incl[:, TT - 1:TT]
        cnt_row = jnp.sum(jnp.where(ei == ej, cnt, 0.0), axis=0, keepdims=True)
        first_rank = jnp.sum(jnp.where(ej < ei, cnt_row, 0.0), axis=1, keepdims=True)
        rank = first_rank + incl - m
        lpos_o[0:1, pl.ds(off, TT)] = jnp.sum(m1 * rank, axis=0, keepdims=True).astype(jnp.int32)
        lpos_o[1:2, pl.ds(off, TT)] = jnp.sum(m2 * rank, axis=0, keepdims=True).astype(jnp.int32)
        here = tlane == i
        return (first_slot + cnt, jnp.where(here, cnt, t_cnt), jnp.where(here, first_slot, t_slot),
                jnp.where(here, first_rank, t_rank))

    zeros = jnp.zeros((N_EXPERTS, LANES), F32)
    _, t_cnt, t_slot, t_rank = lax.fori_loop(0, n_tiles, tile_body,
                                             (bstart * float(BM), zeros, zeros, zeros))
    runs_o[0] = t_cnt.astype(jnp.int32)
    runs_o[1] = t_slot.astype(jnp.int32)
    runs_o[2] = t_rank.astype(jnp.int32)
    pad = jnp.where(tlane == 0, bstart * float(BM) + counts,
                    jnp.where(tlane == 1, nblk * float(BM) - counts, 0.0))
    runs_o[3] = pad.astype(jnp.int32)


def _plan(rt, n_tok, nb_max):
    assert n_tok // TT <= LANES, "run tables hold one token tile per lane"
    meta_w = ((nb_max + 1 + LANES - 1) // LANES) * LANES
    return pl.pallas_call(
        functools.partial(_plan_kernel, n_tok=n_tok, nb_max=nb_max),
        out_shape=[jax.ShapeDtypeStruct((2, n_tok), jnp.int32),
                   jax.ShapeDtypeStruct((4, meta_w), jnp.int32),
                   jax.ShapeDtypeStruct((RUN_FIELDS, N_EXPERTS, LANES), jnp.int32)],
        scratch_shapes=[pltpu.VMEM((TT, TT), BF16)],
        compiler_params=pltpu.CompilerParams(vmem_limit_bytes=VMEM_LIMIT),
        name="plan",
    )(rt)


def _run_entry(runs_ref, field, expert, tile):
    return runs_ref[(field * N_EXPERTS + expert) * LANES + tile]


def _dispatch_kernel(runs_ref, meta_ref, hn_ref, lpos_ref, xs_hbm, xsbuf, zbuf, sem, zsem,
                     *, n_tiles, nb_max, n_pad):
    i = pl.program_id(0)
    slot = lax.rem(i, 2)

    @pl.when(i == 0)
    def _():
        zbuf[...] = jnp.zeros_like(zbuf)
        for e in range(N_EXPERTS):
            _copy_run(zbuf, 0, xs_hbm, _run_entry(runs_ref, 3, e, 0), _run_entry(runs_ref, 3, e, 1),
                      zsem)

        def unused_block(blk, carry):
            dst = xs_hbm.at[pl.ds(pl.multiple_of(blk * (BM * TILE_ROWS), BM * TILE_ROWS),
                                  BM * TILE_ROWS), :]
            pltpu.make_async_copy(zbuf, dst, zsem).start()
            return carry

        lax.fori_loop(meta_ref[nb_max], nb_max, unused_block, 0)

    @pl.when(i >= 2)
    def _():
        _wait_rows(xsbuf.at[slot], TILE_SLOTS, sem.at[slot])

    rank = lax.broadcasted_iota(jnp.int32, (TILE_SLOTS, TT), 0)
    onehot = jnp.where(rank == lpos_ref[0:1, :], 1.0, jnp.where(rank == lpos_ref[1:2, :], 1.0, 0.0))
    _store_token_tiles(xsbuf.at[slot], _dot(onehot.astype(BF16), hn_ref[...]), already_bf16=True)
    for e in range(N_EXPERTS):
        _copy_run(xsbuf.at[slot], _run_entry(runs_ref, 2, e, i), xs_hbm, _run_entry(runs_ref, 1, e, i),
                  _run_entry(runs_ref, 0, e, i), sem.at[slot])

    @pl.when(i == n_tiles - 1)
    def _():
        _wait_rows(xsbuf.at[slot], TILE_SLOTS, sem.at[slot])
        if n_tiles > 1:
            _wait_rows(xsbuf.at[1 - slot], TILE_SLOTS, sem.at[1 - slot])
        _wait_rows(xs_hbm, n_pad, zsem)


def _dispatch(runs, meta, hn, lpos, n_tok, nb_max):
    n_tiles = n_tok // TT
    cap = nb_max * BM
    grid_spec = pltpu.PrefetchScalarGridSpec(
        num_scalar_prefetch=2,
        grid=(n_tiles,),
        in_specs=[pl.BlockSpec((TT, D_MODEL), lambda i, runs_ref, meta_ref: (i, 0)),
                  pl.BlockSpec((2, TT), lambda i, runs_ref, meta_ref: (0, i))],
        out_specs=pl.BlockSpec(memory_space=pl.ANY),
        scratch_shapes=[pltpu.VMEM((2, TILE_SLOTS * TILE_ROWS, LANES), U32),
                        pltpu.VMEM((BM * TILE_ROWS, LANES), U32),
                        pltpu.SemaphoreType.DMA((2,)), pltpu.SemaphoreType.DMA],
    )
    return pl.pallas_call(
        functools.partial(_dispatch_kernel, n_tiles=n_tiles, nb_max=nb_max, n_pad=cap - 2 * n_tok),
        grid_spec=grid_spec,
        out_shape=jax.ShapeDtypeStruct((cap * TILE_ROWS, LANES), U32),
        compiler_params=pltpu.CompilerParams(dimension_semantics=("arbitrary",),
                                             vmem_limit_bytes=VMEM_LIMIT),
        name="dispatch",
    )(runs.reshape(-1), meta.reshape(-1), hn, lpos)


def _experts_kernel(meta_ref, xs_ref, wg_hbm, wu_hbm, wd_hbm, y_ref, wf32, wbf, wsem, *, nb_max, meta_w):
    b = pl.program_id(0)
    n_used = meta_ref[nb_max]

    def start_weights(expert, buf):
        for j, w_hbm in enumerate((wg_hbm, wu_hbm, wd_hbm)):
            pltpu.make_async_copy(w_hbm.at[expert], wf32.at[buf, j], wsem.at[buf]).start(priority=1)

    @pl.when(b < n_used)
    def _():
        owner = meta_ref[b]

        @pl.when(b == 0)
        def _():
            start_weights(owner, 0)

        @pl.when(jnp.logical_or(b == 0, owner != meta_ref[jnp.maximum(b - 1, 0)]))
        def _():
            buf = jnp.bitwise_and(meta_ref[meta_w + b], 1)
            pltpu.make_async_copy(wf32.at[buf], wf32.at[buf], wsem.at[buf]).wait()
            next_owner = meta_ref[2 * meta_w + b]

            @pl.when(next_owner >= 0)
            def _():
                start_weights(next_owner, 1 - buf)

            for j in range(3):
                wbf[j] = wf32[buf, j].astype(BF16)

        def geglu(n_rows):
            rows = pl.ds(0, n_rows * TILE_ROWS)
            x = _load_token_tiles(xs_ref.at[rows, :], n_rows).astype(BF16)
            mid = (jax.nn.gelu(_dot(x, wbf[0])) * _dot(x, wbf[1])).astype(BF16)
            _store_token_tiles(y_ref.at[rows, :], _dot(mid, wbf[2]))

        n_groups = lax.shift_right_logical(meta_ref[3 * meta_w + b] + (SKIP_ROWS - 1),
                                           SKIP_ROWS.bit_length() - 1)
        for g in range(1, BM // SKIP_ROWS + 1):
            @pl.when(n_groups == g)
            def _(g=g):
                geglu(g * SKIP_ROWS)
                if g * SKIP_ROWS < BM:
                    rest = (BM - g * SKIP_ROWS) * TILE_ROWS
                    y_ref[pl.ds(g * SKIP_ROWS * TILE_ROWS, rest), :] = jnp.zeros((rest, LANES), U32)

    @pl.when(b >= n_used)
    def _():
        y_ref[...] = jnp.zeros_like(y_ref)


def _experts(meta, xs, w_gate, w_up, w_down, nb_max):
    whole = pl.BlockSpec(memory_space=pl.ANY)
    rows = pl.BlockSpec((BM * TILE_ROWS, LANES), lambda b, meta_ref: (b, 0))
    grid_spec = pltpu.PrefetchScalarGridSpec(
        num_scalar_prefetch=1,
        grid=(nb_max,),
        in_specs=[rows, whole, whole, whole],
        out_specs=rows,
        scratch_shapes=[pltpu.VMEM((2, 3, D_MODEL, D_MODEL), F32),
                        pltpu.VMEM((3, D_MODEL, D_MODEL), BF16),
                        pltpu.SemaphoreType.DMA((2,))],
    )
    return pl.pallas_call(
        functools.partial(_experts_kernel, nb_max=nb_max, meta_w=meta.shape[1]),
        grid_spec=grid_spec,
        out_shape=jax.ShapeDtypeStruct(xs.shape, U32),
        compiler_params=pltpu.CompilerParams(dimension_semantics=("arbitrary",),
                                             vmem_limit_bytes=VMEM_LIMIT),
        name="experts",
    )(meta.reshape(-1), xs, w_gate, w_up, w_down)


def _combine_kernel(runs_ref, h_ref, lpos_ref, rt_ref, gn_ref, y_hbm, o_ref, ysbuf, sem, *, n_tiles):
    i = pl.program_id(0)
    slot = lax.rem(i, 2)

    def fetch(tile, buf):
        for e in range(N_EXPERTS):
            _copy_run(y_hbm, _run_entry(runs_ref, 1, e, tile), ysbuf.at[buf],
                      _run_entry(runs_ref, 2, e, tile), _run_entry(runs_ref, 0, e, tile), sem.at[buf])

    @pl.when(i == 0)
    def _():
        fetch(0, 0)

    @pl.when(i + 1 < n_tiles)
    def _():
        fetch(i + 1, 1 - slot)

    _wait_rows(ysbuf.at[slot], TILE_SLOTS, sem.at[slot])

    rank = lax.broadcasted_iota(jnp.int32, (TILE_SLOTS, TT), 0)
    first = rank == lpos_ref[0:1, :]
    second = rank == lpos_ref[1:2, :]
    slot_w = jnp.sum(jnp.where(first, rt_ref[2:3, :], 0.0) + jnp.where(second, rt_ref[3:4, :], 0.0),
                     axis=1, keepdims=True)
    onehot = jnp.where(first, 1.0, jnp.where(second, 1.0, 0.0)).astype(BF16)
    ys = (_load_token_tiles(ysbuf.at[slot], TILE_SLOTS) * slot_w).astype(BF16)
    moe = lax.dot_general(onehot, ys, (((0,), (0,)), ((), ())), preferred_element_type=F32)
    o_ref[...] = _rms(h_ref[...] + moe, gn_ref[...])


def _combine(runs, h, lpos, rt, norm_final, y, n_tok):
    n_tiles = n_tok // TT
    grid_spec = pltpu.PrefetchScalarGridSpec(
        num_scalar_prefetch=1,
        grid=(n_tiles,),
        in_specs=[pl.BlockSpec((TT, D_MODEL), lambda i, runs_ref: (i, 0)),
                  pl.BlockSpec((2, TT), lambda i, runs_ref: (0, i)),
                  pl.BlockSpec((SUBLANES, TT), lambda i, runs_ref: (0, i)),
                  pl.BlockSpec((1, D_MODEL), lambda i, runs_ref: (0, 0)),
                  pl.BlockSpec(memory_space=pl.ANY)],
        out_specs=pl.BlockSpec((TT, D_MODEL), lambda i, runs_ref: (i, 0)),
        scratch_shapes=[pltpu.VMEM((2, TILE_SLOTS * TILE_ROWS, LANES), U32),
                        pltpu.SemaphoreType.DMA((2,))],
    )
    return pl.pallas_call(
        functools.partial(_combine_kernel, n_tiles=n_tiles),
        grid_spec=grid_spec,
        out_shape=jax.ShapeDtypeStruct((n_tok, D_MODEL), F32),
        compiler_params=pltpu.CompilerParams(dimension_semantics=("arbitrary",),
                                             vmem_limit_bytes=VMEM_LIMIT),
        name="combine",
    )(runs.reshape(-1), h, lpos, rt, norm_final.reshape(1, D_MODEL).astype(F32), y)


def kernel(x, norm_mix, w_in, conv_w, conv_b, w_rg_a, b_rg_a, w_rg_x, b_rg_x, rg_lambda, rg_norm, w_alpha_up, b_alpha, gla_norm, w_out, norm_ffn, w_router_group, b_router_group, w_router_expert, b_router_expert, w_exp_gate, w_exp_up, w_exp_down, norm_final):
    bsz, seq, d = x.shape
    assert d == D_MODEL and norm_mix.shape[0] == 1, "single-layer model of width D_MODEL expected"
    n_tok = bsz * seq
    nb_max = (2 * n_tok + N_EXPERTS * (BM - 1)) // BM

    x2 = x.reshape(n_tok, d)
    xr, yr, q, k, v, g, la = _inproj(x2, norm_mix[0], w_in[0], w_alpha_up[0], b_alpha[0])
    h, hn, rt = _mixer(xr, yr, q, k, v, g, la, x, conv_w[0], conv_b[0], w_rg_a[0], b_rg_a[0], w_rg_x[0],
                       b_rg_x[0], rg_lambda[0], rg_norm[0], gla_norm[0], w_out[0], norm_ffn[0],
                       w_router_group[0], b_router_group[0], w_router_expert[0], b_router_expert[0])
    lpos, meta, runs = _plan(rt, n_tok, nb_max)
    xs = _dispatch(runs, meta, hn, lpos, n_tok, nb_max)
    y = _experts(meta, xs, w_exp_gate[0], w_exp_up[0], w_exp_down[0], nb_max)
    out = _combine(runs, h, lpos, rt, norm_final, y, n_tok)
    return out.reshape(bsz, seq, d)
```

```python
import functools

import jax
import jax.numpy as jnp
from jax import lax
from jax.experimental import pallas as pl
from jax.experimental.pallas import tpu as pltpu

F32 = jnp.float32
BF16 = jnp.bfloat16

D_MODEL = 1024
RG_WIDTH = 512
RG_BLOCKS = 8
RG_BLOCK = 64
CONV_WIDTH = 4
C_RG = 8.0
GLA_HEADS = 4
GLA_VAL = 512
GLA_KEY = 256
GLA_DK = 64
GLA_DV = 128
GATE_RANK = 16
GATE_NORM = 16.0
CHUNK = 64
N_GROUPS = 4
EXPERTS_PER_GROUP = 8
N_EXPERTS = 32
EPS = 1e-6

LANES = 128
SUBLANES = 8
VMEM_LIMIT = 56 * 1024 * 1024

TS_MIX = 512
BM = 512
SKIP_ROWS = 128
TT = 512
TILE_SLOTS = 2 * TT
RUN_CHUNK_LOG2 = 6
RUN_FIELDS = 4


def _dot(a, b):
    return jnp.dot(a, b, preferred_element_type=F32)


def _softplus(z):
    return jnp.maximum(z, 0.0) + jnp.log1p(jnp.exp(-jnp.abs(z)))


def _rms(x, g):
    return x * lax.rsqrt(jnp.mean(x * x, axis=-1, keepdims=True) + EPS) * g


U32 = jnp.uint32
TILE_ROWS = D_MODEL // 2 // LANES


def _copy_run(src, src_row, dst, dst_row, n_rows, sem):
    def piece(off, rows):
        s = src.at[pl.ds(pl.multiple_of((src_row + off) * TILE_ROWS, TILE_ROWS), rows * TILE_ROWS), :]
        d = dst.at[pl.ds(pl.multiple_of((dst_row + off) * TILE_ROWS, TILE_ROWS), rows * TILE_ROWS), :]
        pltpu.make_async_copy(s, d, sem).start()

    chunk = 1 << RUN_CHUNK_LOG2
    n_chunks = lax.shift_right_logical(n_rows, RUN_CHUNK_LOG2)
    lax.fori_loop(0, n_chunks, lambda c, carry: (piece(c * chunk, chunk), carry)[1], 0)
    off = n_chunks * chunk
    for k in reversed(range(RUN_CHUNK_LOG2)):
        bit = jnp.bitwise_and(n_rows, 1 << k)

        @pl.when(bit != 0)
        def _():
            piece(off, 1 << k)

        off = off + bit


def _wait_rows(ref, n_rows, sem):
    view = ref.at[pl.ds(0, n_rows * TILE_ROWS), :]
    pltpu.make_async_copy(view, view, sem).wait()


def _store_token_tiles(ref, val, already_bf16=False):
    n = val.shape[0]
    if not already_bf16:
        val = val.astype(BF16).astype(F32)
    bits = lax.bitcast_convert_type(val, U32)
    packed = jnp.bitwise_or(bits[:, :D_MODEL // 2], jnp.right_shift(bits[:, D_MODEL // 2:], 16))
    for c in range(TILE_ROWS):
        ref[pl.ds(c, n, stride=TILE_ROWS), :] = packed[:, c * LANES:(c + 1) * LANES]


def _load_token_tiles(ref, n):
    words = [ref[pl.ds(c, n, stride=TILE_ROWS), :] for c in range(TILE_ROWS)]
    high = [lax.bitcast_convert_type(jnp.bitwise_and(w, jnp.uint32(0xFFFF0000)), F32) for w in words]
    low = [lax.bitcast_convert_type(jnp.left_shift(w, 16), F32) for w in words]
    return jnp.concatenate(high + low, axis=1)


def _inproj_part(bi, x_ref, g_ref, wxr, wyr, wq, wk, wv, wg, wal, wup, bal,
                 xr_o, yr_o, q_o, k_o, v_o, g_o, la_o):
    hn = _rms(x_ref[bi], g_ref[...]).astype(BF16)
    xr_o[bi] = _dot(hn, wxr[...])
    yr_o[bi] = _dot(hn, wyr[...])
    q_o[bi] = _dot(hn, wq[...])
    k_o[bi] = _dot(hn, wk[...])
    v_o[bi] = _dot(hn, wv[...])
    g_o[bi] = _dot(hn, wg[...])
    a3 = _dot(hn, wal[...])
    a3_hi = a3.astype(BF16).astype(F32)
    lane = lax.broadcasted_iota(jnp.int32, a3.shape, 1)
    use_low = jnp.logical_and(lane >= GATE_RANK, lane < 2 * GATE_RANK)
    z = _dot(jnp.where(use_low, a3 - a3_hi, a3_hi).astype(BF16), wup[...]) + bal[...]
    log_sig = jnp.minimum(z, 0.0) - jnp.log1p(jnp.exp(-jnp.abs(z)))
    la_o[bi] = log_sig * (1.0 / GATE_NORM)


PROJ_WIDTHS = (RG_WIDTH, RG_WIDTH, GLA_KEY, GLA_KEY, GLA_VAL, GLA_VAL, GLA_KEY)


def _inproj_weights(w_in, w_alpha_up):
    cuts = [0]
    for w in PROJ_WIDTHS[:6]:
        cuts.append(cuts[-1] + w)
    wb = w_in.astype(BF16)
    segs = [wb[:, cuts[i]:cuts[i + 1]] for i in range(6)]
    w_low = wb[:, cuts[6]:cuts[6] + GATE_RANK]
    wal = jnp.pad(jnp.concatenate([w_low, w_low, w_low], axis=1), ((0, 0), (0, LANES - 3 * GATE_RANK)))
    up = w_alpha_up.astype(F32)
    up_hi = up.astype(BF16)
    up_lo = (up - up_hi.astype(F32)).astype(BF16)
    wup = jnp.pad(jnp.concatenate([up_hi, up_hi, up_lo], axis=0), ((0, LANES - 3 * GATE_RANK), (0, 0)))
    return segs + [wal, wup]


def _rglru_part(bi, xr_ref, yr_ref, cw_ref, cb_ref, wa_ref, ba_ref, wx_ref, bx_ref, lam_ref, gn_ref,
                o_ref, tail_ref, hc_ref):
    ts = xr_ref.shape[1]
    n_groups = ts // SUBLANES
    grouped = (n_groups, SUBLANES, RG_WIDTH)
    sub = lax.broadcasted_iota(jnp.int32, grouped, 1)

    x = xr_ref[bi]
    x_grp = x.reshape(grouped)
    x_prev = jnp.concatenate([tail_ref[bi], x[:ts - SUBLANES, :]], axis=0).reshape(grouped)
    tail_ref[bi] = x[ts - SUBLANES:ts, :]
    cw = cw_ref[...]
    xc = cb_ref[...] + cw[CONV_WIDTH - 1:CONV_WIDTH, :] * x
    for s in range(1, CONV_WIDTH):
        mixed = jnp.where(sub >= SUBLANES - s, x_prev, x_grp)
        shifted = pltpu.roll(mixed, s, axis=1)
        xc = xc + cw[CONV_WIDTH - 1 - s:CONV_WIDTH - s, :] * shifted.reshape(ts, RG_WIDTH)

    xb = xc.astype(BF16)
    r = jax.nn.sigmoid(_dot(xb, wa_ref[...]) + ba_ref[...])
    gate_i = jax.nn.sigmoid(_dot(xb, wx_ref[...]) + bx_ref[...])
    log_a = (-C_RG) * r * _softplus(-lam_ref[...])
    a = jnp.exp(log_a)
    z = jnp.tanh(-log_a) * (1.0 + a * a)
    u = (z * lax.rsqrt(jnp.maximum(z, jnp.finfo(F32).tiny))) * (gate_i * xc)

    a = a.reshape(grouped)
    u = u.reshape(grouped)
    d = 1
    while d < SUBLANES:
        keep = sub >= d
        a_sh = jnp.where(keep, pltpu.roll(a, d, axis=1), 1.0)
        u_sh = jnp.where(keep, pltpu.roll(u, d, axis=1), 0.0)
        u = a * u_sh + u
        a = a * a_sh
        d *= 2
    carry = hc_ref[bi, 0:1, :]
    groups = []
    for j in range(n_groups):
        h_j = u[j] + a[j] * carry
        groups.append(h_j)
        carry = h_j[SUBLANES - 1:SUBLANES, :]
    h = jnp.concatenate(groups, axis=0)
    hc_ref[bi] = jnp.broadcast_to(carry, hc_ref.shape[1:])

    y = h * jax.nn.gelu(yr_ref[bi])
    o_ref[bi, :, 0:RG_WIDTH] = _rms(y, gn_ref[...]).astype(o_ref.dtype)


def _block_diag(w):
    eye = jnp.eye(RG_BLOCKS, dtype=w.dtype)
    return jnp.einsum('hij,hg->higj', w, eye).reshape(RG_WIDTH, RG_WIDTH)


def _mixer_kernel(x_ref, mgn_ref, wxr, wyr, wq, wk, wv, wg, wal, wup, bal,
                  cw_ref, cb_ref, wa_ref, ba_ref, wx_ref, bx_ref, lam_ref, rgn_ref, gn_ref,
                  wo_ref, fgn_ref, wr_ref, br_ref,
                  h_o, hn_o, rt_o,
                  xr_ref, yr_ref, q_ref, k_ref, v_ref, g_ref, la_ref, o_ref, tail_ref, hc_ref, st_ref):
    bsz, ts = q_ref.shape[0], q_ref.shape[1]
    n_chunks = ts // CHUNK

    @pl.when(pl.program_id(0) == 0)
    def _():
        tail_ref[...] = jnp.zeros_like(tail_ref)
        hc_ref[...] = jnp.zeros_like(hc_ref)
        st_ref[...] = jnp.zeros_like(st_ref)

    ri = lax.broadcasted_iota(jnp.int32, (ts, ts), 0)
    ci = lax.broadcasted_iota(jnp.int32, (ts, ts), 1)
    chunk_bits = CHUNK.bit_length() - 1
    same_chunk = lax.shift_right_logical(ri, chunk_bits) == lax.shift_right_logical(ci, chunk_bits)
    prefix = jnp.where(jnp.logical_and(same_chunk, ri >= ci), 1.0, 0.0).astype(BF16)
    total = jnp.where(same_chunk, 1.0, 0.0).astype(BF16)
    causal = (ri >= ci)[:CHUNK, :CHUNK]
    scale = GLA_DK ** -0.5
    gn = gn_ref[...]
    nt_dims = (((1,), (1,)), ((), ()))
    tn_dims = (((0,), (0,)), ((), ()))

    for bi in range(bsz):
        _inproj_part(bi, x_ref, mgn_ref, wxr, wyr, wq, wk, wv, wg, wal, wup, bal,
                     xr_ref, yr_ref, q_ref, k_ref, v_ref, g_ref, la_ref)
        _rglru_part(bi, xr_ref, yr_ref, cw_ref, cb_ref, wa_ref, ba_ref, wx_ref, bx_ref, lam_ref, rgn_ref,
                    o_ref, tail_ref, hc_ref)

        la = la_ref[bi]
        la_hi = la.astype(BF16)
        la_lo = (la - la_hi.astype(F32)).astype(BF16)
        b = _dot(prefix, la_hi) + _dot(prefix, la_lo)
        b_tot = _dot(total, la_hi) + _dot(total, la_lo)
        kk = k_ref[bi]
        q_s = (q_ref[bi] * scale) * jnp.exp(b)
        k_s = kk * jnp.exp(-b)
        k_end = kk * jnp.exp(b_tot - b)
        decay = jnp.exp(b_tot)

        o_intra, kv, q_heads = [], [], []
        for c in range(n_chunks):
            rows = slice(c * CHUNK, (c + 1) * CHUNK)
            o_c, kv_c, q_c = [], [], []
            for h in range(GLA_HEADS):
                ks = slice(h * GLA_DK, (h + 1) * GLA_DK)
                vs = slice(h * GLA_DV, (h + 1) * GLA_DV)
                qh = q_s[rows, ks].astype(BF16)
                vb = v_ref[bi, rows, vs].astype(BF16)
                att = lax.dot_general(qh, k_s[rows, ks].astype(BF16), nt_dims, preferred_element_type=F32)
                att = jnp.where(causal, att, 0.0).astype(BF16)
                o_c.append(_dot(att, vb))
                kv_c.append(lax.dot_general(vb, k_end[rows, ks].astype(BF16), tn_dims,
                                            preferred_element_type=F32))
                q_c.append(qh)
            o_intra.append(o_c)
            q_heads.append(q_c)
            kv.append(jnp.concatenate(kv_c, axis=1))

        st = st_ref[bi]
        for c in range(n_chunks):
            rows = slice(c * CHUNK, (c + 1) * CHUNK)
            for h in range(GLA_HEADS):
                ks = slice(h * GLA_DK, (h + 1) * GLA_DK)
                vs = slice(h * GLA_DV, (h + 1) * GLA_DV)
                o = o_intra[c][h] + lax.dot_general(q_heads[c][h], st[:, ks].astype(BF16), nt_dims,
                                                    preferred_element_type=F32)
                gated = _rms(o, gn) * jax.nn.silu(g_ref[bi, rows, vs])
                o_ref[bi, rows, RG_WIDTH + h * GLA_DV:RG_WIDTH + (h + 1) * GLA_DV] = gated.astype(o_ref.dtype)
            st = decay[c * CHUNK:c * CHUNK + 1, :] * st + kv[c]
        st_ref[bi] = st

        _outproj_part(bi, o_ref, x_ref, wo_ref, fgn_ref, wr_ref, br_ref, h_o, hn_o, rt_o)


def _mixer(x, norm_mix, w_in, w_alpha_up, b_alpha, conv_w, conv_b, w_a, b_a, w_x, b_x, lam, rg_norm,
           gla_norm, w_out, norm_ffn, w_rg, b_rg, w_re, b_re):
    bsz, seq, _ = x.shape
    nt = seq // TS_MIX
    blk = lambda w: pl.BlockSpec((bsz, TS_MIX, w), lambda i: (0, i, 0))
    fixed = lambda r, c: pl.BlockSpec((r, c), lambda i: (0, 0))
    vec = lambda a: a.reshape(1, -1).astype(F32)
    proj_w = _inproj_weights(w_in, w_alpha_up)
    wr = jnp.pad(jnp.concatenate([w_rg, w_re], axis=1).astype(F32),
                 ((0, 0), (0, LANES - N_GROUPS - N_EXPERTS)))
    wr_hi = wr.astype(BF16)
    wr = jnp.concatenate([wr_hi, (wr - wr_hi.astype(F32)).astype(BF16)], axis=1)
    br = jnp.pad(jnp.concatenate([b_rg, b_re]).astype(F32), (0, LANES - N_GROUPS - N_EXPERTS))
    h, hn, rt = pl.pallas_call(
        _mixer_kernel,
        grid=(nt,),
        in_specs=[blk(D_MODEL), fixed(1, D_MODEL)]
                 + [fixed(*w.shape) for w in proj_w] + [fixed(1, GLA_KEY),
                  fixed(CONV_WIDTH, RG_WIDTH), fixed(1, RG_WIDTH), fixed(RG_WIDTH, RG_WIDTH),
                  fixed(1, RG_WIDTH), fixed(RG_WIDTH, RG_WIDTH), fixed(1, RG_WIDTH), fixed(1, RG_WIDTH),
                  fixed(1, RG_WIDTH), fixed(1, GLA_DV),
                  fixed(D_MODEL, D_MODEL), fixed(1, D_MODEL), fixed(D_MODEL, 2 * LANES), fixed(1, LANES)],
        out_specs=[blk(D_MODEL), blk(D_MODEL),
                   pl.BlockSpec((bsz, SUBLANES, TS_MIX), lambda i: (0, 0, i))],
        out_shape=[jax.ShapeDtypeStruct((bsz, seq, D_MODEL), F32),
                   jax.ShapeDtypeStruct((bsz, seq, D_MODEL), BF16),
                   jax.ShapeDtypeStruct((bsz, SUBLANES, seq), F32)],
        scratch_shapes=[pltpu.VMEM((bsz, TS_MIX, w), F32) for w in PROJ_WIDTHS]
                      + [pltpu.VMEM((bsz, TS_MIX, D_MODEL), BF16),
                        pltpu.VMEM((bsz, SUBLANES, RG_WIDTH), F32),
                        pltpu.VMEM((bsz, SUBLANES, RG_WIDTH), F32),
                        pltpu.VMEM((bsz, GLA_DV, GLA_KEY), F32)],
        compiler_params=pltpu.CompilerParams(dimension_semantics=("arbitrary",),
                                             vmem_limit_bytes=VMEM_LIMIT),
        name="mixer",
    )(x, vec(norm_mix), *proj_w, vec(b_alpha),
      conv_w.astype(F32), vec(conv_b), _block_diag(w_a).astype(BF16), vec(b_a),
      _block_diag(w_x).astype(BF16), vec(b_x), vec(lam), vec(rg_norm), vec(gla_norm),
      w_out.astype(BF16), vec(norm_ffn), wr, br.reshape(1, LANES))
    n_tok = bsz * seq
    rt = rt.transpose(1, 0, 2).reshape(SUBLANES, n_tok)
    return h.reshape(n_tok, D_MODEL), hn.reshape(n_tok, D_MODEL), rt


def _outproj_part(bi, y_ref, x_ref, wo_ref, gn_ref, wr_ref, br_ref, h_o, hn_o, rt_o):
    tm = x_ref.shape[1]
    h = x_ref[bi] + _dot(y_ref[bi], wo_ref[...])
    h_o[bi] = h
    hn = _rms(h, gn_ref[...])
    hn_hi = hn.astype(BF16)
    hn_o[bi] = hn_hi
    hn_lo = (hn - hn_hi.astype(F32)).astype(BF16)
    hi_parts = _dot(hn_hi, wr_ref[...])
    logits = hi_parts[:, :LANES] + hi_parts[:, LANES:] + _dot(hn_lo, wr_ref[:, :LANES]) + br_ref[...]

    lane = lax.broadcasted_iota(jnp.int32, (tm, LANES), 1).astype(F32)
    neg = -jnp.inf
    glog = jnp.where(lane < N_GROUPS, logits, neg)
    gmax = jnp.max(glog, axis=-1, keepdims=True)
    gidx = jnp.min(jnp.where(glog == gmax, lane, float(LANES)), axis=-1, keepdims=True)
    g_w = 1.0 / jnp.sum(jnp.exp(glog - gmax), axis=-1, keepdims=True)
    lo = N_GROUPS + gidx * EXPERTS_PER_GROUP
    in_group = jnp.logical_and(lane >= lo, lane < lo + EXPERTS_PER_GROUP)
    le = jnp.where(in_group, logits, neg)
    m1 = jnp.max(le, axis=-1, keepdims=True)
    i1 = jnp.min(jnp.where(le == m1, lane, float(LANES)), axis=-1, keepdims=True)
    le2 = jnp.where(lane == i1, neg, le)
    m2 = jnp.max(le2, axis=-1, keepdims=True)
    i2 = jnp.min(jnp.where(le2 == m2, lane, float(LANES)), axis=-1, keepdims=True)
    t2 = jnp.exp(m2 - m1)
    w1 = g_w / (1.0 + t2)
    w2 = g_w * t2 / (1.0 + t2)
    info = jnp.where(lane == 0.0, i1 - N_GROUPS,
                     jnp.where(lane == 1.0, i2 - N_GROUPS,
                               jnp.where(lane == 2.0, w1, jnp.where(lane == 3.0, w2, 0.0))))
    rt_o[bi] = info.T[0:SUBLANES, :]


def _plan_kernel(rt_ref, lpos_o, meta_o, runs_o, tri_ref, *, n_tok, nb_max):
    n_tiles = n_tok // TT
    esub = lax.broadcasted_iota(jnp.int32, (N_EXPERTS, TT), 0).astype(F32)

    def onehots(i):
        off = pl.multiple_of(i * TT, TT)
        e1 = rt_ref[0:1, pl.ds(off, TT)]
        e2 = rt_ref[1:2, pl.ds(off, TT)]
        m1 = jnp.where(esub == e1, 1.0, 0.0)
        m2 = jnp.where(esub == e2, 1.0, 0.0)
        return off, m1, m2

    def count_body(i, cnt):
        _, m1, m2 = onehots(i)
        return cnt + jnp.sum(m1 + m2, axis=1, keepdims=True)

    counts = lax.fori_loop(0, n_tiles, count_body, jnp.zeros((N_EXPERTS, 1), F32))
    nblk = jnp.floor((counts + (BM - 1)) * (1.0 / BM))
    ei = lax.broadcasted_iota(jnp.int32, (N_EXPERTS, N_EXPERTS), 0)
    ej = lax.broadcasted_iota(jnp.int32, (N_EXPERTS, N_EXPERTS), 1)
    nblk_row = jnp.sum(jnp.where(ei == ej, nblk, 0.0), axis=0, keepdims=True)
    bstart = jnp.sum(jnp.where(ej < ei, nblk_row, 0.0), axis=1, keepdims=True)
    bend = bstart + nblk
    n_used = jnp.sum(nblk, axis=0, keepdims=True)

    meta_w = meta_o.shape[1]
    blane = lax.broadcasted_iota(jnp.int32, (N_EXPERTS, meta_w), 1).astype(F32)
    owner = jnp.sum(jnp.where(bend <= blane, 1.0, 0.0), axis=0, keepdims=True)
    owner = jnp.minimum(owner, N_EXPERTS - 1.0)
    lane1 = lax.broadcasted_iota(jnp.int32, (1, meta_w), 1)
    meta_o[0:1, :] = jnp.where(lane1 == nb_max, n_used, owner).astype(jnp.int32)
    nonempty = jnp.where(nblk > 0.0, 1.0, 0.0)
    meta_o[1:2, :] = jnp.sum(jnp.where(bend <= blane, nonempty, 0.0), axis=0,
                             keepdims=True).astype(jnp.int32)
    owned = jnp.logical_and(bstart <= blane, blane < bend)
    next_start = jnp.sum(jnp.where(owned, bend, 0.0), axis=0, keepdims=True)
    next_owner = jnp.sum(jnp.where(bend <= next_start, 1.0, 0.0), axis=0, keepdims=True)
    meta_o[2:3, :] = jnp.where(next_start < n_used, next_owner, -1.0).astype(jnp.int32)
    valid_end = jnp.sum(jnp.where(owned, bstart * float(BM) + counts, 0.0), axis=0, keepdims=True)
    meta_o[3:4, :] = jnp.clip(valid_end - lane1.astype(F32) * float(BM), 0.0, float(BM)).astype(jnp.int32)

    ti = lax.broadcasted_iota(jnp.int32, (TT, TT), 0)
    tj = lax.broadcasted_iota(jnp.int32, (TT, TT), 1)
    tri_ref[...] = jnp.where(ti <= tj, 1.0, 0.0).astype(BF16)
    tlane = lax.broadcasted_iota(jnp.int32, (N_EXPERTS, LANES), 1)

    def tile_body(i, carry):
        first_slot, t_cnt, t_slot, t_rank = carry
        off, m1, m2 = onehots(i)
        m = m1 + m2
        incl = _dot(m.astype(BF16), tri_ref[...])
        cnt = incl[:, TT - 1:TT]
        cnt_row = jnp.sum(jnp.where(ei == ej, cnt, 0.0), axis=0, keepdims=True)
        first_rank = jnp.sum(jnp.where(ej < ei, cnt_row, 0.0), axis=1, keepdims=True)
        rank = first_rank + incl - m
        lpos_o[0:1, pl.ds(off, TT)] = jnp.sum(m1 * rank, axis=0, keepdims=True).astype(jnp.int32)
        lpos_o[1:2, pl.ds(off, TT)] = jnp.sum(m2 * rank, axis=0, keepdims=True).astype(jnp.int32)
        here = tlane == i
        return (first_slot + cnt, jnp.where(here, cnt, t_cnt), jnp.where(here, first_slot, t_slot),
                jnp.where(here, first_rank, t_rank))

    zeros = jnp.zeros((N_EXPERTS, LANES), F32)
    _, t_cnt, t_slot, t_rank = lax.fori_loop(0, n_tiles, tile_body,
                                             (bstart * float(BM), zeros, zeros, zeros))
    runs_o[0] = t_cnt.astype(jnp.int32)
    runs_o[1] = t_slot.astype(jnp.int32)
    runs_o[2] = t_rank.astype(jnp.int32)
    pad = jnp.where(tlane == 0, bstart * float(BM) + counts,
                    jnp.where(tlane == 1, nblk * float(BM) - counts, 0.0))
    runs_o[3] = pad.astype(jnp.int32)


def _plan(rt, n_tok, nb_max):
    assert n_tok // TT <= LANES, "run tables hold one token tile per lane"
    meta_w = ((nb_max + 1 + LANES - 1) // LANES) * LANES
    return pl.pallas_call(
        functools.partial(_plan_kernel, n_tok=n_tok, nb_max=nb_max),
        out_shape=[jax.ShapeDtypeStruct((2, n_tok), jnp.int32),
                   jax.ShapeDtypeStruct((4, meta_w), jnp.int32),
                   jax.ShapeDtypeStruct((RUN_FIELDS, N_EXPERTS, LANES), jnp.int32)],
        scratch_shapes=[pltpu.VMEM((TT, TT), BF16)],
        compiler_params=pltpu.CompilerParams(vmem_limit_bytes=VMEM_LIMIT),
        name="plan",
    )(rt)


def _run_entry(runs_ref, field, expert, tile):
    return runs_ref[(field * N_EXPERTS + expert) * LANES + tile]


def _dispatch_kernel(runs_ref, meta_ref, hn_ref, lpos_ref, xs_hbm, xsbuf, zbuf, sem, zsem,
                     *, n_tiles, nb_max, n_pad):
    i = pl.program_id(0)
    slot = lax.rem(i, 2)

    @pl.when(i == 0)
    def _():
        zbuf[...] = jnp.zeros_like(zbuf)
        for e in range(N_EXPERTS):
            _copy_run(zbuf, 0, xs_hbm, _run_entry(runs_ref, 3, e, 0), _run_entry(runs_ref, 3, e, 1),
                      zsem)

        def unused_block(blk, carry):
            dst = xs_hbm.at[pl.ds(pl.multiple_of(blk * (BM * TILE_ROWS), BM * TILE_ROWS),
                                  BM * TILE_ROWS), :]
            pltpu.make_async_copy(zbuf, dst, zsem).start()
            return carry

        lax.fori_loop(meta_ref[nb_max], nb_max, unused_block, 0)

    @pl.when(i >= 2)
    def _():
        _wait_rows(xsbuf.at[slot], TILE_SLOTS, sem.at[slot])

    rank = lax.broadcasted_iota(jnp.int32, (TILE_SLOTS, TT), 0)
    onehot = jnp.where(rank == lpos_ref[0:1, :], 1.0, jnp.where(rank == lpos_ref[1:2, :], 1.0, 0.0))
    _store_token_tiles(xsbuf.at[slot], _dot(onehot.astype(BF16), hn_ref[...]), already_bf16=True)
    for e in range(N_EXPERTS):
        _copy_run(xsbuf.at[slot], _run_entry(runs_ref, 2, e, i), xs_hbm, _run_entry(runs_ref, 1, e, i),
                  _run_entry(runs_ref, 0, e, i), sem.at[slot])

    @pl.when(i == n_tiles - 1)
    def _():
        _wait_rows(xsbuf.at[slot], TILE_SLOTS, sem.at[slot])
        if n_tiles > 1:
            _wait_rows(xsbuf.at[1 - slot], TILE_SLOTS, sem.at[1 - slot])
        _wait_rows(xs_hbm, n_pad, zsem)


def _dispatch(runs, meta, hn, lpos, n_tok, nb_max):
    n_tiles = n_tok // TT
    cap = nb_max * BM
    grid_spec = pltpu.PrefetchScalarGridSpec(
        num_scalar_prefetch=2,
        grid=(n_tiles,),
        in_specs=[pl.BlockSpec((TT, D_MODEL), lambda i, runs_ref, meta_ref: (i, 0)),
                  pl.BlockSpec((2, TT), lambda i, runs_ref, meta_ref: (0, i))],
        out_specs=pl.BlockSpec(memory_space=pl.ANY),
        scratch_shapes=[pltpu.VMEM((2, TILE_SLOTS * TILE_ROWS, LANES), U32),
                        pltpu.VMEM((BM * TILE_ROWS, LANES), U32),
                        pltpu.SemaphoreType.DMA((2,)), pltpu.SemaphoreType.DMA],
    )
    return pl.pallas_call(
        functools.partial(_dispatch_kernel, n_tiles=n_tiles, nb_max=nb_max, n_pad=cap - 2 * n_tok),
        grid_spec=grid_spec,
        out_shape=jax.ShapeDtypeStruct((cap * TILE_ROWS, LANES), U32),
        compiler_params=pltpu.CompilerParams(dimension_semantics=("arbitrary",),
                                             vmem_limit_bytes=VMEM_LIMIT),
        name="dispatch",
    )(runs.reshape(-1), meta.reshape(-1), hn, lpos)


def _experts_kernel(meta_ref, xs_ref, wg_hbm, wu_hbm, wd_hbm, y_ref, wf32, wbf, wsem, *, nb_max, meta_w):
    b = pl.program_id(0)
    n_used = meta_ref[nb_max]

    def start_weights(expert, buf):
        for j, w_hbm in enumerate((wg_hbm, wu_hbm, wd_hbm)):
            pltpu.make_async_copy(w_hbm.at[expert], wf32.at[buf, j], wsem.at[buf]).start(priority=1)

    @pl.when(b < n_used)
    def _():
        owner = meta_ref[b]

        @pl.when(b == 0)
        def _():
            start_weights(owner, 0)

        @pl.when(jnp.logical_or(b == 0, owner != meta_ref[jnp.maximum(b - 1, 0)]))
        def _():
            buf = jnp.bitwise_and(meta_ref[meta_w + b], 1)
            pltpu.make_async_copy(wf32.at[buf], wf32.at[buf], wsem.at[buf]).wait()
            next_owner = meta_ref[2 * meta_w + b]

            @pl.when(next_owner >= 0)
            def _():
                start_weights(next_owner, 1 - buf)

            for j in range(3):
                wbf[j] = wf32[buf, j].astype(BF16)

        def geglu(n_rows):
            rows = pl.ds(0, n_rows * TILE_ROWS)
            x = _load_token_tiles(xs_ref.at[rows, :], n_rows).astype(BF16)
            mid = (jax.nn.gelu(_dot(x, wbf[0])) * _dot(x, wbf[1])).astype(BF16)
            _store_token_tiles(y_ref.at[rows, :], _dot(mid, wbf[2]))

        n_groups = lax.shift_right_logical(meta_ref[3 * meta_w + b] + (SKIP_ROWS - 1),
                                           SKIP_ROWS.bit_length() - 1)
        for g in range(1, BM // SKIP_ROWS + 1):
            @pl.when(n_groups == g)
            def _(g=g):
                geglu(g * SKIP_ROWS)
                if g * SKIP_ROWS < BM:
                    rest = (BM - g * SKIP_ROWS) * TILE_ROWS
                    y_ref[pl.ds(g * SKIP_ROWS * TILE_ROWS, rest), :] = jnp.zeros((rest, LANES), U32)

    @pl.when(b >= n_used)
    def _():
        y_ref[...] = jnp.zeros_like(y_ref)


def _experts(meta, xs, w_gate, w_up, w_down, nb_max):
    whole = pl.BlockSpec(memory_space=pl.ANY)
    rows = pl.BlockSpec((BM * TILE_ROWS, LANES), lambda b, meta_ref: (b, 0))
    grid_spec = pltpu.PrefetchScalarGridSpec(
        num_scalar_prefetch=1,
        grid=(nb_max,),
        in_specs=[rows, whole, whole, whole],
        out_specs=rows,
        scratch_shapes=[pltpu.VMEM((2, 3, D_MODEL, D_MODEL), F32),
                        pltpu.VMEM((3, D_MODEL, D_MODEL), BF16),
                        pltpu.SemaphoreType.DMA((2,))],
    )
    return pl.pallas_call(
        functools.partial(_experts_kernel, nb_max=nb_max, meta_w=meta.shape[1]),
        grid_spec=grid_spec,
        out_shape=jax.ShapeDtypeStruct(xs.shape, U32),
        compiler_params=pltpu.CompilerParams(dimension_semantics=("arbitrary",),
                                             vmem_limit_bytes=VMEM_LIMIT),
        name="experts",
    )(meta.reshape(-1), xs, w_gate, w_up, w_down)


def _combine_kernel(runs_ref, h_ref, lpos_ref, rt_ref, gn_ref, y_hbm, o_ref, ysbuf, sem, *, n_tiles):
    i = pl.program_id(0)
    slot = lax.rem(i, 2)

    def fetch(tile, buf):
        for e in range(N_EXPERTS):
            _copy_run(y_hbm, _run_entry(runs_ref, 1, e, tile), ysbuf.at[buf],
                      _run_entry(runs_ref, 2, e, tile), _run_entry(runs_ref, 0, e, tile), sem.at[buf])

    @pl.when(i == 0)
    def _():
        fetch(0, 0)

    @pl.when(i + 1 < n_tiles)
    def _():
        fetch(i + 1, 1 - slot)

    _wait_rows(ysbuf.at[slot], TILE_SLOTS, sem.at[slot])

    rank = lax.broadcasted_iota(jnp.int32, (TILE_SLOTS, TT), 0)
    first = rank == lpos_ref[0:1, :]
    second = rank == lpos_ref[1:2, :]
    slot_w = jnp.sum(jnp.where(first, rt_ref[2:3, :], 0.0) + jnp.where(second, rt_ref[3:4, :], 0.0),
                     axis=1, keepdims=True)
    onehot = jnp.where(first, 1.0, jnp.where(second, 1.0, 0.0)).astype(BF16)
    ys = (_load_token_tiles(ysbuf.at[slot], TILE_SLOTS) * slot_w).astype(BF16)
    moe = lax.dot_general(onehot, ys, (((0,), (0,)), ((), ())), preferred_element_type=F32)
    o_ref[...] = _rms(h_ref[...] + moe, gn_ref[...])


def _combine(runs, h, lpos, rt, norm_final, y, n_tok):
    n_tiles = n_tok // TT
    grid_spec = pltpu.PrefetchScalarGridSpec(
        num_scalar_prefetch=1,
        grid=(n_tiles,),
        in_specs=[pl.BlockSpec((TT, D_MODEL), lambda i, runs_ref: (i, 0)),
                  pl.BlockSpec((2, TT), lambda i, runs_ref: (0, i)),
                  pl.BlockSpec((SUBLANES, TT), lambda i, runs_ref: (0, i)),
                  pl.BlockSpec((1, D_MODEL), lambda i, runs_ref: (0, 0)),
                  pl.BlockSpec(memory_space=pl.ANY)],
        out_specs=pl.BlockSpec((TT, D_MODEL), lambda i, runs_ref: (i, 0)),
        scratch_shapes=[pltpu.VMEM((2, TILE_SLOTS * TILE_ROWS, LANES), U32),
                        pltpu.SemaphoreType.DMA((2,))],
    )
    return pl.pallas_call(
        functools.partial(_combine_kernel, n_tiles=n_tiles),
        grid_spec=grid_spec,
        out_shape=jax.ShapeDtypeStruct((n_tok, D_MODEL), F32),
        compiler_params=pltpu.CompilerParams(dimension_semantics=("arbitrary",),
                                             vmem_limit_bytes=VMEM_LIMIT),
        name="combine",
    )(runs.reshape(-1), h, lpos, rt, norm_final.reshape(1, D_MODEL).astype(F32), y)


def kernel(x, norm_mix, w_in, conv_w, conv_b, w_rg_a, b_rg_a, w_rg_x, b_rg_x, rg_lambda, rg_norm, w_alpha_up, b_alpha, gla_norm, w_out, norm_ffn, w_router_group, b_router_group, w_router_expert, b_router_expert, w_exp_gate, w_exp_up, w_exp_down, norm_final):
    bsz, seq, d = x.shape
    assert d == D_MODEL and norm_mix.shape[0] == 1, "single-layer model of width D_MODEL expected"
    n_tok = bsz * seq
    nb_max = (2 * n_tok + N_EXPERTS * (BM - 1)) // BM

    h, hn, rt = _mixer(x, norm_mix[0], w_in[0], w_alpha_up[0], b_alpha[0], conv_w[0], conv_b[0],
                       w_rg_a[0], b_rg_a[0], w_rg_x[0], b_rg_x[0], rg_lambda[0], rg_norm[0], gla_norm[0],
                       w_out[0], norm_ffn[0], w_router_group[0], b_router_group[0],
                       w_router_expert[0], b_router_expert[0])
    lpos, meta, runs = _plan(rt, n_tok, nb_max)
    xs = _dispatch(runs, meta, hn, lpos, n_tok, nb_max)
    y = _experts(meta, xs, w_exp_gate[0], w_exp_up[0], w_exp_down[0], nb_max)
    out = _combine(runs, h, lpos, rt, norm_final, y, n_tok)
    return out.reshape(bsz, seq, d)
```

```python
import functools

import jax
import jax.numpy as jnp
from jax import lax
from jax.experimental import pallas as pl
from jax.experimental.pallas import tpu as pltpu

F32 = jnp.float32
BF16 = jnp.bfloat16

D_MODEL = 1024
RG_WIDTH = 512
RG_BLOCKS = 8
RG_BLOCK = 64
CONV_WIDTH = 4
C_RG = 8.0
GLA_HEADS = 4
GLA_VAL = 512
GLA_KEY = 256
GLA_DK = 64
GLA_DV = 128
GATE_RANK = 16
GATE_NORM = 16.0
CHUNK = 64
N_GROUPS = 4
EXPERTS_PER_GROUP = 8
N_EXPERTS = 32
EPS = 1e-6

LANES = 128
SUBLANES = 8
VMEM_LIMIT = 56 * 1024 * 1024

TS_MIX = 512
PREFIX_ROWS = 256
BM = 512
SKIP_ROWS = 128
TT = 512
TILE_SLOTS = 2 * TT
RUN_CHUNK_LOG2 = 6
RUN_FIELDS = 4


def _dot(a, b):
    return jnp.dot(a, b, preferred_element_type=F32)


def _softplus(z):
    return jnp.maximum(z, 0.0) + jnp.log1p(jnp.exp(-jnp.abs(z)))


def _rms(x, g):
    return x * lax.rsqrt(jnp.mean(x * x, axis=-1, keepdims=True) + EPS) * g


U32 = jnp.uint32
TILE_ROWS = D_MODEL // 2 // LANES


def _copy_run(src, src_row, dst, dst_row, n_rows, sem):
    def piece(off, rows):
        s = src.at[pl.ds(pl.multiple_of((src_row + off) * TILE_ROWS, TILE_ROWS), rows * TILE_ROWS), :]
        d = dst.at[pl.ds(pl.multiple_of((dst_row + off) * TILE_ROWS, TILE_ROWS), rows * TILE_ROWS), :]
        pltpu.make_async_copy(s, d, sem).start()

    chunk = 1 << RUN_CHUNK_LOG2
    n_chunks = lax.shift_right_logical(n_rows, RUN_CHUNK_LOG2)
    lax.fori_loop(0, n_chunks, lambda c, carry: (piece(c * chunk, chunk), carry)[1], 0)
    off = n_chunks * chunk
    for k in reversed(range(RUN_CHUNK_LOG2)):
        bit = jnp.bitwise_and(n_rows, 1 << k)

        @pl.when(bit != 0)
        def _():
            piece(off, 1 << k)

        off = off + bit


def _wait_rows(ref, n_rows, sem):
    view = ref.at[pl.ds(0, n_rows * TILE_ROWS), :]
    pltpu.make_async_copy(view, view, sem).wait()


def _store_token_tiles(ref, val, already_bf16=False):
    n = val.shape[0]
    if not already_bf16:
        val = val.astype(BF16).astype(F32)
    bits = lax.bitcast_convert_type(val, U32)
    packed = jnp.bitwise_or(bits[:, :D_MODEL // 2], jnp.right_shift(bits[:, D_MODEL // 2:], 16))
    for c in range(TILE_ROWS):
        ref[pl.ds(c, n, stride=TILE_ROWS), :] = packed[:, c * LANES:(c + 1) * LANES]


def _load_token_tiles(ref, n):
    words = [ref[pl.ds(c, n, stride=TILE_ROWS), :] for c in range(TILE_ROWS)]
    high = [lax.bitcast_convert_type(jnp.bitwise_and(w, jnp.uint32(0xFFFF0000)), F32) for w in words]
    low = [lax.bitcast_convert_type(jnp.left_shift(w, 16), F32) for w in words]
    return jnp.concatenate(high + low, axis=1)


def _inproj_part(bi, x_ref, g_ref, wxr, wyr, wq, wk, wv, wg, wal, wup, bal,
                 xr_o, yr_o, q_o, k_o, v_o, g_o, la_o):
    hn = _rms(x_ref[bi], g_ref[...]).astype(BF16)
    xr_o[bi] = _dot(hn, wxr[...])
    yr_o[bi] = _dot(hn, wyr[...])
    q_o[bi] = _dot(hn, wq[...])
    k_o[bi] = _dot(hn, wk[...])
    v_o[bi] = _dot(hn, wv[...])
    g_o[bi] = _dot(hn, wg[...])
    a3 = _dot(hn, wal[...])
    a3_hi = a3.astype(BF16).astype(F32)
    lane = lax.broadcasted_iota(jnp.int32, a3.shape, 1)
    use_low = jnp.logical_and(lane >= GATE_RANK, lane < 2 * GATE_RANK)
    z = _dot(jnp.where(use_low, a3 - a3_hi, a3_hi).astype(BF16), wup[...]) + bal[...]
    log_sig = jnp.minimum(z, 0.0) - jnp.log1p(jnp.exp(-jnp.abs(z)))
    la_o[bi] = log_sig * (1.0 / GATE_NORM)


PROJ_WIDTHS = (RG_WIDTH, RG_WIDTH, GLA_KEY, GLA_KEY, GLA_VAL, GLA_VAL, GLA_KEY)


def _inproj_weights(w_in, w_alpha_up):
    cuts = [0]
    for w in PROJ_WIDTHS[:6]:
        cuts.append(cuts[-1] + w)
    wb = w_in.astype(BF16)
    segs = [wb[:, cuts[i]:cuts[i + 1]] for i in range(6)]
    w_low = wb[:, cuts[6]:cuts[6] + GATE_RANK]
    wal = jnp.pad(jnp.concatenate([w_low, w_low, w_low], axis=1), ((0, 0), (0, LANES - 3 * GATE_RANK)))
    up = w_alpha_up.astype(F32)
    up_hi = up.astype(BF16)
    up_lo = (up - up_hi.astype(F32)).astype(BF16)
    wup = jnp.pad(jnp.concatenate([up_hi, up_hi, up_lo], axis=0), ((0, LANES - 3 * GATE_RANK), (0, 0)))
    return segs + [wal, wup]


def _rglru_part(bi, xr_ref, yr_ref, cw_ref, cb_ref, wg_ref, ba_ref, bx_ref, lam_ref, gn_ref,
                o_ref, tail_ref, hc_ref):
    ts = xr_ref.shape[1]
    n_groups = ts // SUBLANES
    grouped = (n_groups, SUBLANES, RG_WIDTH)
    sub = lax.broadcasted_iota(jnp.int32, grouped, 1)

    x = xr_ref[bi]
    x_grp = x.reshape(grouped)
    x_prev = jnp.concatenate([tail_ref[bi], x[:ts - SUBLANES, :]], axis=0).reshape(grouped)
    tail_ref[bi] = x[ts - SUBLANES:ts, :]
    cw = cw_ref[...]
    xc = cb_ref[...] + cw[CONV_WIDTH - 1:CONV_WIDTH, :] * x
    for s in range(1, CONV_WIDTH):
        mixed = jnp.where(sub >= SUBLANES - s, x_prev, x_grp)
        shifted = pltpu.roll(mixed, s, axis=1)
        xc = xc + cw[CONV_WIDTH - 1 - s:CONV_WIDTH - s, :] * shifted.reshape(ts, RG_WIDTH)

    xb = xc.astype(BF16)
    half = RG_WIDTH // 2
    parts = [_dot(xb[:, j * half:(j + 1) * half], wg_ref[j]) for j in range(2)]
    r = jax.nn.sigmoid(jnp.concatenate([p[:, :half] for p in parts], axis=1) + ba_ref[...])
    gate_i = jax.nn.sigmoid(jnp.concatenate([p[:, half:] for p in parts], axis=1) + bx_ref[...])
    log_a = (-C_RG) * r * _softplus(-lam_ref[...])
    a = jnp.exp(log_a)
    z = jnp.tanh(-log_a) * (1.0 + a * a)
    u = (z * lax.rsqrt(jnp.maximum(z, jnp.finfo(F32).tiny))) * (gate_i * xc)

    a = a.reshape(grouped)
    u = u.reshape(grouped)
    d = 1
    while d < SUBLANES:
        keep = sub >= d
        a_sh = jnp.where(keep, pltpu.roll(a, d, axis=1), 1.0)
        u_sh = jnp.where(keep, pltpu.roll(u, d, axis=1), 0.0)
        u = a * u_sh + u
        a = a * a_sh
        d *= 2
    carry = hc_ref[bi, 0:1, :]
    groups = []
    for j in range(n_groups):
        h_j = u[j] + a[j] * carry
        groups.append(h_j)
        carry = h_j[SUBLANES - 1:SUBLANES, :]
    h = jnp.concatenate(groups, axis=0)
    hc_ref[bi] = jnp.broadcast_to(carry, hc_ref.shape[1:])

    y = h * jax.nn.gelu(yr_ref[bi])
    o_ref[bi, :, 0:RG_WIDTH] = _rms(y, gn_ref[...]).astype(o_ref.dtype)


def _gate_weights(w_a, w_x):
    def block_diag(w):
        eye = jnp.eye(RG_BLOCKS, dtype=w.dtype)
        return jnp.einsum('hij,hg->higj', w, eye).reshape(RG_WIDTH, RG_WIDTH)

    half = RG_WIDTH // 2
    a, x = block_diag(w_a), block_diag(w_x)
    return jnp.stack([jnp.concatenate([m[j * half:(j + 1) * half, j * half:(j + 1) * half] for m in (a, x)],
                                      axis=1) for j in range(2)]).astype(BF16)


def _mixer_kernel(x_ref, mgn_ref, wxr, wyr, wq, wk, wv, wg, wal, wup, bal,
                  cw_ref, cb_ref, wgate_ref, ba_ref, bx_ref, lam_ref, rgn_ref, gn_ref,
                  wo_ref, fgn_ref, wr_ref, br_ref,
                  h_o, hn_o, rt_o,
                  xr_ref, yr_ref, q_ref, k_ref, v_ref, g_ref, la_ref, o_ref, tail_ref, hc_ref, st_ref):
    bsz, ts = q_ref.shape[0], q_ref.shape[1]
    n_chunks = ts // CHUNK

    @pl.when(pl.program_id(0) == 0)
    def _():
        tail_ref[...] = jnp.zeros_like(tail_ref)
        hc_ref[...] = jnp.zeros_like(hc_ref)
        st_ref[...] = jnp.zeros_like(st_ref)

    ri = lax.broadcasted_iota(jnp.int32, (PREFIX_ROWS, PREFIX_ROWS), 0)
    ci = lax.broadcasted_iota(jnp.int32, (PREFIX_ROWS, PREFIX_ROWS), 1)
    chunk_bits = CHUNK.bit_length() - 1
    same_chunk = lax.shift_right_logical(ri, chunk_bits) == lax.shift_right_logical(ci, chunk_bits)
    prefix = jnp.where(jnp.logical_and(same_chunk, ri >= ci), 1.0, 0.0).astype(BF16)
    causal = (ri >= ci)[:CHUNK, :CHUNK]
    scale = GLA_DK ** -0.5
    gn = gn_ref[...]
    nt_dims = (((1,), (1,)), ((), ()))
    tn_dims = (((0,), (0,)), ((), ()))

    for bi in range(bsz):
        _inproj_part(bi, x_ref, mgn_ref, wxr, wyr, wq, wk, wv, wg, wal, wup, bal,
                     xr_ref, yr_ref, q_ref, k_ref, v_ref, g_ref, la_ref)

    for bi in range(bsz):
        _rglru_part(bi, xr_ref, yr_ref, cw_ref, cb_ref, wgate_ref, ba_ref, bx_ref, lam_ref, rgn_ref,
                    o_ref, tail_ref, hc_ref)

        la = la_ref[bi]
        la_hi = la.astype(BF16)
        la_lo = (la - la_hi.astype(F32)).astype(BF16)
        b = jnp.concatenate(
            [_dot(prefix, la_hi[r:r + PREFIX_ROWS]) + _dot(prefix, la_lo[r:r + PREFIX_ROWS])
             for r in range(0, ts, PREFIX_ROWS)], axis=0)
        b_tot = jnp.concatenate(
            [jnp.broadcast_to(b[c * CHUNK + CHUNK - 1:(c + 1) * CHUNK, :], (CHUNK, GLA_KEY))
             for c in range(n_chunks)], axis=0)
        kk = k_ref[bi]
        q_s = (q_ref[bi] * scale) * jnp.exp(b)
        k_s = kk * jnp.exp(-b)
        k_end = kk * jnp.exp(b_tot - b)
        decay = jnp.exp(b_tot)

        o_intra, kv, q_heads = [], [], []
        for c in range(n_chunks):
            rows = slice(c * CHUNK, (c + 1) * CHUNK)
            o_c, kv_c, q_c = [], [], []
            for h in range(GLA_HEADS):
                ks = slice(h * GLA_DK, (h + 1) * GLA_DK)
                vs = slice(h * GLA_DV, (h + 1) * GLA_DV)
                qh = q_s[rows, ks].astype(BF16)
                vb = v_ref[bi, rows, vs].astype(BF16)
                att = lax.dot_general(qh, k_s[rows, ks].astype(BF16), nt_dims, preferred_element_type=F32)
                att = jnp.where(causal, att, 0.0).astype(BF16)
                o_c.append(_dot(att, vb))
                kv_c.append(lax.dot_general(vb, k_end[rows, ks].astype(BF16), tn_dims,
                                            preferred_element_type=F32))
                q_c.append(qh)
            o_intra.append(o_c)
            q_heads.append(q_c)
            kv.append(jnp.concatenate(kv_c, axis=1))

        st = st_ref[bi]
        for c in range(n_chunks):
            rows = slice(c * CHUNK, (c + 1) * CHUNK)
            for h in range(GLA_HEADS):
                ks = slice(h * GLA_DK, (h + 1) * GLA_DK)
                vs = slice(h * GLA_DV, (h + 1) * GLA_DV)
                o = o_intra[c][h] + lax.dot_general(q_heads[c][h], st[:, ks].astype(BF16), nt_dims,
                                                    preferred_element_type=F32)
                gated = _rms(o, gn) * jax.nn.silu(g_ref[bi, rows, vs])
                o_ref[bi, rows, RG_WIDTH + h * GLA_DV:RG_WIDTH + (h + 1) * GLA_DV] = gated.astype(o_ref.dtype)
            st = decay[c * CHUNK:c * CHUNK + 1, :] * st + kv[c]
        st_ref[bi] = st

        _outproj_part(bi, o_ref, x_ref, wo_ref, fgn_ref, wr_ref, br_ref, h_o, hn_o, rt_o)


def _mixer(x, norm_mix, w_in, w_alpha_up, b_alpha, conv_w, conv_b, w_a, b_a, w_x, b_x, lam, rg_norm,
           gla_norm, w_out, norm_ffn, w_rg, b_rg, w_re, b_re):
    bsz, seq, _ = x.shape
    nt = seq // TS_MIX
    blk = lambda w: pl.BlockSpec((bsz, TS_MIX, w), lambda i: (0, i, 0))
    fixed = lambda r, c: pl.BlockSpec((r, c), lambda i: (0, 0))
    vec = lambda a: a.reshape(1, -1).astype(F32)
    proj_w = _inproj_weights(w_in, w_alpha_up)
    wr = jnp.pad(jnp.concatenate([w_rg, w_re], axis=1).astype(F32),
                 ((0, 0), (0, LANES - N_GROUPS - N_EXPERTS)))
    wr_hi = wr.astype(BF16)
    wr = jnp.concatenate([wr_hi, (wr - wr_hi.astype(F32)).astype(BF16)], axis=1)
    br = jnp.pad(jnp.concatenate([b_rg, b_re]).astype(F32), (0, LANES - N_GROUPS - N_EXPERTS))
    h, hn, rt = pl.pallas_call(
        _mixer_kernel,
        grid=(nt,),
        in_specs=[blk(D_MODEL), fixed(1, D_MODEL)]
                 + [fixed(*w.shape) for w in proj_w] + [fixed(1, GLA_KEY),
                  fixed(CONV_WIDTH, RG_WIDTH), fixed(1, RG_WIDTH),
                  pl.BlockSpec((2, RG_WIDTH // 2, RG_WIDTH), lambda i: (0, 0, 0)),
                  fixed(1, RG_WIDTH), fixed(1, RG_WIDTH), fixed(1, RG_WIDTH),
                  fixed(1, RG_WIDTH), fixed(1, GLA_DV),
                  fixed(D_MODEL, D_MODEL), fixed(1, D_MODEL), fixed(D_MODEL, 2 * LANES), fixed(1, LANES)],
        out_specs=[blk(D_MODEL), blk(D_MODEL),
                   pl.BlockSpec((bsz, SUBLANES, TS_MIX), lambda i: (0, 0, i))],
        out_shape=[jax.ShapeDtypeStruct((bsz, seq, D_MODEL), F32),
                   jax.ShapeDtypeStruct((bsz, seq, D_MODEL), BF16),
                   jax.ShapeDtypeStruct((bsz, SUBLANES, seq), F32)],
        scratch_shapes=[pltpu.VMEM((bsz, TS_MIX, w), F32) for w in PROJ_WIDTHS]
                      + [pltpu.VMEM((bsz, TS_MIX, D_MODEL), BF16),
                        pltpu.VMEM((bsz, SUBLANES, RG_WIDTH), F32),
                        pltpu.VMEM((bsz, SUBLANES, RG_WIDTH), F32),
                        pltpu.VMEM((bsz, GLA_DV, GLA_KEY), F32)],
        compiler_params=pltpu.CompilerParams(dimension_semantics=("arbitrary",),
                                             vmem_limit_bytes=VMEM_LIMIT),
        name="mixer",
    )(x, vec(norm_mix), *proj_w, vec(b_alpha),
      conv_w.astype(F32), vec(conv_b), _gate_weights(w_a, w_x), vec(b_a), vec(b_x),
      vec(lam), vec(rg_norm), vec(gla_norm),
      w_out.astype(BF16), vec(norm_ffn), wr, br.reshape(1, LANES))
    n_tok = bsz * seq
    rt = rt.transpose(1, 0, 2).reshape(SUBLANES, n_tok)
    return h.reshape(n_tok, D_MODEL), hn.reshape(n_tok, D_MODEL), rt


def _outproj_part(bi, y_ref, x_ref, wo_ref, gn_ref, wr_ref, br_ref, h_o, hn_o, rt_o):
    tm = x_ref.shape[1]
    h = x_ref[bi] + _dot(y_ref[bi], wo_ref[...])
    h_o[bi] = h
    hn = _rms(h, gn_ref[...])
    hn_hi = hn.astype(BF16)
    hn_o[bi] = hn_hi
    hn_lo = (hn - hn_hi.astype(F32)).astype(BF16)
    hi_parts = _dot(hn_hi, wr_ref[...])
    logits = hi_parts[:, :LANES] + hi_parts[:, LANES:] + _dot(hn_lo, wr_ref[:, :LANES]) + br_ref[...]

    lane = lax.broadcasted_iota(jnp.int32, (tm, LANES), 1).astype(F32)
    neg = -jnp.inf
    glog = jnp.where(lane < N_GROUPS, logits, neg)
    gmax = jnp.max(glog, axis=-1, keepdims=True)
    gidx = jnp.min(jnp.where(glog == gmax, lane, float(LANES)), axis=-1, keepdims=True)
    g_w = 1.0 / jnp.sum(jnp.exp(glog - gmax), axis=-1, keepdims=True)
    lo = N_GROUPS + gidx * EXPERTS_PER_GROUP
    in_group = jnp.logical_and(lane >= lo, lane < lo + EXPERTS_PER_GROUP)
    le = jnp.where(in_group, logits, neg)
    m1 = jnp.max(le, axis=-1, keepdims=True)
    i1 = jnp.min(jnp.where(le == m1, lane, float(LANES)), axis=-1, keepdims=True)
    le2 = jnp.where(lane == i1, neg, le)
    m2 = jnp.max(le2, axis=-1, keepdims=True)
    i2 = jnp.min(jnp.where(le2 == m2, lane, float(LANES)), axis=-1, keepdims=True)
    t2 = jnp.exp(m2 - m1)
    w1 = g_w / (1.0 + t2)
    w2 = g_w * t2 / (1.0 + t2)
    info = jnp.where(lane == 0.0, i1 - N_GROUPS,
                     jnp.where(lane == 1.0, i2 - N_GROUPS,
                               jnp.where(lane == 2.0, w1, jnp.where(lane == 3.0, w2, 0.0))))
    rt_o[bi] = info.T[0:SUBLANES, :]


def _plan_kernel(rt_ref, lpos_o, meta_o, runs_o, tri_ref, *, n_tok, nb_max):
    n_tiles = n_tok // TT
    esub = lax.broadcasted_iota(jnp.int32, (N_EXPERTS, TT), 0).astype(F32)

    def onehots(i):
        off = pl.multiple_of(i * TT, TT)
        e1 = rt_ref[0:1, pl.ds(off, TT)]
        e2 = rt_ref[1:2, pl.ds(off, TT)]
        m1 = jnp.where(esub == e1, 1.0, 0.0)
        m2 = jnp.where(esub == e2, 1.0, 0.0)
        return off, m1, m2

    def count_body(i, cnt):
        _, m1, m2 = onehots(i)
        return cnt + jnp.sum(m1 + m2, axis=1, keepdims=True)

    counts = lax.fori_loop(0, n_tiles, count_body, jnp.zeros((N_EXPERTS, 1), F32))
    nblk = jnp.floor((counts + (BM - 1)) * (1.0 / BM))
    ei = lax.broadcasted_iota(jnp.int32, (N_EXPERTS, N_EXPERTS), 0)
    ej = lax.broadcasted_iota(jnp.int32, (N_EXPERTS, N_EXPERTS), 1)
    nblk_row = jnp.sum(jnp.where(ei == ej, nblk, 0.0), axis=0, keepdims=True)
    bstart = jnp.sum(jnp.where(ej < ei, nblk_row, 0.0), axis=1, keepdims=True)
    bend = bstart + nblk
    n_used = jnp.sum(nblk, axis=0, keepdims=True)

    meta_w = meta_o.shape[1]
    blane = lax.broadcasted_iota(jnp.int32, (N_EXPERTS, meta_w), 1).astype(F32)
    owner = jnp.sum(jnp.where(bend <= blane, 1.0, 0.0), axis=0, keepdims=True)
    owner = jnp.minimum(owner, N_EXPERTS - 1.0)
    lane1 = lax.broadcasted_iota(jnp.int32, (1, meta_w), 1)
    meta_o[0:1, :] = jnp.where(lane1 == nb_max, n_used, owner).astype(jnp.int32)
    nonempty = jnp.where(nblk > 0.0, 1.0, 0.0)
    meta_o[1:2, :] = jnp.sum(jnp.where(bend <= blane, nonempty, 0.0), axis=0,
                             keepdims=True).astype(jnp.int32)
    owned = jnp.logical_and(bstart <= blane, blane < bend)
    next_start = jnp.sum(jnp.where(owned, bend, 0.0), axis=0, keepdims=True)
    next_owner = jnp.sum(jnp.where(bend <= next_start, 1.0, 0.0), axis=0, keepdims=True)
    meta_o[2:3, :] = jnp.where(next_start < n_used, next_owner, -1.0).astype(jnp.int32)
    valid_end = jnp.sum(jnp.where(owned, bstart * float(BM) + counts, 0.0), axis=0, keepdims=True)
    meta_o[3:4, :] = jnp.clip(valid_end - lane1.astype(F32) * float(BM), 0.0, float(BM)).astype(jnp.int32)

    ti = lax.broadcasted_iota(jnp.int32, (TT, TT), 0)
    tj = lax.broadcasted_iota(jnp.int32, (TT, TT), 1)
    tri_ref[...] = jnp.where(ti <= tj, 1.0, 0.0).astype(BF16)
    tlane = lax.broadcasted_iota(jnp.int32, (N_EXPERTS, LANES), 1)

    def tile_body(i, carry):
        first_slot, t_cnt, t_slot, t_rank = carry
        off, m1, m2 = onehots(i)
        m = m1 + m2
        incl = _dot(m.astype(BF16), tri_ref[...])
        cnt = incl[:, TT - 1:TT]
        cnt_row = jnp.sum(jnp.where(ei == ej, cnt, 0.0), axis=0, keepdims=True)
        first_rank = jnp.sum(jnp.where(ej < ei, cnt_row, 0.0), axis=1, keepdims=True)
        rank = first_rank + incl - m
        lpos_o[0:1, pl.ds(off, TT)] = jnp.sum(m1 * rank, axis=0, keepdims=True).astype(jnp.int32)
        lpos_o[1:2, pl.ds(off, TT)] = jnp.sum(m2 * rank, axis=0, keepdims=True).astype(jnp.int32)
        here = tlane == i
        return (first_slot + cnt, jnp.where(here, cnt, t_cnt), jnp.where(here, first_slot, t_slot),
                jnp.where(here, first_rank, t_rank))

    zeros = jnp.zeros((N_EXPERTS, LANES), F32)
    _, t_cnt, t_slot, t_rank = lax.fori_loop(0, n_tiles, tile_body,
                                             (bstart * float(BM), zeros, zeros, zeros))
    runs_o[0] = t_cnt.astype(jnp.int32)
    runs_o[1] = t_slot.astype(jnp.int32)
    runs_o[2] = t_rank.astype(jnp.int32)
    pad = jnp.where(tlane == 0, bstart * float(BM) + counts,
                    jnp.where(tlane == 1, nblk * float(BM) - counts, 0.0))
    runs_o[3] = pad.astype(jnp.int32)


def _plan(rt, n_tok, nb_max):
    assert n_tok // TT <= LANES, "run tables hold one token tile per lane"
    meta_w = ((nb_max + 1 + LANES - 1) // LANES) * LANES
    return pl.pallas_call(
        functools.partial(_plan_kernel, n_tok=n_tok, nb_max=nb_max),
        out_shape=[jax.ShapeDtypeStruct((2, n_tok), jnp.int32),
                   jax.ShapeDtypeStruct((4, meta_w), jnp.int32),
                   jax.ShapeDtypeStruct((RUN_FIELDS, N_EXPERTS, LANES), jnp.int32)],
        scratch_shapes=[pltpu.VMEM((TT, TT), BF16)],
        compiler_params=pltpu.CompilerParams(vmem_limit_bytes=VMEM_LIMIT),
        name="plan",
    )(rt)


def _run_entry(runs_ref, field, expert, tile):
    return runs_ref[(field * N_EXPERTS + expert) * LANES + tile]


def _dispatch_kernel(runs_ref, meta_ref, hn_ref, lpos_ref, xs_hbm, xsbuf, zbuf, sem, zsem,
                     *, n_tiles, nb_max, n_pad):
    i = pl.program_id(0)
    slot = lax.rem(i, 2)

    @pl.when(i == 0)
    def _():
        zbuf[...] = jnp.zeros_like(zbuf)
        for e in range(N_EXPERTS):
            _copy_run(zbuf, 0, xs_hbm, _run_entry(runs_ref, 3, e, 0), _run_entry(runs_ref, 3, e, 1),
                      zsem)

        def unused_block(blk, carry):
            dst = xs_hbm.at[pl.ds(pl.multiple_of(blk * (BM * TILE_ROWS), BM * TILE_ROWS),
                                  BM * TILE_ROWS), :]
            pltpu.make_async_copy(zbuf, dst, zsem).start()
            return carry

        lax.fori_loop(meta_ref[nb_max], nb_max, unused_block, 0)

    @pl.when(i >= 2)
    def _():
        _wait_rows(xsbuf.at[slot], TILE_SLOTS, sem.at[slot])

    rank = lax.broadcasted_iota(jnp.int32, (TILE_SLOTS, TT), 0)
    onehot = jnp.where(rank == lpos_ref[0:1, :], 1.0, jnp.where(rank == lpos_ref[1:2, :], 1.0, 0.0))
    _store_token_tiles(xsbuf.at[slot], _dot(onehot.astype(BF16), hn_ref[...]), already_bf16=True)
    for e in range(N_EXPERTS):
        _copy_run(xsbuf.at[slot], _run_entry(runs_ref, 2, e, i), xs_hbm, _run_entry(runs_ref, 1, e, i),
                  _run_entry(runs_ref, 0, e, i), sem.at[slot])

    @pl.when(i == n_tiles - 1)
    def _():
        _wait_rows(xsbuf.at[slot], TILE_SLOTS, sem.at[slot])
        if n_tiles > 1:
            _wait_rows(xsbuf.at[1 - slot], TILE_SLOTS, sem.at[1 - slot])
        _wait_rows(xs_hbm, n_pad, zsem)


def _dispatch(runs, meta, hn, lpos, n_tok, nb_max):
    n_tiles = n_tok // TT
    cap = nb_max * BM
    grid_spec = pltpu.PrefetchScalarGridSpec(
        num_scalar_prefetch=2,
        grid=(n_tiles,),
        in_specs=[pl.BlockSpec((TT, D_MODEL), lambda i, runs_ref, meta_ref: (i, 0)),
                  pl.BlockSpec((2, TT), lambda i, runs_ref, meta_ref: (0, i))],
        out_specs=pl.BlockSpec(memory_space=pl.ANY),
        scratch_shapes=[pltpu.VMEM((2, TILE_SLOTS * TILE_ROWS, LANES), U32),
                        pltpu.VMEM((BM * TILE_ROWS, LANES), U32),
                        pltpu.SemaphoreType.DMA((2,)), pltpu.SemaphoreType.DMA],
    )
    return pl.pallas_call(
        functools.partial(_dispatch_kernel, n_tiles=n_tiles, nb_max=nb_max, n_pad=cap - 2 * n_tok),
        grid_spec=grid_spec,
        out_shape=jax.ShapeDtypeStruct((cap * TILE_ROWS, LANES), U32),
        compiler_params=pltpu.CompilerParams(dimension_semantics=("arbitrary",),
                                             vmem_limit_bytes=VMEM_LIMIT),
        name="dispatch",
    )(runs.reshape(-1), meta.reshape(-1), hn, lpos)


def _experts_kernel(meta_ref, xs_ref, wg_hbm, wu_hbm, wd_hbm, y_ref, wf32, wbf, wsem, *, nb_max, meta_w):
    b = pl.program_id(0)
    n_used = meta_ref[nb_max]

    def start_weights(expert, buf):
        for j, w_hbm in enumerate((wg_hbm, wu_hbm, wd_hbm)):
            pltpu.make_async_copy(w_hbm.at[expert], wf32.at[buf, j], wsem.at[buf]).start(priority=1)

    @pl.when(b < n_used)
    def _():
        owner = meta_ref[b]

        @pl.when(b == 0)
        def _():
            start_weights(owner, 0)

        @pl.when(jnp.logical_or(b == 0, owner != meta_ref[jnp.maximum(b - 1, 0)]))
        def _():
            buf = jnp.bitwise_and(meta_ref[meta_w + b], 1)
            pltpu.make_async_copy(wf32.at[buf], wf32.at[buf], wsem.at[buf]).wait()
            next_owner = meta_ref[2 * meta_w + b]

            @pl.when(next_owner >= 0)
            def _():
                start_weights(next_owner, 1 - buf)

            for j in range(3):
                wbf[j] = wf32[buf, j].astype(BF16)

        def geglu(n_rows):
            rows = pl.ds(0, n_rows * TILE_ROWS)
            x = _load_token_tiles(xs_ref.at[rows, :], n_rows).astype(BF16)
            mid = (jax.nn.gelu(_dot(x, wbf[0])) * _dot(x, wbf[1])).astype(BF16)
            _store_token_tiles(y_ref.at[rows, :], _dot(mid, wbf[2]))

        n_groups = lax.shift_right_logical(meta_ref[3 * meta_w + b] + (SKIP_ROWS - 1),
                                           SKIP_ROWS.bit_length() - 1)
        for g in range(1, BM // SKIP_ROWS + 1):
            @pl.when(n_groups == g)
            def _(g=g):
                geglu(g * SKIP_ROWS)
                if g * SKIP_ROWS < BM:
                    rest = (BM - g * SKIP_ROWS) * TILE_ROWS
                    y_ref[pl.ds(g * SKIP_ROWS * TILE_ROWS, rest), :] = jnp.zeros((rest, LANES), U32)

    @pl.when(b >= n_used)
    def _():
        y_ref[...] = jnp.zeros_like(y_ref)


def _experts(meta, xs, w_gate, w_up, w_down, nb_max):
    whole = pl.BlockSpec(memory_space=pl.ANY)
    rows = pl.BlockSpec((BM * TILE_ROWS, LANES), lambda b, meta_ref: (b, 0))
    grid_spec = pltpu.PrefetchScalarGridSpec(
        num_scalar_prefetch=1,
        grid=(nb_max,),
        in_specs=[rows, whole, whole, whole],
        out_specs=rows,
        scratch_shapes=[pltpu.VMEM((2, 3, D_MODEL, D_MODEL), F32),
                        pltpu.VMEM((3, D_MODEL, D_MODEL), BF16),
                        pltpu.SemaphoreType.DMA((2,))],
    )
    return pl.pallas_call(
        functools.partial(_experts_kernel, nb_max=nb_max, meta_w=meta.shape[1]),
        grid_spec=grid_spec,
        out_shape=jax.ShapeDtypeStruct(xs.shape, U32),
        compiler_params=pltpu.CompilerParams(dimension_semantics=("arbitrary",),
                                             vmem_limit_bytes=VMEM_LIMIT),
        name="experts",
    )(meta.reshape(-1), xs, w_gate, w_up, w_down)


def _combine_kernel(runs_ref, h_ref, lpos_ref, rt_ref, gn_ref, y_hbm, o_ref, ysbuf, sem, *, n_tiles):
    i = pl.program_id(0)
    slot = lax.rem(i, 2)

    def fetch(tile, buf):
        for e in range(N_EXPERTS):
            _copy_run(y_hbm, _run_entry(runs_ref, 1, e, tile), ysbuf.at[buf],
                      _run_entry(runs_ref, 2, e, tile), _run_entry(runs_ref, 0, e, tile), sem.at[buf])

    @pl.when(i == 0)
    def _():
        fetch(0, 0)

    @pl.when(i + 1 < n_tiles)
    def _():
        fetch(i + 1, 1 - slot)

    _wait_rows(ysbuf.at[slot], TILE_SLOTS, sem.at[slot])

    rank = lax.broadcasted_iota(jnp.int32, (TILE_SLOTS, TT), 0)
    first = rank == lpos_ref[0:1, :]
    second = rank == lpos_ref[1:2, :]
    slot_w = jnp.sum(jnp.where(first, rt_ref[2:3, :], 0.0) + jnp.where(second, rt_ref[3:4, :], 0.0),
                     axis=1, keepdims=True)
    onehot = jnp.where(first, 1.0, jnp.where(second, 1.0, 0.0)).astype(BF16)
    ys = (_load_token_tiles(ysbuf.at[slot], TILE_SLOTS) * slot_w).astype(BF16)
    moe = lax.dot_general(onehot, ys, (((0,), (0,)), ((), ())), preferred_element_type=F32)
    o_ref[...] = _rms(h_ref[...] + moe, gn_ref[...])


def _combine(runs, h, lpos, rt, norm_final, y, n_tok):
    n_tiles = n_tok // TT
    grid_spec = pltpu.PrefetchScalarGridSpec(
        num_scalar_prefetch=1,
        grid=(n_tiles,),
        in_specs=[pl.BlockSpec((TT, D_MODEL), lambda i, runs_ref: (i, 0)),
                  pl.BlockSpec((2, TT), lambda i, runs_ref: (0, i)),
                  pl.BlockSpec((SUBLANES, TT), lambda i, runs_ref: (0, i)),
                  pl.BlockSpec((1, D_MODEL), lambda i, runs_ref: (0, 0)),
                  pl.BlockSpec(memory_space=pl.ANY)],
        out_specs=pl.BlockSpec((TT, D_MODEL), lambda i, runs_ref: (i, 0)),
        scratch_shapes=[pltpu.VMEM((2, TILE_SLOTS * TILE_ROWS, LANES), U32),
                        pltpu.SemaphoreType.DMA((2,))],
    )
    return pl.pallas_call(
        functools.partial(_combine_kernel, n_tiles=n_tiles),
        grid_spec=grid_spec,
        out_shape=jax.ShapeDtypeStruct((n_tok, D_MODEL), F32),
        compiler_params=pltpu.CompilerParams(dimension_semantics=("arbitrary",),
                                             vmem_limit_bytes=VMEM_LIMIT),
        name="combine",
    )(runs.reshape(-1), h, lpos, rt, norm_final.reshape(1, D_MODEL).astype(F32), y)


def kernel(x, norm_mix, w_in, conv_w, conv_b, w_rg_a, b_rg_a, w_rg_x, b_rg_x, rg_lambda, rg_norm, w_alpha_up, b_alpha, gla_norm, w_out, norm_ffn, w_router_group, b_router_group, w_router_expert, b_router_expert, w_exp_gate, w_exp_up, w_exp_down, norm_final):
    bsz, seq, d = x.shape
    assert d == D_MODEL and norm_mix.shape[0] == 1, "single-layer model of width D_MODEL expected"
    n_tok = bsz * seq
    nb_max = (2 * n_tok + N_EXPERTS * (BM - 1)) // BM

    h, hn, rt = _mixer(x, norm_mix[0], w_in[0], w_alpha_up[0], b_alpha[0], conv_w[0], conv_b[0],
                       w_rg_a[0], b_rg_a[0], w_rg_x[0], b_rg_x[0], rg_lambda[0], rg_norm[0], gla_norm[0],
                       w_out[0], norm_ffn[0], w_router_group[0], b_router_group[0],
                       w_router_expert[0], b_router_expert[0])
    lpos, meta, runs = _plan(rt, n_tok, nb_max)
    xs = _dispatch(runs, meta, hn, lpos, n_tok, nb_max)
    y = _experts(meta, xs, w_exp_gate[0], w_exp_up[0], w_exp_down[0], nb_max)
    out = _combine(runs, h, lpos, rt, norm_final, y, n_tok)
    return out.reshape(bsz, seq, d)
```

```python
import functools

import jax
import jax.numpy as jnp
from jax import lax
from jax.experimental import pallas as pl
from jax.experimental.pallas import tpu as pltpu

F32 = jnp.float32
BF16 = jnp.bfloat16

D_MODEL = 1024
RG_WIDTH = 512
RG_BLOCKS = 8
RG_BLOCK = 64
CONV_WIDTH = 4
C_RG = 8.0
GLA_HEADS = 4
GLA_VAL = 512
GLA_KEY = 256
GLA_DK = 64
GLA_DV = 128
GATE_RANK = 16
GATE_NORM = 16.0
CHUNK = 64
N_GROUPS = 4
EXPERTS_PER_GROUP = 8
N_EXPERTS = 32
EPS = 1e-6

LANES = 128
SUBLANES = 8
VMEM_LIMIT = 56 * 1024 * 1024

TS_MIX = 512
PREFIX_ROWS = 256
BM = 512
SKIP_ROWS = 128
TT = 512
TILE_SLOTS = 2 * TT
RUN_CHUNK_LOG2 = 6
RUN_FIELDS = 4


def _dot(a, b):
    return jnp.dot(a, b, preferred_element_type=F32)


def _softplus(z):
    return jnp.maximum(z, 0.0) + jnp.log1p(jnp.exp(-jnp.abs(z)))


def _rms(x, g):
    return x * lax.rsqrt(jnp.mean(x * x, axis=-1, keepdims=True) + EPS) * g


U32 = jnp.uint32
TILE_ROWS = D_MODEL // 2 // LANES


def _copy_run(src, src_row, dst, dst_row, n_rows, sem):
    def piece(off, rows):
        s = src.at[pl.ds(pl.multiple_of((src_row + off) * TILE_ROWS, TILE_ROWS), rows * TILE_ROWS), :]
        d = dst.at[pl.ds(pl.multiple_of((dst_row + off) * TILE_ROWS, TILE_ROWS), rows * TILE_ROWS), :]
        pltpu.make_async_copy(s, d, sem).start()

    chunk = 1 << RUN_CHUNK_LOG2
    n_chunks = lax.shift_right_logical(n_rows, RUN_CHUNK_LOG2)
    lax.fori_loop(0, n_chunks, lambda c, carry: (piece(c * chunk, chunk), carry)[1], 0)
    off = n_chunks * chunk
    for k in reversed(range(RUN_CHUNK_LOG2)):
        bit = jnp.bitwise_and(n_rows, 1 << k)

        @pl.when(bit != 0)
        def _():
            piece(off, 1 << k)

        off = off + bit


def _wait_rows(ref, n_rows, sem):
    view = ref.at[pl.ds(0, n_rows * TILE_ROWS), :]
    pltpu.make_async_copy(view, view, sem).wait()


def _store_token_tiles(ref, val, already_bf16=False):
    n = val.shape[0]
    if not already_bf16:
        val = val.astype(BF16).astype(F32)
    bits = lax.bitcast_convert_type(val, U32)
    packed = jnp.bitwise_or(bits[:, :D_MODEL // 2], jnp.right_shift(bits[:, D_MODEL // 2:], 16))
    for c in range(TILE_ROWS):
        ref[pl.ds(c, n, stride=TILE_ROWS), :] = packed[:, c * LANES:(c + 1) * LANES]


def _load_token_tiles(ref, n):
    words = [ref[pl.ds(c, n, stride=TILE_ROWS), :] for c in range(TILE_ROWS)]
    high = [lax.bitcast_convert_type(jnp.bitwise_and(w, jnp.uint32(0xFFFF0000)), F32) for w in words]
    low = [lax.bitcast_convert_type(jnp.left_shift(w, 16), F32) for w in words]
    return jnp.concatenate(high + low, axis=1)


def _inproj_part(bi, x_ref, g_ref, wseg, wal, wup, bal, seg_outs, la_o):
    hn = _rms(x_ref[bi], g_ref[...]).astype(BF16)
    col = 0
    for out, width in zip(seg_outs, PROJ_WIDTHS):
        out[bi] = _dot(hn, wseg[:, col:col + width])
        col += width
    a3 = _dot(hn, wal[...])
    a3_hi = a3.astype(BF16).astype(F32)
    lane = lax.broadcasted_iota(jnp.int32, a3.shape, 1)
    use_low = jnp.logical_and(lane >= GATE_RANK, lane < 2 * GATE_RANK)
    z = _dot(jnp.where(use_low, a3 - a3_hi, a3_hi).astype(BF16), wup[...]) + bal[...]
    log_sig = jnp.minimum(z, 0.0) - jnp.log1p(jnp.exp(-jnp.abs(z)))
    la_o[bi] = log_sig * (1.0 / GATE_NORM)


PROJ_WIDTHS = (RG_WIDTH, RG_WIDTH, GLA_KEY, GLA_KEY, GLA_VAL, GLA_VAL, GLA_KEY)


def _inproj_weights(w_in, w_alpha_up):
    n_seg = sum(PROJ_WIDTHS[:6])
    wb = w_in.astype(BF16)
    w_low = wb[:, n_seg:n_seg + GATE_RANK]
    wal = jnp.pad(jnp.concatenate([w_low, w_low, w_low], axis=1), ((0, 0), (0, LANES - 3 * GATE_RANK)))
    up = w_alpha_up.astype(F32)
    up_hi = up.astype(BF16)
    up_lo = (up - up_hi.astype(F32)).astype(BF16)
    wup = jnp.pad(jnp.concatenate([up_hi, up_hi, up_lo], axis=0), ((0, LANES - 3 * GATE_RANK), (0, 0)))
    return [wb[:, :n_seg], wal, wup]


def _rglru_part(bi, xr_ref, yr_ref, cw_ref, cb_ref, wg_ref, ba_ref, bx_ref, lam_ref, gn_ref,
                o_ref, tail_ref, hc_ref):
    ts = xr_ref.shape[1]
    n_groups = ts // SUBLANES
    grouped = (n_groups, SUBLANES, RG_WIDTH)
    sub = lax.broadcasted_iota(jnp.int32, grouped, 1)

    x = xr_ref[bi]
    x_grp = x.reshape(grouped)
    x_prev = jnp.concatenate([tail_ref[bi], x[:ts - SUBLANES, :]], axis=0).reshape(grouped)
    tail_ref[bi] = x[ts - SUBLANES:ts, :]
    cw = cw_ref[...]
    xc = cb_ref[...] + cw[CONV_WIDTH - 1:CONV_WIDTH, :] * x
    for s in range(1, CONV_WIDTH):
        mixed = jnp.where(sub >= SUBLANES - s, x_prev, x_grp)
        shifted = pltpu.roll(mixed, s, axis=1)
        xc = xc + cw[CONV_WIDTH - 1 - s:CONV_WIDTH - s, :] * shifted.reshape(ts, RG_WIDTH)

    xb = xc.astype(BF16)
    half = RG_WIDTH // 2
    parts = [_dot(xb[:, j * half:(j + 1) * half], wg_ref[j]) for j in range(2)]
    r = jax.nn.sigmoid(jnp.concatenate([p[:, :half] for p in parts], axis=1) + ba_ref[...])
    gate_i = jax.nn.sigmoid(jnp.concatenate([p[:, half:] for p in parts], axis=1) + bx_ref[...])
    log_a = (-C_RG) * r * _softplus(-lam_ref[...])
    a = jnp.exp(log_a)
    z = jnp.tanh(-log_a) * (1.0 + a * a)
    u = (z * lax.rsqrt(jnp.maximum(z, jnp.finfo(F32).tiny))) * (gate_i * xc)

    a = a.reshape(grouped)
    u = u.reshape(grouped)
    d = 1
    while d < SUBLANES:
        keep = sub >= d
        a_sh = jnp.where(keep, pltpu.roll(a, d, axis=1), 1.0)
        u_sh = jnp.where(keep, pltpu.roll(u, d, axis=1), 0.0)
        u = a * u_sh + u
        a = a * a_sh
        d *= 2
    carry = hc_ref[bi, 0:1, :]
    groups = []
    for j in range(n_groups):
        h_j = u[j] + a[j] * carry
        groups.append(h_j)
        carry = h_j[SUBLANES - 1:SUBLANES, :]
    h = jnp.concatenate(groups, axis=0)
    hc_ref[bi] = jnp.broadcast_to(carry, hc_ref.shape[1:])

    y = h * jax.nn.gelu(yr_ref[bi])
    o_ref[bi, :, 0:RG_WIDTH] = _rms(y, gn_ref[...]).astype(o_ref.dtype)


def _gate_weights(w_a, w_x):
    def block_diag(w):
        eye = jnp.eye(RG_BLOCKS, dtype=w.dtype)
        return jnp.einsum('hij,hg->higj', w, eye).reshape(RG_WIDTH, RG_WIDTH)

    half = RG_WIDTH // 2
    a, x = block_diag(w_a), block_diag(w_x)
    return jnp.stack([jnp.concatenate([m[j * half:(j + 1) * half, j * half:(j + 1) * half] for m in (a, x)],
                                      axis=1) for j in range(2)]).astype(BF16)


def _mixer_kernel(x_ref, mgn_ref, wseg, wal, wup, bal,
                  cw_ref, cb_ref, wgate_ref, ba_ref, bx_ref, lam_ref, rgn_ref, gn_ref,
                  wo_ref, fgn_ref, wr_ref, br_ref,
                  h_o, hn_o, rt_o,
                  xr_ref, yr_ref, q_ref, k_ref, v_ref, g_ref, la_ref, o_ref, tail_ref, hc_ref, st_ref):
    bsz, ts = q_ref.shape[0], q_ref.shape[1]
    n_chunks = ts // CHUNK

    @pl.when(pl.program_id(0) == 0)
    def _():
        tail_ref[...] = jnp.zeros_like(tail_ref)
        hc_ref[...] = jnp.zeros_like(hc_ref)
        st_ref[...] = jnp.zeros_like(st_ref)

    ri = lax.broadcasted_iota(jnp.int32, (PREFIX_ROWS, PREFIX_ROWS), 0)
    ci = lax.broadcasted_iota(jnp.int32, (PREFIX_ROWS, PREFIX_ROWS), 1)
    chunk_bits = CHUNK.bit_length() - 1
    same_chunk = lax.shift_right_logical(ri, chunk_bits) == lax.shift_right_logical(ci, chunk_bits)
    prefix = jnp.where(jnp.logical_and(same_chunk, ri >= ci), 1.0, 0.0).astype(BF16)
    causal = (ri >= ci)[:CHUNK, :CHUNK]
    scale = GLA_DK ** -0.5
    gn = gn_ref[...]
    nt_dims = (((1,), (1,)), ((), ()))
    tn_dims = (((0,), (0,)), ((), ()))

    for bi in range(bsz):
        _inproj_part(bi, x_ref, mgn_ref, wseg, wal, wup, bal,
                     (xr_ref, yr_ref, q_ref, k_ref, v_ref, g_ref), la_ref)

    for bi in range(bsz):
        _rglru_part(bi, xr_ref, yr_ref, cw_ref, cb_ref, wgate_ref, ba_ref, bx_ref, lam_ref, rgn_ref,
                    o_ref, tail_ref, hc_ref)

    for bi in range(bsz):
        la = la_ref[bi]
        la_hi = la.astype(BF16)
        la_lo = (la - la_hi.astype(F32)).astype(BF16)
        b = jnp.concatenate(
            [_dot(prefix, la_hi[r:r + PREFIX_ROWS]) + _dot(prefix, la_lo[r:r + PREFIX_ROWS])
             for r in range(0, ts, PREFIX_ROWS)], axis=0)
        b_tot = jnp.concatenate(
            [jnp.broadcast_to(b[c * CHUNK + CHUNK - 1:(c + 1) * CHUNK, :], (CHUNK, GLA_KEY))
             for c in range(n_chunks)], axis=0)
        kk = k_ref[bi]
        q_s = (q_ref[bi] * scale) * jnp.exp(b)
        k_s = kk * jnp.exp(-b)
        k_end = kk * jnp.exp(b_tot - b)
        decay = jnp.exp(b_tot)

        o_intra, kv, q_heads = [], [], []
        for c in range(n_chunks):
            rows = slice(c * CHUNK, (c + 1) * CHUNK)
            o_c, kv_c, q_c = [], [], []
            for h in range(GLA_HEADS):
                ks = slice(h * GLA_DK, (h + 1) * GLA_DK)
                vs = slice(h * GLA_DV, (h + 1) * GLA_DV)
                qh = q_s[rows, ks].astype(BF16)
                vb = v_ref[bi, rows, vs].astype(BF16)
                att = lax.dot_general(qh, k_s[rows, ks].astype(BF16), nt_dims, preferred_element_type=F32)
                att = jnp.where(causal, att, 0.0).astype(BF16)
                o_c.append(_dot(att, vb))
                kv_c.append(lax.dot_general(vb, k_end[rows, ks].astype(BF16), tn_dims,
                                            preferred_element_type=F32))
                q_c.append(qh)
            o_intra.append(o_c)
            q_heads.append(q_c)
            kv.append(jnp.concatenate(kv_c, axis=1))

        st = st_ref[bi]
        for c in range(n_chunks):
            rows = slice(c * CHUNK, (c + 1) * CHUNK)
            for h in range(GLA_HEADS):
                ks = slice(h * GLA_DK, (h + 1) * GLA_DK)
                vs = slice(h * GLA_DV, (h + 1) * GLA_DV)
                o = o_intra[c][h] + lax.dot_general(q_heads[c][h], st[:, ks].astype(BF16), nt_dims,
                                                    preferred_element_type=F32)
                gated = _rms(o, gn) * jax.nn.silu(g_ref[bi, rows, vs])
                o_ref[bi, rows, RG_WIDTH + h * GLA_DV:RG_WIDTH + (h + 1) * GLA_DV] = gated.astype(o_ref.dtype)
            st = decay[c * CHUNK:c * CHUNK + 1, :] * st + kv[c]
        st_ref[bi] = st

    for bi in range(bsz):
        _outproj_part(bi, o_ref, x_ref, wo_ref, fgn_ref, wr_ref, br_ref, h_o, hn_o, rt_o)


def _mixer(x, norm_mix, w_in, w_alpha_up, b_alpha, conv_w, conv_b, w_a, b_a, w_x, b_x, lam, rg_norm,
           gla_norm, w_out, norm_ffn, w_rg, b_rg, w_re, b_re):
    bsz, seq, _ = x.shape
    nt = seq // TS_MIX
    blk = lambda w: pl.BlockSpec((bsz, TS_MIX, w), lambda i: (0, i, 0))
    fixed = lambda r, c: pl.BlockSpec((r, c), lambda i: (0, 0))
    vec = lambda a: a.reshape(1, -1).astype(F32)
    proj_w = _inproj_weights(w_in, w_alpha_up)
    wr = jnp.pad(jnp.concatenate([w_rg, w_re], axis=1).astype(F32),
                 ((0, 0), (0, LANES - N_GROUPS - N_EXPERTS)))
    wr_hi = wr.astype(BF16)
    wr = jnp.concatenate([wr_hi, (wr - wr_hi.astype(F32)).astype(BF16)], axis=1)
    br = jnp.pad(jnp.concatenate([b_rg, b_re]).astype(F32), (0, LANES - N_GROUPS - N_EXPERTS))
    h, hn, rt = pl.pallas_call(
        _mixer_kernel,
        grid=(nt,),
        in_specs=[blk(D_MODEL), fixed(1, D_MODEL)]
                 + [fixed(*w.shape) for w in proj_w] + [fixed(1, GLA_KEY),
                  fixed(CONV_WIDTH, RG_WIDTH), fixed(1, RG_WIDTH),
                  pl.BlockSpec((2, RG_WIDTH // 2, RG_WIDTH), lambda i: (0, 0, 0)),
                  fixed(1, RG_WIDTH), fixed(1, RG_WIDTH), fixed(1, RG_WIDTH),
                  fixed(1, RG_WIDTH), fixed(1, GLA_DV),
                  fixed(D_MODEL, D_MODEL), fixed(1, D_MODEL), fixed(D_MODEL, 2 * LANES), fixed(1, LANES)],
        out_specs=[blk(D_MODEL), blk(D_MODEL),
                   pl.BlockSpec((bsz, SUBLANES, TS_MIX), lambda i: (0, 0, i))],
        out_shape=[jax.ShapeDtypeStruct((bsz, seq, D_MODEL), F32),
                   jax.ShapeDtypeStruct((bsz, seq, D_MODEL), BF16),
                   jax.ShapeDtypeStruct((bsz, SUBLANES, seq), F32)],
        scratch_shapes=[pltpu.VMEM((bsz, TS_MIX, w), F32) for w in PROJ_WIDTHS]
                      + [pltpu.VMEM((bsz, TS_MIX, D_MODEL), BF16),
                        pltpu.VMEM((bsz, SUBLANES, RG_WIDTH), F32),
                        pltpu.VMEM((bsz, SUBLANES, RG_WIDTH), F32),
                        pltpu.VMEM((bsz, GLA_DV, GLA_KEY), F32)],
        compiler_params=pltpu.CompilerParams(dimension_semantics=("arbitrary",),
                                             vmem_limit_bytes=VMEM_LIMIT),
        name="mixer",
    )(x, vec(norm_mix), *proj_w, vec(b_alpha),
      conv_w.astype(F32), vec(conv_b), _gate_weights(w_a, w_x), vec(b_a), vec(b_x),
      vec(lam), vec(rg_norm), vec(gla_norm),
      w_out.astype(BF16), vec(norm_ffn), wr, br.reshape(1, LANES))
    n_tok = bsz * seq
    return h.reshape(n_tok, D_MODEL), hn.reshape(n_tok, D_MODEL), rt


def _outproj_part(bi, y_ref, x_ref, wo_ref, gn_ref, wr_ref, br_ref, h_o, hn_o, rt_o):
    tm = x_ref.shape[1]
    h = x_ref[bi] + _dot(y_ref[bi], wo_ref[...])
    h_o[bi] = h
    hn = _rms(h, gn_ref[...])
    hn_hi = hn.astype(BF16)
    hn_o[bi] = hn_hi
    hn_lo = (hn - hn_hi.astype(F32)).astype(BF16)
    hi_parts = _dot(hn_hi, wr_ref[...])
    logits = hi_parts[:, :LANES] + hi_parts[:, LANES:] + _dot(hn_lo, wr_ref[:, :LANES]) + br_ref[...]

    lane = lax.broadcasted_iota(jnp.int32, (tm, LANES), 1).astype(F32)
    neg = -jnp.inf
    glog = jnp.where(lane < N_GROUPS, logits, neg)
    gmax = jnp.max(glog, axis=-1, keepdims=True)
    gidx = jnp.min(jnp.where(glog == gmax, lane, float(LANES)), axis=-1, keepdims=True)
    g_w = 1.0 / jnp.sum(jnp.exp(glog - gmax), axis=-1, keepdims=True)
    lo = N_GROUPS + gidx * EXPERTS_PER_GROUP
    in_group = jnp.logical_and(lane >= lo, lane < lo + EXPERTS_PER_GROUP)
    le = jnp.where(in_group, logits, neg)
    m1 = jnp.max(le, axis=-1, keepdims=True)
    i1 = jnp.min(jnp.where(le == m1, lane, float(LANES)), axis=-1, keepdims=True)
    le2 = jnp.where(lane == i1, neg, le)
    m2 = jnp.max(le2, axis=-1, keepdims=True)
    i2 = jnp.min(jnp.where(le2 == m2, lane, float(LANES)), axis=-1, keepdims=True)
    t2 = jnp.exp(m2 - m1)
    w1 = g_w / (1.0 + t2)
    w2 = g_w * t2 / (1.0 + t2)
    info = jnp.where(lane == 0.0, i1 - N_GROUPS,
                     jnp.where(lane == 1.0, i2 - N_GROUPS,
                               jnp.where(lane == 2.0, w1, jnp.where(lane == 3.0, w2, 0.0))))
    rt_o[bi] = info.T[0:SUBLANES, :]


def _plan_kernel(rt_ref, lpos_o, meta_o, runs_o, tri_ref, *, n_tok, nb_max):
    n_tiles = n_tok // TT
    esub = lax.broadcasted_iota(jnp.int32, (N_EXPERTS, TT), 0).astype(F32)

    tiles_per_seq = rt_ref.shape[2] // TT

    def onehots(i):
        off = pl.multiple_of(i * TT, TT)
        b, s = i // tiles_per_seq, pl.multiple_of((i % tiles_per_seq) * TT, TT)
        e1 = rt_ref[b, 0:1, pl.ds(s, TT)]
        e2 = rt_ref[b, 1:2, pl.ds(s, TT)]
        m1 = jnp.where(esub == e1, 1.0, 0.0)
        m2 = jnp.where(esub == e2, 1.0, 0.0)
        return off, m1, m2

    def count_body(i, cnt):
        _, m1, m2 = onehots(i)
        return cnt + jnp.sum(m1 + m2, axis=1, keepdims=True)

    counts = lax.fori_loop(0, n_tiles, count_body, jnp.zeros((N_EXPERTS, 1), F32))
    nblk = jnp.floor((counts + (BM - 1)) * (1.0 / BM))
    ei = lax.broadcasted_iota(jnp.int32, (N_EXPERTS, N_EXPERTS), 0)
    ej = lax.broadcasted_iota(jnp.int32, (N_EXPERTS, N_EXPERTS), 1)
    nblk_row = jnp.sum(jnp.where(ei == ej, nblk, 0.0), axis=0, keepdims=True)
    bstart = jnp.sum(jnp.where(ej < ei, nblk_row, 0.0), axis=1, keepdims=True)
    bend = bstart + nblk
    n_used = jnp.sum(nblk, axis=0, keepdims=True)

    meta_w = meta_o.shape[1]
    blane = lax.broadcasted_iota(jnp.int32, (N_EXPERTS, meta_w), 1).astype(F32)
    owner = jnp.sum(jnp.where(bend <= blane, 1.0, 0.0), axis=0, keepdims=True)
    owner = jnp.minimum(owner, N_EXPERTS - 1.0)
    lane1 = lax.broadcasted_iota(jnp.int32, (1, meta_w), 1)
    meta_o[0:1, :] = jnp.where(lane1 == nb_max, n_used, owner).astype(jnp.int32)
    nonempty = jnp.where(nblk > 0.0, 1.0, 0.0)
    meta_o[1:2, :] = jnp.sum(jnp.where(bend <= blane, nonempty, 0.0), axis=0,
                             keepdims=True).astype(jnp.int32)
    owned = jnp.logical_and(bstart <= blane, blane < bend)
    next_start = jnp.sum(jnp.where(owned, bend, 0.0), axis=0, keepdims=True)
    next_owner = jnp.sum(jnp.where(bend <= next_start, 1.0, 0.0), axis=0, keepdims=True)
    meta_o[2:3, :] = jnp.where(next_start < n_used, next_owner, -1.0).astype(jnp.int32)
    valid_end = jnp.sum(jnp.where(owned, bstart * float(BM) + counts, 0.0), axis=0, keepdims=True)
    meta_o[3:4, :] = jnp.clip(valid_end - lane1.astype(F32) * float(BM), 0.0, float(BM)).astype(jnp.int32)

    ti = lax.broadcasted_iota(jnp.int32, (TT, TT), 0)
    tj = lax.broadcasted_iota(jnp.int32, (TT, TT), 1)
    tri_ref[...] = jnp.where(ti <= tj, 1.0, 0.0).astype(BF16)
    tlane = lax.broadcasted_iota(jnp.int32, (N_EXPERTS, LANES), 1)

    def tile_body(i, carry):
        first_slot, t_cnt, t_slot, t_rank = carry
        off, m1, m2 = onehots(i)
        m = m1 + m2
        incl = _dot(m.astype(BF16), tri_ref[...])
        cnt = incl[:, TT - 1:TT]
        cnt_row = jnp.sum(jnp.where(ei == ej, cnt, 0.0), axis=0, keepdims=True)
        first_rank = jnp.sum(jnp.where(ej < ei, cnt_row, 0.0), axis=1, keepdims=True)
        rank = first_rank + incl - m
        lpos_o[0:1, pl.ds(off, TT)] = jnp.sum(m1 * rank, axis=0, keepdims=True).astype(jnp.int32)
        lpos_o[1:2, pl.ds(off, TT)] = jnp.sum(m2 * rank, axis=0, keepdims=True).astype(jnp.int32)
        here = tlane == i
        return (first_slot + cnt, jnp.where(here, cnt, t_cnt), jnp.where(here, first_slot, t_slot),
                jnp.where(here, first_rank, t_rank))

    zeros = jnp.zeros((N_EXPERTS, LANES), F32)
    _, t_cnt, t_slot, t_rank = lax.fori_loop(0, n_tiles, tile_body,
                                             (bstart * float(BM), zeros, zeros, zeros))
    runs_o[0] = t_cnt.astype(jnp.int32)
    runs_o[1] = t_slot.astype(jnp.int32)
    runs_o[2] = t_rank.astype(jnp.int32)
    pad = jnp.where(tlane == 0, bstart * float(BM) + counts,
                    jnp.where(tlane == 1, nblk * float(BM) - counts, 0.0))
    runs_o[3] = pad.astype(jnp.int32)


def _plan(rt, n_tok, nb_max):
    assert n_tok // TT <= LANES, "run tables hold one token tile per lane"
    meta_w = ((nb_max + 1 + LANES - 1) // LANES) * LANES
    return pl.pallas_call(
        functools.partial(_plan_kernel, n_tok=n_tok, nb_max=nb_max),
        out_shape=[jax.ShapeDtypeStruct((2, n_tok), jnp.int32),
                   jax.ShapeDtypeStruct((4, meta_w), jnp.int32),
                   jax.ShapeDtypeStruct((RUN_FIELDS, N_EXPERTS, LANES), jnp.int32)],
        scratch_shapes=[pltpu.VMEM((TT, TT), BF16)],
        compiler_params=pltpu.CompilerParams(vmem_limit_bytes=VMEM_LIMIT),
        name="plan",
    )(rt)


def _run_entry(runs_ref, field, expert, tile):
    return runs_ref[(field * N_EXPERTS + expert) * LANES + tile]


def _dispatch_kernel(runs_ref, meta_ref, hn_ref, lpos_ref, xs_hbm, xsbuf, zbuf, sem, zsem,
                     *, n_tiles, nb_max, n_pad):
    i = pl.program_id(0)
    slot = lax.rem(i, 2)

    @pl.when(i == 0)
    def _():
        zbuf[...] = jnp.zeros_like(zbuf)
        for e in range(N_EXPERTS):
            _copy_run(zbuf, 0, xs_hbm, _run_entry(runs_ref, 3, e, 0), _run_entry(runs_ref, 3, e, 1),
                      zsem)

        def unused_block(blk, carry):
            dst = xs_hbm.at[pl.ds(pl.multiple_of(blk * (BM * TILE_ROWS), BM * TILE_ROWS),
                                  BM * TILE_ROWS), :]
            pltpu.make_async_copy(zbuf, dst, zsem).start()
            return carry

        lax.fori_loop(meta_ref[nb_max], nb_max, unused_block, 0)

    @pl.when(i >= 2)
    def _():
        _wait_rows(xsbuf.at[slot], TILE_SLOTS, sem.at[slot])

    rank = lax.broadcasted_iota(jnp.int32, (TILE_SLOTS, TT), 0)
    onehot = jnp.where(rank == lpos_ref[0:1, :], 1.0, jnp.where(rank == lpos_ref[1:2, :], 1.0, 0.0))
    _store_token_tiles(xsbuf.at[slot], _dot(onehot.astype(BF16), hn_ref[...]), already_bf16=True)
    for e in range(N_EXPERTS):
        _copy_run(xsbuf.at[slot], _run_entry(runs_ref, 2, e, i), xs_hbm, _run_entry(runs_ref, 1, e, i),
                  _run_entry(runs_ref, 0, e, i), sem.at[slot])

    @pl.when(i == n_tiles - 1)
    def _():
        _wait_rows(xsbuf.at[slot], TILE_SLOTS, sem.at[slot])
        if n_tiles > 1:
            _wait_rows(xsbuf.at[1 - slot], TILE_SLOTS, sem.at[1 - slot])
        _wait_rows(xs_hbm, n_pad, zsem)


def _dispatch(runs, meta, hn, lpos, n_tok, nb_max):
    n_tiles = n_tok // TT
    cap = nb_max * BM
    grid_spec = pltpu.PrefetchScalarGridSpec(
        num_scalar_prefetch=2,
        grid=(n_tiles,),
        in_specs=[pl.BlockSpec((TT, D_MODEL), lambda i, runs_ref, meta_ref: (i, 0)),
                  pl.BlockSpec((2, TT), lambda i, runs_ref, meta_ref: (0, i))],
        out_specs=pl.BlockSpec(memory_space=pl.ANY),
        scratch_shapes=[pltpu.VMEM((2, TILE_SLOTS * TILE_ROWS, LANES), U32),
                        pltpu.VMEM((BM * TILE_ROWS, LANES), U32),
                        pltpu.SemaphoreType.DMA((2,)), pltpu.SemaphoreType.DMA],
    )
    return pl.pallas_call(
        functools.partial(_dispatch_kernel, n_tiles=n_tiles, nb_max=nb_max, n_pad=cap - 2 * n_tok),
        grid_spec=grid_spec,
        out_shape=jax.ShapeDtypeStruct((cap * TILE_ROWS, LANES), U32),
        compiler_params=pltpu.CompilerParams(dimension_semantics=("arbitrary",),
                                             vmem_limit_bytes=VMEM_LIMIT),
        name="dispatch",
    )(runs.reshape(-1), meta.reshape(-1), hn, lpos)


def _experts_kernel(meta_ref, xs_ref, wg_hbm, wu_hbm, wd_hbm, y_ref, wf32, wbf, wsem, *, nb_max, meta_w):
    b = pl.program_id(0)
    n_used = meta_ref[nb_max]

    def start_weights(expert, buf):
        for j, w_hbm in enumerate((wg_hbm, wu_hbm, wd_hbm)):
            pltpu.make_async_copy(w_hbm.at[expert], wf32.at[buf, j], wsem.at[buf]).start(priority=1)

    @pl.when(b < n_used)
    def _():
        owner = meta_ref[b]

        @pl.when(b == 0)
        def _():
            start_weights(owner, 0)

        @pl.when(jnp.logical_or(b == 0, owner != meta_ref[jnp.maximum(b - 1, 0)]))
        def _():
            buf = jnp.bitwise_and(meta_ref[meta_w + b], 1)
            pltpu.make_async_copy(wf32.at[buf], wf32.at[buf], wsem.at[buf]).wait()
            next_owner = meta_ref[2 * meta_w + b]

            @pl.when(next_owner >= 0)
            def _():
                start_weights(next_owner, 1 - buf)

            for j in range(3):
                wbf[j] = wf32[buf, j].astype(BF16)

        def geglu(n_rows):
            rows = pl.ds(0, n_rows * TILE_ROWS)
            x = _load_token_tiles(xs_ref.at[rows, :], n_rows).astype(BF16)
            mid = (jax.nn.gelu(_dot(x, wbf[0])) * _dot(x, wbf[1])).astype(BF16)
            _store_token_tiles(y_ref.at[rows, :], _dot(mid, wbf[2]))

        n_groups = lax.shift_right_logical(meta_ref[3 * meta_w + b] + (SKIP_ROWS - 1),
                                           SKIP_ROWS.bit_length() - 1)
        for g in range(1, BM // SKIP_ROWS + 1):
            @pl.when(n_groups == g)
            def _(g=g):
                geglu(g * SKIP_ROWS)
                if g * SKIP_ROWS < BM:
                    rest = (BM - g * SKIP_ROWS) * TILE_ROWS
                    y_ref[pl.ds(g * SKIP_ROWS * TILE_ROWS, rest), :] = jnp.zeros((rest, LANES), U32)

    @pl.when(b >= n_used)
    def _():
        y_ref[...] = jnp.zeros_like(y_ref)


def _experts(meta, xs, w_gate, w_up, w_down, nb_max):
    whole = pl.BlockSpec(memory_space=pl.ANY)
    rows = pl.BlockSpec((BM * TILE_ROWS, LANES), lambda b, meta_ref: (b, 0))
    grid_spec = pltpu.PrefetchScalarGridSpec(
        num_scalar_prefetch=1,
        grid=(nb_max,),
        in_specs=[rows, whole, whole, whole],
        out_specs=rows,
        scratch_shapes=[pltpu.VMEM((2, 3, D_MODEL, D_MODEL), F32),
                        pltpu.VMEM((3, D_MODEL, D_MODEL), BF16),
                        pltpu.SemaphoreType.DMA((2,))],
    )
    return pl.pallas_call(
        functools.partial(_experts_kernel, nb_max=nb_max, meta_w=meta.shape[1]),
        grid_spec=grid_spec,
        out_shape=jax.ShapeDtypeStruct(xs.shape, U32),
        compiler_params=pltpu.CompilerParams(dimension_semantics=("arbitrary",),
                                             vmem_limit_bytes=VMEM_LIMIT),
        name="experts",
    )(meta.reshape(-1), xs, w_gate, w_up, w_down)


def _combine_kernel(runs_ref, h_ref, lpos_ref, rt_ref, gn_ref, y_hbm, o_ref, ysbuf, sem, *, n_tiles):
    i = pl.program_id(0)
    slot = lax.rem(i, 2)

    def fetch(tile, buf):
        for e in range(N_EXPERTS):
            _copy_run(y_hbm, _run_entry(runs_ref, 1, e, tile), ysbuf.at[buf],
                      _run_entry(runs_ref, 2, e, tile), _run_entry(runs_ref, 0, e, tile), sem.at[buf])

    @pl.when(i == 0)
    def _():
        fetch(0, 0)

    @pl.when(i + 1 < n_tiles)
    def _():
        fetch(i + 1, 1 - slot)

    _wait_rows(ysbuf.at[slot], TILE_SLOTS, sem.at[slot])

    rank = lax.broadcasted_iota(jnp.int32, (TILE_SLOTS, TT), 0)
    first = rank == lpos_ref[0:1, :]
    second = rank == lpos_ref[1:2, :]
    slot_w = jnp.sum(jnp.where(first, rt_ref[2:3, :], 0.0) + jnp.where(second, rt_ref[3:4, :], 0.0),
                     axis=1, keepdims=True)
    onehot = jnp.where(first, 1.0, jnp.where(second, 1.0, 0.0)).astype(BF16)
    ys = (_load_token_tiles(ysbuf.at[slot], TILE_SLOTS) * slot_w).astype(BF16)
    moe = lax.dot_general(onehot, ys, (((0,), (0,)), ((), ())), preferred_element_type=F32)
    o_ref[...] = _rms(h_ref[...] + moe, gn_ref[...])


def _combine(runs, h, lpos, rt, norm_final, y, n_tok):
    n_tiles = n_tok // TT
    tiles_per_seq = rt.shape[2] // TT
    grid_spec = pltpu.PrefetchScalarGridSpec(
        num_scalar_prefetch=1,
        grid=(n_tiles,),
        in_specs=[pl.BlockSpec((TT, D_MODEL), lambda i, runs_ref: (i, 0)),
                  pl.BlockSpec((2, TT), lambda i, runs_ref: (0, i)),
                  pl.BlockSpec((None, SUBLANES, TT),
                               lambda i, runs_ref: (i // tiles_per_seq, 0, i % tiles_per_seq)),
                  pl.BlockSpec((1, D_MODEL), lambda i, runs_ref: (0, 0)),
                  pl.BlockSpec(memory_space=pl.ANY)],
        out_specs=pl.BlockSpec((TT, D_MODEL), lambda i, runs_ref: (i, 0)),
        scratch_shapes=[pltpu.VMEM((2, TILE_SLOTS * TILE_ROWS, LANES), U32),
                        pltpu.SemaphoreType.DMA((2,))],
    )
    return pl.pallas_call(
        functools.partial(_combine_kernel, n_tiles=n_tiles),
        grid_spec=grid_spec,
        out_shape=jax.ShapeDtypeStruct((n_tok, D_MODEL), F32),
        compiler_params=pltpu.CompilerParams(dimension_semantics=("arbitrary",),
                                             vmem_limit_bytes=VMEM_LIMIT),
        name="combine",
    )(runs.reshape(-1), h, lpos, rt, norm_final.reshape(1, D_MODEL).astype(F32), y)


def kernel(x, norm_mix, w_in, conv_w, conv_b, w_rg_a, b_rg_a, w_rg_x, b_rg_x, rg_lambda, rg_norm, w_alpha_up, b_alpha, gla_norm, w_out, norm_ffn, w_router_group, b_router_group, w_router_expert, b_router_expert, w_exp_gate, w_exp_up, w_exp_down, norm_final):
    bsz, seq, d = x.shape
    assert d == D_MODEL and norm_mix.shape[0] == 1, "single-layer model of width D_MODEL expected"
    n_tok = bsz * seq
    nb_max = (2 * n_tok + N_EXPERTS * (BM - 1)) // BM

    h, hn, rt = _mixer(x, norm_mix[0], w_in[0], w_alpha_up[0], b_alpha[0], conv_w[0], conv_b[0],
                       w_rg_a[0], b_rg_a[0], w_rg_x[0], b_rg_x[0], rg_lambda[0], rg_norm[0], gla_norm[0],
                       w_out[0], norm_ffn[0], w_router_group[0], b_router_group[0],
                       w_router_expert[0], b_router_expert[0])
    lpos, meta, runs = _plan(rt, n_tok, nb_max)
    xs = _dispatch(runs, meta, hn, lpos, n_tok, nb_max)
    y = _experts(meta, xs, w_exp_gate[0], w_exp_up[0], w_exp_down[0], nb_max)
    out = _combine(runs, h, lpos, rt, norm_final, y, n_tok)
    return out.reshape(bsz, seq, d)
```

```python
import functools

import jax
import jax.numpy as jnp
from jax import lax
from jax.experimental import pallas as pl
from jax.experimental.pallas import tpu as pltpu

F32 = jnp.float32
BF16 = jnp.bfloat16

D_MODEL = 1024
RG_WIDTH = 512
RG_BLOCKS = 8
RG_BLOCK = 64
CONV_WIDTH = 4
C_RG = 8.0
GLA_HEADS = 4
GLA_VAL = 512
GLA_KEY = 256
GLA_DK = 64
GLA_DV = 128
GATE_RANK = 16
GATE_NORM = 16.0
CHUNK = 64
N_GROUPS = 4
EXPERTS_PER_GROUP = 8
N_EXPERTS = 32
EPS = 1e-6

LANES = 128
SUBLANES = 8
VMEM_LIMIT = 56 * 1024 * 1024

TS_MIX = 512
PREFIX_ROWS = 256
BM = 512
SKIP_ROWS = 128
TT = 512
TILE_SLOTS = 2 * TT
RUN_CHUNK_LOG2 = 6
RUN_FIELDS = 4


def _dot(a, b):
    return jnp.dot(a, b, preferred_element_type=F32)


def _softplus(z):
    return jnp.maximum(z, 0.0) + jnp.log1p(jnp.exp(-jnp.abs(z)))


def _rms(x, g):
    return x * lax.rsqrt(jnp.mean(x * x, axis=-1, keepdims=True) + EPS) * g


U32 = jnp.uint32
TILE_ROWS = D_MODEL // 2 // LANES


def _copy_run(src, src_row, dst, dst_row, n_rows, sem):
    def piece(off, rows):
        s = src.at[pl.ds(pl.multiple_of((src_row + off) * TILE_ROWS, TILE_ROWS), rows * TILE_ROWS), :]
        d = dst.at[pl.ds(pl.multiple_of((dst_row + off) * TILE_ROWS, TILE_ROWS), rows * TILE_ROWS), :]
        pltpu.make_async_copy(s, d, sem).start()

    chunk = 1 << RUN_CHUNK_LOG2
    n_chunks = lax.shift_right_logical(n_rows, RUN_CHUNK_LOG2)
    lax.fori_loop(0, n_chunks, lambda c, carry: (piece(c * chunk, chunk), carry)[1], 0)
    off = n_chunks * chunk
    for k in reversed(range(RUN_CHUNK_LOG2)):
        bit = jnp.bitwise_and(n_rows, 1 << k)

        @pl.when(bit != 0)
        def _():
            piece(off, 1 << k)

        off = off + bit


def _wait_rows(ref, n_rows, sem):
    view = ref.at[pl.ds(0, n_rows * TILE_ROWS), :]
    pltpu.make_async_copy(view, view, sem).wait()


def _store_token_tiles(ref, val, already_bf16=False):
    n = val.shape[0]
    if not already_bf16:
        val = val.astype(BF16).astype(F32)
    bits = lax.bitcast_convert_type(val, U32)
    packed = jnp.bitwise_or(bits[:, :D_MODEL // 2], jnp.right_shift(bits[:, D_MODEL // 2:], 16))
    for c in range(TILE_ROWS):
        ref[pl.ds(c, n, stride=TILE_ROWS), :] = packed[:, c * LANES:(c + 1) * LANES]


def _load_token_tiles(ref, n):
    words = [ref[pl.ds(c, n, stride=TILE_ROWS), :] for c in range(TILE_ROWS)]
    high = [lax.bitcast_convert_type(jnp.bitwise_and(w, jnp.uint32(0xFFFF0000)), F32) for w in words]
    low = [lax.bitcast_convert_type(jnp.left_shift(w, 16), F32) for w in words]
    return jnp.concatenate(high + low, axis=1)


def _proj_part(bi, hn_ref, wseg, seg_outs, first):
    hn = hn_ref[bi]
    col = sum(PROJ_WIDTHS[:first])
    for out, width in zip(seg_outs, PROJ_WIDTHS[first:]):
        out[bi] = _dot(hn, wseg[:, col:col + width])
        col += width


def _gate_logit_part(bi, hn_ref, wal, wup, bal, la_o):
    hn = hn_ref[bi]
    a3 = _dot(hn, wal[...])
    a3_hi = a3.astype(BF16).astype(F32)
    lane = lax.broadcasted_iota(jnp.int32, a3.shape, 1)
    use_low = jnp.logical_and(lane >= GATE_RANK, lane < 2 * GATE_RANK)
    z = _dot(jnp.where(use_low, a3 - a3_hi, a3_hi).astype(BF16), wup[...]) + bal[...]
    log_sig = jnp.minimum(z, 0.0) - jnp.log1p(jnp.exp(-jnp.abs(z)))
    la_o[bi] = log_sig * (1.0 / GATE_NORM)


PROJ_WIDTHS = (RG_WIDTH, RG_WIDTH, GLA_KEY, GLA_KEY, GLA_VAL, GLA_VAL, GLA_KEY)


def _inproj_weights(w_in, w_alpha_up):
    n_seg = sum(PROJ_WIDTHS[:6])
    wb = w_in.astype(BF16)
    w_low = wb[:, n_seg:n_seg + GATE_RANK]
    wal = jnp.pad(jnp.concatenate([w_low, w_low, w_low], axis=1), ((0, 0), (0, LANES - 3 * GATE_RANK)))
    up = w_alpha_up.astype(F32)
    up_hi = up.astype(BF16)
    up_lo = (up - up_hi.astype(F32)).astype(BF16)
    wup = jnp.pad(jnp.concatenate([up_hi, up_hi, up_lo], axis=0), ((0, LANES - 3 * GATE_RANK), (0, 0)))
    return [wb[:, :n_seg], wal, wup]


def _rglru_part(bi, xr_ref, yr_ref, cw_ref, cb_ref, wg_ref, ba_ref, bx_ref, lam_ref, gn_ref,
                o_ref, tail_ref, hc_ref):
    ts = xr_ref.shape[1]
    n_groups = ts // SUBLANES
    grouped = (n_groups, SUBLANES, RG_WIDTH)
    sub = lax.broadcasted_iota(jnp.int32, grouped, 1)

    x = xr_ref[bi]
    x_grp = x.reshape(grouped)
    x_prev = jnp.concatenate([tail_ref[bi], x[:ts - SUBLANES, :]], axis=0).reshape(grouped)
    tail_ref[bi] = x[ts - SUBLANES:ts, :]
    cw = cw_ref[...]
    xc = cb_ref[...] + cw[CONV_WIDTH - 1:CONV_WIDTH, :] * x
    for s in range(1, CONV_WIDTH):
        mixed = jnp.where(sub >= SUBLANES - s, x_prev, x_grp)
        shifted = pltpu.roll(mixed, s, axis=1)
        xc = xc + cw[CONV_WIDTH - 1 - s:CONV_WIDTH - s, :] * shifted.reshape(ts, RG_WIDTH)

    xb = xc.astype(BF16)
    half = RG_WIDTH // 2
    parts = [_dot(xb[:, j * half:(j + 1) * half], wg_ref[j]) for j in range(2)]
    r = jax.nn.sigmoid(jnp.concatenate([p[:, :half] for p in parts], axis=1) + ba_ref[...])
    gate_i = jax.nn.sigmoid(jnp.concatenate([p[:, half:] for p in parts], axis=1) + bx_ref[...])
    log_a = (-C_RG) * r * _softplus(-lam_ref[...])
    a = jnp.exp(log_a)
    z = jnp.tanh(-log_a) * (1.0 + a * a)
    u = (z * lax.rsqrt(jnp.maximum(z, jnp.finfo(F32).tiny))) * (gate_i * xc)

    a = a.reshape(grouped)
    u = u.reshape(grouped)
    d = 1
    while d < SUBLANES:
        keep = sub >= d
        a_sh = jnp.where(keep, pltpu.roll(a, d, axis=1), 1.0)
        u_sh = jnp.where(keep, pltpu.roll(u, d, axis=1), 0.0)
        u = a * u_sh + u
        a = a * a_sh
        d *= 2
    carry = hc_ref[bi, 0:1, :]
    groups = []
    for j in range(n_groups):
        h_j = u[j] + a[j] * carry
        groups.append(h_j)
        carry = h_j[SUBLANES - 1:SUBLANES, :]
    h = jnp.concatenate(groups, axis=0)
    hc_ref[bi] = jnp.broadcast_to(carry, hc_ref.shape[1:])

    y = h * jax.nn.gelu(yr_ref[bi])
    o_ref[bi, :, 0:RG_WIDTH] = _rms(y, gn_ref[...]).astype(o_ref.dtype)


def _gate_weights(w_a, w_x):
    def block_diag(w):
        eye = jnp.eye(RG_BLOCKS, dtype=w.dtype)
        return jnp.einsum('hij,hg->higj', w, eye).reshape(RG_WIDTH, RG_WIDTH)

    half = RG_WIDTH // 2
    a, x = block_diag(w_a), block_diag(w_x)
    return jnp.stack([jnp.concatenate([m[j * half:(j + 1) * half, j * half:(j + 1) * half] for m in (a, x)],
                                      axis=1) for j in range(2)]).astype(BF16)


def _mixer_kernel(x_ref, mgn_ref, wseg, wal, wup, bal,
                  cw_ref, cb_ref, wgate_ref, ba_ref, bx_ref, lam_ref, rgn_ref, gn_ref,
                  wo_ref, fgn_ref, wr_ref, br_ref,
                  h_o, hn_o, rt_o,
                  xr_ref, yr_ref, q_ref, k_ref, v_ref, g_ref, la_ref, hn_ref, o_ref, tail_ref, hc_ref,
                  st_ref):
    bsz, ts = q_ref.shape[0], q_ref.shape[1]
    n_chunks = ts // CHUNK

    @pl.when(pl.program_id(0) == 0)
    def _():
        tail_ref[...] = jnp.zeros_like(tail_ref)
        hc_ref[...] = jnp.zeros_like(hc_ref)
        st_ref[...] = jnp.zeros_like(st_ref)

    ri = lax.broadcasted_iota(jnp.int32, (PREFIX_ROWS, PREFIX_ROWS), 0)
    ci = lax.broadcasted_iota(jnp.int32, (PREFIX_ROWS, PREFIX_ROWS), 1)
    chunk_bits = CHUNK.bit_length() - 1
    same_chunk = lax.shift_right_logical(ri, chunk_bits) == lax.shift_right_logical(ci, chunk_bits)
    prefix = jnp.where(jnp.logical_and(same_chunk, ri >= ci), 1.0, 0.0).astype(BF16)
    causal = (ri >= ci)[:CHUNK, :CHUNK]
    scale = GLA_DK ** -0.5
    gn = gn_ref[...]
    nt_dims = (((1,), (1,)), ((), ()))
    tn_dims = (((0,), (0,)), ((), ()))

    for bi in range(bsz):
        hn_ref[bi] = _rms(x_ref[bi], mgn_ref[...]).astype(BF16)
        _proj_part(bi, hn_ref, wseg, (xr_ref, yr_ref), 0)

    for bi in range(bsz):
        _rglru_part(bi, xr_ref, yr_ref, cw_ref, cb_ref, wgate_ref, ba_ref, bx_ref, lam_ref, rgn_ref,
                    o_ref, tail_ref, hc_ref)
        _proj_part(bi, hn_ref, wseg, (q_ref, k_ref, v_ref, g_ref), 2)
        _gate_logit_part(bi, hn_ref, wal, wup, bal, la_ref)

    for bi in range(bsz):
        la = la_ref[bi]
        la_hi = la.astype(BF16)
        la_lo = (la - la_hi.astype(F32)).astype(BF16)
        b = jnp.concatenate(
            [_dot(prefix, la_hi[r:r + PREFIX_ROWS]) + _dot(prefix, la_lo[r:r + PREFIX_ROWS])
             for r in range(0, ts, PREFIX_ROWS)], axis=0)
        b_tot = jnp.concatenate(
            [jnp.broadcast_to(b[c * CHUNK + CHUNK - 1:(c + 1) * CHUNK, :], (CHUNK, GLA_KEY))
             for c in range(n_chunks)], axis=0)
        kk = k_ref[bi]
        q_s = (q_ref[bi] * scale) * jnp.exp(b)
        k_s = kk * jnp.exp(-b)
        k_end = kk * jnp.exp(b_tot - b)
        decay = jnp.exp(b_tot)

        o_intra, kv, q_heads = [], [], []
        for c in range(n_chunks):
            rows = slice(c * CHUNK, (c + 1) * CHUNK)
            o_c, kv_c, q_c = [], [], []
            for h in range(GLA_HEADS):
                ks = slice(h * GLA_DK, (h + 1) * GLA_DK)
                vs = slice(h * GLA_DV, (h + 1) * GLA_DV)
                qh = q_s[rows, ks].astype(BF16)
                vb = v_ref[bi, rows, vs].astype(BF16)
                att = lax.dot_general(qh, k_s[rows, ks].astype(BF16), nt_dims, preferred_element_type=F32)
                att = jnp.where(causal, att, 0.0).astype(BF16)
                o_c.append(_dot(att, vb))
                kv_c.append(lax.dot_general(vb, k_end[rows, ks].astype(BF16), tn_dims,
                                            preferred_element_type=F32))
                q_c.append(qh)
            o_intra.append(o_c)
            q_heads.append(q_c)
            kv.append(jnp.concatenate(kv_c, axis=1))

        st = st_ref[bi]
        for c in range(n_chunks):
            rows = slice(c * CHUNK, (c + 1) * CHUNK)
            for h in range(GLA_HEADS):
                ks = slice(h * GLA_DK, (h + 1) * GLA_DK)
                vs = slice(h * GLA_DV, (h + 1) * GLA_DV)
                o = o_intra[c][h] + lax.dot_general(q_heads[c][h], st[:, ks].astype(BF16), nt_dims,
                                                    preferred_element_type=F32)
                gated = _rms(o, gn) * jax.nn.silu(g_ref[bi, rows, vs])
                o_ref[bi, rows, RG_WIDTH + h * GLA_DV:RG_WIDTH + (h + 1) * GLA_DV] = gated.astype(o_ref.dtype)
            st = decay[c * CHUNK:c * CHUNK + 1, :] * st + kv[c]
        st_ref[bi] = st

    for bi in range(bsz):
        _outproj_part(bi, o_ref, x_ref, wo_ref, fgn_ref, wr_ref, br_ref, h_o, hn_o, rt_o)


def _mixer(x, norm_mix, w_in, w_alpha_up, b_alpha, conv_w, conv_b, w_a, b_a, w_x, b_x, lam, rg_norm,
           gla_norm, w_out, norm_ffn, w_rg, b_rg, w_re, b_re):
    bsz, seq, _ = x.shape
    nt = seq // TS_MIX
    blk = lambda w: pl.BlockSpec((bsz, TS_MIX, w), lambda i: (0, i, 0))
    fixed = lambda r, c: pl.BlockSpec((r, c), lambda i: (0, 0))
    vec = lambda a: a.reshape(1, -1).astype(F32)
    proj_w = _inproj_weights(w_in, w_alpha_up)
    wr = jnp.pad(jnp.concatenate([w_rg, w_re], axis=1).astype(F32),
                 ((0, 0), (0, LANES - N_GROUPS - N_EXPERTS)))
    wr_hi = wr.astype(BF16)
    wr = jnp.concatenate([wr_hi, (wr - wr_hi.astype(F32)).astype(BF16)], axis=1)
    br = jnp.pad(jnp.concatenate([b_rg, b_re]).astype(F32), (0, LANES - N_GROUPS - N_EXPERTS))
    h, hn, rt = pl.pallas_call(
        _mixer_kernel,
        grid=(nt,),
        in_specs=[blk(D_MODEL), fixed(1, D_MODEL)]
                 + [fixed(*w.shape) for w in proj_w] + [fixed(1, GLA_KEY),
                  fixed(CONV_WIDTH, RG_WIDTH), fixed(1, RG_WIDTH),
                  pl.BlockSpec((2, RG_WIDTH // 2, RG_WIDTH), lambda i: (0, 0, 0)),
                  fixed(1, RG_WIDTH), fixed(1, RG_WIDTH), fixed(1, RG_WIDTH),
                  fixed(1, RG_WIDTH), fixed(1, GLA_DV),
                  fixed(D_MODEL, D_MODEL), fixed(1, D_MODEL), fixed(D_MODEL, 2 * LANES), fixed(1, LANES)],
        out_specs=[blk(D_MODEL), blk(D_MODEL),
                   pl.BlockSpec((bsz, SUBLANES, TS_MIX), lambda i: (0, 0, i))],
        out_shape=[jax.ShapeDtypeStruct((bsz, seq, D_MODEL), F32),
                   jax.ShapeDtypeStruct((bsz, seq, D_MODEL), BF16),
                   jax.ShapeDtypeStruct((bsz, SUBLANES, seq), F32)],
        scratch_shapes=[pltpu.VMEM((bsz, TS_MIX, w), F32) for w in PROJ_WIDTHS]
                      + [pltpu.VMEM((bsz, TS_MIX, D_MODEL), BF16),
                        pltpu.VMEM((bsz, TS_MIX, D_MODEL), BF16),
                        pltpu.VMEM((bsz, SUBLANES, RG_WIDTH), F32),
                        pltpu.VMEM((bsz, SUBLANES, RG_WIDTH), F32),
                        pltpu.VMEM((bsz, GLA_DV, GLA_KEY), F32)],
        compiler_params=pltpu.CompilerParams(dimension_semantics=("arbitrary",),
                                             vmem_limit_bytes=VMEM_LIMIT),
        name="mixer",
    )(x, vec(norm_mix), *proj_w, vec(b_alpha),
      conv_w.astype(F32), vec(conv_b), _gate_weights(w_a, w_x), vec(b_a), vec(b_x),
      vec(lam), vec(rg_norm), vec(gla_norm),
      w_out.astype(BF16), vec(norm_ffn), wr, br.reshape(1, LANES))
    n_tok = bsz * seq
    return h.reshape(n_tok, D_MODEL), hn.reshape(n_tok, D_MODEL), rt


def _outproj_part(bi, y_ref, x_ref, wo_ref, gn_ref, wr_ref, br_ref, h_o, hn_o, rt_o):
    tm = x_ref.shape[1]
    h = x_ref[bi] + _dot(y_ref[bi], wo_ref[...])
    h_o[bi] = h
    hn = _rms(h, gn_ref[...])
    hn_hi = hn.astype(BF16)
    hn_o[bi] = hn_hi
    hn_lo = (hn - hn_hi.astype(F32)).astype(BF16)
    hi_parts = _dot(hn_hi, wr_ref[...])
    logits = hi_parts[:, :LANES] + hi_parts[:, LANES:] + _dot(hn_lo, wr_ref[:, :LANES]) + br_ref[...]

    lane = lax.broadcasted_iota(jnp.int32, (tm, LANES), 1).astype(F32)
    neg = -jnp.inf
    glog = jnp.where(lane < N_GROUPS, logits, neg)
    gmax = jnp.max(glog, axis=-1, keepdims=True)
    gidx = jnp.min(jnp.where(glog == gmax, lane, float(LANES)), axis=-1, keepdims=True)
    g_w = 1.0 / jnp.sum(jnp.exp(glog - gmax), axis=-1, keepdims=True)
    lo = N_GROUPS + gidx * EXPERTS_PER_GROUP
    in_group = jnp.logical_and(lane >= lo, lane < lo + EXPERTS_PER_GROUP)
    le = jnp.where(in_group, logits, neg)
    m1 = jnp.max(le, axis=-1, keepdims=True)
    i1 = jnp.min(jnp.where(le == m1, lane, float(LANES)), axis=-1, keepdims=True)
    le2 = jnp.where(lane == i1, neg, le)
    m2 = jnp.max(le2, axis=-1, keepdims=True)
    i2 = jnp.min(jnp.where(le2 == m2, lane, float(LANES)), axis=-1, keepdims=True)
    t2 = jnp.exp(m2 - m1)
    w1 = g_w / (1.0 + t2)
    w2 = g_w * t2 / (1.0 + t2)
    info = jnp.where(lane == 0.0, i1 - N_GROUPS,
                     jnp.where(lane == 1.0, i2 - N_GROUPS,
                               jnp.where(lane == 2.0, w1, jnp.where(lane == 3.0, w2, 0.0))))
    rt_o[bi] = info.T[0:SUBLANES, :]


def _plan_kernel(rt_ref, lpos_o, meta_o, runs_o, tri_ref, *, n_tok, nb_max):
    n_tiles = n_tok // TT
    esub = lax.broadcasted_iota(jnp.int32, (N_EXPERTS, TT), 0).astype(F32)

    tiles_per_seq = rt_ref.shape[2] // TT

    def onehots(i):
        off = pl.multiple_of(i * TT, TT)
        b, s = i // tiles_per_seq, pl.multiple_of((i % tiles_per_seq) * TT, TT)
        e1 = rt_ref[b, 0:1, pl.ds(s, TT)]
        e2 = rt_ref[b, 1:2, pl.ds(s, TT)]
        m1 = jnp.where(esub == e1, 1.0, 0.0)
        m2 = jnp.where(esub == e2, 1.0, 0.0)
        return off, m1, m2

    def count_body(i, cnt):
        _, m1, m2 = onehots(i)
        return cnt + jnp.sum(m1 + m2, axis=1, keepdims=True)

    counts = lax.fori_loop(0, n_tiles, count_body, jnp.zeros((N_EXPERTS, 1), F32))
    nblk = jnp.floor((counts + (BM - 1)) * (1.0 / BM))
    ei = lax.broadcasted_iota(jnp.int32, (N_EXPERTS, N_EXPERTS), 0)
    ej = lax.broadcasted_iota(jnp.int32, (N_EXPERTS, N_EXPERTS), 1)
    nblk_row = jnp.sum(jnp.where(ei == ej, nblk, 0.0), axis=0, keepdims=True)
    bstart = jnp.sum(jnp.where(ej < ei, nblk_row, 0.0), axis=1, keepdims=True)
    bend = bstart + nblk
    n_used = jnp.sum(nblk, axis=0, keepdims=True)

    meta_w = meta_o.shape[1]
    blane = lax.broadcasted_iota(jnp.int32, (N_EXPERTS, meta_w), 1).astype(F32)
    owner = jnp.sum(jnp.where(bend <= blane, 1.0, 0.0), axis=0, keepdims=True)
    owner = jnp.minimum(owner, N_EXPERTS - 1.0)
    lane1 = lax.broadcasted_iota(jnp.int32, (1, meta_w), 1)
    meta_o[0:1, :] = jnp.where(lane1 == nb_max, n_used, owner).astype(jnp.int32)
    nonempty = jnp.where(nblk > 0.0, 1.0, 0.0)
    meta_o[1:2, :] = jnp.sum(jnp.where(bend <= blane, nonempty, 0.0), axis=0,
                             keepdims=True).astype(jnp.int32)
    owned = jnp.logical_and(bstart <= blane, blane < bend)
    next_start = jnp.sum(jnp.where(owned, bend, 0.0), axis=0, keepdims=True)
    next_owner = jnp.sum(jnp.where(bend <= next_start, 1.0, 0.0), axis=0, keepdims=True)
    meta_o[2:3, :] = jnp.where(next_start < n_used, next_owner, -1.0).astype(jnp.int32)
    valid_end = jnp.sum(jnp.where(owned, bstart * float(BM) + counts, 0.0), axis=0, keepdims=True)
    meta_o[3:4, :] = jnp.clip(valid_end - lane1.astype(F32) * float(BM), 0.0, float(BM)).astype(jnp.int32)

    ti = lax.broadcasted_iota(jnp.int32, (TT, TT), 0)
    tj = lax.broadcasted_iota(jnp.int32, (TT, TT), 1)
    tri_ref[...] = jnp.where(ti <= tj, 1.0, 0.0).astype(BF16)
    tlane = lax.broadcasted_iota(jnp.int32, (N_EXPERTS, LANES), 1)

    def tile_body(i, carry):
        first_slot, t_cnt, t_slot, t_rank = carry
        off, m1, m2 = onehots(i)
        m = m1 + m2
        incl = _dot(m.astype(BF16), tri_ref[...])
        cnt = incl[:, TT - 1:TT]
        cnt_row = jnp.sum(jnp.where(ei == ej, cnt, 0.0), axis=0, keepdims=True)
        first_rank = jnp.sum(jnp.where(ej < ei, cnt_row, 0.0), axis=1, keepdims=True)
        rank = first_rank + incl - m
        lpos_o[0:1, pl.ds(off, TT)] = jnp.sum(m1 * rank, axis=0, keepdims=True).astype(jnp.int32)
        lpos_o[1:2, pl.ds(off, TT)] = jnp.sum(m2 * rank, axis=0, keepdims=True).astype(jnp.int32)
        here = tlane == i
        return (first_slot + cnt, jnp.where(here, cnt, t_cnt), jnp.where(here, first_slot, t_slot),
                jnp.where(here, first_rank, t_rank))

    zeros = jnp.zeros((N_EXPERTS, LANES), F32)
    _, t_cnt, t_slot, t_rank = lax.fori_loop(0, n_tiles, tile_body,
                                             (bstart * float(BM), zeros, zeros, zeros))
    runs_o[0] = t_cnt.astype(jnp.int32)
    runs_o[1] = t_slot.astype(jnp.int32)
    runs_o[2] = t_rank.astype(jnp.int32)
    pad = jnp.where(tlane == 0, bstart * float(BM) + counts,
                    jnp.where(tlane == 1, nblk * float(BM) - counts, 0.0))
    runs_o[3] = pad.astype(jnp.int32)


def _plan(rt, n_tok, nb_max):
    assert n_tok // TT <= LANES, "run tables hold one token tile per lane"
    meta_w = ((nb_max + 1 + LANES - 1) // LANES) * LANES
    return pl.pallas_call(
        functools.partial(_plan_kernel, n_tok=n_tok, nb_max=nb_max),
        out_shape=[jax.ShapeDtypeStruct((2, n_tok), jnp.int32),
                   jax.ShapeDtypeStruct((4, meta_w), jnp.int32),
                   jax.ShapeDtypeStruct((RUN_FIELDS, N_EXPERTS, LANES), jnp.int32)],
        scratch_shapes=[pltpu.VMEM((TT, TT), BF16)],
        compiler_params=pltpu.CompilerParams(vmem_limit_bytes=VMEM_LIMIT),
        name="plan",
    )(rt)


def _run_entry(runs_ref, field, expert, tile):
    return runs_ref[(field * N_EXPERTS + expert) * LANES + tile]


def _dispatch_kernel(runs_ref, meta_ref, hn_ref, lpos_ref, xs_hbm, xsbuf, zbuf, sem, zsem,
                     *, n_tiles, nb_max, n_pad):
    i = pl.program_id(0)
    slot = lax.rem(i, 2)

    @pl.when(i == 0)
    def _():
        zbuf[...] = jnp.zeros_like(zbuf)
        for e in range(N_EXPERTS):
            _copy_run(zbuf, 0, xs_hbm, _run_entry(runs_ref, 3, e, 0), _run_entry(runs_ref, 3, e, 1),
                      zsem)

        def unused_block(blk, carry):
            dst = xs_hbm.at[pl.ds(pl.multiple_of(blk * (BM * TILE_ROWS), BM * TILE_ROWS),
                                  BM * TILE_ROWS), :]
            pltpu.make_async_copy(zbuf, dst, zsem).start()
            return carry

        lax.fori_loop(meta_ref[nb_max], nb_max, unused_block, 0)

    @pl.when(i >= 2)
    def _():
        _wait_rows(xsbuf.at[slot], TILE_SLOTS, sem.at[slot])

    rank = lax.broadcasted_iota(jnp.int32, (TILE_SLOTS, TT), 0)
    onehot = jnp.where(rank == lpos_ref[0:1, :], 1.0, jnp.where(rank == lpos_ref[1:2, :], 1.0, 0.0))
    _store_token_tiles(xsbuf.at[slot], _dot(onehot.astype(BF16), hn_ref[...]), already_bf16=True)
    for e in range(N_EXPERTS):
        _copy_run(xsbuf.at[slot], _run_entry(runs_ref, 2, e, i), xs_hbm, _run_entry(runs_ref, 1, e, i),
                  _run_entry(runs_ref, 0, e, i), sem.at[slot])

    @pl.when(i == n_tiles - 1)
    def _():
        _wait_rows(xsbuf.at[slot], TILE_SLOTS, sem.at[slot])
        if n_tiles > 1:
            _wait_rows(xsbuf.at[1 - slot], TILE_SLOTS, sem.at[1 - slot])
        _wait_rows(xs_hbm, n_pad, zsem)


def _dispatch(runs, meta, hn, lpos, n_tok, nb_max):
    n_tiles = n_tok // TT
    cap = nb_max * BM
    grid_spec = pltpu.PrefetchScalarGridSpec(
        num_scalar_prefetch=2,
        grid=(n_tiles,),
        in_specs=[pl.BlockSpec((TT, D_MODEL), lambda i, runs_ref, meta_ref: (i, 0)),
                  pl.BlockSpec((2, TT), lambda i, runs_ref, meta_ref: (0, i))],
        out_specs=pl.BlockSpec(memory_space=pl.ANY),
        scratch_shapes=[pltpu.VMEM((2, TILE_SLOTS * TILE_ROWS, LANES), U32),
                        pltpu.VMEM((BM * TILE_ROWS, LANES), U32),
                        pltpu.SemaphoreType.DMA((2,)), pltpu.SemaphoreType.DMA],
    )
    return pl.pallas_call(
        functools.partial(_dispatch_kernel, n_tiles=n_tiles, nb_max=nb_max, n_pad=cap - 2 * n_tok),
        grid_spec=grid_spec,
        out_shape=jax.ShapeDtypeStruct((cap * TILE_ROWS, LANES), U32),
        compiler_params=pltpu.CompilerParams(dimension_semantics=("arbitrary",),
                                             vmem_limit_bytes=VMEM_LIMIT),
        name="dispatch",
    )(runs.reshape(-1), meta.reshape(-1), hn, lpos)


def _experts_kernel(meta_ref, xs_ref, wg_hbm, wu_hbm, wd_hbm, y_ref, wf32, wbf, wsem, *, nb_max, meta_w):
    b = pl.program_id(0)
    n_used = meta_ref[nb_max]

    def start_weights(expert, buf):
        for j, w_hbm in enumerate((wg_hbm, wu_hbm, wd_hbm)):
            pltpu.make_async_copy(w_hbm.at[expert], wf32.at[buf, j], wsem.at[buf]).start(priority=1)

    @pl.when(b < n_used)
    def _():
        owner = meta_ref[b]

        @pl.when(b == 0)
        def _():
            start_weights(owner, 0)

        @pl.when(jnp.logical_or(b == 0, owner != meta_ref[jnp.maximum(b - 1, 0)]))
        def _():
            buf = jnp.bitwise_and(meta_ref[meta_w + b], 1)
            pltpu.make_async_copy(wf32.at[buf], wf32.at[buf], wsem.at[buf]).wait()
            next_owner = meta_ref[2 * meta_w + b]

            @pl.when(next_owner >= 0)
            def _():
                start_weights(next_owner, 1 - buf)

            for j in range(3):
                wbf[j] = wf32[buf, j].astype(BF16)

        def geglu(n_rows):
            rows = pl.ds(0, n_rows * TILE_ROWS)
            x = _load_token_tiles(xs_ref.at[rows, :], n_rows).astype(BF16)
            mid = (jax.nn.gelu(_dot(x, wbf[0])) * _dot(x, wbf[1])).astype(BF16)
            _store_token_tiles(y_ref.at[rows, :], _dot(mid, wbf[2]))

        n_groups = lax.shift_right_logical(meta_ref[3 * meta_w + b] + (SKIP_ROWS - 1),
                                           SKIP_ROWS.bit_length() - 1)
        for g in range(1, BM // SKIP_ROWS + 1):
            @pl.when(n_groups == g)
            def _(g=g):
                geglu(g * SKIP_ROWS)
                if g * SKIP_ROWS < BM:
                    rest = (BM - g * SKIP_ROWS) * TILE_ROWS
                    y_ref[pl.ds(g * SKIP_ROWS * TILE_ROWS, rest), :] = jnp.zeros((rest, LANES), U32)

    @pl.when(b >= n_used)
    def _():
        y_ref[...] = jnp.zeros_like(y_ref)


def _experts(meta, xs, w_gate, w_up, w_down, nb_max):
    whole = pl.BlockSpec(memory_space=pl.ANY)
    rows = pl.BlockSpec((BM * TILE_ROWS, LANES), lambda b, meta_ref: (b, 0))
    grid_spec = pltpu.PrefetchScalarGridSpec(
        num_scalar_prefetch=1,
        grid=(nb_max,),
        in_specs=[rows, whole, whole, whole],
        out_specs=rows,
        scratch_shapes=[pltpu.VMEM((2, 3, D_MODEL, D_MODEL), F32),
                        pltpu.VMEM((3, D_MODEL, D_MODEL), BF16),
                        pltpu.SemaphoreType.DMA((2,))],
    )
    return pl.pallas_call(
        functools.partial(_experts_kernel, nb_max=nb_max, meta_w=meta.shape[1]),
        grid_spec=grid_spec,
        out_shape=jax.ShapeDtypeStruct(xs.shape, U32),
        compiler_params=pltpu.CompilerParams(dimension_semantics=("arbitrary",),
                                             vmem_limit_bytes=VMEM_LIMIT),
        name="experts",
    )(meta.reshape(-1), xs, w_gate, w_up, w_down)


def _combine_kernel(runs_ref, h_ref, lpos_ref, rt_ref, gn_ref, y_hbm, o_ref, ysbuf, sem, *, n_tiles):
    i = pl.program_id(0)
    slot = lax.rem(i, 2)

    def fetch(tile, buf):
        for e in range(N_EXPERTS):
            _copy_run(y_hbm, _run_entry(runs_ref, 1, e, tile), ysbuf.at[buf],
                      _run_entry(runs_ref, 2, e, tile), _run_entry(runs_ref, 0, e, tile), sem.at[buf])

    @pl.when(i == 0)
    def _():
        fetch(0, 0)

    @pl.when(i + 1 < n_tiles)
    def _():
        fetch(i + 1, 1 - slot)

    _wait_rows(ysbuf.at[slot], TILE_SLOTS, sem.at[slot])

    rank = lax.broadcasted_iota(jnp.int32, (TILE_SLOTS, TT), 0)
    first = rank == lpos_ref[0:1, :]
    second = rank == lpos_ref[1:2, :]
    slot_w = jnp.sum(jnp.where(first, rt_ref[2:3, :], 0.0) + jnp.where(second, rt_ref[3:4, :], 0.0),
                     axis=1, keepdims=True)
    onehot = jnp.where(first, 1.0, jnp.where(second, 1.0, 0.0)).astype(BF16)
    ys = (_load_token_tiles(ysbuf.at[slot], TILE_SLOTS) * slot_w).astype(BF16)
    moe = lax.dot_general(onehot, ys, (((0,), (0,)), ((), ())), preferred_element_type=F32)
    o_ref[...] = _rms(h_ref[...] + moe, gn_ref[...])


def _combine(runs, h, lpos, rt, norm_final, y, n_tok):
    n_tiles = n_tok // TT
    tiles_per_seq = rt.shape[2] // TT
    grid_spec = pltpu.PrefetchScalarGridSpec(
        num_scalar_prefetch=1,
        grid=(n_tiles,),
        in_specs=[pl.BlockSpec((TT, D_MODEL), lambda i, runs_ref: (i, 0)),
                  pl.BlockSpec((2, TT), lambda i, runs_ref: (0, i)),
                  pl.BlockSpec((None, SUBLANES, TT),
                               lambda i, runs_ref: (i // tiles_per_seq, 0, i % tiles_per_seq)),
                  pl.BlockSpec((1, D_MODEL), lambda i, runs_ref: (0, 0)),
                  pl.BlockSpec(memory_space=pl.ANY)],
        out_specs=pl.BlockSpec((TT, D_MODEL), lambda i, runs_ref: (i, 0)),
        scratch_shapes=[pltpu.VMEM((2, TILE_SLOTS * TILE_ROWS, LANES), U32),
                        pltpu.SemaphoreType.DMA((2,))],
    )
    return pl.pallas_call(
        functools.partial(_combine_kernel, n_tiles=n_tiles),
        grid_spec=grid_spec,
        out_shape=jax.ShapeDtypeStruct((n_tok, D_MODEL), F32),
        compiler_params=pltpu.CompilerParams(dimension_semantics=("arbitrary",),
                                             vmem_limit_bytes=VMEM_LIMIT),
        name="combine",
    )(runs.reshape(-1), h, lpos, rt, norm_final.reshape(1, D_MODEL).astype(F32), y)


def kernel(x, norm_mix, w_in, conv_w, conv_b, w_rg_a, b_rg_a, w_rg_x, b_rg_x, rg_lambda, rg_norm, w_alpha_up, b_alpha, gla_norm, w_out, norm_ffn, w_router_group, b_router_group, w_router_expert, b_router_expert, w_exp_gate, w_exp_up, w_exp_down, norm_final):
    bsz, seq, d = x.shape
    assert d == D_MODEL and norm_mix.shape[0] == 1, "single-layer model of width D_MODEL expected"
    n_tok = bsz * seq
    nb_max = (2 * n_tok + N_EXPERTS * (BM - 1)) // BM

    h, hn, rt = _mixer(x, norm_mix[0], w_in[0], w_alpha_up[0], b_alpha[0], conv_w[0], conv_b[0],
                       w_rg_a[0], b_rg_a[0], w_rg_x[0], b_rg_x[0], rg_lambda[0], rg_norm[0], gla_norm[0],
                       w_out[0], norm_ffn[0], w_router_group[0], b_router_group[0],
                       w_router_expert[0], b_router_expert[0])
    lpos, meta, runs = _plan(rt, n_tok, nb_max)
    xs = _dispatch(runs, meta, hn, lpos, n_tok, nb_max)
    y = _experts(meta, xs, w_exp_gate[0], w_exp_up[0], w_exp_down[0], nb_max)
    out = _combine(runs, h, lpos, rt, norm_final, y, n_tok)
    return out.reshape(bsz, seq, d)
```

```python
import functools

import jax
import jax.numpy as jnp
from jax import lax
from jax.experimental import pallas as pl
from jax.experimental.pallas import tpu as pltpu

F32 = jnp.float32
BF16 = jnp.bfloat16

D_MODEL = 1024
RG_WIDTH = 512
RG_BLOCKS = 8
RG_BLOCK = 64
CONV_WIDTH = 4
C_RG = 8.0
GLA_HEADS = 4
GLA_VAL = 512
GLA_KEY = 256
GLA_DK = 64
GLA_DV = 128
GATE_RANK = 16
GATE_NORM = 16.0
CHUNK = 64
N_GROUPS = 4
EXPERTS_PER_GROUP = 8
N_EXPERTS = 32
EPS = 1e-6

LANES = 128
SUBLANES = 8
VMEM_LIMIT = 56 * 1024 * 1024

TS_MIX = 512
PREFIX_ROWS = 256
BM = 512
SKIP_ROWS = 64
TT = 512
TILE_SLOTS = 2 * TT
RUN_CHUNK_LOG2 = 6
RUN_FIELDS = 4


def _dot(a, b):
    return jnp.dot(a, b, preferred_element_type=F32)


def _softplus(z):
    return jnp.maximum(z, 0.0) + jnp.log1p(jnp.exp(-jnp.abs(z)))


def _rms(x, g):
    return x * lax.rsqrt(jnp.mean(x * x, axis=-1, keepdims=True) + EPS) * g


U32 = jnp.uint32
TILE_ROWS = D_MODEL // 2 // LANES


def _copy_run(src, src_row, dst, dst_row, n_rows, sem):
    def piece(off, rows):
        s = src.at[pl.ds(pl.multiple_of((src_row + off) * TILE_ROWS, TILE_ROWS), rows * TILE_ROWS), :]
        d = dst.at[pl.ds(pl.multiple_of((dst_row + off) * TILE_ROWS, TILE_ROWS), rows * TILE_ROWS), :]
        pltpu.make_async_copy(s, d, sem).start()

    chunk = 1 << RUN_CHUNK_LOG2
    n_chunks = lax.shift_right_logical(n_rows, RUN_CHUNK_LOG2)
    lax.fori_loop(0, n_chunks, lambda c, carry: (piece(c * chunk, chunk), carry)[1], 0)
    off = n_chunks * chunk
    for k in reversed(range(RUN_CHUNK_LOG2)):
        bit = jnp.bitwise_and(n_rows, 1 << k)

        @pl.when(bit != 0)
        def _():
            piece(off, 1 << k)

        off = off + bit


def _wait_rows(ref, n_rows, sem):
    view = ref.at[pl.ds(0, n_rows * TILE_ROWS), :]
    pltpu.make_async_copy(view, view, sem).wait()


def _store_token_tiles(ref, val, already_bf16=False):
    n = val.shape[0]
    if not already_bf16:
        val = val.astype(BF16).astype(F32)
    bits = lax.bitcast_convert_type(val, U32)
    packed = jnp.bitwise_or(bits[:, :D_MODEL // 2], jnp.right_shift(bits[:, D_MODEL // 2:], 16))
    for c in range(TILE_ROWS):
        ref[pl.ds(c, n, stride=TILE_ROWS), :] = packed[:, c * LANES:(c + 1) * LANES]


def _load_token_tiles(ref, n):
    words = [ref[pl.ds(c, n, stride=TILE_ROWS), :] for c in range(TILE_ROWS)]
    high = [lax.bitcast_convert_type(jnp.bitwise_and(w, jnp.uint32(0xFFFF0000)), F32) for w in words]
    low = [lax.bitcast_convert_type(jnp.left_shift(w, 16), F32) for w in words]
    return jnp.concatenate(high + low, axis=1)


def _proj_part(bi, hn_ref, wseg, seg_outs, first):
    hn = hn_ref[bi]
    col = sum(PROJ_WIDTHS[:first])
    for out, width in zip(seg_outs, PROJ_WIDTHS[first:]):
        out[bi] = _dot(hn, wseg[:, col:col + width])
        col += width


def _gate_logit_part(bi, hn_ref, wal, wup, bal, la_o):
    hn = hn_ref[bi]
    a3 = _dot(hn, wal[...])
    a3_hi = a3.astype(BF16).astype(F32)
    lane = lax.broadcasted_iota(jnp.int32, a3.shape, 1)
    use_low = jnp.logical_and(lane >= GATE_RANK, lane < 2 * GATE_RANK)
    z = _dot(jnp.where(use_low, a3 - a3_hi, a3_hi).astype(BF16), wup[...]) + bal[...]
    log_sig = jnp.minimum(z, 0.0) - jnp.log1p(jnp.exp(-jnp.abs(z)))
    la_o[bi] = log_sig * (1.0 / GATE_NORM)


PROJ_WIDTHS = (RG_WIDTH, RG_WIDTH, GLA_KEY, GLA_KEY, GLA_VAL, GLA_VAL, GLA_KEY)


def _inproj_weights(w_in, w_alpha_up):
    n_seg = sum(PROJ_WIDTHS[:6])
    wb = w_in.astype(BF16)
    w_low = wb[:, n_seg:n_seg + GATE_RANK]
    wal = jnp.pad(jnp.concatenate([w_low, w_low, w_low], axis=1), ((0, 0), (0, LANES - 3 * GATE_RANK)))
    up = w_alpha_up.astype(F32)
    up_hi = up.astype(BF16)
    up_lo = (up - up_hi.astype(F32)).astype(BF16)
    wup = jnp.pad(jnp.concatenate([up_hi, up_hi, up_lo], axis=0), ((0, LANES - 3 * GATE_RANK), (0, 0)))
    return [wb[:, :n_seg], wal, wup]


def _rglru_part(bi, xr_ref, yr_ref, cw_ref, cb_ref, wg_ref, ba_ref, bx_ref, lam_ref, gn_ref,
                o_ref, tail_ref, hc_ref):
    ts = xr_ref.shape[1]
    n_groups = ts // SUBLANES
    grouped = (n_groups, SUBLANES, RG_WIDTH)
    sub = lax.broadcasted_iota(jnp.int32, grouped, 1)

    x = xr_ref[bi]
    x_grp = x.reshape(grouped)
    x_prev = jnp.concatenate([tail_ref[bi], x[:ts - SUBLANES, :]], axis=0).reshape(grouped)
    tail_ref[bi] = x[ts - SUBLANES:ts, :]
    cw = cw_ref[...]
    xc = cb_ref[...] + cw[CONV_WIDTH - 1:CONV_WIDTH, :] * x
    for s in range(1, CONV_WIDTH):
        mixed = jnp.where(sub >= SUBLANES - s, x_prev, x_grp)
        shifted = pltpu.roll(mixed, s, axis=1)
        xc = xc + cw[CONV_WIDTH - 1 - s:CONV_WIDTH - s, :] * shifted.reshape(ts, RG_WIDTH)

    xb = xc.astype(BF16)
    half = RG_WIDTH // 2
    parts = [_dot(xb[:, j * half:(j + 1) * half], wg_ref[j]) for j in range(2)]
    r = jax.nn.sigmoid(jnp.concatenate([p[:, :half] for p in parts], axis=1) + ba_ref[...])
    gate_i = jax.nn.sigmoid(jnp.concatenate([p[:, half:] for p in parts], axis=1) + bx_ref[...])
    log_a = (-C_RG) * r * _softplus(-lam_ref[...])
    a = jnp.exp(log_a)
    z = jnp.tanh(-log_a) * (1.0 + a * a)
    u = (z * lax.rsqrt(jnp.maximum(z, jnp.finfo(F32).tiny))) * (gate_i * xc)

    a = a.reshape(grouped)
    u = u.reshape(grouped)
    d = 1
    while d < SUBLANES:
        keep = sub >= d
        a_sh = jnp.where(keep, pltpu.roll(a, d, axis=1), 1.0)
        u_sh = jnp.where(keep, pltpu.roll(u, d, axis=1), 0.0)
        u = a * u_sh + u
        a = a * a_sh
        d *= 2
    carry = hc_ref[bi, 0:1, :]
    groups = []
    for j in range(n_groups):
        h_j = u[j] + a[j] * carry
        groups.append(h_j)
        carry = h_j[SUBLANES - 1:SUBLANES, :]
    h = jnp.concatenate(groups, axis=0)
    hc_ref[bi] = jnp.broadcast_to(carry, hc_ref.shape[1:])

    y = h * jax.nn.gelu(yr_ref[bi])
    o_ref[bi, :, 0:RG_WIDTH] = _rms(y, gn_ref[...]).astype(o_ref.dtype)


def _gate_weights(w_a, w_x):
    def block_diag(w):
        eye = jnp.eye(RG_BLOCKS, dtype=w.dtype)
        return jnp.einsum('hij,hg->higj', w, eye).reshape(RG_WIDTH, RG_WIDTH)

    half = RG_WIDTH // 2
    a, x = block_diag(w_a), block_diag(w_x)
    return jnp.stack([jnp.concatenate([m[j * half:(j + 1) * half, j * half:(j + 1) * half] for m in (a, x)],
                                      axis=1) for j in range(2)]).astype(BF16)


def _mixer_kernel(x_ref, mgn_ref, wseg, wal, wup, bal,
                  cw_ref, cb_ref, wgate_ref, ba_ref, bx_ref, lam_ref, rgn_ref, gn_ref,
                  wo_ref, fgn_ref, wr_ref, br_ref,
                  h_o, hn_o, rt_o,
                  xr_ref, yr_ref, q_ref, k_ref, v_ref, g_ref, la_ref, hn_ref, o_ref, tail_ref, hc_ref,
                  st_ref):
    bsz, ts = q_ref.shape[0], q_ref.shape[1]
    n_chunks = ts // CHUNK

    @pl.when(pl.program_id(0) == 0)
    def _():
        tail_ref[...] = jnp.zeros_like(tail_ref)
        hc_ref[...] = jnp.zeros_like(hc_ref)
        st_ref[...] = jnp.zeros_like(st_ref)

    ri = lax.broadcasted_iota(jnp.int32, (PREFIX_ROWS, PREFIX_ROWS), 0)
    ci = lax.broadcasted_iota(jnp.int32, (PREFIX_ROWS, PREFIX_ROWS), 1)
    chunk_bits = CHUNK.bit_length() - 1
    same_chunk = lax.shift_right_logical(ri, chunk_bits) == lax.shift_right_logical(ci, chunk_bits)
    prefix = jnp.where(jnp.logical_and(same_chunk, ri >= ci), 1.0, 0.0).astype(BF16)
    causal = (ri >= ci)[:CHUNK, :CHUNK]
    scale = GLA_DK ** -0.5
    gn = gn_ref[...]
    nt_dims = (((1,), (1,)), ((), ()))
    tn_dims = (((0,), (0,)), ((), ()))

    for bi in range(bsz):
        hn_ref[bi] = _rms(x_ref[bi], mgn_ref[...]).astype(BF16)
        _proj_part(bi, hn_ref, wseg, (xr_ref, yr_ref), 0)

    for bi in range(bsz):
        _rglru_part(bi, xr_ref, yr_ref, cw_ref, cb_ref, wgate_ref, ba_ref, bx_ref, lam_ref, rgn_ref,
                    o_ref, tail_ref, hc_ref)
        _proj_part(bi, hn_ref, wseg, (q_ref, k_ref, v_ref, g_ref), 2)
        _gate_logit_part(bi, hn_ref, wal, wup, bal, la_ref)

    for bi in range(bsz):
        la = la_ref[bi]
        la_hi = la.astype(BF16)
        la_lo = (la - la_hi.astype(F32)).astype(BF16)
        b = jnp.concatenate(
            [_dot(prefix, la_hi[r:r + PREFIX_ROWS]) + _dot(prefix, la_lo[r:r + PREFIX_ROWS])
             for r in range(0, ts, PREFIX_ROWS)], axis=0)
        b_tot = jnp.concatenate(
            [jnp.broadcast_to(b[c * CHUNK + CHUNK - 1:(c + 1) * CHUNK, :], (CHUNK, GLA_KEY))
             for c in range(n_chunks)], axis=0)
        kk = k_ref[bi]
        q_s = (q_ref[bi] * scale) * jnp.exp(b)
        k_s = kk * jnp.exp(-b)
        k_end = kk * jnp.exp(b_tot - b)
        decay = jnp.exp(b_tot)

        o_intra, kv, q_heads = [], [], []
        for c in range(n_chunks):
            rows = slice(c * CHUNK, (c + 1) * CHUNK)
            o_c, kv_c, q_c = [], [], []
            for h in range(GLA_HEADS):
                ks = slice(h * GLA_DK, (h + 1) * GLA_DK)
                vs = slice(h * GLA_DV, (h + 1) * GLA_DV)
                qh = q_s[rows, ks].astype(BF16)
                vb = v_ref[bi, rows, vs].astype(BF16)
                att = lax.dot_general(qh, k_s[rows, ks].astype(BF16), nt_dims, preferred_element_type=F32)
                att = jnp.where(causal, att, 0.0).astype(BF16)
                o_c.append(_dot(att, vb))
                kv_c.append(lax.dot_general(vb, k_end[rows, ks].astype(BF16), tn_dims,
                                            preferred_element_type=F32))
                q_c.append(qh)
            o_intra.append(o_c)
            q_heads.append(q_c)
            kv.append(jnp.concatenate(kv_c, axis=1))

        st = st_ref[bi]
        for c in range(n_chunks):
            rows = slice(c * CHUNK, (c + 1) * CHUNK)
            for h in range(GLA_HEADS):
                ks = slice(h * GLA_DK, (h + 1) * GLA_DK)
                vs = slice(h * GLA_DV, (h + 1) * GLA_DV)
                o = o_intra[c][h] + lax.dot_general(q_heads[c][h], st[:, ks].astype(BF16), nt_dims,
                                                    preferred_element_type=F32)
                gated = _rms(o, gn) * jax.nn.silu(g_ref[bi, rows, vs])
                o_ref[bi, rows, RG_WIDTH + h * GLA_DV:RG_WIDTH + (h + 1) * GLA_DV] = gated.astype(o_ref.dtype)
            st = decay[c * CHUNK:c * CHUNK + 1, :] * st + kv[c]
        st_ref[bi] = st

    for bi in range(bsz):
        _outproj_part(bi, o_ref, x_ref, wo_ref, fgn_ref, wr_ref, br_ref, h_o, hn_o, rt_o)


def _mixer(x, norm_mix, w_in, w_alpha_up, b_alpha, conv_w, conv_b, w_a, b_a, w_x, b_x, lam, rg_norm,
           gla_norm, w_out, norm_ffn, w_rg, b_rg, w_re, b_re):
    bsz, seq, _ = x.shape
    nt = seq // TS_MIX
    blk = lambda w: pl.BlockSpec((bsz, TS_MIX, w), lambda i: (0, i, 0))
    fixed = lambda r, c: pl.BlockSpec((r, c), lambda i: (0, 0))
    vec = lambda a: a.reshape(1, -1).astype(F32)
    proj_w = _inproj_weights(w_in, w_alpha_up)
    wr = jnp.pad(jnp.concatenate([w_rg, w_re], axis=1).astype(F32),
                 ((0, 0), (0, LANES - N_GROUPS - N_EXPERTS)))
    wr_hi = wr.astype(BF16)
    wr = jnp.concatenate([wr_hi, (wr - wr_hi.astype(F32)).astype(BF16)], axis=1)
    br = jnp.pad(jnp.concatenate([b_rg, b_re]).astype(F32), (0, LANES - N_GROUPS - N_EXPERTS))
    h, hn, rt = pl.pallas_call(
        _mixer_kernel,
        grid=(nt,),
        in_specs=[blk(D_MODEL), fixed(1, D_MODEL)]
                 + [fixed(*w.shape) for w in proj_w] + [fixed(1, GLA_KEY),
                  fixed(CONV_WIDTH, RG_WIDTH), fixed(1, RG_WIDTH),
                  pl.BlockSpec((2, RG_WIDTH // 2, RG_WIDTH), lambda i: (0, 0, 0)),
                  fixed(1, RG_WIDTH), fixed(1, RG_WIDTH), fixed(1, RG_WIDTH),
                  fixed(1, RG_WIDTH), fixed(1, GLA_DV),
                  fixed(D_MODEL, D_MODEL), fixed(1, D_MODEL), fixed(D_MODEL, 2 * LANES), fixed(1, LANES)],
        out_specs=[blk(D_MODEL), blk(D_MODEL),
                   pl.BlockSpec((bsz, SUBLANES, TS_MIX), lambda i: (0, 0, i))],
        out_shape=[jax.ShapeDtypeStruct((bsz, seq, D_MODEL), F32),
                   jax.ShapeDtypeStruct((bsz, seq, D_MODEL), BF16),
                   jax.ShapeDtypeStruct((bsz, SUBLANES, seq), F32)],
        scratch_shapes=[pltpu.VMEM((bsz, TS_MIX, w), F32) for w in PROJ_WIDTHS]
                      + [pltpu.VMEM((bsz, TS_MIX, D_MODEL), BF16),
                        pltpu.VMEM((bsz, TS_MIX, D_MODEL), BF16),
                        pltpu.VMEM((bsz, SUBLANES, RG_WIDTH), F32),
                        pltpu.VMEM((bsz, SUBLANES, RG_WIDTH), F32),
                        pltpu.VMEM((bsz, GLA_DV, GLA_KEY), F32)],
        compiler_params=pltpu.CompilerParams(dimension_semantics=("arbitrary",),
                                             vmem_limit_bytes=VMEM_LIMIT),
        name="mixer",
    )(x, vec(norm_mix), *proj_w, vec(b_alpha),
      conv_w.astype(F32), vec(conv_b), _gate_weights(w_a, w_x), vec(b_a), vec(b_x),
      vec(lam), vec(rg_norm), vec(gla_norm),
      w_out.astype(BF16), vec(norm_ffn), wr, br.reshape(1, LANES))
    n_tok = bsz * seq
    return h.reshape(n_tok, D_MODEL), hn.reshape(n_tok, D_MODEL), rt


def _outproj_part(bi, y_ref, x_ref, wo_ref, gn_ref, wr_ref, br_ref, h_o, hn_o, rt_o):
    tm = x_ref.shape[1]
    h = x_ref[bi] + _dot(y_ref[bi], wo_ref[...])
    h_o[bi] = h
    hn = _rms(h, gn_ref[...])
    hn_hi = hn.astype(BF16)
    hn_o[bi] = hn_hi
    hn_lo = (hn - hn_hi.astype(F32)).astype(BF16)
    hi_parts = _dot(hn_hi, wr_ref[...])
    logits = hi_parts[:, :LANES] + hi_parts[:, LANES:] + _dot(hn_lo, wr_ref[:, :LANES]) + br_ref[...]

    lane = lax.broadcasted_iota(jnp.int32, (tm, LANES), 1).astype(F32)
    neg = -jnp.inf
    glog = jnp.where(lane < N_GROUPS, logits, neg)
    gmax = jnp.max(glog, axis=-1, keepdims=True)
    gidx = jnp.min(jnp.where(glog == gmax, lane, float(LANES)), axis=-1, keepdims=True)
    g_w = 1.0 / jnp.sum(jnp.exp(glog - gmax), axis=-1, keepdims=True)
    lo = N_GROUPS + gidx * EXPERTS_PER_GROUP
    in_group = jnp.logical_and(lane >= lo, lane < lo + EXPERTS_PER_GROUP)
    le = jnp.where(in_group, logits, neg)
    m1 = jnp.max(le, axis=-1, keepdims=True)
    i1 = jnp.min(jnp.where(le == m1, lane, float(LANES)), axis=-1, keepdims=True)
    le2 = jnp.where(lane == i1, neg, le)
    m2 = jnp.max(le2, axis=-1, keepdims=True)
    i2 = jnp.min(jnp.where(le2 == m2, lane, float(LANES)), axis=-1, keepdims=True)
    t2 = jnp.exp(m2 - m1)
    w1 = g_w / (1.0 + t2)
    w2 = g_w * t2 / (1.0 + t2)
    info = jnp.where(lane == 0.0, i1 - N_GROUPS,
                     jnp.where(lane == 1.0, i2 - N_GROUPS,
                               jnp.where(lane == 2.0, w1, jnp.where(lane == 3.0, w2, 0.0))))
    rt_o[bi] = info.T[0:SUBLANES, :]


def _plan_kernel(rt_ref, lpos_o, meta_o, runs_o, tri_ref, *, n_tok, nb_max):
    n_tiles = n_tok // TT
    esub = lax.broadcasted_iota(jnp.int32, (N_EXPERTS, TT), 0).astype(F32)

    tiles_per_seq = rt_ref.shape[2] // TT

    def onehots(i):
        off = pl.multiple_of(i * TT, TT)
        b, s = i // tiles_per_seq, pl.multiple_of((i % tiles_per_seq) * TT, TT)
        e1 = rt_ref[b, 0:1, pl.ds(s, TT)]
        e2 = rt_ref[b, 1:2, pl.ds(s, TT)]
        m1 = jnp.where(esub == e1, 1.0, 0.0)
        m2 = jnp.where(esub == e2, 1.0, 0.0)
        return off, m1, m2

    def count_body(i, cnt):
        _, m1, m2 = onehots(i)
        return cnt + jnp.sum(m1 + m2, axis=1, keepdims=True)

    counts = lax.fori_loop(0, n_tiles, count_body, jnp.zeros((N_EXPERTS, 1), F32))
    nblk = jnp.floor((counts + (BM - 1)) * (1.0 / BM))
    ei = lax.broadcasted_iota(jnp.int32, (N_EXPERTS, N_EXPERTS), 0)
    ej = lax.broadcasted_iota(jnp.int32, (N_EXPERTS, N_EXPERTS), 1)
    nblk_row = jnp.sum(jnp.where(ei == ej, nblk, 0.0), axis=0, keepdims=True)
    bstart = jnp.sum(jnp.where(ej < ei, nblk_row, 0.0), axis=1, keepdims=True)
    bend = bstart + nblk
    n_used = jnp.sum(nblk, axis=0, keepdims=True)

    meta_w = meta_o.shape[1]
    blane = lax.broadcasted_iota(jnp.int32, (N_EXPERTS, meta_w), 1).astype(F32)
    owner = jnp.sum(jnp.where(bend <= blane, 1.0, 0.0), axis=0, keepdims=True)
    owner = jnp.minimum(owner, N_EXPERTS - 1.0)
    lane1 = lax.broadcasted_iota(jnp.int32, (1, meta_w), 1)
    meta_o[0:1, :] = jnp.where(lane1 == nb_max, n_used, owner).astype(jnp.int32)
    nonempty = jnp.where(nblk > 0.0, 1.0, 0.0)
    meta_o[1:2, :] = jnp.sum(jnp.where(bend <= blane, nonempty, 0.0), axis=0,
                             keepdims=True).astype(jnp.int32)
    owned = jnp.logical_and(bstart <= blane, blane < bend)
    next_start = jnp.sum(jnp.where(owned, bend, 0.0), axis=0, keepdims=True)
    next_owner = jnp.sum(jnp.where(bend <= next_start, 1.0, 0.0), axis=0, keepdims=True)
    meta_o[2:3, :] = jnp.where(next_start < n_used, next_owner, -1.0).astype(jnp.int32)
    valid_end = jnp.sum(jnp.where(owned, bstart * float(BM) + counts, 0.0), axis=0, keepdims=True)
    meta_o[3:4, :] = jnp.clip(valid_end - lane1.astype(F32) * float(BM), 0.0, float(BM)).astype(jnp.int32)

    ti = lax.broadcasted_iota(jnp.int32, (TT, TT), 0)
    tj = lax.broadcasted_iota(jnp.int32, (TT, TT), 1)
    tri_ref[...] = jnp.where(ti <= tj, 1.0, 0.0).astype(BF16)
    tlane = lax.broadcasted_iota(jnp.int32, (N_EXPERTS, LANES), 1)

    def tile_body(i, carry):
        first_slot, t_cnt, t_slot, t_rank = carry
        off, m1, m2 = onehots(i)
        m = m1 + m2
        incl = _dot(m.astype(BF16), tri_ref[...])
        cnt = incl[:, TT - 1:TT]
        cnt_row = jnp.sum(jnp.where(ei == ej, cnt, 0.0), axis=0, keepdims=True)
        first_rank = jnp.sum(jnp.where(ej < ei, cnt_row, 0.0), axis=1, keepdims=True)
        rank = first_rank + incl - m
        lpos_o[0:1, pl.ds(off, TT)] = jnp.sum(m1 * rank, axis=0, keepdims=True).astype(jnp.int32)
        lpos_o[1:2, pl.ds(off, TT)] = jnp.sum(m2 * rank, axis=0, keepdims=True).astype(jnp.int32)
        here = tlane == i
        return (first_slot + cnt, jnp.where(here, cnt, t_cnt), jnp.where(here, first_slot, t_slot),
                jnp.where(here, first_rank, t_rank))

    zeros = jnp.zeros((N_EXPERTS, LANES), F32)
    _, t_cnt, t_slot, t_rank = lax.fori_loop(0, n_tiles, tile_body,
                                             (bstart * float(BM), zeros, zeros, zeros))
    runs_o[0] = t_cnt.astype(jnp.int32)
    runs_o[1] = t_slot.astype(jnp.int32)
    runs_o[2] = t_rank.astype(jnp.int32)
    pad = jnp.where(tlane == 0, bstart * float(BM) + counts,
                    jnp.where(tlane == 1, nblk * float(BM) - counts, 0.0))
    runs_o[3] = pad.astype(jnp.int32)


def _plan(rt, n_tok, nb_max):
    assert n_tok // TT <= LANES, "run tables hold one token tile per lane"
    meta_w = ((nb_max + 1 + LANES - 1) // LANES) * LANES
    return pl.pallas_call(
        functools.partial(_plan_kernel, n_tok=n_tok, nb_max=nb_max),
        out_shape=[jax.ShapeDtypeStruct((2, n_tok), jnp.int32),
                   jax.ShapeDtypeStruct((4, meta_w), jnp.int32),
                   jax.ShapeDtypeStruct((RUN_FIELDS, N_EXPERTS, LANES), jnp.int32)],
        scratch_shapes=[pltpu.VMEM((TT, TT), BF16)],
        compiler_params=pltpu.CompilerParams(vmem_limit_bytes=VMEM_LIMIT),
        name="plan",
    )(rt)


def _run_entry(runs_ref, field, expert, tile):
    return runs_ref[(field * N_EXPERTS + expert) * LANES + tile]


def _dispatch_kernel(runs_ref, meta_ref, hn_ref, lpos_ref, xs_hbm, xsbuf, zbuf, sem, zsem,
                     *, n_tiles, nb_max, n_pad):
    i = pl.program_id(0)
    slot = lax.rem(i, 2)

    @pl.when(i == 0)
    def _():
        zbuf[...] = jnp.zeros_like(zbuf)
        for e in range(N_EXPERTS):
            _copy_run(zbuf, 0, xs_hbm, _run_entry(runs_ref, 3, e, 0), _run_entry(runs_ref, 3, e, 1),
                      zsem)

        def unused_block(blk, carry):
            dst = xs_hbm.at[pl.ds(pl.multiple_of(blk * (BM * TILE_ROWS), BM * TILE_ROWS),
                                  BM * TILE_ROWS), :]
            pltpu.make_async_copy(zbuf, dst, zsem).start()
            return carry

        lax.fori_loop(meta_ref[nb_max], nb_max, unused_block, 0)

    @pl.when(i >= 2)
    def _():
        _wait_rows(xsbuf.at[slot], TILE_SLOTS, sem.at[slot])

    rank = lax.broadcasted_iota(jnp.int32, (TILE_SLOTS, TT), 0)
    onehot = jnp.where(rank == lpos_ref[0:1, :], 1.0, jnp.where(rank == lpos_ref[1:2, :], 1.0, 0.0))
    _store_token_tiles(xsbuf.at[slot], _dot(onehot.astype(BF16), hn_ref[...]), already_bf16=True)
    for e in range(N_EXPERTS):
        _copy_run(xsbuf.at[slot], _run_entry(runs_ref, 2, e, i), xs_hbm, _run_entry(runs_ref, 1, e, i),
                  _run_entry(runs_ref, 0, e, i), sem.at[slot])

    @pl.when(i == n_tiles - 1)
    def _():
        _wait_rows(xsbuf.at[slot], TILE_SLOTS, sem.at[slot])
        if n_tiles > 1:
            _wait_rows(xsbuf.at[1 - slot], TILE_SLOTS, sem.at[1 - slot])
        _wait_rows(xs_hbm, n_pad, zsem)


def _dispatch(runs, meta, hn, lpos, n_tok, nb_max):
    n_tiles = n_tok // TT
    cap = nb_max * BM
    grid_spec = pltpu.PrefetchScalarGridSpec(
        num_scalar_prefetch=2,
        grid=(n_tiles,),
        in_specs=[pl.BlockSpec((TT, D_MODEL), lambda i, runs_ref, meta_ref: (i, 0)),
                  pl.BlockSpec((2, TT), lambda i, runs_ref, meta_ref: (0, i))],
        out_specs=pl.BlockSpec(memory_space=pl.ANY),
        scratch_shapes=[pltpu.VMEM((2, TILE_SLOTS * TILE_ROWS, LANES), U32),
                        pltpu.VMEM((BM * TILE_ROWS, LANES), U32),
                        pltpu.SemaphoreType.DMA((2,)), pltpu.SemaphoreType.DMA],
    )
    return pl.pallas_call(
        functools.partial(_dispatch_kernel, n_tiles=n_tiles, nb_max=nb_max, n_pad=cap - 2 * n_tok),
        grid_spec=grid_spec,
        out_shape=jax.ShapeDtypeStruct((cap * TILE_ROWS, LANES), U32),
        compiler_params=pltpu.CompilerParams(dimension_semantics=("arbitrary",),
                                             vmem_limit_bytes=VMEM_LIMIT),
        name="dispatch",
    )(runs.reshape(-1), meta.reshape(-1), hn, lpos)


def _experts_kernel(meta_ref, xs_ref, wg_hbm, wu_hbm, wd_hbm, y_ref, wf32, wbf, wsem, *, nb_max, meta_w):
    b = pl.program_id(0)
    n_used = meta_ref[nb_max]

    def start_weights(expert, buf):
        for j, w_hbm in enumerate((wg_hbm, wu_hbm, wd_hbm)):
            pltpu.make_async_copy(w_hbm.at[expert], wf32.at[buf, j], wsem.at[buf]).start(priority=1)

    @pl.when(b < n_used)
    def _():
        owner = meta_ref[b]

        @pl.when(b == 0)
        def _():
            start_weights(owner, 0)

        @pl.when(jnp.logical_or(b == 0, owner != meta_ref[jnp.maximum(b - 1, 0)]))
        def _():
            buf = jnp.bitwise_and(meta_ref[meta_w + b], 1)
            pltpu.make_async_copy(wf32.at[buf], wf32.at[buf], wsem.at[buf]).wait()
            next_owner = meta_ref[2 * meta_w + b]

            @pl.when(next_owner >= 0)
            def _():
                start_weights(next_owner, 1 - buf)

            for j in range(3):
                wbf[j] = wf32[buf, j].astype(BF16)

        def geglu(n_rows):
            rows = pl.ds(0, n_rows * TILE_ROWS)
            x = _load_token_tiles(xs_ref.at[rows, :], n_rows).astype(BF16)
            mid = (jax.nn.gelu(_dot(x, wbf[0])) * _dot(x, wbf[1])).astype(BF16)
            _store_token_tiles(y_ref.at[rows, :], _dot(mid, wbf[2]))

        n_groups = lax.shift_right_logical(meta_ref[3 * meta_w + b] + (SKIP_ROWS - 1),
                                           SKIP_ROWS.bit_length() - 1)
        for g in range(1, BM // SKIP_ROWS + 1):
            @pl.when(n_groups == g)
            def _(g=g):
                geglu(g * SKIP_ROWS)
                if g * SKIP_ROWS < BM:
                    rest = (BM - g * SKIP_ROWS) * TILE_ROWS
                    y_ref[pl.ds(g * SKIP_ROWS * TILE_ROWS, rest), :] = jnp.zeros((rest, LANES), U32)

    @pl.when(b >= n_used)
    def _():
        y_ref[...] = jnp.zeros_like(y_ref)


def _experts(meta, xs, w_gate, w_up, w_down, nb_max):
    whole = pl.BlockSpec(memory_space=pl.ANY)
    rows = pl.BlockSpec((BM * TILE_ROWS, LANES), lambda b, meta_ref: (b, 0))
    grid_spec = pltpu.PrefetchScalarGridSpec(
        num_scalar_prefetch=1,
        grid=(nb_max,),
        in_specs=[rows, whole, whole, whole],
        out_specs=rows,
        scratch_shapes=[pltpu.VMEM((2, 3, D_MODEL, D_MODEL), F32),
                        pltpu.VMEM((3, D_MODEL, D_MODEL), BF16),
                        pltpu.SemaphoreType.DMA((2,))],
    )
    return pl.pallas_call(
        functools.partial(_experts_kernel, nb_max=nb_max, meta_w=meta.shape[1]),
        grid_spec=grid_spec,
        out_shape=jax.ShapeDtypeStruct(xs.shape, U32),
        compiler_params=pltpu.CompilerParams(dimension_semantics=("arbitrary",),
                                             vmem_limit_bytes=VMEM_LIMIT),
        name="experts",
    )(meta.reshape(-1), xs, w_gate, w_up, w_down)


def _combine_kernel(runs_ref, h_ref, lpos_ref, rt_ref, gn_ref, y_hbm, o_ref, ysbuf, sem, *, n_tiles):
    i = pl.program_id(0)
    slot = lax.rem(i, 2)

    def fetch(tile, buf):
        for e in range(N_EXPERTS):
            _copy_run(y_hbm, _run_entry(runs_ref, 1, e, tile), ysbuf.at[buf],
                      _run_entry(runs_ref, 2, e, tile), _run_entry(runs_ref, 0, e, tile), sem.at[buf])

    @pl.when(i == 0)
    def _():
        fetch(0, 0)

    @pl.when(i + 1 < n_tiles)
    def _():
        fetch(i + 1, 1 - slot)

    _wait_rows(ysbuf.at[slot], TILE_SLOTS, sem.at[slot])

    rank = lax.broadcasted_iota(jnp.int32, (TILE_SLOTS, TT), 0)
    first = rank == lpos_ref[0:1, :]
    second = rank == lpos_ref[1:2, :]
    slot_w = jnp.sum(jnp.where(first, rt_ref[2:3, :], 0.0) + jnp.where(second, rt_ref[3:4, :], 0.0),
                     axis=1, keepdims=True)
    onehot = jnp.where(first, 1.0, jnp.where(second, 1.0, 0.0)).astype(BF16)
    ys = (_load_token_tiles(ysbuf.at[slot], TILE_SLOTS) * slot_w).astype(BF16)
    moe = lax.dot_general(onehot, ys, (((0,), (0,)), ((), ())), preferred_element_type=F32)
    o_ref[...] = _rms(h_ref[...] + moe, gn_ref[...])


def _combine(runs, h, lpos, rt, norm_final, y, n_tok):
    n_tiles = n_tok // TT
    tiles_per_seq = rt.shape[2] // TT
    grid_spec = pltpu.PrefetchScalarGridSpec(
        num_scalar_prefetch=1,
        grid=(n_tiles,),
        in_specs=[pl.BlockSpec((TT, D_MODEL), lambda i, runs_ref: (i, 0)),
                  pl.BlockSpec((2, TT), lambda i, runs_ref: (0, i)),
                  pl.BlockSpec((None, SUBLANES, TT),
                               lambda i, runs_ref: (i // tiles_per_seq, 0, i % tiles_per_seq)),
                  pl.BlockSpec((1, D_MODEL), lambda i, runs_ref: (0, 0)),
                  pl.BlockSpec(memory_space=pl.ANY)],
        out_specs=pl.BlockSpec((TT, D_MODEL), lambda i, runs_ref: (i, 0)),
        scratch_shapes=[pltpu.VMEM((2, TILE_SLOTS * TILE_ROWS, LANES), U32),
                        pltpu.SemaphoreType.DMA((2,))],
    )
    return pl.pallas_call(
        functools.partial(_combine_kernel, n_tiles=n_tiles),
        grid_spec=grid_spec,
        out_shape=jax.ShapeDtypeStruct((n_tok, D_MODEL), F32),
        compiler_params=pltpu.CompilerParams(dimension_semantics=("arbitrary",),
                                             vmem_limit_bytes=VMEM_LIMIT),
        name="combine",
    )(runs.reshape(-1), h, lpos, rt, norm_final.reshape(1, D_MODEL).astype(F32), y)


def kernel(x, norm_mix, w_in, conv_w, conv_b, w_rg_a, b_rg_a, w_rg_x, b_rg_x, rg_lambda, rg_norm, w_alpha_up, b_alpha, gla_norm, w_out, norm_ffn, w_router_group, b_router_group, w_router_expert, b_router_expert, w_exp_gate, w_exp_up, w_exp_down, norm_final):
    bsz, seq, d = x.shape
    assert d == D_MODEL and norm_mix.shape[0] == 1, "single-layer model of width D_MODEL expected"
    n_tok = bsz * seq
    nb_max = (2 * n_tok + N_EXPERTS * (BM - 1)) // BM

    h, hn, rt = _mixer(x, norm_mix[0], w_in[0], w_alpha_up[0], b_alpha[0], conv_w[0], conv_b[0],
                       w_rg_a[0], b_rg_a[0], w_rg_x[0], b_rg_x[0], rg_lambda[0], rg_norm[0], gla_norm[0],
                       w_out[0], norm_ffn[0], w_router_group[0], b_router_group[0],
                       w_router_expert[0], b_router_expert[0])
    lpos, meta, runs = _plan(rt, n_tok, nb_max)
    xs = _dispatch(runs, meta, hn, lpos, n_tok, nb_max)
    y = _experts(meta, xs, w_exp_gate[0], w_exp_up[0], w_exp_down[0], nb_max)
    out = _combine(runs, h, lpos, rt, norm_final, y, n_tok)
    return out.reshape(bsz, seq, d)
```
